```python
import jax, jax.numpy as jnp
from jax import lax
import numpy as np

D_MODEL = 1024
BATCH = 16
SEQ = 256
DEPTH = 1
DEC_BATCH = 4
DEC_SEQ = 2048
PAST_LEN = 256

GRID_W = 64
RET_HEADS = 4
RET_DK = 128
RET_DV = 256
RET_CHUNK = 128
HG_HEADS = 8
HG_DK = 128
HG_DV = 128
HG_CHUNK = 32
D_FF = 2816
N_MOD = 9
ROPE_BASE = 10000.0
EPS = 1e-6
RET_QK = RET_HEADS * RET_DK
RET_V = RET_HEADS * RET_DV
HG_KW = HG_HEADS * HG_DK
HG_VW = HG_HEADS * HG_DV
SPLITS = (RET_QK, RET_QK, RET_V, RET_V, HG_KW, HG_KW, HG_KW, HG_VW, HG_VW, D_MODEL, D_MODEL)
IN_WIDTH = sum(SPLITS)

kernel_name = 'hybrid_retention_hgrn2_diffusion_step'


def rms_norm(x, w):
    xf = x.astype(jnp.float32)
    y = xf * lax.rsqrt(jnp.mean(xf * xf, axis=-1, keepdims=True) + EPS)
    return (y * w.astype(jnp.float32)).astype(x.dtype)


def head_rms(x):
    return x * lax.rsqrt(jnp.mean(x * x, axis=-1, keepdims=True) + EPS)


def modulate(x, shift, scale):
    return x * (1.0 + scale) + shift


def swiglu(h, w13, w2):
    a, b = jnp.split(h @ w13, 2, axis=-1)
    return (jax.nn.silu(a) * b) @ w2


def rope_2d(x, n_tokens):
    rows = n_tokens // GRID_W
    r, cl = jnp.meshgrid(jnp.arange(rows), jnp.arange(GRID_W), indexing='ij')
    r = r.reshape(-1).astype(jnp.float32)
    cl = cl.reshape(-1).astype(jnp.float32)
    half = x.shape[-1] // 2
    quarter = half // 2
    inv_freq = ROPE_BASE ** (-jnp.arange(quarter, dtype=jnp.float32) / quarter)

    def rot(xh, pos):
        ang = pos[:, None] * inv_freq[None, :]
        cos = jnp.cos(ang)[None, :, None, :]
        sin = jnp.sin(ang)[None, :, None, :]
        x1, x2 = jnp.split(xh, 2, axis=-1)
        return jnp.concatenate([x1 * cos - x2 * sin, x2 * cos + x1 * sin], axis=-1)

    return jnp.concatenate([rot(x[..., :half], r), rot(x[..., half:], cl)], axis=-1)


def to_chunks(x, chunk):
    b, n = x.shape[:2]
    return x.reshape(b, n // chunk, chunk, *x.shape[2:]).swapaxes(0, 1)


def from_chunks(y):
    n, b, chunk = y.shape[:3]
    return y.swapaxes(0, 1).reshape(b, n * chunk, *y.shape[3:])


def retention_scan(q, k, v, log_g, s0):
    C = RET_CHUNK
    idx = jnp.arange(C, dtype=jnp.float32)
    diff = idx[:, None] - idx[None, :]
    causal = diff >= 0
    dmat = jnp.where(causal[None], jnp.exp(jnp.where(causal, diff, 0.0)[None] * log_g[:, None, None]), 0.0)
    q_dec = jnp.exp((idx + 1.0)[:, None] * log_g[None, :])
    k_dec = jnp.exp((C - 1.0 - idx)[:, None] * log_g[None, :])
    c_dec = jnp.exp(C * log_g)

    def step(s, inp):
        qc, kc, vc = inp
        scores = jnp.einsum('bihd,bjhd->bhij', qc, kc) * dmat[None]
        o = (jnp.einsum('bhij,bjhe->bihe', scores, vc)
             + jnp.einsum('bihd,bhde->bihe', qc * q_dec[None, :, :, None], s))
        s = c_dec[None, :, None, None] * s + jnp.einsum('bjhd,bjhe->bhde', kc * k_dec[None, :, :, None], vc)
        return s, o

    s_fin, o = lax.scan(step, s0, (to_chunks(q, C), to_chunks(k, C), to_chunks(v, C)))
    return from_chunks(o), s_fin


def gla_scan(q, k, v, log_f, s0):
    C = HG_CHUNK
    tri = jnp.tril(jnp.ones((C, C), dtype=bool))[None, :, :, None, None]

    def step(s, inp):
        qc, kc, vc, gc = inp
        b = jnp.cumsum(gc, axis=1)
        diff = b[:, :, None] - b[:, None, :]
        dec = jnp.where(tri, jnp.exp(jnp.where(tri, diff, 0.0)), 0.0)
        a = jnp.einsum('bthk,bshk,btshk->bhts', qc, kc, dec)
        o = jnp.einsum('bhts,bshv->bthv', a, vc) + jnp.einsum('bthk,bhkv->bthv', qc * jnp.exp(b), s)
        b_last = b[:, -1]
        s = (jnp.exp(b_last)[..., None] * s
             + jnp.einsum('bshk,bshv->bhkv', kc * jnp.exp(b_last[:, None] - b), vc))
        return s, o

    s_fin, o = lax.scan(step, s0, (to_chunks(q, C), to_chunks(k, C), to_chunks(v, C), to_chunks(log_f, C)))
    return from_chunks(o), s_fin


def flip(t):
    return jnp.flip(t, axis=1)


def mixer(h, s0_ret, s0_hg, use_rope, lp):
    f32 = jnp.float32
    B, L, _ = h.shape
    proj = h @ lp['w_in']
    cuts = np.cumsum(SPLITS)[:-1].tolist()
    rq, rk, rv, rg, hq, hff, hfb, hi, hog, gr, gh = jnp.split(proj, cuts, axis=-1)

    rq = rq.reshape(B, L, RET_HEADS, RET_DK).astype(f32)
    rk = rk.reshape(B, L, RET_HEADS, RET_DK).astype(f32)
    rv = rv.reshape(B, L, RET_HEADS, RET_DV).astype(f32)
    if use_rope:
        rq = rope_2d(rq, L)
        rk = rope_2d(rk, L)
    rk = rk * (RET_DK ** -0.5)
    log_gam = jax.nn.log_sigmoid(lp['ret_decay'].astype(f32))
    of, sf = retention_scan(rq, rk, rv, log_gam[0], s0_ret[:, 0])
    ob, sb = retention_scan(flip(rq), flip(rk), flip(rv), log_gam[1], s0_ret[:, 1])
    o_ret = head_rms(of + flip(ob)).reshape(B, L, RET_V) * jax.nn.silu(rg.astype(f32))
    y_ret = o_ret.astype(h.dtype) @ lp['w_ret_proj']

    q = jax.nn.silu(hq.reshape(B, L, HG_HEADS, HG_DK).astype(f32)) * (HG_DK ** -0.5)
    iv = hi.reshape(B, L, HG_HEADS, HG_DV).astype(f32)
    lb = lp['lb'].reshape(2, HG_HEADS, HG_DK)
    f_f = lb[0] + (1.0 - lb[0]) * jax.nn.sigmoid(hff.reshape(B, L, HG_HEADS, HG_DK).astype(f32))
    f_b = lb[1] + (1.0 - lb[1]) * jax.nn.sigmoid(hfb.reshape(B, L, HG_HEADS, HG_DK).astype(f32))
    of2, sf2 = gla_scan(q, 1.0 - f_f, iv, jnp.log(f_f), s0_hg[:, 0])
    ob2, sb2 = gla_scan(flip(q), flip(1.0 - f_b), flip(iv), flip(jnp.log(f_b)), s0_hg[:, 1])
    o_hg = head_rms(of2 + flip(ob2)) * lp['hg_norm_w'].astype(f32).reshape(HG_HEADS, HG_DV)
    o_hg = o_hg.reshape(B, L, HG_VW) * jax.nn.silu(hog.astype(f32))
    y_hg = o_hg.astype(h.dtype) @ lp['w_hg_proj']

    merged = jax.nn.sigmoid(gr) * y_ret + jax.nn.sigmoid(gh) * y_hg
    out = merged @ lp['w_o']
    return out, jnp.stack([sf, sb], axis=1), jnp.stack([sf2, sb2], axis=1)


def trunk_layer(x, mod, s0_ret, s0_hg, use_rope, lp):
    sh1, sc1, g1, sh2, sc2, g2, sh3, sc3, g3 = jnp.split(mod, N_MOD, axis=-1)
    nw = lp['norm_w']
    x = x + 0.5 * g1 * swiglu(modulate(rms_norm(x, nw[0]), sh1, sc1), lp['ffn1_w13'], lp['ffn1_w2'])
    mix, s_ret, s_hg = mixer(modulate(rms_norm(x, nw[1]), sh2, sc2), s0_ret, s0_hg, use_rope, lp)
    x = x + g2 * mix
    x = x + 0.5 * g3 * swiglu(modulate(rms_norm(x, nw[2]), sh3, sc3), lp['ffn2_w13'], lp['ffn2_w2'])
    return x, s_ret, s_hg


def setup_inputs(seed: int = 0) -> dict:
    key = jax.random.key(seed)
    ks = jax.random.split(key, 24)
    f32 = jnp.float32

    def nrm(k, shape, scale):
        return jax.random.normal(k, shape, f32) * scale

    ret_base = jnp.asarray(np.log(2.0 ** (5.0 + np.arange(RET_HEADS)) - 1.0), f32)
    return {
        'x_prompt': nrm(ks[0], (BATCH, SEQ, D_MODEL), 1.0),
        'x_sample': nrm(ks[1], (DEC_BATCH, DEC_SEQ, D_MODEL), 1.0),
        'state_ret': nrm(ks[2], (DEC_BATCH, DEPTH, 2, RET_HEADS, RET_DK, RET_DV), 0.5),
        'state_hgrn': nrm(ks[3], (DEC_BATCH, DEPTH, 2, HG_HEADS, HG_DK, HG_DV), 0.5),
        'c': nrm(ks[4], (DEC_BATCH, D_MODEL), 1.0),
        'c_ctx': nrm(ks[5], (D_MODEL,), 1.0),
        'ada_w': nrm(ks[6], (DEPTH, D_MODEL, N_MOD * D_MODEL), 0.5 * D_MODEL ** -0.5),
        'ada_b': nrm(ks[7], (DEPTH, N_MOD * D_MODEL), 0.01),
        'norm_w': 1.0 + nrm(ks[8], (DEPTH, 3, D_MODEL), 0.01),
        'ffn1_w13': nrm(ks[9], (DEPTH, D_MODEL, 2 * D_FF), D_MODEL ** -0.5),
        'ffn1_w2': nrm(ks[10], (DEPTH, D_FF, D_MODEL), D_FF ** -0.5),
        'ffn2_w13': nrm(ks[11], (DEPTH, D_MODEL, 2 * D_FF), D_MODEL ** -0.5),
        'ffn2_w2': nrm(ks[12], (DEPTH, D_FF, D_MODEL), D_FF ** -0.5),
        'w_in': nrm(ks[13], (DEPTH, D_MODEL, IN_WIDTH), D_MODEL ** -0.5),
        'ret_decay': ret_base[None, None, :] + nrm(ks[14], (DEPTH, 2, RET_HEADS), 0.1),
        'hg_lb_logits': nrm(ks[15], (2, DEPTH + 1, HG_KW), 0.1),
        'hg_norm_w': 1.0 + nrm(ks[16], (DEPTH, HG_VW), 0.01),
        'w_ret_proj': nrm(ks[17], (DEPTH, RET_V, D_MODEL), RET_V ** -0.5),
        'w_hg_proj': nrm(ks[18], (DEPTH, HG_VW, D_MODEL), HG_VW ** -0.5),
        'w_o': nrm(ks[19], (DEPTH, D_MODEL, D_MODEL), D_MODEL ** -0.5),
        'final_norm_w': 1.0 + nrm(ks[20], (D_MODEL,), 0.01),
    }


def reference(x_prompt, x_sample, state_ret, state_hgrn, c, c_ctx, ada_w, ada_b, norm_w,
              ffn1_w13, ffn1_w2, ffn2_w13, ffn2_w2, w_in, ret_decay, hg_lb_logits, hg_norm_w,
              w_ret_proj, w_hg_proj, w_o, final_norm_w):
    f32 = jnp.float32
    lb_all = jnp.cumsum(jax.nn.softmax(hg_lb_logits.astype(f32), axis=1), axis=1)
    bp = x_prompt.shape[0]
    zero_ret = jnp.zeros((bp, 2, RET_HEADS, RET_DK, RET_DV), f32)
    zero_hg = jnp.zeros((bp, 2, HG_HEADS, HG_DK, HG_DV), f32)
    xp, xs = x_prompt, x_sample
    new_ret, new_hg = [], []
    for l in range(DEPTH):
        lp = {
            'norm_w': norm_w[l], 'ffn1_w13': ffn1_w13[l], 'ffn1_w2': ffn1_w2[l],
            'ffn2_w13': ffn2_w13[l], 'ffn2_w2': ffn2_w2[l], 'w_in': w_in[l],
            'ret_decay': ret_decay[l], 'lb': lb_all[:, l], 'hg_norm_w': hg_norm_w[l],
            'w_ret_proj': w_ret_proj[l], 'w_hg_proj': w_hg_proj[l], 'w_o': w_o[l],
        }
        mod_ctx = (jax.nn.silu(c_ctx) @ ada_w[l] + ada_b[l])[None, None, :]
        mod_lat = (jax.nn.silu(c) @ ada_w[l] + ada_b[l])[:, None, :]
        xp, s_ret, s_hg = trunk_layer(xp, mod_ctx, zero_ret, zero_hg, False, lp)
        new_ret.append(s_ret)
        new_hg.append(s_hg)
        xs, _, _ = trunk_layer(xs, mod_lat, state_ret[:, l].astype(f32), state_hgrn[:, l].astype(f32), True, lp)
    y_prompt = rms_norm(xp, final_norm_w)
    y_sample = rms_norm(xs, final_norm_w)
    new_state_ret = jnp.stack(new_ret, axis=1).astype(x_prompt.dtype)
    new_state_hgrn = jnp.stack(new_hg, axis=1).astype(x_prompt.dtype)
    return (y_prompt, y_sample, new_state_ret, new_state_hgrn)
```

```python
import functools

import numpy as np
import jax
import jax.numpy as jnp
from jax import lax
from jax.experimental import pallas as pl
from jax.experimental.pallas import tpu as pltpu

F32 = jnp.float32
BF16 = jnp.bfloat16

D_MODEL = 1024
GRID_W = 64
RET_HEADS = 4
RET_DK = 128
RET_DV = 256
HG_HEADS = 8
HG_DK = 128
HG_DV = 128
D_FF = 2816
N_MOD = 9
ROPE_BASE = 10000.0
EPS = 1e-6

C_RQ, C_RK, C_RV, C_RG, C_HQ, C_HFF, C_HFB, C_HI, C_HOG, C_GR, C_GH = (
    0, 512, 1024, 2048, 3072, 4096, 5120, 6144, 7168, 8192, 9216)
IN_WIDTH = 10240

VMEM_LIMIT_BYTES = 56 * 1024 * 1024

SCAN_BLOCK = 128
FF_TILE = 1408
EXP_CLAMP = 80.0


def _dot(a, b):
    return jnp.dot(a, b, preferred_element_type=F32)


def _dot_nt(a, b):
    return lax.dot_general(a, b, (((1,), (1,)), ((), ())), preferred_element_type=F32)


def _dot_tn(a, b):
    return lax.dot_general(a, b, (((0,), (0,)), ((), ())), preferred_element_type=F32)


def _sigmoid(x):
    return 1.0 / (1.0 + jnp.exp(-x))


def _rms(x, w):
    ms = jnp.mean(x * x, axis=-1, keepdims=True)
    return x * lax.rsqrt(ms + EPS) * w


def _params(sem):
    return pltpu.CompilerParams(dimension_semantics=sem, vmem_limit_bytes=VMEM_LIMIT_BYTES)


def _mod_kernel(c_ref, w_ref, b_ref, o_ref):
    c = c_ref[...]
    s = (c * _sigmoid(c)).astype(BF16)
    o_ref[...] = _dot(s, w_ref[...].astype(BF16)) + b_ref[...]


def _mod_call(cvec, ada_w, ada_b):
    n = ada_w.shape[1]
    tn = 1024
    return pl.pallas_call(
        _mod_kernel,
        grid=(n // tn,),
        in_specs=[
            pl.BlockSpec((8, D_MODEL), lambda j: (0, 0)),
            pl.BlockSpec((D_MODEL, tn), lambda j: (0, j)),
            pl.BlockSpec((1, tn), lambda j: (0, j)),
        ],
        out_specs=pl.BlockSpec((8, tn), lambda j: (0, j)),
        out_shape=jax.ShapeDtypeStruct((8, n), F32),
        compiler_params=_params(("arbitrary",)),
        name="mod",
    )(cvec, ada_w, ada_b)


def _ffn_kernel(x_ref, mod_ref, nw_ref, fw_ref, wa_ref, wb_ref, w2_ref, o_ref, h_scr, acc_scr,
                *, mod_base, norm_row, final, n_ff):
    j = pl.program_id(1)

    @pl.when(j == 0)
    def _():
        y = _rms(x_ref[...], nw_ref[norm_row:norm_row + 1, :])
        sh = mod_ref[mod_base:mod_base + 1, :]
        sc = mod_ref[mod_base + 1:mod_base + 2, :]
        h_scr[...] = (y * (1.0 + sc) + sh).astype(BF16)

    h = h_scr[...]
    a = _dot(h, wa_ref[...])
    b = _dot(h, wb_ref[...])
    act = (a * _sigmoid(a) * b).astype(BF16)
    part = _dot(act, w2_ref[...])

    @pl.when(j == 0)
    def _():
        acc_scr[...] = part

    @pl.when(jnp.logical_and(j > 0, j < n_ff - 1))
    def _():
        acc_scr[...] += part

    @pl.when(j == n_ff - 1)
    def _():
        tot = acc_scr[...] + part
        g = mod_ref[mod_base + 2:mod_base + 3, :]
        xo = x_ref[...] + 0.5 * g * tot
        if final:
            xo = _rms(xo, fw_ref[...])
        o_ref[...] = xo


def _ffn_call(x, mod3, norm_w, final_w, w13, w2, *, mod_base, norm_row, final, mod_row_fn, tm, name):
    n = x.shape[0]
    n_ff = D_FF // FF_TILE
    kern = functools.partial(_ffn_kernel, mod_base=mod_base, norm_row=norm_row, final=final, n_ff=n_ff)
    return pl.pallas_call(
        kern,
        grid=(n // tm, n_ff),
        in_specs=[
            pl.BlockSpec((tm, D_MODEL), lambda i, j: (i, 0)),
            pl.BlockSpec((None, N_MOD, D_MODEL), lambda i, j: (mod_row_fn(i, tm), 0, 0)),
            pl.BlockSpec((3, D_MODEL), lambda i, j: (0, 0)),
            pl.BlockSpec((1, D_MODEL), lambda i, j: (0, 0)),
            pl.BlockSpec((D_MODEL, FF_TILE), lambda i, j: (0, j)),
            pl.BlockSpec((D_MODEL, FF_TILE), lambda i, j: (0, n_ff + j)),
            pl.BlockSpec((FF_TILE, D_MODEL), lambda i, j: (j, 0)),
        ],
        out_specs=pl.BlockSpec((tm, D_MODEL), lambda i, j: (i, 0)),
        out_shape=jax.ShapeDtypeStruct((n, D_MODEL), F32),
        scratch_shapes=[pltpu.VMEM((tm, D_MODEL), BF16), pltpu.VMEM((tm, D_MODEL), F32)],
        compiler_params=_params(("parallel", "arbitrary")),
        name=name,
    )(x, mod3, norm_w, final_w, w13, w13, w2)


def _proj_kernel(x_ref, mod_ref, nw_ref, lbl_ref, w_ref,
                 qk_ref, rv_ref, rg_ref, hq_ref, g_ref, hi_ref, hog_ref, gr_ref, gh_ref, *, layer):
    y = _rms(x_ref[...], nw_ref[1:2, :])
    h = (y * (1.0 + mod_ref[4:5, :]) + mod_ref[3:4, :]).astype(BF16)

    def mm(c0, width):
        return _dot(h, w_ref[:, c0:c0 + width])

    qk_ref[:, 0:512] = mm(C_RQ, 512)
    qk_ref[:, 512:1024] = mm(C_RK, 512) * (RET_DK ** -0.5)
    rv_ref[...] = mm(C_RV, 1024).astype(BF16)
    t = mm(C_RG, 1024)
    rg_ref[...] = t * _sigmoid(t)
    t = mm(C_HQ, 1024)
    hq_ref[...] = t * _sigmoid(t) * (HG_DK ** -0.5)
    for d in range(2):
        lg = lbl_ref[d]
        e = jnp.exp(lg - jnp.max(lg, axis=0, keepdims=True))
        lb = jnp.sum(e[0:layer + 1, :], axis=0, keepdims=True) / jnp.sum(e, axis=0, keepdims=True)
        t = mm(C_HFF + 1024 * d, 1024)
        f = lb + (1.0 - lb) * _sigmoid(t)
        g_ref[:, 1024 * d:1024 * (d + 1)] = jnp.log(f)
    hi_ref[...] = mm(C_HI, 1024).astype(BF16)
    t = mm(C_HOG, 1024)
    hog_ref[...] = t * _sigmoid(t)
    gr_ref[...] = _sigmoid(mm(C_GR, 1024))
    gh_ref[...] = _sigmoid(mm(C_GH, 1024))


def _proj_call(x, mod3, norm_w, lb_logits, w_in, *, layer, mod_row_fn, tm, name):
    n = x.shape[0]
    slots = lb_logits.shape[1]
    row = lambda w: pl.BlockSpec((tm, w), lambda i: (i, 0))
    outs = [(1024, F32), (1024, BF16), (1024, F32), (1024, F32), (2048, F32),
            (1024, BF16), (1024, F32), (1024, F32), (1024, F32)]
    return pl.pallas_call(
        functools.partial(_proj_kernel, layer=layer),
        grid=(n // tm,),
        in_specs=[
            row(D_MODEL),
            pl.BlockSpec((None, N_MOD, D_MODEL), lambda i: (mod_row_fn(i, tm), 0, 0)),
            pl.BlockSpec((3, D_MODEL), lambda i: (0, 0)),
            pl.BlockSpec((2, slots, 1024), lambda i: (0, 0, 0)),
            pl.BlockSpec((D_MODEL, IN_WIDTH), lambda i: (0, 0), pipeline_mode=pl.Buffered(1)),
        ],
        out_specs=[row(w) for w, _ in outs],
        out_shape=[jax.ShapeDtypeStruct((n, w), dt) for w, dt in outs],
        compiler_params=_params(("parallel",)),
        name=name,
    )(x, mod3, norm_w, lb_logits, w_in)


def _log_sigmoid(x):
    return -(jnp.maximum(-x, 0.0) + jnp.log1p(jnp.exp(-jnp.abs(x))))


def _rope(x, cos, sins, first_quarter):
    partner = jnp.where(first_quarter, pltpu.roll(x, 96, 1), pltpu.roll(x, 32, 1))
    return x * cos + partner * sins


def _ret_kernel(*refs, L, rope, has_s0, out_state):
    it = iter(refs)
    dec_ref = next(it)
    q_ref, k_ref, v_ref, gate_ref = next(it), next(it), next(it), next(it)
    cos_ref = sin_ref = s0_ref = st_ref = None
    if rope:
        cos_ref, sin_ref = next(it), next(it)
    if has_s0:
        s0_ref = next(it)
    o_ref = next(it)
    if out_state:
        st_ref = next(it)
    oacc, sf_scr, sb_scr = next(it), next(it), next(it)

    C = SCAN_BLOCK
    n = L // C
    hd = pl.program_id(1)
    ii = lax.broadcasted_iota(jnp.int32, (C, C), 0).astype(F32)
    jj = lax.broadcasted_iota(jnp.int32, (C, C), 1).astype(F32)
    lgf = _log_sigmoid(jnp.full((C, C), dec_ref[0, hd], F32))
    lgb = _log_sigmoid(jnp.full((C, C), dec_ref[1, hd], F32))
    diff = ii - jj
    dmat = (jnp.where(diff >= 0, jnp.exp(jnp.maximum(diff, 0.0) * lgf), 0.0)
            + jnp.where(diff <= 0, jnp.exp(jnp.maximum(-diff, 0.0) * lgb), 0.0))
    qdf = jnp.exp((ii + 1.0) * lgf)
    kdf = jnp.exp((C - 1.0 - ii) * lgf)
    qdb = jnp.exp((C - ii) * lgb)
    kdb = jnp.exp(ii * lgb)
    cdf = jnp.exp(C * _log_sigmoid(jnp.full((RET_DK, RET_DV), dec_ref[0, hd], F32)))
    cdb = jnp.exp(C * _log_sigmoid(jnp.full((RET_DK, RET_DV), dec_ref[1, hd], F32)))
    lane = lax.broadcasted_iota(jnp.int32, (C, RET_DK), 1)
    first_quarter = (lane & 32) == 0

    if has_s0:
        sf_scr[...] = s0_ref[0]
        sb_scr[...] = s0_ref[1]
    else:
        sf_scr[...] = jnp.zeros((RET_DK, RET_DV), F32)
        sb_scr[...] = jnp.zeros((RET_DK, RET_DV), F32)

    def load_qk(sl):
        q = q_ref[sl, :]
        k = k_ref[sl, :]
        if rope:
            cos = cos_ref[sl, :]
            sins = sin_ref[sl, :]
            q = _rope(q, cos, sins, first_quarter)
            k = _rope(k, cos, sins, first_quarter)
        return q, k

    def fwd(c, carry):
        sl = pl.ds(pl.multiple_of(c * C, C), C)
        q, k = load_qk(sl)
        v = v_ref[sl, :]
        s = _dot_nt(q.astype(BF16), k.astype(BF16))
        p = (s * dmat).astype(BF16)
        st = sf_scr[...]
        oacc[sl, :] = _dot(p, v) + _dot((q * qdf).astype(BF16), st.astype(BF16))
        sf_scr[...] = cdf * st + _dot_tn((k * kdf).astype(BF16), v)
        return carry

    def bwd(t, carry):
        c = n - 1 - t
        sl = pl.ds(pl.multiple_of(c * C, C), C)
        q, k = load_qk(sl)
        v = v_ref[sl, :]
        st = sb_scr[...]
        o = oacc[sl, :] + _dot((q * qdb).astype(BF16), st.astype(BF16))
        sb_scr[...] = cdb * st + _dot_tn((k * kdb).astype(BF16), v)
        ms = jnp.mean(o * o, axis=-1, keepdims=True)
        o = o * lax.rsqrt(ms + EPS)
        o_ref[sl, :] = (o * gate_ref[sl, :]).astype(BF16)
        return carry

    lax.fori_loop(0, n, fwd, 0, unroll=2)
    lax.fori_loop(0, n, bwd, 0, unroll=2)
    if out_state:
        st_ref[0] = sf_scr[...]
        st_ref[1] = sb_scr[...]


def _ret_call(dec, qk, rv, rg, cos, sins, s0, *, B, L, rope, out_state, name):
    has_s0 = s0 is not None
    H = RET_HEADS
    in_specs = [
        pl.BlockSpec(memory_space=pltpu.SMEM),
        pl.BlockSpec((None, L, RET_DK), lambda b, h: (b, 0, h)),
        pl.BlockSpec((None, L, RET_DK), lambda b, h: (b, 0, H + h)),
        pl.BlockSpec((None, L, RET_DV), lambda b, h: (b, 0, h)),
        pl.BlockSpec((None, L, RET_DV), lambda b, h: (b, 0, h)),
    ]
    args = [dec, qk, qk, rv, rg]
    if rope:
        in_specs += [pl.BlockSpec((L, RET_DK), lambda b, h: (0, 0))] * 2
        args += [cos, sins]
    if has_s0:
        in_specs.append(pl.BlockSpec((None, 2, None, RET_DK, RET_DV), lambda b, h: (b, 0, h, 0, 0)))
        args.append(s0)
    out_specs = [pl.BlockSpec((None, L, RET_DV), lambda b, h: (b, 0, h))]
    out_shape = [jax.ShapeDtypeStruct((B, L, H * RET_DV), BF16)]
    if out_state:
        out_specs.append(pl.BlockSpec((None, 2, None, RET_DK, RET_DV), lambda b, h: (b, 0, h, 0, 0)))
        out_shape.append(jax.ShapeDtypeStruct((B, 2, H, RET_DK, RET_DV), F32))
    res = pl.pallas_call(
        functools.partial(_ret_kernel, L=L, rope=rope, has_s0=has_s0, out_state=out_state),
        grid=(B, H),
        in_specs=in_specs,
        out_specs=out_specs,
        out_shape=out_shape,
        scratch_shapes=[pltpu.VMEM((L, RET_DV), F32), pltpu.VMEM((RET_DK, RET_DV), F32),
                        pltpu.VMEM((RET_DK, RET_DV), F32)],
        compiler_params=_params(("parallel", "parallel")),
        name=name,
    )(*args)
    return res


def _split3(x):
    hi = x.astype(BF16)
    r = x - hi.astype(F32)
    mid = r.astype(BF16)
    lo = (r - mid.astype(F32)).astype(BF16)
    return hi, mid, lo


def _gla_block(q, g, v, st_t, tri, xor_ts, order_ts, row, *, reverse):
    T = SCAN_BLOCK
    K = q.shape[1]
    k = 1.0 - jnp.exp(g)
    hi, mid, lo = _split3(g)
    b = _dot(tri, hi) + _dot(tri, mid) + _dot(tri, lo)

    a = jnp.zeros((T, T), F32)
    for m in (64, 32, 16, 8):
        seg = 2 * m
        b3 = b.reshape(T // seg, seg, K)
        ref_row = m if reverse else m - 1
        d = (b3 - b3[:, ref_row:ref_row + 1, :]).reshape(T, K)
        e = jnp.exp(-jnp.abs(d))
        far_half = (row & m) != 0
        q_role = jnp.logical_not(far_half) if reverse else far_half
        qm = jnp.where(q_role, q * e, 0.0).astype(BF16)
        km = jnp.where(q_role, 0.0, k * e).astype(BF16)
        a = a + jnp.where(xor_ts < seg, _dot_nt(qm, km), 0.0)
    b3 = b.reshape(T // 8, 8, K)
    ref_row = 4 if reverse else 3
    d = (b3 - b3[:, ref_row:ref_row + 1, :]).reshape(T, K)
    ql = (q * jnp.exp(jnp.minimum(d, EXP_CLAMP))).astype(BF16)
    kl = (k * jnp.exp(jnp.minimum(-d, EXP_CLAMP))).astype(BF16)
    leaf = jnp.logical_and(xor_ts < 8, order_ts)
    a = a + jnp.where(leaf, _dot_nt(ql, kl), 0.0)

    o = _dot(a.astype(BF16), v) + _dot_nt((q * jnp.exp(b)).astype(BF16), st_t.astype(BF16))
    end_row = 0 if reverse else T - 1
    b_end = b.reshape(1, T, K)[:, end_row:end_row + 1, :]
    b_end_full = jnp.broadcast_to(b_end, (1, T, K)).reshape(T, K)
    kd = (k * jnp.exp(b_end_full - b)).astype(BF16)
    new_st = st_t * jnp.exp(b_end.reshape(1, K)) + _dot_tn(v, kd)
    return o, new_st


def _hgrn_kernel(*refs, L, has_s0, out_state):
    it = iter(refs)
    q_ref, g_f_ref, g_b_ref, v_ref, gate_ref, nw_ref = (next(it) for _ in range(6))
    s0_ref = next(it) if has_s0 else None
    o_ref = next(it)
    st_ref = next(it) if out_state else None
    of_acc, ob_acc, sf_scr, sb_scr = next(it), next(it), next(it), next(it)

    T = SCAN_BLOCK
    n = L // T
    ti = lax.broadcasted_iota(jnp.int32, (T, T), 0)
    si = lax.broadcasted_iota(jnp.int32, (T, T), 1)
    xor_ts = ti ^ si
    le_ts = si <= ti
    ge_ts = si >= ti
    tri_f = le_ts.astype(BF16)
    tri_b = ge_ts.astype(BF16)
    row = lax.broadcasted_iota(jnp.int32, (T, HG_DK), 0)

    if has_s0:
        sf_scr[...] = s0_ref[0].T
        sb_scr[...] = s0_ref[1].T
    else:
        sf_scr[...] = jnp.zeros((HG_DV, HG_DK), F32)
        sb_scr[...] = jnp.zeros((HG_DV, HG_DK), F32)

    def step(i, carry):
        sl = pl.ds(pl.multiple_of(i * T, T), T)
        o, st = _gla_block(q_ref[sl, :], g_f_ref[sl, :], v_ref[sl, :], sf_scr[...],
                           tri_f, xor_ts, le_ts, row, reverse=False)
        of_acc[sl, :] = o
        sf_scr[...] = st
        sl = pl.ds(pl.multiple_of((n - 1 - i) * T, T), T)
        o, st = _gla_block(q_ref[sl, :], g_b_ref[sl, :], v_ref[sl, :], sb_scr[...],
                           tri_b, xor_ts, ge_ts, row, reverse=True)
        ob_acc[sl, :] = o
        sb_scr[...] = st
        return carry

    lax.fori_loop(0, n, step, 0)

    def fin(i, carry):
        sl = pl.ds(pl.multiple_of(i * T, T), T)
        o = of_acc[sl, :] + ob_acc[sl, :]
        ms = jnp.mean(o * o, axis=-1, keepdims=True)
        o = o * lax.rsqrt(ms + EPS) * nw_ref[...]
        o_ref[sl, :] = (o * gate_ref[sl, :]).astype(BF16)
        return carry

    lax.fori_loop(0, n, fin, 0)
    if out_state:
        st_ref[0] = sf_scr[...].T
        st_ref[1] = sb_scr[...].T


def _hgrn_call(hq, g, hi, hog, nw, s0, *, B, L, out_state, name):
    has_s0 = s0 is not None
    H = HG_HEADS
    blk = lambda off: pl.BlockSpec((None, L, HG_DK), lambda b, h: (b, 0, off + h))
    in_specs = [blk(0), blk(0), blk(H), blk(0), blk(0), pl.BlockSpec((1, HG_DV), lambda b, h: (0, h))]
    args = [hq, g, g, hi, hog, nw]
    if has_s0:
        in_specs.append(pl.BlockSpec((None, 2, None, HG_DK, HG_DV), lambda b, h: (b, 0, h, 0, 0)))
        args.append(s0)
    out_specs = [blk(0)]
    out_shape = [jax.ShapeDtypeStruct((B, L, H * HG_DV), BF16)]
    if out_state:
        out_specs.append(pl.BlockSpec((None, 2, None, HG_DK, HG_DV), lambda b, h: (b, 0, h, 0, 0)))
        out_shape.append(jax.ShapeDtypeStruct((B, 2, H, HG_DK, HG_DV), F32))
    return pl.pallas_call(
        functools.partial(_hgrn_kernel, L=L, has_s0=has_s0, out_state=out_state),
        grid=(B, H),
        in_specs=in_specs,
        out_specs=out_specs,
        out_shape=out_shape,
        scratch_shapes=[pltpu.VMEM((L, HG_DV), F32), pltpu.VMEM((L, HG_DV), F32),
                        pltpu.VMEM((HG_DV, HG_DK), F32), pltpu.VMEM((HG_DV, HG_DK), F32)],
        compiler_params=_params(("parallel", "parallel")),
        name=name,
    )(*args)


def _merge_kernel(x_ref, mod_ref, oret_ref, ohg_ref, gr_ref, gh_ref, wr_ref, wh_ref, wo_ref, o_ref):
    y = gr_ref[...] * _dot(oret_ref[...], wr_ref[...]) + gh_ref[...] * _dot(ohg_ref[...], wh_ref[...])
    out = _dot(y.astype(BF16), wo_ref[...])
    o_ref[...] = x_ref[...] + mod_ref[5:6, :] * out


def _merge_call(x, mod3, oret, ohg, gr, gh, wr, wh, wo, *, mod_row_fn, tm, name):
    n = x.shape[0]
    row = pl.BlockSpec((tm, D_MODEL), lambda i: (i, 0))
    wspec = pl.BlockSpec((D_MODEL, D_MODEL), lambda i: (0, 0))
    return pl.pallas_call(
        _merge_kernel,
        grid=(n // tm,),
        in_specs=[row, pl.BlockSpec((None, N_MOD, D_MODEL), lambda i: (mod_row_fn(i, tm), 0, 0)),
                  row, row, row, row, wspec, wspec, wspec],
        out_specs=row,
        out_shape=jax.ShapeDtypeStruct((n, D_MODEL), F32),
        compiler_params=_params(("parallel",)),
        name=name,
    )(x, mod3, oret, ohg, gr, gh, wr, wh, wo)


def _rope_tables(n_tokens):
    pos = np.arange(n_tokens)
    r = (pos // GRID_W).astype(np.float32)
    cl = (pos % GRID_W).astype(np.float32)
    quarter = RET_DK // 4
    inv_freq = ROPE_BASE ** (-jnp.arange(quarter, dtype=F32) / quarter)
    ang_r = jnp.asarray(r)[:, None] * inv_freq[None, :]
    ang_c = jnp.asarray(cl)[:, None] * inv_freq[None, :]
    cos = jnp.concatenate([jnp.cos(ang_r)] * 2 + [jnp.cos(ang_c)] * 2, axis=-1)
    sins = jnp.concatenate([-jnp.sin(ang_r), jnp.sin(ang_r), -jnp.sin(ang_c), jnp.sin(ang_c)], axis=-1)
    return cos, sins


def _trunk(x, mod3, mod_row_fn, B, L, s0_ret, s0_hg, rope, out_state, wts, tag):
    (norm_w, final_w, ffn1_w13, ffn1_w2, ffn2_w13, ffn2_w2, w_in, dec, lb_logits, hg_nw,
     w_ret, w_hg, w_o, layer) = wts
    x = _ffn_call(x, mod3, norm_w, final_w, ffn1_w13, ffn1_w2, mod_base=0, norm_row=0, final=False,
                  mod_row_fn=mod_row_fn, tm=512, name="ffn1_" + tag)
    qk, rv, rg, hq, g, hi, hog, gr, gh = _proj_call(x, mod3, norm_w, lb_logits, w_in, layer=layer,
                                                    mod_row_fn=mod_row_fn, tm=256, name="proj_" + tag)
    r3 = lambda t: t.reshape(B, L, t.shape[-1])
    cos = sins = None
    if rope:
        cos, sins = _rope_tables(L)
    ret = _ret_call(dec, r3(qk), r3(rv), r3(rg), cos, sins, s0_ret, B=B, L=L, rope=rope,
                    out_state=out_state, name="ret_" + tag)
    hg = _hgrn_call(r3(hq), r3(g), r3(hi), r3(hog), hg_nw, s0_hg, B=B, L=L, out_state=out_state,
                    name="hgrn_" + tag)
    x = _merge_call(x, mod3, ret[0].reshape(B * L, -1), hg[0].reshape(B * L, -1), gr, gh, w_ret, w_hg, w_o,
                    mod_row_fn=mod_row_fn, tm=512, name="merge_" + tag)
    y = _ffn_call(x, mod3, norm_w, final_w, ffn2_w13, ffn2_w2, mod_base=6, norm_row=2, final=True,
                  mod_row_fn=mod_row_fn, tm=512, name="ffn2_" + tag)
    if out_state:
        return y, ret[1], hg[1]
    return y, None, None


def kernel(x_prompt, x_sample, state_ret, state_hgrn, c, c_ctx, ada_w, ada_b, norm_w, ffn1_w13, ffn1_w2,
           ffn2_w13, ffn2_w2, w_in, ret_decay, hg_lb_logits, hg_norm_w, w_ret_proj, w_hg_proj, w_o,
           final_norm_w):
    depth = ada_w.shape[0]
    assert depth == 1
    l = 0
    bp, lp, _ = x_prompt.shape
    bs, ls, _ = x_sample.shape

    cvec = jnp.concatenate([c_ctx[None, :], c, jnp.zeros((8 - 1 - bs, D_MODEL), F32)], axis=0)
    mod = _mod_call(cvec, ada_w[l], ada_b[l][None, :])
    mod3 = mod.reshape(8, N_MOD, D_MODEL)

    bf = lambda w: w.astype(BF16)
    wts = (norm_w[l], final_norm_w[None, :], bf(ffn1_w13[l]), bf(ffn1_w2[l]), bf(ffn2_w13[l]), bf(ffn2_w2[l]),
           bf(w_in[l]), ret_decay[l], hg_lb_logits, hg_norm_w[l][None, :],
           bf(w_ret_proj[l]), bf(w_hg_proj[l]), bf(w_o[l]), l)

    yp, s_ret, s_hg = _trunk(x_prompt.reshape(bp * lp, D_MODEL), mod3, lambda i, tm: 0, bp, lp, None, None,
                             False, True, wts, "ctx")
    ys, _, _ = _trunk(x_sample.reshape(bs * ls, D_MODEL), mod3, lambda i, tm: 1 + (i * tm) // ls, bs, ls,
                      state_ret[:, l], state_hgrn[:, l], True, False, wts, "lat")
    return (yp.reshape(bp, lp, D_MODEL), ys.reshape(bs, ls, D_MODEL),
            s_ret[:, None], s_hg[:, None])
```

```python
import functools

import numpy as np
import jax
import jax.numpy as jnp
from jax import lax
from jax.experimental import pallas as pl
from jax.experimental.pallas import tpu as pltpu

F32 = jnp.float32
BF16 = jnp.bfloat16

D_MODEL = 1024
GRID_W = 64
RET_HEADS = 4
RET_DK = 128
RET_DV = 256
HG_HEADS = 8
HG_DK = 128
HG_DV = 128
D_FF = 2816
N_MOD = 9
ROPE_BASE = 10000.0
EPS = 1e-6

C_RQ, C_RK, C_RV, C_RG, C_HQ, C_HFF, C_HFB, C_HI, C_HOG, C_GR, C_GH = (
    0, 512, 1024, 2048, 3072, 4096, 5120, 6144, 7168, 8192, 9216)
IN_WIDTH = 10240

VMEM_LIMIT_BYTES = 56 * 1024 * 1024

SCAN_BLOCK = 128
HG_PACK = 2
FF_TILE = 1408
LOG2E = 1.4426950408889634
EXP2_CLAMP = 115.0


def _dot(a, b):
    return jnp.dot(a, b, preferred_element_type=F32)


def _dot_nt(a, b):
    return lax.dot_general(a, b, (((1,), (1,)), ((), ())), preferred_element_type=F32)


def _dot_tn(a, b):
    return lax.dot_general(a, b, (((0,), (0,)), ((), ())), preferred_element_type=F32)


def _sigmoid(x):
    return 1.0 / (1.0 + jnp.exp(-x))


def _rms(x, w):
    ms = jnp.mean(x * x, axis=-1, keepdims=True)
    return x * lax.rsqrt(ms + EPS) * w


def _params(sem):
    return pltpu.CompilerParams(dimension_semantics=sem, vmem_limit_bytes=VMEM_LIMIT_BYTES)


def _mod_kernel(c_ref, w_ref, b_ref, o_ref):
    c = c_ref[...]
    s = (c * _sigmoid(c)).astype(BF16)
    o_ref[...] = _dot(s, w_ref[...].astype(BF16)) + b_ref[...]


def _mod_call(cvec, ada_w, ada_b):
    n = ada_w.shape[1]
    tn = 1024
    return pl.pallas_call(
        _mod_kernel,
        grid=(n // tn,),
        in_specs=[
            pl.BlockSpec((8, D_MODEL), lambda j: (0, 0)),
            pl.BlockSpec((D_MODEL, tn), lambda j: (0, j)),
            pl.BlockSpec((1, tn), lambda j: (0, j)),
        ],
        out_specs=pl.BlockSpec((8, tn), lambda j: (0, j)),
        out_shape=jax.ShapeDtypeStruct((8, n), F32),
        compiler_params=_params(("arbitrary",)),
        name="mod",
    )(cvec, ada_w, ada_b)


def _ffn_kernel(x_ref, mod_ref, nw_ref, fw_ref, wa_ref, wb_ref, w2_ref, o_ref, h_scr, acc_scr,
                *, mod_base, norm_row, final, n_ff):
    j = pl.program_id(1)

    @pl.when(j == 0)
    def _():
        y = _rms(x_ref[...], nw_ref[norm_row:norm_row + 1, :])
        sh = mod_ref[mod_base:mod_base + 1, :]
        sc = mod_ref[mod_base + 1:mod_base + 2, :]
        h_scr[...] = (y * (1.0 + sc) + sh).astype(BF16)

    h = h_scr[...]
    a = _dot(h, wa_ref[...])
    b = _dot(h, wb_ref[...])
    act = (a * _sigmoid(a) * b).astype(BF16)
    part = _dot(act, w2_ref[...])

    @pl.when(j == 0)
    def _():
        acc_scr[...] = part

    @pl.when(jnp.logical_and(j > 0, j < n_ff - 1))
    def _():
        acc_scr[...] += part

    @pl.when(j == n_ff - 1)
    def _():
        tot = acc_scr[...] + part
        g = mod_ref[mod_base + 2:mod_base + 3, :]
        xo = x_ref[...] + 0.5 * g * tot
        if final:
            xo = _rms(xo, fw_ref[...])
        o_ref[...] = xo


def _ffn_call(x, mod3, norm_w, final_w, w13, w2, *, mod_base, norm_row, final, mod_row_fn, tm, name):
    n = x.shape[0]
    n_ff = D_FF // FF_TILE
    kern = functools.partial(_ffn_kernel, mod_base=mod_base, norm_row=norm_row, final=final, n_ff=n_ff)
    return pl.pallas_call(
        kern,
        grid=(n // tm, n_ff),
        in_specs=[
            pl.BlockSpec((tm, D_MODEL), lambda i, j: (i, 0)),
            pl.BlockSpec((None, N_MOD, D_MODEL), lambda i, j: (mod_row_fn(i, tm), 0, 0)),
            pl.BlockSpec((3, D_MODEL), lambda i, j: (0, 0)),
            pl.BlockSpec((1, D_MODEL), lambda i, j: (0, 0)),
            pl.BlockSpec((D_MODEL, FF_TILE), lambda i, j: (0, j)),
            pl.BlockSpec((D_MODEL, FF_TILE), lambda i, j: (0, n_ff + j)),
            pl.BlockSpec((FF_TILE, D_MODEL), lambda i, j: (j, 0)),
        ],
        out_specs=pl.BlockSpec((tm, D_MODEL), lambda i, j: (i, 0)),
        out_shape=jax.ShapeDtypeStruct((n, D_MODEL), F32),
        scratch_shapes=[pltpu.VMEM((tm, D_MODEL), BF16), pltpu.VMEM((tm, D_MODEL), F32)],
        compiler_params=_params(("parallel", "arbitrary")),
        name=name,
    )(x, mod3, norm_w, final_w, w13, w13, w2)


def _proj_kernel(x_ref, mod_ref, nw_ref, lbl_ref, w_ref,
                 qk_ref, rv_ref, rg_ref, hq_ref, g_ref, hi_ref, hog_ref, gr_ref, gh_ref, *, layer):
    y = _rms(x_ref[...], nw_ref[1:2, :])
    h = (y * (1.0 + mod_ref[4:5, :]) + mod_ref[3:4, :]).astype(BF16)

    def mm(c0, width):
        return _dot(h, w_ref[:, c0:c0 + width])

    qk_ref[:, 0:512] = mm(C_RQ, 512)
    qk_ref[:, 512:1024] = mm(C_RK, 512) * (RET_DK ** -0.5)
    rv_ref[...] = mm(C_RV, 1024).astype(BF16)
    t = mm(C_RG, 1024)
    rg_ref[...] = t * _sigmoid(t)
    t = mm(C_HQ, 1024)
    hq_ref[...] = t * _sigmoid(t) * (HG_DK ** -0.5)
    for d in range(2):
        lg = lbl_ref[d]
        e = jnp.exp(lg - jnp.max(lg, axis=0, keepdims=True))
        lb = jnp.sum(e[0:layer + 1, :], axis=0, keepdims=True) / jnp.sum(e, axis=0, keepdims=True)
        t = mm(C_HFF + 1024 * d, 1024)
        f = lb + (1.0 - lb) * _sigmoid(t)
        g_ref[:, 1024 * d:1024 * (d + 1)] = jnp.log(f)
    hi_ref[...] = mm(C_HI, 1024).astype(BF16)
    t = mm(C_HOG, 1024)
    hog_ref[...] = t * _sigmoid(t)
    gr_ref[...] = _sigmoid(mm(C_GR, 1024))
    gh_ref[...] = _sigmoid(mm(C_GH, 1024))


def _proj_call(x, mod3, norm_w, lb_logits, w_in, *, layer, mod_row_fn, tm, name):
    n = x.shape[0]
    slots = lb_logits.shape[1]
    row = lambda w: pl.BlockSpec((tm, w), lambda i: (i, 0))
    outs = [(1024, F32), (1024, BF16), (1024, F32), (1024, F32), (2048, F32),
            (1024, BF16), (1024, F32), (1024, F32), (1024, F32)]
    return pl.pallas_call(
        functools.partial(_proj_kernel, layer=layer),
        grid=(n // tm,),
        in_specs=[
            row(D_MODEL),
            pl.BlockSpec((None, N_MOD, D_MODEL), lambda i: (mod_row_fn(i, tm), 0, 0)),
            pl.BlockSpec((3, D_MODEL), lambda i: (0, 0)),
            pl.BlockSpec((2, slots, 1024), lambda i: (0, 0, 0)),
            pl.BlockSpec((D_MODEL, IN_WIDTH), lambda i: (0, 0), pipeline_mode=pl.Buffered(1)),
        ],
        out_specs=[row(w) for w, _ in outs],
        out_shape=[jax.ShapeDtypeStruct((n, w), dt) for w, dt in outs],
        compiler_params=_params(("parallel",)),
        name=name,
    )(x, mod3, norm_w, lb_logits, w_in)


def _log_sigmoid(x):
    return -(jnp.maximum(-x, 0.0) + jnp.log1p(jnp.exp(-jnp.abs(x))))


def _rope(x, cos, sins, first_quarter):
    partner = jnp.where(first_quarter, pltpu.roll(x, 96, 1), pltpu.roll(x, 32, 1))
    return x * cos + partner * sins


def _ret_kernel(*refs, L, rope, has_s0, out_state):
    it = iter(refs)
    dec_ref = next(it)
    q_ref, k_ref, v_ref, gate_ref = next(it), next(it), next(it), next(it)
    cos_ref = sin_ref = s0_ref = st_ref = None
    if rope:
        cos_ref, sin_ref = next(it), next(it)
    if has_s0:
        s0_ref = next(it)
    o_ref = next(it)
    if out_state:
        st_ref = next(it)
    oacc, sf_scr, sb_scr = next(it), next(it), next(it)

    C = SCAN_BLOCK
    n = L // C
    hd = pl.program_id(1)
    ii = lax.broadcasted_iota(jnp.int32, (C, C), 0).astype(F32)
    jj = lax.broadcasted_iota(jnp.int32, (C, C), 1).astype(F32)
    lgf = _log_sigmoid(jnp.full((C, C), dec_ref[0, hd], F32))
    lgb = _log_sigmoid(jnp.full((C, C), dec_ref[1, hd], F32))
    diff = ii - jj
    dmat = (jnp.where(diff >= 0, jnp.exp(jnp.maximum(diff, 0.0) * lgf), 0.0)
            + jnp.where(diff <= 0, jnp.exp(jnp.maximum(-diff, 0.0) * lgb), 0.0))
    qdf = jnp.exp((ii + 1.0) * lgf)
    kdf = jnp.exp((C - 1.0 - ii) * lgf)
    qdb = jnp.exp((C - ii) * lgb)
    kdb = jnp.exp(ii * lgb)
    cdf = jnp.exp(C * _log_sigmoid(jnp.full((RET_DK, RET_DV), dec_ref[0, hd], F32)))
    cdb = jnp.exp(C * _log_sigmoid(jnp.full((RET_DK, RET_DV), dec_ref[1, hd], F32)))
    lane = lax.broadcasted_iota(jnp.int32, (C, RET_DK), 1)
    first_quarter = (lane & 32) == 0

    if has_s0:
        sf_scr[...] = s0_ref[0]
        sb_scr[...] = s0_ref[1]
    else:
        sf_scr[...] = jnp.zeros((RET_DK, RET_DV), F32)
        sb_scr[...] = jnp.zeros((RET_DK, RET_DV), F32)

    def load_qk(sl):
        q = q_ref[sl, :]
        k = k_ref[sl, :]
        if rope:
            cos = cos_ref[sl, :]
            sins = sin_ref[sl, :]
            q = _rope(q, cos, sins, first_quarter)
            k = _rope(k, cos, sins, first_quarter)
        return q, k

    def fwd(c, carry):
        sl = pl.ds(pl.multiple_of(c * C, C), C)
        q, k = load_qk(sl)
        v = v_ref[sl, :]
        s = _dot_nt(q.astype(BF16), k.astype(BF16))
        p = (s * dmat).astype(BF16)
        st = sf_scr[...]
        oacc[sl, :] = _dot(p, v) + _dot((q * qdf).astype(BF16), st.astype(BF16))
        sf_scr[...] = cdf * st + _dot_tn((k * kdf).astype(BF16), v)
        return carry

    def bwd(t, carry):
        c = n - 1 - t
        sl = pl.ds(pl.multiple_of(c * C, C), C)
        q, k = load_qk(sl)
        v = v_ref[sl, :]
        st = sb_scr[...]
        o = oacc[sl, :] + _dot((q * qdb).astype(BF16), st.astype(BF16))
        sb_scr[...] = cdb * st + _dot_tn((k * kdb).astype(BF16), v)
        ms = jnp.mean(o * o, axis=-1, keepdims=True)
        o = o * lax.rsqrt(ms + EPS)
        o_ref[sl, :] = (o * gate_ref[sl, :]).astype(BF16)
        return carry

    lax.fori_loop(0, n, fwd, 0, unroll=2)
    lax.fori_loop(0, n, bwd, 0, unroll=2)
    if out_state:
        st_ref[0] = sf_scr[...]
        st_ref[1] = sb_scr[...]


def _ret_call(dec, qk, rv, rg, cos, sins, s0, *, B, L, rope, out_state, name):
    has_s0 = s0 is not None
    H = RET_HEADS
    in_specs = [
        pl.BlockSpec(memory_space=pltpu.SMEM),
        pl.BlockSpec((None, L, RET_DK), lambda b, h: (b, 0, h)),
        pl.BlockSpec((None, L, RET_DK), lambda b, h: (b, 0, H + h)),
        pl.BlockSpec((None, L, RET_DV), lambda b, h: (b, 0, h)),
        pl.BlockSpec((None, L, RET_DV), lambda b, h: (b, 0, h)),
    ]
    args = [dec, qk, qk, rv, rg]
    if rope:
        in_specs += [pl.BlockSpec((L, RET_DK), lambda b, h: (0, 0))] * 2
        args += [cos, sins]
    if has_s0:
        in_specs.append(pl.BlockSpec((None, 2, None, RET_DK, RET_DV), lambda b, h: (b, 0, h, 0, 0)))
        args.append(s0)
    out_specs = [pl.BlockSpec((None, L, RET_DV), lambda b, h: (b, 0, h))]
    out_shape = [jax.ShapeDtypeStruct((B, L, H * RET_DV), BF16)]
    if out_state:
        out_specs.append(pl.BlockSpec((None, 2, None, RET_DK, RET_DV), lambda b, h: (b, 0, h, 0, 0)))
        out_shape.append(jax.ShapeDtypeStruct((B, 2, H, RET_DK, RET_DV), F32))
    res = pl.pallas_call(
        functools.partial(_ret_kernel, L=L, rope=rope, has_s0=has_s0, out_state=out_state),
        grid=(B, H),
        in_specs=in_specs,
        out_specs=out_specs,
        out_shape=out_shape,
        scratch_shapes=[pltpu.VMEM((L, RET_DV), F32), pltpu.VMEM((RET_DK, RET_DV), F32),
                        pltpu.VMEM((RET_DK, RET_DV), F32)],
        compiler_params=_params(("parallel", "parallel")),
        name=name,
    )(*args)
    return res


def _gla_blocks(chains, tri_ref, code_ref):
    T = SCAN_BLOCK
    K = HG_DK
    for c in chains:
        g2 = c["g"] * LOG2E
        c["k"] = 1.0 - jnp.exp2(g2)
        hi = g2.astype(BF16)
        lo = (g2 - hi.astype(F32)).astype(BF16)
        tri = tri_ref[int(c["rev"])]
        c["b"] = _dot(tri, hi) + _dot(tri, lo)

    for c in chains:
        b8 = c["b"].reshape(T // 8, 8, K)
        r = 4 if c["rev"] else 3
        d = jnp.clip(b8 - b8[:, r:r + 1, :], -EXP2_CLAMP, EXP2_CLAMP).reshape(T, K)
        e = jnp.exp2(d)
        p = _dot_nt((c["q"] * e).astype(BF16), (c["k"] / e).astype(BF16))
        c["a"] = jnp.where(code_ref[int(c["rev"])] == 0, p, 0.0)

    for lv, m in ((3, 8), (4, 16), (5, 32), (6, 64)):
        seg = 2 * m
        ng = T // seg
        for c in chains:
            b3, q3, k3 = (c[n].reshape(ng, seg, K) for n in ("b", "q", "k"))
            if c["rev"]:
                qs, ks, r = slice(0, m), slice(m, seg), m
            else:
                qs, ks, r = slice(m, seg), slice(0, m), m - 1
            ref = b3[:, r:r + 1, :]
            qm = (q3[:, qs, :] * jnp.exp2(b3[:, qs, :] - ref)).reshape(ng * m, K).astype(BF16)
            kz = k3[:, ks, :] * jnp.exp2(ref - b3[:, ks, :])
            zeros = jnp.zeros((ng, m, K), F32)
            kfull = jnp.concatenate([zeros, kz] if c["rev"] else [kz, zeros], axis=1)
            p = _dot_nt(qm, kfull.reshape(T, K).astype(BF16)).reshape(ng, m, T)
            a3 = c["a"].reshape(ng, seg, T)
            c3 = code_ref[int(c["rev"])].reshape(ng, seg, T)
            upd = jnp.where(c3[:, qs, :] == lv, p, a3[:, qs, :])
            parts = [upd, a3[:, ks, :]] if c["rev"] else [a3[:, ks, :], upd]
            c["a"] = jnp.concatenate(parts, axis=1).reshape(T, T)

    out = []
    for c in chains:
        b, q, k, v, st = c["b"], c["q"], c["k"], c["v"], c["st"]
        o = _dot(c["a"].astype(BF16), v) + _dot_nt((q * jnp.exp2(b)).astype(BF16), st.astype(BF16))
        end_row = 0 if c["rev"] else T - 1
        b1 = b.reshape(1, T, K)
        b_end = b1[:, end_row:end_row + 1, :]
        kd = (k * jnp.exp2(b_end - b1).reshape(T, K)).astype(BF16)
        out.append((o, st * jnp.exp2(b_end.reshape(1, K)) + _dot_tn(v, kd)))
    return out


def _hgrn_kernel(*refs, L, has_s0, out_state):
    it = iter(refs)
    q_ref, g_f_ref, g_b_ref, v_ref, gate_ref, nw_ref = (next(it) for _ in range(6))
    s0_ref = next(it) if has_s0 else None
    o_ref = next(it)
    st_ref = next(it) if out_state else None
    of_acc, ob_acc, st_scr, tri_ref, code_ref = (next(it) for _ in range(5))

    T = SCAN_BLOCK
    P = HG_PACK
    n = L // T
    ti = lax.broadcasted_iota(jnp.int32, (T, T), 0)
    si = lax.broadcasted_iota(jnp.int32, (T, T), 1)
    x = ti ^ si
    tri_ref[0] = (si <= ti).astype(BF16)
    tri_ref[1] = (si >= ti).astype(BF16)
    lvl = jnp.where(x < 8, 0, 3 + (x >= 16).astype(jnp.int32) + (x >= 32).astype(jnp.int32)
                    + (x >= 64).astype(jnp.int32))
    code_ref[0] = jnp.where(si <= ti, lvl, -1)
    code_ref[1] = jnp.where(si >= ti, lvl, -1)

    for h in range(P):
        for d in range(2):
            st_scr[2 * h + d] = s0_ref[d, h].T if has_s0 else jnp.zeros((HG_DV, HG_DK), F32)

    def step(i, carry):
        sls = (pl.ds(pl.multiple_of(i * T, T), T), pl.ds(pl.multiple_of((n - 1 - i) * T, T), T))
        chains = []
        for h in range(P):
            lanes = slice(h * HG_DK, (h + 1) * HG_DK)
            for d, g_ref in enumerate((g_f_ref, g_b_ref)):
                chains.append(dict(q=q_ref[sls[d], lanes], g=g_ref[sls[d], lanes], v=v_ref[sls[d], lanes],
                                   st=st_scr[2 * h + d], rev=bool(d)))
        res = _gla_blocks(chains, tri_ref, code_ref)
        for h in range(P):
            lanes = slice(h * HG_DK, (h + 1) * HG_DK)
            for d, acc in enumerate((of_acc, ob_acc)):
                o, st = res[2 * h + d]
                acc[sls[d], lanes] = o
                st_scr[2 * h + d] = st
        return carry

    lax.fori_loop(0, n, step, 0, unroll=2)

    def fin(i, carry):
        sl = pl.ds(pl.multiple_of(i * T, T), T)
        for h in range(P):
            lanes = slice(h * HG_DK, (h + 1) * HG_DK)
            o = of_acc[sl, lanes] + ob_acc[sl, lanes]
            ms = jnp.mean(o * o, axis=-1, keepdims=True)
            o = o * lax.rsqrt(ms + EPS) * nw_ref[:, lanes]
            o_ref[sl, lanes] = (o * gate_ref[sl, lanes]).astype(BF16)
        return carry

    lax.fori_loop(0, n, fin, 0)
    if out_state:
        for h in range(P):
            for d in range(2):
                st_ref[d, h] = st_scr[2 * h + d].T


def _hgrn_call(hq, g, hi, hog, nw, s0, *, B, L, out_state, name):
    has_s0 = s0 is not None
    H = HG_HEADS
    P = HG_PACK
    W = P * HG_DK
    blk = lambda off: pl.BlockSpec((None, L, W), lambda b, h: (b, 0, off + h))
    st_spec = pl.BlockSpec((None, 2, P, HG_DK, HG_DV), lambda b, h: (b, 0, h, 0, 0))
    in_specs = [blk(0), blk(0), blk(H // P), blk(0), blk(0), pl.BlockSpec((1, W), lambda b, h: (0, h))]
    args = [hq, g, g, hi, hog, nw]
    if has_s0:
        in_specs.append(st_spec)
        args.append(s0)
    out_specs = [blk(0)]
    out_shape = [jax.ShapeDtypeStruct((B, L, H * HG_DV), BF16)]
    if out_state:
        out_specs.append(st_spec)
        out_shape.append(jax.ShapeDtypeStruct((B, 2, H, HG_DK, HG_DV), F32))
    return pl.pallas_call(
        functools.partial(_hgrn_kernel, L=L, has_s0=has_s0, out_state=out_state),
        grid=(B, H // P),
        in_specs=in_specs,
        out_specs=out_specs,
        out_shape=out_shape,
        scratch_shapes=[pltpu.VMEM((L, W), F32), pltpu.VMEM((L, W), F32),
                        pltpu.VMEM((2 * P, HG_DV, HG_DK), F32),
                        pltpu.VMEM((2, SCAN_BLOCK, SCAN_BLOCK), BF16),
                        pltpu.VMEM((2, SCAN_BLOCK, SCAN_BLOCK), jnp.int32)],
        compiler_params=_params(("parallel", "parallel")),
        name=name,
    )(*args)


def _merge_kernel(x_ref, mod_ref, oret_ref, ohg_ref, gr_ref, gh_ref, wr_ref, wh_ref, wo_ref, o_ref):
    y = gr_ref[...] * _dot(oret_ref[...], wr_ref[...]) + gh_ref[...] * _dot(ohg_ref[...], wh_ref[...])
    out = _dot(y.astype(BF16), wo_ref[...])
    o_ref[...] = x_ref[...] + mod_ref[5:6, :] * out


def _merge_call(x, mod3, oret, ohg, gr, gh, wr, wh, wo, *, mod_row_fn, tm, name):
    n = x.shape[0]
    row = pl.BlockSpec((tm, D_MODEL), lambda i: (i, 0))
    wspec = pl.BlockSpec((D_MODEL, D_MODEL), lambda i: (0, 0))
    return pl.pallas_call(
        _merge_kernel,
        grid=(n // tm,),
        in_specs=[row, pl.BlockSpec((None, N_MOD, D_MODEL), lambda i: (mod_row_fn(i, tm), 0, 0)),
                  row, row, row, row, wspec, wspec, wspec],
        out_specs=row,
        out_shape=jax.ShapeDtypeStruct((n, D_MODEL), F32),
        compiler_params=_params(("parallel",)),
        name=name,
    )(x, mod3, oret, ohg, gr, gh, wr, wh, wo)


def _rope_tables(n_tokens):
    pos = np.arange(n_tokens)
    r = (pos // GRID_W).astype(np.float32)
    cl = (pos % GRID_W).astype(np.float32)
    quarter = RET_DK // 4
    inv_freq = ROPE_BASE ** (-jnp.arange(quarter, dtype=F32) / quarter)
    ang_r = jnp.asarray(r)[:, None] * inv_freq[None, :]
    ang_c = jnp.asarray(cl)[:, None] * inv_freq[None, :]
    cos = jnp.concatenate([jnp.cos(ang_r)] * 2 + [jnp.cos(ang_c)] * 2, axis=-1)
    sins = jnp.concatenate([-jnp.sin(ang_r), jnp.sin(ang_r), -jnp.sin(ang_c), jnp.sin(ang_c)], axis=-1)
    return cos, sins


def _trunk(x, mod3, mod_row_fn, B, L, s0_ret, s0_hg, rope, out_state, wts, tag):
    (norm_w, final_w, ffn1_w13, ffn1_w2, ffn2_w13, ffn2_w2, w_in, dec, lb_logits, hg_nw,
     w_ret, w_hg, w_o, layer) = wts
    x = _ffn_call(x, mod3, norm_w, final_w, ffn1_w13, ffn1_w2, mod_base=0, norm_row=0, final=False,
                  mod_row_fn=mod_row_fn, tm=512, name="ffn1_" + tag)
    qk, rv, rg, hq, g, hi, hog, gr, gh = _proj_call(x, mod3, norm_w, lb_logits, w_in, layer=layer,
                                                    mod_row_fn=mod_row_fn, tm=256, name="proj_" + tag)
    r3 = lambda t: t.reshape(B, L, t.shape[-1])
    cos = sins = None
    if rope:
        cos, sins = _rope_tables(L)
    ret = _ret_call(dec, r3(qk), r3(rv), r3(rg), cos, sins, s0_ret, B=B, L=L, rope=rope,
                    out_state=out_state, name="ret_" + tag)
    hg = _hgrn_call(r3(hq), r3(g), r3(hi), r3(hog), hg_nw, s0_hg, B=B, L=L, out_state=out_state,
                    name="hgrn_" + tag)
    x = _merge_call(x, mod3, ret[0].reshape(B * L, -1), hg[0].reshape(B * L, -1), gr, gh, w_ret, w_hg, w_o,
                    mod_row_fn=mod_row_fn, tm=512, name="merge_" + tag)
    y = _ffn_call(x, mod3, norm_w, final_w, ffn2_w13, ffn2_w2, mod_base=6, norm_row=2, final=True,
                  mod_row_fn=mod_row_fn, tm=512, name="ffn2_" + tag)
    if out_state:
        return y, ret[1], hg[1]
    return y, None, None


def kernel(x_prompt, x_sample, state_ret, state_hgrn, c, c_ctx, ada_w, ada_b, norm_w, ffn1_w13, ffn1_w2,
           ffn2_w13, ffn2_w2, w_in, ret_decay, hg_lb_logits, hg_norm_w, w_ret_proj, w_hg_proj, w_o,
           final_norm_w):
    depth = ada_w.shape[0]
    assert depth == 1
    l = 0
    bp, lp, _ = x_prompt.shape
    bs, ls, _ = x_sample.shape

    cvec = jnp.concatenate([c_ctx[None, :], c, jnp.zeros((8 - 1 - bs, D_MODEL), F32)], axis=0)
    mod = _mod_call(cvec, ada_w[l], ada_b[l][None, :])
    mod3 = mod.reshape(8, N_MOD, D_MODEL)

    bf = lambda w: w.astype(BF16)
    wts = (norm_w[l], final_norm_w[None, :], bf(ffn1_w13[l]), bf(ffn1_w2[l]), bf(ffn2_w13[l]), bf(ffn2_w2[l]),
           bf(w_in[l]), ret_decay[l], hg_lb_logits, hg_norm_w[l][None, :],
           bf(w_ret_proj[l]), bf(w_hg_proj[l]), bf(w_o[l]), l)

    yp, s_ret, s_hg = _trunk(x_prompt.reshape(bp * lp, D_MODEL), mod3, lambda i, tm: 0, bp, lp, None, None,
                             False, True, wts, "ctx")
    ys, _, _ = _trunk(x_sample.reshape(bs * ls, D_MODEL), mod3, lambda i, tm: 1 + (i * tm) // ls, bs, ls,
                      state_ret[:, l], state_hgrn[:, l], True, False, wts, "lat")
    return (yp.reshape(bp, lp, D_MODEL), ys.reshape(bs, ls, D_MODEL),
            s_ret[:, None], s_hg[:, None])
```

```python
import functools

import numpy as np
import jax
import jax.numpy as jnp
from jax import lax
from jax.experimental import pallas as pl
from jax.experimental.pallas import tpu as pltpu

F32 = jnp.float32
BF16 = jnp.bfloat16

D_MODEL = 1024
GRID_W = 64
RET_HEADS = 4
RET_DK = 128
RET_DV = 256
HG_HEADS = 8
HG_DK = 128
HG_DV = 128
D_FF = 2816
N_MOD = 9
ROPE_BASE = 10000.0
EPS = 1e-6

C_RQ, C_RK, C_RV, C_RG, C_HQ, C_HFF, C_HFB, C_HI, C_HOG, C_GR, C_GH = (
    0, 512, 1024, 2048, 3072, 4096, 5120, 6144, 7168, 8192, 9216)
IN_WIDTH = 10240

VMEM_LIMIT_BYTES = 56 * 1024 * 1024

SCAN_BLOCK = 128
RET_PACK = 2
HG_PACK = 2
FF_CHUNK = 256
LOG2E = 1.4426950408889634
EXP2_CLAMP = 115.0


def _dot(a, b):
    return jnp.dot(a, b, preferred_element_type=F32)


def _dot_nt(a, b):
    return lax.dot_general(a, b, (((1,), (1,)), ((), ())), preferred_element_type=F32)


def _dot_tn(a, b):
    return lax.dot_general(a, b, (((0,), (0,)), ((), ())), preferred_element_type=F32)


def _sigmoid(x):
    return 1.0 / (1.0 + jnp.exp(-x))


def _rms(x, w):
    ms = jnp.mean(x * x, axis=-1, keepdims=True)
    return x * lax.rsqrt(ms + EPS) * w


def _params(sem):
    return pltpu.CompilerParams(dimension_semantics=sem, vmem_limit_bytes=VMEM_LIMIT_BYTES)


def _mod_kernel(c_ref, w_ref, b_ref, o_ref):
    c = c_ref[...]
    s = (c * _sigmoid(c)).astype(BF16)
    o_ref[...] = _dot(s, w_ref[...].astype(BF16)) + b_ref[...]


def _mod_call(cvec, ada_w, ada_b):
    n = ada_w.shape[1]
    tn = 1024
    return pl.pallas_call(
        _mod_kernel,
        grid=(n // tn,),
        in_specs=[
            pl.BlockSpec((8, D_MODEL), lambda j: (0, 0)),
            pl.BlockSpec((D_MODEL, tn), lambda j: (0, j)),
            pl.BlockSpec((1, tn), lambda j: (0, j)),
        ],
        out_specs=pl.BlockSpec((8, tn), lambda j: (0, j)),
        out_shape=jax.ShapeDtypeStruct((8, n), F32),
        compiler_params=_params(("arbitrary",)),
        name="mod",
    )(cvec, ada_w, ada_b)


def _ffn_kernel(x_ref, mod_ref, nw_ref, fw_ref, w13_ref, w2_ref, o_ref, *, mod_base, norm_row, final):
    y = _rms(x_ref[...], nw_ref[norm_row:norm_row + 1, :])
    sh = mod_ref[mod_base:mod_base + 1, :]
    sc = mod_ref[mod_base + 1:mod_base + 2, :]
    h = (y * (1.0 + sc) + sh).astype(BF16)
    acc = None
    for c0 in range(0, D_FF, FF_CHUNK):
        a = _dot(h, w13_ref[:, c0:c0 + FF_CHUNK])
        b = _dot(h, w13_ref[:, D_FF + c0:D_FF + c0 + FF_CHUNK])
        act = (a * _sigmoid(a) * b).astype(BF16)
        part = _dot(act, w2_ref[c0:c0 + FF_CHUNK, :])
        acc = part if acc is None else acc + part
    g = mod_ref[mod_base + 2:mod_base + 3, :]
    xo = x_ref[...] + 0.5 * g * acc
    if final:
        xo = _rms(xo, fw_ref[...])
    o_ref[...] = xo


def _ffn_call(x, mod3, norm_w, final_w, w13, w2, *, mod_base, norm_row, final, mod_row_fn, tm, name):
    n = x.shape[0]
    kern = functools.partial(_ffn_kernel, mod_base=mod_base, norm_row=norm_row, final=final)
    const = lambda shape: pl.BlockSpec(shape, lambda i: (0, 0), pipeline_mode=pl.Buffered(1))
    return pl.pallas_call(
        kern,
        grid=(n // tm,),
        in_specs=[
            pl.BlockSpec((tm, D_MODEL), lambda i: (i, 0)),
            pl.BlockSpec((None, N_MOD, D_MODEL), lambda i: (mod_row_fn(i, tm), 0, 0)),
            pl.BlockSpec((3, D_MODEL), lambda i: (0, 0)),
            pl.BlockSpec((1, D_MODEL), lambda i: (0, 0)),
            const((D_MODEL, 2 * D_FF)),
            const((D_FF, D_MODEL)),
        ],
        out_specs=pl.BlockSpec((tm, D_MODEL), lambda i: (i, 0)),
        out_shape=jax.ShapeDtypeStruct((n, D_MODEL), F32),
        compiler_params=_params(("parallel",)),
        name=name,
    )(x, mod3, norm_w, final_w, w13, w2)


def _rope(x, cos, sins, first_quarter):
    partner = jnp.where(first_quarter, pltpu.roll(x, 96, 1), pltpu.roll(x, 32, 1))
    return x * cos + partner * sins


def _proj_kernel(*refs, layer, rope):
    it = iter(refs)
    x_ref, mod_ref, nw_ref, lbl_ref, w_ref = (next(it) for _ in range(5))
    cos_ref, sin_ref = (next(it), next(it)) if rope else (None, None)
    qk_ref, rv_ref, rg_ref, hq_ref, g_ref, hi_ref, hog_ref, gr_ref, gh_ref = (next(it) for _ in range(9))
    y = _rms(x_ref[...], nw_ref[1:2, :])
    h = (y * (1.0 + mod_ref[4:5, :]) + mod_ref[3:4, :]).astype(BF16)

    def mm(c0, width):
        return _dot(h, w_ref[:, c0:c0 + width])

    qk = (mm(C_RQ, 512), mm(C_RK, 512) * (RET_DK ** -0.5))
    if rope:
        lane = lax.broadcasted_iota(jnp.int32, (x_ref.shape[0], RET_DK), 1)
        first_quarter = (lane & 32) == 0
    for j, t in enumerate(qk):
        for hd in range(RET_HEADS):
            th = t[:, hd * RET_DK:(hd + 1) * RET_DK]
            if rope:
                th = _rope(th, cos_ref[...], sin_ref[...], first_quarter)
            qk_ref[:, 512 * j + hd * RET_DK:512 * j + (hd + 1) * RET_DK] = th.astype(BF16)
    rv_ref[...] = mm(C_RV, 1024).astype(BF16)
    t = mm(C_RG, 1024)
    rg_ref[...] = (t * _sigmoid(t)).astype(BF16)
    t = mm(C_HQ, 1024)
    hq_ref[...] = (t * _sigmoid(t) * (HG_DK ** -0.5)).astype(BF16)
    for d in range(2):
        lg = lbl_ref[d]
        e = jnp.exp(lg - jnp.max(lg, axis=0, keepdims=True))
        lb = jnp.sum(e[0:layer + 1, :], axis=0, keepdims=True) / jnp.sum(e, axis=0, keepdims=True)
        t = mm(C_HFF + 1024 * d, 1024)
        f = lb + (1.0 - lb) * _sigmoid(t)
        g_ref[:, 1024 * d:1024 * (d + 1)] = jnp.log(f)
    hi_ref[...] = mm(C_HI, 1024).astype(BF16)
    t = mm(C_HOG, 1024)
    hog_ref[...] = (t * _sigmoid(t)).astype(BF16)
    gr_ref[...] = _sigmoid(mm(C_GR, 1024)).astype(BF16)
    gh_ref[...] = _sigmoid(mm(C_GH, 1024)).astype(BF16)


def _proj_call(x, mod3, norm_w, lb_logits, w_in, rope_tabs, *, layer, mod_row_fn, tm, name):
    n = x.shape[0]
    slots = lb_logits.shape[1]
    row = lambda w: pl.BlockSpec((tm, w), lambda i: (i, 0))
    outs = [(1024, BF16), (1024, BF16), (1024, BF16), (1024, BF16), (2048, F32),
            (1024, BF16), (1024, BF16), (1024, BF16), (1024, BF16)]
    in_specs = [
        row(D_MODEL),
        pl.BlockSpec((None, N_MOD, D_MODEL), lambda i: (mod_row_fn(i, tm), 0, 0)),
        pl.BlockSpec((3, D_MODEL), lambda i: (0, 0)),
        pl.BlockSpec((2, slots, 1024), lambda i: (0, 0, 0)),
        pl.BlockSpec((D_MODEL, IN_WIDTH), lambda i: (0, 0), pipeline_mode=pl.Buffered(1)),
    ]
    args = [x, mod3, norm_w, lb_logits, w_in]
    if rope_tabs is not None:
        tiles_per_seq = rope_tabs[0].shape[0] // tm
        in_specs += [pl.BlockSpec((tm, RET_DK), lambda i: (i % tiles_per_seq, 0))] * 2
        args += list(rope_tabs)
    return pl.pallas_call(
        functools.partial(_proj_kernel, layer=layer, rope=rope_tabs is not None),
        grid=(n // tm,),
        in_specs=in_specs,
        out_specs=[row(w) for w, _ in outs],
        out_shape=[jax.ShapeDtypeStruct((n, w), dt) for w, dt in outs],
        compiler_params=_params(("parallel",)),
        name=name,
    )(*args)


def _log_sigmoid(x):
    return -(jnp.maximum(-x, 0.0) + jnp.log1p(jnp.exp(-jnp.abs(x))))


def _ret_kernel(*refs, L, has_s0, out_state):
    it = iter(refs)
    dec_ref = next(it)
    q_ref, k_ref, v_ref, gate_ref = next(it), next(it), next(it), next(it)
    s0_ref = next(it) if has_s0 else None
    o_ref = next(it)
    st_ref = next(it) if out_state else None
    of_acc, ob_acc, st_scr, cst = next(it), next(it), next(it), next(it)

    C = SCAN_BLOCK
    P = RET_PACK
    n = L // C
    pair = pl.program_id(1)
    ii = lax.broadcasted_iota(jnp.int32, (C, C), 0).astype(F32)
    jj = lax.broadcasted_iota(jnp.int32, (C, C), 1).astype(F32)
    diff = ii - jj
    cds = []
    for h in range(P):
        hd = pair * P + h
        lgf = _log_sigmoid(jnp.full((C, C), dec_ref[0, hd], F32))
        lgb = _log_sigmoid(jnp.full((C, C), dec_ref[1, hd], F32))
        cst[h, 0] = (jnp.where(diff >= 0, jnp.exp(jnp.maximum(diff, 0.0) * lgf), 0.0)
                     + jnp.where(diff <= 0, jnp.exp(jnp.maximum(-diff, 0.0) * lgb), 0.0))
        cst[h, 1] = jnp.exp((ii + 1.0) * lgf)
        cst[h, 2] = jnp.exp((C - 1.0 - ii) * lgf)
        cst[h, 3] = jnp.exp((C - ii) * lgb)
        cst[h, 4] = jnp.exp(ii * lgb)
        cds.append([jnp.exp(C * _log_sigmoid(jnp.full((1, RET_DV), dec_ref[d, hd], F32))) for d in range(2)])
        for d in range(2):
            st_scr[2 * h + d] = s0_ref[d, h] if has_s0 else jnp.zeros((RET_DK, RET_DV), F32)

    def step(i, finalize):
        sls = (pl.ds(pl.multiple_of(i * C, C), C), pl.ds(pl.multiple_of((n - 1 - i) * C, C), C))
        chains = []
        for h in range(P):
            ql = slice(h * RET_DK, (h + 1) * RET_DK)
            vl = slice(h * RET_DV, (h + 1) * RET_DV)
            for d in range(2):
                chains.append(dict(h=h, d=d, vl=vl, q=q_ref[sls[d], ql], k=k_ref[sls[d], ql],
                                   v=v_ref[sls[d], vl]))
        for c in chains:
            if c["d"] == 0:
                c["s"] = _dot_nt(c["q"], c["k"])
        for c in chains:
            h, d, v, vl = c["h"], c["d"], c["v"], c["vl"]
            st = st_scr[2 * h + d]
            qd = (c["q"].astype(F32) * cst[h, 1 + 2 * d]).astype(BF16)
            if d == 0:
                p = (c["s"] * cst[h, 0]).astype(BF16)
                o = _dot(jnp.concatenate([p, qd], axis=1), jnp.concatenate([v, st.astype(BF16)], axis=0))
            else:
                o = _dot(qd, st.astype(BF16))
            kd = (c["k"].astype(F32) * cst[h, 2 + 2 * d]).astype(BF16)
            st_scr[2 * h + d] = cds[h][d] * st + _dot_tn(kd, v)
            if finalize:
                o = o + (ob_acc, of_acc)[d][sls[d], vl]
                ms = jnp.mean(o * o, axis=-1, keepdims=True)
                o_ref[sls[d], vl] = (o * lax.rsqrt(ms + EPS) * gate_ref[sls[d], vl]).astype(BF16)
            else:
                (of_acc, ob_acc)[d][sls[d], vl] = o

    def first_half(i, carry):
        step(i, False)
        return carry

    def second_half(i, carry):
        step(i, True)
        return carry

    lax.fori_loop(0, n // 2, first_half, 0, unroll=min(2, n // 2))
    lax.fori_loop(n // 2, n, second_half, 0, unroll=min(2, n // 2))
    if out_state:
        for h in range(P):
            for d in range(2):
                st_ref[d, h] = st_scr[2 * h + d]


def _ret_call(dec, qk, rv, rg, s0, *, B, L, out_state, name):
    has_s0 = s0 is not None
    H = RET_HEADS
    P = RET_PACK
    st_spec = pl.BlockSpec((None, 2, P, RET_DK, RET_DV), lambda b, h: (b, 0, h, 0, 0))
    in_specs = [
        pl.BlockSpec(memory_space=pltpu.SMEM),
        pl.BlockSpec((None, L, P * RET_DK), lambda b, h: (b, 0, h)),
        pl.BlockSpec((None, L, P * RET_DK), lambda b, h: (b, 0, H // P + h)),
        pl.BlockSpec((None, L, P * RET_DV), lambda b, h: (b, 0, h)),
        pl.BlockSpec((None, L, P * RET_DV), lambda b, h: (b, 0, h)),
    ]
    args = [dec, qk, qk, rv, rg]
    if has_s0:
        in_specs.append(st_spec)
        args.append(s0)
    out_specs = [pl.BlockSpec((None, L, P * RET_DV), lambda b, h: (b, 0, h))]
    out_shape = [jax.ShapeDtypeStruct((B, L, H * RET_DV), BF16)]
    if out_state:
        out_specs.append(st_spec)
        out_shape.append(jax.ShapeDtypeStruct((B, 2, H, RET_DK, RET_DV), F32))
    res = pl.pallas_call(
        functools.partial(_ret_kernel, L=L, has_s0=has_s0, out_state=out_state),
        grid=(B, H // P),
        in_specs=in_specs,
        out_specs=out_specs,
        out_shape=out_shape,
        scratch_shapes=[pltpu.VMEM((L, P * RET_DV), F32), pltpu.VMEM((L, P * RET_DV), F32),
                        pltpu.VMEM((2 * P, RET_DK, RET_DV), F32),
                        pltpu.VMEM((P, 5, SCAN_BLOCK, SCAN_BLOCK), F32)],
        compiler_params=_params(("parallel", "parallel")),
        name=name,
    )(*args)
    return res


def _gla_blocks(chains, tri_ref, code_ref):
    T = SCAN_BLOCK
    K = HG_DK
    for c in chains:
        g2 = c["g"] * LOG2E
        c["k"] = 1.0 - jnp.exp2(g2)
        hi = g2.astype(BF16)
        lo = (g2 - hi.astype(F32)).astype(BF16)
        tri = tri_ref[int(c["rev"])]
        c["b"] = _dot(tri, hi) + _dot(tri, lo)

    for c in chains:
        b8 = c["b"].reshape(T // 8, 8, K)
        r = 4 if c["rev"] else 3
        d = jnp.clip(b8 - b8[:, r:r + 1, :], -EXP2_CLAMP, EXP2_CLAMP).reshape(T, K)
        e = jnp.exp2(d)
        p = _dot_nt((c["q"] * e).astype(BF16), (c["k"] / e).astype(BF16))
        c["a"] = jnp.where(code_ref[int(c["rev"])] == 0, p, 0.0)

    for lv, m in ((3, 8), (4, 16), (5, 32), (6, 64)):
        seg = 2 * m
        ng = T // seg
        for c in chains:
            b3, q3, k3 = (c[n].reshape(ng, seg, K) for n in ("b", "q", "k"))
            if c["rev"]:
                qs, ks, r = slice(0, m), slice(m, seg), m
            else:
                qs, ks, r = slice(m, seg), slice(0, m), m - 1
            ref = b3[:, r:r + 1, :]
            qm = (q3[:, qs, :] * jnp.exp2(b3[:, qs, :] - ref)).reshape(ng * m, K).astype(BF16)
            kz = k3[:, ks, :] * jnp.exp2(ref - b3[:, ks, :])
            zeros = jnp.zeros((ng, m, K), F32)
            kfull = jnp.concatenate([zeros, kz] if c["rev"] else [kz, zeros], axis=1)
            p = _dot_nt(qm, kfull.reshape(T, K).astype(BF16)).reshape(ng, m, T)
            a3 = c["a"].reshape(ng, seg, T)
            c3 = code_ref[int(c["rev"])].reshape(ng, seg, T)
            upd = jnp.where(c3[:, qs, :] == lv, p, a3[:, qs, :])
            parts = [upd, a3[:, ks, :]] if c["rev"] else [a3[:, ks, :], upd]
            c["a"] = jnp.concatenate(parts, axis=1).reshape(T, T)

    out = []
    for c in chains:
        b, q, k, v, st = c["b"], c["q"], c["k"], c["v"], c["st"]
        o = _dot(c["a"].astype(BF16), v) + _dot_nt((q * jnp.exp2(b)).astype(BF16), st.astype(BF16))
        end_row = 0 if c["rev"] else T - 1
        b1 = b.reshape(1, T, K)
        b_end = b1[:, end_row:end_row + 1, :]
        kd = (k * jnp.exp2(b_end - b1).reshape(T, K)).astype(BF16)
        out.append((o, st * jnp.exp2(b_end.reshape(1, K)) + _dot_tn(v, kd)))
    return out


def _hgrn_kernel(*refs, L, has_s0, out_state):
    it = iter(refs)
    q_ref, g_f_ref, g_b_ref, v_ref, gate_ref, nw_ref = (next(it) for _ in range(6))
    s0_ref = next(it) if has_s0 else None
    o_ref = next(it)
    st_ref = next(it) if out_state else None
    of_acc, ob_acc, st_scr, tri_ref, code_ref = (next(it) for _ in range(5))

    T = SCAN_BLOCK
    P = HG_PACK
    n = L // T
    ti = lax.broadcasted_iota(jnp.int32, (T, T), 0)
    si = lax.broadcasted_iota(jnp.int32, (T, T), 1)
    x = ti ^ si
    tri_ref[0] = (si <= ti).astype(BF16)
    tri_ref[1] = (si >= ti).astype(BF16)
    lvl = jnp.where(x < 8, 0, 3 + (x >= 16).astype(jnp.int32) + (x >= 32).astype(jnp.int32)
                    + (x >= 64).astype(jnp.int32))
    code_ref[0] = jnp.where(si <= ti, lvl, -1)
    code_ref[1] = jnp.where(si >= ti, lvl, -1)

    for h in range(P):
        for d in range(2):
            st_scr[2 * h + d] = s0_ref[d, h].T if has_s0 else jnp.zeros((HG_DV, HG_DK), F32)

    def step(i, finalize):
        sls = (pl.ds(pl.multiple_of(i * T, T), T), pl.ds(pl.multiple_of((n - 1 - i) * T, T), T))
        chains = []
        for h in range(P):
            lanes = slice(h * HG_DK, (h + 1) * HG_DK)
            for d, g_ref in enumerate((g_f_ref, g_b_ref)):
                chains.append(dict(q=q_ref[sls[d], lanes].astype(F32), g=g_ref[sls[d], lanes],
                                   v=v_ref[sls[d], lanes],
                                   st=st_scr[2 * h + d], rev=bool(d)))
        res = _gla_blocks(chains, tri_ref, code_ref)
        for h in range(P):
            lanes = slice(h * HG_DK, (h + 1) * HG_DK)
            for d in range(2):
                o, st = res[2 * h + d]
                st_scr[2 * h + d] = st
                if finalize:
                    o = o + (ob_acc, of_acc)[d][sls[d], lanes]
                    ms = jnp.mean(o * o, axis=-1, keepdims=True)
                    o = o * lax.rsqrt(ms + EPS) * nw_ref[:, lanes]
                    o_ref[sls[d], lanes] = (o * gate_ref[sls[d], lanes]).astype(BF16)
                else:
                    (of_acc, ob_acc)[d][sls[d], lanes] = o

    def first_half(i, carry):
        step(i, False)
        return carry

    def second_half(i, carry):
        step(i, True)
        return carry

    lax.fori_loop(0, n // 2, first_half, 0, unroll=min(2, n // 2))
    lax.fori_loop(n // 2, n, second_half, 0, unroll=min(2, n // 2))
    if out_state:
        for h in range(P):
            for d in range(2):
                st_ref[d, h] = st_scr[2 * h + d].T


def _hgrn_call(hq, g, hi, hog, nw, s0, *, B, L, out_state, name):
    has_s0 = s0 is not None
    H = HG_HEADS
    P = HG_PACK
    W = P * HG_DK
    blk = lambda off: pl.BlockSpec((None, L, W), lambda b, h: (b, 0, off + h))
    st_spec = pl.BlockSpec((None, 2, P, HG_DK, HG_DV), lambda b, h: (b, 0, h, 0, 0))
    in_specs = [blk(0), blk(0), blk(H // P), blk(0), blk(0), pl.BlockSpec((1, W), lambda b, h: (0, h))]
    args = [hq, g, g, hi, hog, nw]
    if has_s0:
        in_specs.append(st_spec)
        args.append(s0)
    out_specs = [blk(0)]
    out_shape = [jax.ShapeDtypeStruct((B, L, H * HG_DV), BF16)]
    if out_state:
        out_specs.append(st_spec)
        out_shape.append(jax.ShapeDtypeStruct((B, 2, H, HG_DK, HG_DV), F32))
    return pl.pallas_call(
        functools.partial(_hgrn_kernel, L=L, has_s0=has_s0, out_state=out_state),
        grid=(B, H // P),
        in_specs=in_specs,
        out_specs=out_specs,
        out_shape=out_shape,
        scratch_shapes=[pltpu.VMEM((L, W), F32), pltpu.VMEM((L, W), F32),
                        pltpu.VMEM((2 * P, HG_DV, HG_DK), F32),
                        pltpu.VMEM((2, SCAN_BLOCK, SCAN_BLOCK), BF16),
                        pltpu.VMEM((2, SCAN_BLOCK, SCAN_BLOCK), jnp.int32)],
        compiler_params=_params(("parallel", "parallel")),
        name=name,
    )(*args)


def _merge_kernel(x_ref, mod_ref, oret_ref, ohg_ref, gr_ref, gh_ref, wr_ref, wh_ref, wo_ref, o_ref):
    y = gr_ref[...] * _dot(oret_ref[...], wr_ref[...]) + gh_ref[...] * _dot(ohg_ref[...], wh_ref[...])
    out = _dot(y.astype(BF16), wo_ref[...])
    o_ref[...] = x_ref[...] + mod_ref[5:6, :] * out


def _merge_call(x, mod3, oret, ohg, gr, gh, wr, wh, wo, *, mod_row_fn, tm, name):
    n = x.shape[0]
    row = pl.BlockSpec((tm, D_MODEL), lambda i: (i, 0))
    wspec = pl.BlockSpec((D_MODEL, D_MODEL), lambda i: (0, 0))
    return pl.pallas_call(
        _merge_kernel,
        grid=(n // tm,),
        in_specs=[row, pl.BlockSpec((None, N_MOD, D_MODEL), lambda i: (mod_row_fn(i, tm), 0, 0)),
                  row, row, row, row, wspec, wspec, wspec],
        out_specs=row,
        out_shape=jax.ShapeDtypeStruct((n, D_MODEL), F32),
        compiler_params=_params(("parallel",)),
        name=name,
    )(x, mod3, oret, ohg, gr, gh, wr, wh, wo)


def _rope_tables(n_tokens):
    pos = np.arange(n_tokens)
    r = (pos // GRID_W).astype(np.float32)
    cl = (pos % GRID_W).astype(np.float32)
    quarter = RET_DK // 4
    inv_freq = ROPE_BASE ** (-jnp.arange(quarter, dtype=F32) / quarter)
    ang_r = jnp.asarray(r)[:, None] * inv_freq[None, :]
    ang_c = jnp.asarray(cl)[:, None] * inv_freq[None, :]
    cos = jnp.concatenate([jnp.cos(ang_r)] * 2 + [jnp.cos(ang_c)] * 2, axis=-1)
    sins = jnp.concatenate([-jnp.sin(ang_r), jnp.sin(ang_r), -jnp.sin(ang_c), jnp.sin(ang_c)], axis=-1)
    return cos, sins


def _trunk(x, mod3, mod_row_fn, B, L, s0_ret, s0_hg, rope, out_state, wts, tag):
    (norm_w, final_w, ffn1_w13, ffn1_w2, ffn2_w13, ffn2_w2, w_in, dec, lb_logits, hg_nw,
     w_ret, w_hg, w_o, layer) = wts
    x = _ffn_call(x, mod3, norm_w, final_w, ffn1_w13, ffn1_w2, mod_base=0, norm_row=0, final=False,
                  mod_row_fn=mod_row_fn, tm=512, name="ffn1_" + tag)
    qk, rv, rg, hq, g, hi, hog, gr, gh = _proj_call(x, mod3, norm_w, lb_logits, w_in,
                                                    _rope_tables(L) if rope else None, layer=layer,
                                                    mod_row_fn=mod_row_fn, tm=256, name="proj_" + tag)
    r3 = lambda t: t.reshape(B, L, t.shape[-1])
    ret = _ret_call(dec, r3(qk), r3(rv), r3(rg), s0_ret, B=B, L=L, out_state=out_state, name="ret_" + tag)
    hg = _hgrn_call(r3(hq), r3(g), r3(hi), r3(hog), hg_nw, s0_hg, B=B, L=L, out_state=out_state,
                    name="hgrn_" + tag)
    x = _merge_call(x, mod3, ret[0].reshape(B * L, -1), hg[0].reshape(B * L, -1), gr, gh, w_ret, w_hg, w_o,
                    mod_row_fn=mod_row_fn, tm=512, name="merge_" + tag)
    y = _ffn_call(x, mod3, norm_w, final_w, ffn2_w13, ffn2_w2, mod_base=6, norm_row=2, final=True,
                  mod_row_fn=mod_row_fn, tm=512, name="ffn2_" + tag)
    if out_state:
        return y, ret[1], hg[1]
    return y, None, None


def kernel(x_prompt, x_sample, state_ret, state_hgrn, c, c_ctx, ada_w, ada_b, norm_w, ffn1_w13, ffn1_w2,
           ffn2_w13, ffn2_w2, w_in, ret_decay, hg_lb_logits, hg_norm_w, w_ret_proj, w_hg_proj, w_o,
           final_norm_w):
    depth = ada_w.shape[0]
    assert depth == 1
    l = 0
    bp, lp, _ = x_prompt.shape
    bs, ls, _ = x_sample.shape

    cvec = jnp.concatenate([c_ctx[None, :], c, jnp.zeros((8 - 1 - bs, D_MODEL), F32)], axis=0)
    mod = _mod_call(cvec, ada_w[l], ada_b[l][None, :])
    mod3 = mod.reshape(8, N_MOD, D_MODEL)

    bf = lambda w: w.astype(BF16)
    wts = (norm_w[l], final_norm_w[None, :], bf(ffn1_w13[l]), bf(ffn1_w2[l]), bf(ffn2_w13[l]), bf(ffn2_w2[l]),
           bf(w_in[l]), ret_decay[l], hg_lb_logits, hg_norm_w[l][None, :],
           bf(w_ret_proj[l]), bf(w_hg_proj[l]), bf(w_o[l]), l)

    yp, s_ret, s_hg = _trunk(x_prompt.reshape(bp * lp, D_MODEL), mod3, lambda i, tm: 0, bp, lp, None, None,
                             False, True, wts, "ctx")
    ys, _, _ = _trunk(x_sample.reshape(bs * ls, D_MODEL), mod3, lambda i, tm: 1 + (i * tm) // ls, bs, ls,
                      state_ret[:, l], state_hgrn[:, l], True, False, wts, "lat")
    return (yp.reshape(bp, lp, D_MODEL), ys.reshape(bs, ls, D_MODEL),
            s_ret[:, None], s_hg[:, None])
```

```python
import functools

import numpy as np
import jax
import jax.numpy as jnp
from jax import lax
from jax.experimental import pallas as pl
from jax.experimental.pallas import tpu as pltpu

F32 = jnp.float32
BF16 = jnp.bfloat16

D_MODEL = 1024
GRID_W = 64
RET_HEADS = 4
RET_DK = 128
RET_DV = 256
HG_HEADS = 8
HG_DK = 128
HG_DV = 128
D_FF = 2816
N_MOD = 9
ROPE_BASE = 10000.0
EPS = 1e-6

C_RQ, C_RK, C_RV, C_RG, C_HQ, C_HFF, C_HFB, C_HI, C_HOG, C_GR, C_GH = (
    0, 512, 1024, 2048, 3072, 4096, 5120, 6144, 7168, 8192, 9216)
IN_WIDTH = 10240

VMEM_LIMIT_BYTES = 56 * 1024 * 1024

TM_FFN = 1024
TM_PROJ = 512
TM_MERGE = 1024
SCAN_BLOCK = 128
RET_PACK = 2
HG_PACK = 2
FF_CHUNK = 256
PROJ_CHUNK = 256
LOG2E = 1.4426950408889634
EXP2_CLAMP = 115.0


def _dot(a, b):
    return jnp.dot(a, b, preferred_element_type=F32)


def _dot_nt(a, b):
    return lax.dot_general(a, b, (((1,), (1,)), ((), ())), preferred_element_type=F32)


def _dot_tn(a, b):
    return lax.dot_general(a, b, (((0,), (0,)), ((), ())), preferred_element_type=F32)


def _sigmoid(x):
    return 1.0 / (1.0 + jnp.exp(-x))


def _rms(x, w):
    ms = jnp.mean(x * x, axis=-1, keepdims=True)
    return x * lax.rsqrt(ms + EPS) * w


def _params(sem):
    return pltpu.CompilerParams(dimension_semantics=sem, vmem_limit_bytes=VMEM_LIMIT_BYTES)


def _mod_kernel(c_ref, w_ref, b_ref, o_ref):
    c = c_ref[...]
    s = (c * _sigmoid(c)).astype(BF16)
    o_ref[...] = _dot(s, w_ref[...].astype(BF16)) + b_ref[...]


def _mod_call(cvec, ada_w, ada_b):
    n = ada_w.shape[1]
    tn = 1024
    return pl.pallas_call(
        _mod_kernel,
        grid=(n // tn,),
        in_specs=[
            pl.BlockSpec((8, D_MODEL), lambda j: (0, 0)),
            pl.BlockSpec((D_MODEL, tn), lambda j: (0, j)),
            pl.BlockSpec((1, tn), lambda j: (0, j)),
        ],
        out_specs=pl.BlockSpec((8, tn), lambda j: (0, j)),
        out_shape=jax.ShapeDtypeStruct((8, n), F32),
        compiler_params=_params(("arbitrary",)),
        name="mod",
    )(cvec, ada_w, ada_b)


def _ffn_kernel(x_ref, mod_ref, nw_ref, fw_ref, w13_ref, w2_ref, o_ref, *, mod_base, norm_row, final):
    y = _rms(x_ref[...], nw_ref[norm_row:norm_row + 1, :])
    sh = mod_ref[mod_base:mod_base + 1, :]
    sc = mod_ref[mod_base + 1:mod_base + 2, :]
    h = (y * (1.0 + sc) + sh).astype(BF16)
    acc = None
    for c0 in range(0, D_FF, FF_CHUNK):
        a = _dot(h, w13_ref[:, c0:c0 + FF_CHUNK])
        b = _dot(h, w13_ref[:, D_FF + c0:D_FF + c0 + FF_CHUNK])
        act = (a * _sigmoid(a) * b).astype(BF16)
        part = _dot(act, w2_ref[c0:c0 + FF_CHUNK, :])
        acc = part if acc is None else acc + part
    g = mod_ref[mod_base + 2:mod_base + 3, :]
    xo = x_ref[...] + 0.5 * g * acc
    if final:
        xo = _rms(xo, fw_ref[...])
    o_ref[...] = xo


def _ffn_call(x, mod3, norm_w, final_w, w13, w2, *, mod_base, norm_row, final, mod_row_fn, tm, name):
    n = x.shape[0]
    kern = functools.partial(_ffn_kernel, mod_base=mod_base, norm_row=norm_row, final=final)
    const = lambda shape: pl.BlockSpec(shape, lambda i: (0, 0), pipeline_mode=pl.Buffered(1))
    return pl.pallas_call(
        kern,
        grid=(n // tm,),
        in_specs=[
            pl.BlockSpec((tm, D_MODEL), lambda i: (i, 0)),
            pl.BlockSpec((None, N_MOD, D_MODEL), lambda i: (mod_row_fn(i, tm), 0, 0)),
            pl.BlockSpec((3, D_MODEL), lambda i: (0, 0)),
            pl.BlockSpec((1, D_MODEL), lambda i: (0, 0)),
            const((D_MODEL, 2 * D_FF)),
            const((D_FF, D_MODEL)),
        ],
        out_specs=pl.BlockSpec((tm, D_MODEL), lambda i: (i, 0)),
        out_shape=jax.ShapeDtypeStruct((n, D_MODEL), F32),
        compiler_params=_params(("parallel",)),
        name=name,
    )(x, mod3, norm_w, final_w, w13, w2)


def _rope(x, cos, sins, first_quarter):
    partner = jnp.where(first_quarter, pltpu.roll(x, 96, 1), pltpu.roll(x, 32, 1))
    return x * cos + partner * sins


def _proj_kernel(*refs, layer, rope):
    it = iter(refs)
    x_ref, mod_ref, nw_ref, lbl_ref, w_ref = (next(it) for _ in range(5))
    cos_ref, sin_ref = (next(it), next(it)) if rope else (None, None)
    qk_ref, rv_ref, rg_ref, hq_ref, g_ref, hi_ref, hog_ref, gr_ref, gh_ref = (next(it) for _ in range(9))
    y = _rms(x_ref[...], nw_ref[1:2, :])
    h = (y * (1.0 + mod_ref[4:5, :]) + mod_ref[3:4, :]).astype(BF16)

    if rope:
        lane = lax.broadcasted_iota(jnp.int32, (x_ref.shape[0], RET_DK), 1)
        first_quarter = (lane & 32) == 0

    def rot(t):
        if not rope:
            return t
        heads = [_rope(t[:, j:j + RET_DK], cos_ref[...], sin_ref[...], first_quarter)
                 for j in range(0, t.shape[1], RET_DK)]
        return jnp.concatenate(heads, axis=1)

    def silu(t):
        return t * _sigmoid(t)

    def log_forget(d):
        def fn(t, c):
            lg = lbl_ref[d, :, c:c + PROJ_CHUNK]
            e = jnp.exp(lg - jnp.max(lg, axis=0, keepdims=True))
            lb = jnp.sum(e[0:layer + 1, :], axis=0, keepdims=True) / jnp.sum(e, axis=0, keepdims=True)
            return jnp.log(lb + (1.0 - lb) * _sigmoid(t))
        return fn

    parts = (
        (C_RQ, 512, qk_ref, 0, lambda t, c: rot(t)),
        (C_RK, 512, qk_ref, 512, lambda t, c: rot(t * (RET_DK ** -0.5))),
        (C_RV, 1024, rv_ref, 0, lambda t, c: t),
        (C_RG, 1024, rg_ref, 0, lambda t, c: silu(t)),
        (C_HQ, 1024, hq_ref, 0, lambda t, c: silu(t) * (HG_DK ** -0.5)),
        (C_HFF, 1024, g_ref, 0, log_forget(0)),
        (C_HFB, 1024, g_ref, 1024, log_forget(1)),
        (C_HI, 1024, hi_ref, 0, lambda t, c: t),
        (C_HOG, 1024, hog_ref, 0, lambda t, c: silu(t)),
        (C_GR, 1024, gr_ref, 0, lambda t, c: _sigmoid(t)),
        (C_GH, 1024, gh_ref, 0, lambda t, c: _sigmoid(t)),
    )
    for w0, width, o_ref, o0, fn in parts:
        for c in range(0, width, PROJ_CHUNK):
            t = _dot(h, w_ref[:, w0 + c:w0 + c + PROJ_CHUNK])
            o_ref[:, o0 + c:o0 + c + PROJ_CHUNK] = fn(t, c).astype(o_ref.dtype)


def _proj_call(x, mod3, norm_w, lb_logits, w_in, rope_tabs, *, layer, mod_row_fn, tm, name):
    n = x.shape[0]
    slots = lb_logits.shape[1]
    row = lambda w: pl.BlockSpec((tm, w), lambda i: (i, 0))
    outs = [(1024, BF16), (1024, BF16), (1024, BF16), (1024, BF16), (2048, F32),
            (1024, BF16), (1024, BF16), (1024, BF16), (1024, BF16)]
    in_specs = [
        row(D_MODEL),
        pl.BlockSpec((None, N_MOD, D_MODEL), lambda i: (mod_row_fn(i, tm), 0, 0)),
        pl.BlockSpec((3, D_MODEL), lambda i: (0, 0)),
        pl.BlockSpec((2, slots, 1024), lambda i: (0, 0, 0)),
        pl.BlockSpec((D_MODEL, IN_WIDTH), lambda i: (0, 0), pipeline_mode=pl.Buffered(1)),
    ]
    args = [x, mod3, norm_w, lb_logits, w_in]
    if rope_tabs is not None:
        tiles_per_seq = rope_tabs[0].shape[0] // tm
        in_specs += [pl.BlockSpec((tm, RET_DK), lambda i: (i % tiles_per_seq, 0))] * 2
        args += list(rope_tabs)
    return pl.pallas_call(
        functools.partial(_proj_kernel, layer=layer, rope=rope_tabs is not None),
        grid=(n // tm,),
        in_specs=in_specs,
        out_specs=[row(w) for w, _ in outs],
        out_shape=[jax.ShapeDtypeStruct((n, w), dt) for w, dt in outs],
        compiler_params=_params(("parallel",)),
        name=name,
    )(*args)


def _log_sigmoid(x):
    return -(jnp.maximum(-x, 0.0) + jnp.log1p(jnp.exp(-jnp.abs(x))))


def _ret_kernel(*refs, L, has_s0, out_state):
    it = iter(refs)
    dec_ref = next(it)
    q_ref, k_ref, v_ref, gate_ref = next(it), next(it), next(it), next(it)
    s0_ref = next(it) if has_s0 else None
    o_ref = next(it)
    st_ref = next(it) if out_state else None
    of_acc, ob_acc, st_scr, cst = next(it), next(it), next(it), next(it)

    C = SCAN_BLOCK
    P = RET_PACK
    n = L // C
    pair = pl.program_id(1)
    ii = lax.broadcasted_iota(jnp.int32, (C, C), 0).astype(F32)
    jj = lax.broadcasted_iota(jnp.int32, (C, C), 1).astype(F32)
    diff = ii - jj
    cds = []
    for h in range(P):
        hd = pair * P + h
        lgf = _log_sigmoid(jnp.full((C, C), dec_ref[0, hd], F32))
        lgb = _log_sigmoid(jnp.full((C, C), dec_ref[1, hd], F32))
        cst[h, 0] = (jnp.where(diff >= 0, jnp.exp(jnp.maximum(diff, 0.0) * lgf), 0.0)
                     + jnp.where(diff <= 0, jnp.exp(jnp.maximum(-diff, 0.0) * lgb), 0.0))
        cst[h, 1] = jnp.exp((ii + 1.0) * lgf)
        cst[h, 2] = jnp.exp((C - 1.0 - ii) * lgf)
        cst[h, 3] = jnp.exp((C - ii) * lgb)
        cst[h, 4] = jnp.exp(ii * lgb)
        cds.append([jnp.exp(C * _log_sigmoid(jnp.full((1, RET_DV), dec_ref[d, hd], F32))) for d in range(2)])
        for d in range(2):
            st_scr[2 * h + d] = s0_ref[d, h] if has_s0 else jnp.zeros((RET_DK, RET_DV), F32)

    def step(i, finalize):
        sls = (pl.ds(pl.multiple_of(i * C, C), C), pl.ds(pl.multiple_of((n - 1 - i) * C, C), C))
        chains = []
        for h in range(P):
            ql = slice(h * RET_DK, (h + 1) * RET_DK)
            vl = slice(h * RET_DV, (h + 1) * RET_DV)
            for d in range(2):
                chains.append(dict(h=h, d=d, vl=vl, q=q_ref[sls[d], ql], k=k_ref[sls[d], ql],
                                   v=v_ref[sls[d], vl]))
        for c in chains:
            if c["d"] == 0:
                c["s"] = _dot_nt(c["q"], c["k"])
        for c in chains:
            h, d, v, vl = c["h"], c["d"], c["v"], c["vl"]
            st = st_scr[2 * h + d]
            qd = (c["q"].astype(F32) * cst[h, 1 + 2 * d]).astype(BF16)
            if d == 0:
                p = (c["s"] * cst[h, 0]).astype(BF16)
                o = _dot(jnp.concatenate([p, qd], axis=1), jnp.concatenate([v, st.astype(BF16)], axis=0))
            else:
                o = _dot(qd, st.astype(BF16))
            kd = (c["k"].astype(F32) * cst[h, 2 + 2 * d]).astype(BF16)
            st_scr[2 * h + d] = cds[h][d] * st + _dot_tn(kd, v)
            if finalize:
                o = o + (ob_acc, of_acc)[d][sls[d], vl]
                ms = jnp.mean(o * o, axis=-1, keepdims=True)
                o_ref[sls[d], vl] = (o * lax.rsqrt(ms + EPS) * gate_ref[sls[d], vl]).astype(BF16)
            else:
                (of_acc, ob_acc)[d][sls[d], vl] = o

    def first_half(i, carry):
        step(i, False)
        return carry

    def second_half(i, carry):
        step(i, True)
        return carry

    lax.fori_loop(0, n // 2, first_half, 0, unroll=min(2, n // 2))
    lax.fori_loop(n // 2, n, second_half, 0, unroll=min(2, n // 2))
    if out_state:
        for h in range(P):
            for d in range(2):
                st_ref[d, h] = st_scr[2 * h + d]


def _ret_call(dec, qk, rv, rg, s0, *, B, L, out_state, name):
    has_s0 = s0 is not None
    H = RET_HEADS
    P = RET_PACK
    st_spec = pl.BlockSpec((None, 2, P, RET_DK, RET_DV), lambda b, h: (b, 0, h, 0, 0))
    in_specs = [
        pl.BlockSpec(memory_space=pltpu.SMEM),
        pl.BlockSpec((None, L, P * RET_DK), lambda b, h: (b, 0, h)),
        pl.BlockSpec((None, L, P * RET_DK), lambda b, h: (b, 0, H // P + h)),
        pl.BlockSpec((None, L, P * RET_DV), lambda b, h: (b, 0, h)),
        pl.BlockSpec((None, L, P * RET_DV), lambda b, h: (b, 0, h)),
    ]
    args = [dec, qk, qk, rv, rg]
    if has_s0:
        in_specs.append(st_spec)
        args.append(s0)
    out_specs = [pl.BlockSpec((None, L, P * RET_DV), lambda b, h: (b, 0, h))]
    out_shape = [jax.ShapeDtypeStruct((B, L, H * RET_DV), BF16)]
    if out_state:
        out_specs.append(st_spec)
        out_shape.append(jax.ShapeDtypeStruct((B, 2, H, RET_DK, RET_DV), F32))
    res = pl.pallas_call(
        functools.partial(_ret_kernel, L=L, has_s0=has_s0, out_state=out_state),
        grid=(B, H // P),
        in_specs=in_specs,
        out_specs=out_specs,
        out_shape=out_shape,
        scratch_shapes=[pltpu.VMEM((L, P * RET_DV), F32), pltpu.VMEM((L, P * RET_DV), F32),
                        pltpu.VMEM((2 * P, RET_DK, RET_DV), F32),
                        pltpu.VMEM((P, 5, SCAN_BLOCK, SCAN_BLOCK), F32)],
        compiler_params=_params(("parallel", "parallel")),
        name=name,
    )(*args)
    return res


def _gla_blocks(chains, tri_ref, code_ref):
    T = SCAN_BLOCK
    K = HG_DK
    for c in chains:
        g2 = c["g"] * LOG2E
        c["k"] = 1.0 - jnp.exp2(g2)
        hi = g2.astype(BF16)
        lo = (g2 - hi.astype(F32)).astype(BF16)
        tri = tri_ref[int(c["rev"])]
        c["b"] = _dot(tri, hi) + _dot(tri, lo)

    for c in chains:
        b8 = c["b"].reshape(T // 8, 8, K)
        r = 4 if c["rev"] else 3
        d = jnp.clip(b8 - b8[:, r:r + 1, :], -EXP2_CLAMP, EXP2_CLAMP).reshape(T, K)
        e = jnp.exp2(d)
        p = _dot_nt((c["q"] * e).astype(BF16), (c["k"] / e).astype(BF16))
        c["a"] = jnp.where(code_ref[int(c["rev"])] == 0, p, 0.0)

    for lv, m in ((3, 8), (4, 16), (5, 32), (6, 64)):
        seg = 2 * m
        ng = T // seg
        for c in chains:
            b3, q3, k3 = (c[n].reshape(ng, seg, K) for n in ("b", "q", "k"))
            if c["rev"]:
                qs, ks, r = slice(0, m), slice(m, seg), m
            else:
                qs, ks, r = slice(m, seg), slice(0, m), m - 1
            ref = b3[:, r:r + 1, :]
            qm = (q3[:, qs, :] * jnp.exp2(b3[:, qs, :] - ref)).reshape(ng * m, K).astype(BF16)
            kz = k3[:, ks, :] * jnp.exp2(ref - b3[:, ks, :])
            zeros = jnp.zeros((ng, m, K), F32)
            kfull = jnp.concatenate([zeros, kz] if c["rev"] else [kz, zeros], axis=1)
            p = _dot_nt(qm, kfull.reshape(T, K).astype(BF16)).reshape(ng, m, T)
            a3 = c["a"].reshape(ng, seg, T)
            c3 = code_ref[int(c["rev"])].reshape(ng, seg, T)
            upd = jnp.where(c3[:, qs, :] == lv, p, a3[:, qs, :])
            parts = [upd, a3[:, ks, :]] if c["rev"] else [a3[:, ks, :], upd]
            c["a"] = jnp.concatenate(parts, axis=1).reshape(T, T)

    out = []
    for c in chains:
        b, q, k, v, st = c["b"], c["q"], c["k"], c["v"], c["st"]
        o = _dot(c["a"].astype(BF16), v) + _dot_nt((q * jnp.exp2(b)).astype(BF16), st.astype(BF16))
        end_row = 0 if c["rev"] else T - 1
        b1 = b.reshape(1, T, K)
        b_end = b1[:, end_row:end_row + 1, :]
        kd = (k * jnp.exp2(b_end - b1).reshape(T, K)).astype(BF16)
        out.append((o, st * jnp.exp2(b_end.reshape(1, K)) + _dot_tn(v, kd)))
    return out


def _hgrn_kernel(*refs, L, has_s0, out_state):
    it = iter(refs)
    q_ref, g_f_ref, g_b_ref, v_ref, gate_ref, nw_ref = (next(it) for _ in range(6))
    s0_ref = next(it) if has_s0 else None
    o_ref = next(it)
    st_ref = next(it) if out_state else None
    of_acc, ob_acc, st_scr, tri_ref, code_ref = (next(it) for _ in range(5))

    T = SCAN_BLOCK
    P = HG_PACK
    n = L // T
    ti = lax.broadcasted_iota(jnp.int32, (T, T), 0)
    si = lax.broadcasted_iota(jnp.int32, (T, T), 1)
    x = ti ^ si
    tri_ref[0] = (si <= ti).astype(BF16)
    tri_ref[1] = (si >= ti).astype(BF16)
    lvl = jnp.where(x < 8, 0, 3 + (x >= 16).astype(jnp.int32) + (x >= 32).astype(jnp.int32)
                    + (x >= 64).astype(jnp.int32))
    code_ref[0] = jnp.where(si <= ti, lvl, -1)
    code_ref[1] = jnp.where(si >= ti, lvl, -1)

    for h in range(P):
        for d in range(2):
            st_scr[2 * h + d] = s0_ref[d, h].T if has_s0 else jnp.zeros((HG_DV, HG_DK), F32)

    def step(i, finalize):
        sls = (pl.ds(pl.multiple_of(i * T, T), T), pl.ds(pl.multiple_of((n - 1 - i) * T, T), T))
        chains = []
        for h in range(P):
            lanes = slice(h * HG_DK, (h + 1) * HG_DK)
            for d, g_ref in enumerate((g_f_ref, g_b_ref)):
                chains.append(dict(q=q_ref[sls[d], lanes].astype(F32), g=g_ref[sls[d], lanes],
                                   v=v_ref[sls[d], lanes],
                                   st=st_scr[2 * h + d], rev=bool(d)))
        res = _gla_blocks(chains, tri_ref, code_ref)
        for h in range(P):
            lanes = slice(h * HG_DK, (h + 1) * HG_DK)
            for d in range(2):
                o, st = res[2 * h + d]
                st_scr[2 * h + d] = st
                if finalize:
                    o = o + (ob_acc, of_acc)[d][sls[d], lanes]
                    ms = jnp.mean(o * o, axis=-1, keepdims=True)
                    o = o * lax.rsqrt(ms + EPS) * nw_ref[:, lanes]
                    o_ref[sls[d], lanes] = (o * gate_ref[sls[d], lanes]).astype(BF16)
                else:
                    (of_acc, ob_acc)[d][sls[d], lanes] = o

    def first_half(i, carry):
        step(i, False)
        return carry

    def second_half(i, carry):
        step(i, True)
        return carry

    lax.fori_loop(0, n // 2, first_half, 0, unroll=min(2, n // 2))
    lax.fori_loop(n // 2, n, second_half, 0, unroll=min(2, n // 2))
    if out_state:
        for h in range(P):
            for d in range(2):
                st_ref[d, h] = st_scr[2 * h + d].T


def _hgrn_call(hq, g, hi, hog, nw, s0, *, B, L, out_state, name):
    has_s0 = s0 is not None
    H = HG_HEADS
    P = HG_PACK
    W = P * HG_DK
    blk = lambda off: pl.BlockSpec((None, L, W), lambda b, h: (b, 0, off + h))
    st_spec = pl.BlockSpec((None, 2, P, HG_DK, HG_DV), lambda b, h: (b, 0, h, 0, 0))
    in_specs = [blk(0), blk(0), blk(H // P), blk(0), blk(0), pl.BlockSpec((1, W), lambda b, h: (0, h))]
    args = [hq, g, g, hi, hog, nw]
    if has_s0:
        in_specs.append(st_spec)
        args.append(s0)
    out_specs = [blk(0)]
    out_shape = [jax.ShapeDtypeStruct((B, L, H * HG_DV), BF16)]
    if out_state:
        out_specs.append(st_spec)
        out_shape.append(jax.ShapeDtypeStruct((B, 2, H, HG_DK, HG_DV), F32))
    return pl.pallas_call(
        functools.partial(_hgrn_kernel, L=L, has_s0=has_s0, out_state=out_state),
        grid=(B, H // P),
        in_specs=in_specs,
        out_specs=out_specs,
        out_shape=out_shape,
        scratch_shapes=[pltpu.VMEM((L, W), F32), pltpu.VMEM((L, W), F32),
                        pltpu.VMEM((2 * P, HG_DV, HG_DK), F32),
                        pltpu.VMEM((2, SCAN_BLOCK, SCAN_BLOCK), BF16),
                        pltpu.VMEM((2, SCAN_BLOCK, SCAN_BLOCK), jnp.int32)],
        compiler_params=_params(("parallel", "parallel")),
        name=name,
    )(*args)


def _merge_kernel(x_ref, mod_ref, oret_ref, ohg_ref, gr_ref, gh_ref, wr_ref, wh_ref, wo_ref, o_ref):
    y = gr_ref[...] * _dot(oret_ref[...], wr_ref[...]) + gh_ref[...] * _dot(ohg_ref[...], wh_ref[...])
    out = _dot(y.astype(BF16), wo_ref[...])
    o_ref[...] = x_ref[...] + mod_ref[5:6, :] * out


def _merge_call(x, mod3, oret, ohg, gr, gh, wr, wh, wo, *, mod_row_fn, tm, name):
    n = x.shape[0]
    row = pl.BlockSpec((tm, D_MODEL), lambda i: (i, 0))
    wspec = pl.BlockSpec((D_MODEL, D_MODEL), lambda i: (0, 0), pipeline_mode=pl.Buffered(1))
    return pl.pallas_call(
        _merge_kernel,
        grid=(n // tm,),
        in_specs=[row, pl.BlockSpec((None, N_MOD, D_MODEL), lambda i: (mod_row_fn(i, tm), 0, 0)),
                  row, row, row, row, wspec, wspec, wspec],
        out_specs=row,
        out_shape=jax.ShapeDtypeStruct((n, D_MODEL), F32),
        compiler_params=_params(("parallel",)),
        name=name,
    )(x, mod3, oret, ohg, gr, gh, wr, wh, wo)


def _rope_tables(n_tokens):
    pos = np.arange(n_tokens)
    r = (pos // GRID_W).astype(np.float32)
    cl = (pos % GRID_W).astype(np.float32)
    quarter = RET_DK // 4
    inv_freq = ROPE_BASE ** (-jnp.arange(quarter, dtype=F32) / quarter)
    ang_r = jnp.asarray(r)[:, None] * inv_freq[None, :]
    ang_c = jnp.asarray(cl)[:, None] * inv_freq[None, :]
    cos = jnp.concatenate([jnp.cos(ang_r)] * 2 + [jnp.cos(ang_c)] * 2, axis=-1)
    sins = jnp.concatenate([-jnp.sin(ang_r), jnp.sin(ang_r), -jnp.sin(ang_c), jnp.sin(ang_c)], axis=-1)
    return cos, sins


def _trunk(x, mod3, mod_row_fn, B, L, s0_ret, s0_hg, rope, out_state, wts, tag):
    (norm_w, final_w, ffn1_w13, ffn1_w2, ffn2_w13, ffn2_w2, w_in, dec, lb_logits, hg_nw,
     w_ret, w_hg, w_o, layer) = wts
    x = _ffn_call(x, mod3, norm_w, final_w, ffn1_w13, ffn1_w2, mod_base=0, norm_row=0, final=False,
                  mod_row_fn=mod_row_fn, tm=TM_FFN, name="ffn1_" + tag)
    qk, rv, rg, hq, g, hi, hog, gr, gh = _proj_call(x, mod3, norm_w, lb_logits, w_in,
                                                    _rope_tables(L) if rope else None, layer=layer,
                                                    mod_row_fn=mod_row_fn, tm=TM_PROJ, name="proj_" + tag)
    r3 = lambda t: t.reshape(B, L, t.shape[-1])
    ret = _ret_call(dec, r3(qk), r3(rv), r3(rg), s0_ret, B=B, L=L, out_state=out_state, name="ret_" + tag)
    hg = _hgrn_call(r3(hq), r3(g), r3(hi), r3(hog), hg_nw, s0_hg, B=B, L=L, out_state=out_state,
                    name="hgrn_" + tag)
    x = _merge_call(x, mod3, ret[0].reshape(B * L, -1), hg[0].reshape(B * L, -1), gr, gh, w_ret, w_hg, w_o,
                    mod_row_fn=mod_row_fn, tm=TM_MERGE, name="merge_" + tag)
    y = _ffn_call(x, mod3, norm_w, final_w, ffn2_w13, ffn2_w2, mod_base=6, norm_row=2, final=True,
                  mod_row_fn=mod_row_fn, tm=TM_FFN, name="ffn2_" + tag)
    if out_state:
        return y, ret[1], hg[1]
    return y, None, None


def kernel(x_prompt, x_sample, state_ret, state_hgrn, c, c_ctx, ada_w, ada_b, norm_w, ffn1_w13, ffn1_w2,
           ffn2_w13, ffn2_w2, w_in, ret_decay, hg_lb_logits, hg_norm_w, w_ret_proj, w_hg_proj, w_o,
           final_norm_w):
    depth = ada_w.shape[0]
    assert depth == 1
    l = 0
    bp, lp, _ = x_prompt.shape
    bs, ls, _ = x_sample.shape

    cvec = jnp.concatenate([c_ctx[None, :], c, jnp.zeros((8 - 1 - bs, D_MODEL), F32)], axis=0)
    mod = _mod_call(cvec, ada_w[l], ada_b[l][None, :])
    mod3 = mod.reshape(8, N_MOD, D_MODEL)

    bf = lambda w: w.astype(BF16)
    wts = (norm_w[l], final_norm_w[None, :], bf(ffn1_w13[l]), bf(ffn1_w2[l]), bf(ffn2_w13[l]), bf(ffn2_w2[l]),
           bf(w_in[l]), ret_decay[l], hg_lb_logits, hg_norm_w[l][None, :],
           bf(w_ret_proj[l]), bf(w_hg_proj[l]), bf(w_o[l]), l)

    yp, s_ret, s_hg = _trunk(x_prompt.reshape(bp * lp, D_MODEL), mod3, lambda i, tm: 0, bp, lp, None, None,
                             False, True, wts, "ctx")
    ys, _, _ = _trunk(x_sample.reshape(bs * ls, D_MODEL), mod3, lambda i, tm: 1 + (i * tm) // ls, bs, ls,
                      state_ret[:, l], state_hgrn[:, l], True, False, wts, "lat")
    return (yp.reshape(bp, lp, D_MODEL), ys.reshape(bs, ls, D_MODEL),
            s_ret[:, None], s_hg[:, None])
```

```python
import functools

import numpy as np
import jax
import jax.numpy as jnp
from jax import lax
from jax.experimental import pallas as pl
from jax.experimental.pallas import tpu as pltpu

F32 = jnp.float32
BF16 = jnp.bfloat16

D_MODEL = 1024
GRID_W = 64
RET_HEADS = 4
RET_DK = 128
RET_DV = 256
HG_HEADS = 8
HG_DK = 128
HG_DV = 128
D_FF = 2816
N_MOD = 9
ROPE_BASE = 10000.0
EPS = 1e-6

C_RQ, C_RK, C_RV, C_RG, C_HQ, C_HFF, C_HFB, C_HI, C_HOG, C_GR, C_GH = (
    0, 512, 1024, 2048, 3072, 4096, 5120, 6144, 7168, 8192, 9216)
IN_WIDTH = 10240

VMEM_LIMIT_BYTES = 56 * 1024 * 1024

TM_FFN = 1024
TM_PROJ = 256
TM_MERGE = 1024
SCAN_BLOCK = 128
RET_PACK = 4
RET_UNROLL = 1
HG_PACK = 4
HG_UNROLL = 1
FF_CHUNK = 256
PROJ_CHUNK = 256
LOG2E = 1.4426950408889634
EXP2_CLAMP = 115.0


def _dot(a, b):
    return jnp.dot(a, b, preferred_element_type=F32)


def _dot_nt(a, b):
    return lax.dot_general(a, b, (((1,), (1,)), ((), ())), preferred_element_type=F32)


def _dot_tn(a, b):
    return lax.dot_general(a, b, (((0,), (0,)), ((), ())), preferred_element_type=F32)


def _sigmoid(x):
    return 1.0 / (1.0 + jnp.exp(-x))


def _rms(x, w):
    ms = jnp.mean(x * x, axis=-1, keepdims=True)
    return x * lax.rsqrt(ms + EPS) * w


def _params(sem):
    return pltpu.CompilerParams(dimension_semantics=sem, vmem_limit_bytes=VMEM_LIMIT_BYTES)


def _mod_kernel(c_ref, w_ref, b_ref, o_ref):
    c = c_ref[...]
    s = (c * _sigmoid(c)).astype(BF16)
    o_ref[...] = _dot(s, w_ref[...].astype(BF16)) + b_ref[...]


def _mod_call(cvec, ada_w, ada_b):
    n = ada_w.shape[1]
    tn = 1024
    return pl.pallas_call(
        _mod_kernel,
        grid=(n // tn,),
        in_specs=[
            pl.BlockSpec((8, D_MODEL), lambda j: (0, 0)),
            pl.BlockSpec((D_MODEL, tn), lambda j: (0, j)),
            pl.BlockSpec((1, tn), lambda j: (0, j)),
        ],
        out_specs=pl.BlockSpec((8, tn), lambda j: (0, j)),
        out_shape=jax.ShapeDtypeStruct((8, n), F32),
        compiler_params=_params(("arbitrary",)),
        name="mod",
    )(cvec, ada_w, ada_b)


def _ffn_kernel(x_ref, mod_ref, nw_ref, fw_ref, w13_ref, w2_ref, o_ref, *, mod_base, norm_row, final):
    y = _rms(x_ref[...], nw_ref[norm_row:norm_row + 1, :])
    sh = mod_ref[mod_base:mod_base + 1, :]
    sc = mod_ref[mod_base + 1:mod_base + 2, :]
    h = (y * (1.0 + sc) + sh).astype(BF16)
    acc = None
    for c0 in range(0, D_FF, FF_CHUNK):
        a = _dot(h, w13_ref[:, c0:c0 + FF_CHUNK])
        b = _dot(h, w13_ref[:, D_FF + c0:D_FF + c0 + FF_CHUNK])
        act = (a * _sigmoid(a) * b).astype(BF16)
        part = _dot(act, w2_ref[c0:c0 + FF_CHUNK, :])
        acc = part if acc is None else acc + part
    g = mod_ref[mod_base + 2:mod_base + 3, :]
    xo = x_ref[...] + 0.5 * g * acc
    if final:
        xo = _rms(xo, fw_ref[...])
    o_ref[...] = xo


def _ffn_call(x, mod3, norm_w, final_w, w13, w2, *, mod_base, norm_row, final, mod_row_fn, tm, name):
    n = x.shape[0]
    kern = functools.partial(_ffn_kernel, mod_base=mod_base, norm_row=norm_row, final=final)
    const = lambda shape: pl.BlockSpec(shape, lambda i: (0, 0), pipeline_mode=pl.Buffered(1))
    return pl.pallas_call(
        kern,
        grid=(n // tm,),
        in_specs=[
            pl.BlockSpec((tm, D_MODEL), lambda i: (i, 0)),
            pl.BlockSpec((None, N_MOD, D_MODEL), lambda i: (mod_row_fn(i, tm), 0, 0)),
            pl.BlockSpec((3, D_MODEL), lambda i: (0, 0)),
            pl.BlockSpec((1, D_MODEL), lambda i: (0, 0)),
            const((D_MODEL, 2 * D_FF)),
            const((D_FF, D_MODEL)),
        ],
        out_specs=pl.BlockSpec((tm, D_MODEL), lambda i: (i, 0)),
        out_shape=jax.ShapeDtypeStruct((n, D_MODEL), F32),
        compiler_params=_params(("parallel",)),
        name=name,
    )(x, mod3, norm_w, final_w, w13, w2)


def _rope(x, cos, sins, first_quarter):
    partner = jnp.where(first_quarter, pltpu.roll(x, 96, 1), pltpu.roll(x, 32, 1))
    return x * cos + partner * sins


def _proj_kernel(*refs, layer, rope):
    it = iter(refs)
    x_ref, mod_ref, nw_ref, lbl_ref, w_ref = (next(it) for _ in range(5))
    cos_ref, sin_ref = (next(it), next(it)) if rope else (None, None)
    qk_ref, rv_ref, rg_ref, hq_ref, g_ref, hi_ref, hog_ref, gr_ref, gh_ref = (next(it) for _ in range(9))
    y = _rms(x_ref[...], nw_ref[1:2, :])
    h = (y * (1.0 + mod_ref[4:5, :]) + mod_ref[3:4, :]).astype(BF16)

    if rope:
        lane = lax.broadcasted_iota(jnp.int32, (x_ref.shape[0], RET_DK), 1)
        first_quarter = (lane & 32) == 0

    def rot(t):
        if not rope:
            return t
        heads = [_rope(t[:, j:j + RET_DK], cos_ref[...], sin_ref[...], first_quarter)
                 for j in range(0, t.shape[1], RET_DK)]
        return jnp.concatenate(heads, axis=1)

    def silu(t):
        return t * _sigmoid(t)

    def log_forget(d):
        def fn(t, c):
            lg = lbl_ref[d, :, c:c + PROJ_CHUNK]
            e = jnp.exp(lg - jnp.max(lg, axis=0, keepdims=True))
            lb = jnp.sum(e[0:layer + 1, :], axis=0, keepdims=True) / jnp.sum(e, axis=0, keepdims=True)
            return jnp.log(lb + (1.0 - lb) * _sigmoid(t))
        return fn

    parts = (
        (C_RQ, 512, qk_ref, 0, lambda t, c: rot(t)),
        (C_RK, 512, qk_ref, 512, lambda t, c: rot(t * (RET_DK ** -0.5))),
        (C_RV, 1024, rv_ref, 0, lambda t, c: t),
        (C_RG, 1024, rg_ref, 0, lambda t, c: silu(t)),
        (C_HQ, 1024, hq_ref, 0, lambda t, c: silu(t) * (HG_DK ** -0.5)),
        (C_HFF, 1024, g_ref, 0, log_forget(0)),
        (C_HFB, 1024, g_ref, 1024, log_forget(1)),
        (C_HI, 1024, hi_ref, 0, lambda t, c: t),
        (C_HOG, 1024, hog_ref, 0, lambda t, c: silu(t)),
        (C_GR, 1024, gr_ref, 0, lambda t, c: _sigmoid(t)),
        (C_GH, 1024, gh_ref, 0, lambda t, c: _sigmoid(t)),
    )
    for w0, width, o_ref, o0, fn in parts:
        for c in range(0, width, PROJ_CHUNK):
            t = _dot(h, w_ref[:, w0 + c:w0 + c + PROJ_CHUNK])
            o_ref[:, o0 + c:o0 + c + PROJ_CHUNK] = fn(t, c).astype(o_ref.dtype)


def _proj_call(x, mod3, norm_w, lb_logits, w_in, rope_tabs, *, layer, mod_row_fn, tm, name):
    n = x.shape[0]
    slots = lb_logits.shape[1]
    row = lambda w: pl.BlockSpec((tm, w), lambda i: (i, 0))
    outs = [(1024, BF16), (1024, BF16), (1024, BF16), (1024, BF16), (2048, F32),
            (1024, BF16), (1024, BF16), (1024, BF16), (1024, BF16)]
    in_specs = [
        row(D_MODEL),
        pl.BlockSpec((None, N_MOD, D_MODEL), lambda i: (mod_row_fn(i, tm), 0, 0)),
        pl.BlockSpec((3, D_MODEL), lambda i: (0, 0)),
        pl.BlockSpec((2, slots, 1024), lambda i: (0, 0, 0)),
        pl.BlockSpec((D_MODEL, IN_WIDTH), lambda i: (0, 0), pipeline_mode=pl.Buffered(1)),
    ]
    args = [x, mod3, norm_w, lb_logits, w_in]
    if rope_tabs is not None:
        tiles_per_seq = rope_tabs[0].shape[0] // tm
        in_specs += [pl.BlockSpec((tm, RET_DK), lambda i: (i % tiles_per_seq, 0))] * 2
        args += list(rope_tabs)
    return pl.pallas_call(
        functools.partial(_proj_kernel, layer=layer, rope=rope_tabs is not None),
        grid=(n // tm,),
        in_specs=in_specs,
        out_specs=[row(w) for w, _ in outs],
        out_shape=[jax.ShapeDtypeStruct((n, w), dt) for w, dt in outs],
        compiler_params=_params(("parallel",)),
        name=name,
    )(*args)


def _log_sigmoid(x):
    return -(jnp.maximum(-x, 0.0) + jnp.log1p(jnp.exp(-jnp.abs(x))))


def _half_acc_slices(i, n, blk, finalize):
    if finalize:
        starts = ((i - n // 2) * blk, (n - 1 - i) * blk)
    else:
        starts = (i * blk, (n // 2 - 1 - i) * blk)
    return tuple(pl.ds(pl.multiple_of(s, blk), blk) for s in starts)


def _ret_kernel(*refs, L, has_s0, out_state):
    it = iter(refs)
    dec_ref = next(it)
    q_ref, k_ref, v_ref, gate_ref = next(it), next(it), next(it), next(it)
    s0_ref = next(it) if has_s0 else None
    o_ref = next(it)
    st_ref = next(it) if out_state else None
    of_acc, ob_acc, st_scr, cst = next(it), next(it), next(it), next(it)

    C = SCAN_BLOCK
    P = RET_PACK
    n = L // C
    pair = pl.program_id(1)
    ii = lax.broadcasted_iota(jnp.int32, (C, C), 0).astype(F32)
    jj = lax.broadcasted_iota(jnp.int32, (C, C), 1).astype(F32)
    diff = ii - jj

    @pl.when(jnp.logical_and(pl.program_id(0) == 0, pair == 0))
    def _():
        for hd in range(RET_HEADS):
            lgf = _log_sigmoid(jnp.full((C, C), dec_ref[0, hd], F32))
            lgb = _log_sigmoid(jnp.full((C, C), dec_ref[1, hd], F32))
            cst[hd, 0] = (jnp.where(diff >= 0, jnp.exp(jnp.maximum(diff, 0.0) * lgf), 0.0)
                          + jnp.where(diff <= 0, jnp.exp(jnp.maximum(-diff, 0.0) * lgb), 0.0))
            cst[hd, 1] = jnp.exp((ii + 1.0) * lgf)
            cst[hd, 2] = jnp.exp((C - 1.0 - ii) * lgf)
            cst[hd, 3] = jnp.exp((C - ii) * lgb)
            cst[hd, 4] = jnp.exp(ii * lgb)

    cds = []
    for h in range(P):
        hd = pair * P + h
        cds.append([jnp.exp(C * _log_sigmoid(jnp.full((1, RET_DV), dec_ref[d, hd], F32))) for d in range(2)])
        for d in range(2):
            st_scr[2 * h + d] = s0_ref[d, h] if has_s0 else jnp.zeros((RET_DK, RET_DV), F32)

    def step(i, finalize):
        sls = (pl.ds(pl.multiple_of(i * C, C), C), pl.ds(pl.multiple_of((n - 1 - i) * C, C), C))
        acc_sls = _half_acc_slices(i, n, C, finalize)
        chains = []
        for h in range(P):
            ql = slice(h * RET_DK, (h + 1) * RET_DK)
            vl = slice(h * RET_DV, (h + 1) * RET_DV)
            for d in range(2):
                chains.append(dict(h=h, d=d, vl=vl, q=q_ref[sls[d], ql], k=k_ref[sls[d], ql],
                                   v=v_ref[sls[d], vl]))
        for c in chains:
            if c["d"] == 0:
                c["s"] = _dot_nt(c["q"], c["k"])
        for c in chains:
            h, d, v, vl = c["h"], c["d"], c["v"], c["vl"]
            hd = pair * P + h
            st = st_scr[2 * h + d]
            qd = (c["q"].astype(F32) * cst[hd, 1 + 2 * d]).astype(BF16)
            if d == 0:
                p = (c["s"] * cst[hd, 0]).astype(BF16)
                o = _dot(jnp.concatenate([p, qd], axis=1), jnp.concatenate([v, st.astype(BF16)], axis=0))
            else:
                o = _dot(qd, st.astype(BF16))
            kd = (c["k"].astype(F32) * cst[hd, 2 + 2 * d]).astype(BF16)
            st_scr[2 * h + d] = cds[h][d] * st + _dot_tn(kd, v)
            if finalize:
                o = o + (ob_acc, of_acc)[d][acc_sls[d], vl]
                ms = jnp.mean(o * o, axis=-1, keepdims=True)
                o_ref[sls[d], vl] = (o * lax.rsqrt(ms + EPS) * gate_ref[sls[d], vl]).astype(BF16)
            else:
                (of_acc, ob_acc)[d][acc_sls[d], vl] = o

    def first_half(i, carry):
        step(i, False)
        return carry

    def second_half(i, carry):
        step(i, True)
        return carry

    lax.fori_loop(0, n // 2, first_half, 0, unroll=min(RET_UNROLL, n // 2))
    lax.fori_loop(n // 2, n, second_half, 0, unroll=min(RET_UNROLL, n // 2))
    if out_state:
        for h in range(P):
            for d in range(2):
                st_ref[d, h] = st_scr[2 * h + d]


def _ret_call(dec, qk, rv, rg, s0, *, B, L, out_state, name):
    has_s0 = s0 is not None
    H = RET_HEADS
    P = RET_PACK
    st_spec = pl.BlockSpec((None, 2, P, RET_DK, RET_DV), lambda b, h: (b, 0, h, 0, 0))
    in_specs = [
        pl.BlockSpec(memory_space=pltpu.SMEM),
        pl.BlockSpec((None, L, P * RET_DK), lambda b, h: (b, 0, h)),
        pl.BlockSpec((None, L, P * RET_DK), lambda b, h: (b, 0, H // P + h)),
        pl.BlockSpec((None, L, P * RET_DV), lambda b, h: (b, 0, h)),
        pl.BlockSpec((None, L, P * RET_DV), lambda b, h: (b, 0, h)),
    ]
    args = [dec, qk, qk, rv, rg]
    if has_s0:
        in_specs.append(st_spec)
        args.append(s0)
    out_specs = [pl.BlockSpec((None, L, P * RET_DV), lambda b, h: (b, 0, h))]
    out_shape = [jax.ShapeDtypeStruct((B, L, H * RET_DV), BF16)]
    if out_state:
        out_specs.append(st_spec)
        out_shape.append(jax.ShapeDtypeStruct((B, 2, H, RET_DK, RET_DV), F32))
    res = pl.pallas_call(
        functools.partial(_ret_kernel, L=L, has_s0=has_s0, out_state=out_state),
        grid=(B, H // P),
        in_specs=in_specs,
        out_specs=out_specs,
        out_shape=out_shape,
        scratch_shapes=[pltpu.VMEM((L // 2, P * RET_DV), F32), pltpu.VMEM((L // 2, P * RET_DV), F32),
                        pltpu.VMEM((2 * P, RET_DK, RET_DV), F32),
                        pltpu.VMEM((H, 5, SCAN_BLOCK, SCAN_BLOCK), F32)],
        compiler_params=_params(("arbitrary", "arbitrary")),
        name=name,
    )(*args)
    return res


def _gla_blocks(chains, tri_ref, code_ref):
    T = SCAN_BLOCK
    K = HG_DK
    for c in chains:
        g2 = c["g"] * LOG2E
        c["k"] = 1.0 - jnp.exp2(g2)
        hi = g2.astype(BF16)
        lo = (g2 - hi.astype(F32)).astype(BF16)
        tri = tri_ref[int(c["rev"])]
        c["b"] = _dot(tri, hi) + _dot(tri, lo)

    for c in chains:
        b8 = c["b"].reshape(T // 8, 8, K)
        r = 4 if c["rev"] else 3
        d = jnp.clip(b8 - b8[:, r:r + 1, :], -EXP2_CLAMP, EXP2_CLAMP).reshape(T, K)
        e = jnp.exp2(d)
        p = _dot_nt((c["q"] * e).astype(BF16), (c["k"] / e).astype(BF16))
        c["a"] = jnp.where(code_ref[int(c["rev"])] == 0, p, 0.0)

    for lv, m in ((3, 8), (4, 16), (5, 32), (6, 64)):
        seg = 2 * m
        ng = T // seg
        for c in chains:
            b3, q3, k3 = (c[n].reshape(ng, seg, K) for n in ("b", "q", "k"))
            if c["rev"]:
                qs, ks, r = slice(0, m), slice(m, seg), m
            else:
                qs, ks, r = slice(m, seg), slice(0, m), m - 1
            ref = b3[:, r:r + 1, :]
            qm = (q3[:, qs, :] * jnp.exp2(b3[:, qs, :] - ref)).reshape(ng * m, K).astype(BF16)
            kz = k3[:, ks, :] * jnp.exp2(ref - b3[:, ks, :])
            zeros = jnp.zeros((ng, m, K), F32)
            kfull = jnp.concatenate([zeros, kz] if c["rev"] else [kz, zeros], axis=1)
            p = _dot_nt(qm, kfull.reshape(T, K).astype(BF16)).reshape(ng, m, T)
            a3 = c["a"].reshape(ng, seg, T)
            c3 = code_ref[int(c["rev"])].reshape(ng, seg, T)
            upd = jnp.where(c3[:, qs, :] == lv, p, a3[:, qs, :])
            parts = [upd, a3[:, ks, :]] if c["rev"] else [a3[:, ks, :], upd]
            c["a"] = jnp.concatenate(parts, axis=1).reshape(T, T)

    out = []
    for c in chains:
        b, q, k, v, st = c["b"], c["q"], c["k"], c["v"], c["st"]
        o = _dot(c["a"].astype(BF16), v) + _dot_nt((q * jnp.exp2(b)).astype(BF16), st.astype(BF16))
        end_row = 0 if c["rev"] else T - 1
        b1 = b.reshape(1, T, K)
        b_end = b1[:, end_row:end_row + 1, :]
        kd = (k * jnp.exp2(b_end - b1).reshape(T, K)).astype(BF16)
        out.append((o, st * jnp.exp2(b_end.reshape(1, K)) + _dot_tn(v, kd)))
    return out


def _hgrn_kernel(*refs, L, has_s0, out_state):
    it = iter(refs)
    q_ref, g_f_ref, g_b_ref, v_ref, gate_ref, nw_ref = (next(it) for _ in range(6))
    s0_ref = next(it) if has_s0 else None
    o_ref = next(it)
    st_ref = next(it) if out_state else None
    of_acc, ob_acc, st_scr, tri_ref, code_ref = (next(it) for _ in range(5))

    T = SCAN_BLOCK
    P = HG_PACK
    n = L // T

    @pl.when(jnp.logical_and(pl.program_id(0) == 0, pl.program_id(1) == 0))
    def _():
        ti = lax.broadcasted_iota(jnp.int32, (T, T), 0)
        si = lax.broadcasted_iota(jnp.int32, (T, T), 1)
        x = ti ^ si
        tri_ref[0] = (si <= ti).astype(BF16)
        tri_ref[1] = (si >= ti).astype(BF16)
        lvl = jnp.where(x < 8, 0, 3 + (x >= 16).astype(jnp.int32) + (x >= 32).astype(jnp.int32)
                        + (x >= 64).astype(jnp.int32))
        code_ref[0] = jnp.where(si <= ti, lvl, -1)
        code_ref[1] = jnp.where(si >= ti, lvl, -1)

    for h in range(P):
        for d in range(2):
            st_scr[2 * h + d] = s0_ref[d, h].T if has_s0 else jnp.zeros((HG_DV, HG_DK), F32)

    def step(i, finalize):
        sls = (pl.ds(pl.multiple_of(i * T, T), T), pl.ds(pl.multiple_of((n - 1 - i) * T, T), T))
        acc_sls = _half_acc_slices(i, n, T, finalize)
        chains = []
        for h in range(P):
            lanes = slice(h * HG_DK, (h + 1) * HG_DK)
            for d, g_ref in enumerate((g_f_ref, g_b_ref)):
                chains.append(dict(q=q_ref[sls[d], lanes].astype(F32), g=g_ref[sls[d], lanes],
                                   v=v_ref[sls[d], lanes],
                                   st=st_scr[2 * h + d], rev=bool(d)))
        res = _gla_blocks(chains, tri_ref, code_ref)
        for h in range(P):
            lanes = slice(h * HG_DK, (h + 1) * HG_DK)
            for d in range(2):
                o, st = res[2 * h + d]
                st_scr[2 * h + d] = st
                if finalize:
                    o = o + (ob_acc, of_acc)[d][acc_sls[d], lanes]
                    ms = jnp.mean(o * o, axis=-1, keepdims=True)
                    o = o * lax.rsqrt(ms + EPS) * nw_ref[:, lanes]
                    o_ref[sls[d], lanes] = (o * gate_ref[sls[d], lanes]).astype(BF16)
                else:
                    (of_acc, ob_acc)[d][acc_sls[d], lanes] = o

    def first_half(i, carry):
        step(i, False)
        return carry

    def second_half(i, carry):
        step(i, True)
        return carry

    lax.fori_loop(0, n // 2, first_half, 0, unroll=min(HG_UNROLL, n // 2))
    lax.fori_loop(n // 2, n, second_half, 0, unroll=min(HG_UNROLL, n // 2))
    if out_state:
        for h in range(P):
            for d in range(2):
                st_ref[d, h] = st_scr[2 * h + d].T


def _hgrn_call(hq, g, hi, hog, nw, s0, *, B, L, out_state, name):
    has_s0 = s0 is not None
    H = HG_HEADS
    P = HG_PACK
    W = P * HG_DK
    blk = lambda off: pl.BlockSpec((None, L, W), lambda b, h: (b, 0, off + h))
    st_spec = pl.BlockSpec((None, 2, P, HG_DK, HG_DV), lambda b, h: (b, 0, h, 0, 0))
    in_specs = [blk(0), blk(0), blk(H // P), blk(0), blk(0), pl.BlockSpec((1, W), lambda b, h: (0, h))]
    args = [hq, g, g, hi, hog, nw]
    if has_s0:
        in_specs.append(st_spec)
        args.append(s0)
    out_specs = [blk(0)]
    out_shape = [jax.ShapeDtypeStruct((B, L, H * HG_DV), BF16)]
    if out_state:
        out_specs.append(st_spec)
        out_shape.append(jax.ShapeDtypeStruct((B, 2, H, HG_DK, HG_DV), F32))
    return pl.pallas_call(
        functools.partial(_hgrn_kernel, L=L, has_s0=has_s0, out_state=out_state),
        grid=(B, H // P),
        in_specs=in_specs,
        out_specs=out_specs,
        out_shape=out_shape,
        scratch_shapes=[pltpu.VMEM((L // 2, W), F32), pltpu.VMEM((L // 2, W), F32),
                        pltpu.VMEM((2 * P, HG_DV, HG_DK), F32),
                        pltpu.VMEM((2, SCAN_BLOCK, SCAN_BLOCK), BF16),
                        pltpu.VMEM((2, SCAN_BLOCK, SCAN_BLOCK), jnp.int32)],
        compiler_params=_params(("arbitrary", "arbitrary")),
        name=name,
    )(*args)


def _merge_kernel(x_ref, mod_ref, oret_ref, ohg_ref, gr_ref, gh_ref, wr_ref, wh_ref, wo_ref, o_ref):
    y = gr_ref[...] * _dot(oret_ref[...], wr_ref[...]) + gh_ref[...] * _dot(ohg_ref[...], wh_ref[...])
    out = _dot(y.astype(BF16), wo_ref[...])
    o_ref[...] = x_ref[...] + mod_ref[5:6, :] * out


def _merge_call(x, mod3, oret, ohg, gr, gh, wr, wh, wo, *, mod_row_fn, tm, name):
    n = x.shape[0]
    row = pl.BlockSpec((tm, D_MODEL), lambda i: (i, 0))
    wspec = pl.BlockSpec((D_MODEL, D_MODEL), lambda i: (0, 0), pipeline_mode=pl.Buffered(1))
    return pl.pallas_call(
        _merge_kernel,
        grid=(n // tm,),
        in_specs=[row, pl.BlockSpec((None, N_MOD, D_MODEL), lambda i: (mod_row_fn(i, tm), 0, 0)),
                  row, row, row, row, wspec, wspec, wspec],
        out_specs=row,
        out_shape=jax.ShapeDtypeStruct((n, D_MODEL), F32),
        compiler_params=_params(("parallel",)),
        name=name,
    )(x, mod3, oret, ohg, gr, gh, wr, wh, wo)


def _rope_tables(n_tokens):
    pos = np.arange(n_tokens)
    r = (pos // GRID_W).astype(np.float32)
    cl = (pos % GRID_W).astype(np.float32)
    quarter = RET_DK // 4
    inv_freq = ROPE_BASE ** (-jnp.arange(quarter, dtype=F32) / quarter)
    ang_r = jnp.asarray(r)[:, None] * inv_freq[None, :]
    ang_c = jnp.asarray(cl)[:, None] * inv_freq[None, :]
    cos = jnp.concatenate([jnp.cos(ang_r)] * 2 + [jnp.cos(ang_c)] * 2, axis=-1)
    sins = jnp.concatenate([-jnp.sin(ang_r), jnp.sin(ang_r), -jnp.sin(ang_c), jnp.sin(ang_c)], axis=-1)
    return cos, sins


def _trunk(x, mod3, mod_row_fn, B, L, s0_ret, s0_hg, rope, out_state, wts, tag):
    (norm_w, final_w, ffn1_w13, ffn1_w2, ffn2_w13, ffn2_w2, w_in, dec, lb_logits, hg_nw,
     w_ret, w_hg, w_o, layer) = wts
    x = _ffn_call(x, mod3, norm_w, final_w, ffn1_w13, ffn1_w2, mod_base=0, norm_row=0, final=False,
                  mod_row_fn=mod_row_fn, tm=TM_FFN, name="ffn1_" + tag)
    qk, rv, rg, hq, g, hi, hog, gr, gh = _proj_call(x, mod3, norm_w, lb_logits, w_in,
                                                    _rope_tables(L) if rope else None, layer=layer,
                                                    mod_row_fn=mod_row_fn, tm=TM_PROJ, name="proj_" + tag)
    r3 = lambda t: t.reshape(B, L, t.shape[-1])
    ret = _ret_call(dec, r3(qk), r3(rv), r3(rg), s0_ret, B=B, L=L, out_state=out_state, name="ret_" + tag)
    hg = _hgrn_call(r3(hq), r3(g), r3(hi), r3(hog), hg_nw, s0_hg, B=B, L=L, out_state=out_state,
                    name="hgrn_" + tag)
    x = _merge_call(x, mod3, ret[0].reshape(B * L, -1), hg[0].reshape(B * L, -1), gr, gh, w_ret, w_hg, w_o,
                    mod_row_fn=mod_row_fn, tm=TM_MERGE, name="merge_" + tag)
    y = _ffn_call(x, mod3, norm_w, final_w, ffn2_w13, ffn2_w2, mod_base=6, norm_row=2, final=True,
                  mod_row_fn=mod_row_fn, tm=TM_FFN, name="ffn2_" + tag)
    if out_state:
        return y, ret[1], hg[1]
    return y, None, None


def kernel(x_prompt, x_sample, state_ret, state_hgrn, c, c_ctx, ada_w, ada_b, norm_w, ffn1_w13, ffn1_w2,
           ffn2_w13, ffn2_w2, w_in, ret_decay, hg_lb_logits, hg_norm_w, w_ret_proj, w_hg_proj, w_o,
           final_norm_w):
    depth = ada_w.shape[0]
    assert depth == 1
    l = 0
    bp, lp, _ = x_prompt.shape
    bs, ls, _ = x_sample.shape

    cvec = jnp.concatenate([c_ctx[None, :], c, jnp.zeros((8 - 1 - bs, D_MODEL), F32)], axis=0)
    mod = _mod_call(cvec, ada_w[l], ada_b[l][None, :])
    mod3 = mod.reshape(8, N_MOD, D_MODEL)

    bf = lambda w: w.astype(BF16)
    wts = (norm_w[l], final_norm_w[None, :], bf(ffn1_w13[l]), bf(ffn1_w2[l]), bf(ffn2_w13[l]), bf(ffn2_w2[l]),
           bf(w_in[l]), ret_decay[l], hg_lb_logits, hg_norm_w[l][None, :],
           bf(w_ret_proj[l]), bf(w_hg_proj[l]), bf(w_o[l]), l)

    yp, s_ret, s_hg = _trunk(x_prompt.reshape(bp * lp, D_MODEL), mod3, lambda i, tm: 0, bp, lp, None, None,
                             False, True, wts, "ctx")
    ys, _, _ = _trunk(x_sample.reshape(bs * ls, D_MODEL), mod3, lambda i, tm: 1 + (i * tm) // ls, bs, ls,
                      state_ret[:, l], state_hgrn[:, l], True, False, wts, "lat")
    return (yp.reshape(bp, lp, D_MODEL), ys.reshape(bs, ls, D_MODEL),
            s_ret[:, None], s_hg[:, None])
```

```python
import functools

import numpy as np
import jax
import jax.numpy as jnp
from jax import lax
from jax.experimental import pallas as pl
from jax.experimental.pallas import tpu as pltpu

F32 = jnp.float32
BF16 = jnp.bfloat16

D_MODEL = 1024
GRID_W = 64
RET_HEADS = 4
RET_DK = 128
RET_DV = 256
HG_HEADS = 8
HG_DK = 128
HG_DV = 128
D_FF = 2816
N_MOD = 9
ROPE_BASE = 10000.0
EPS = 1e-6

C_RQ, C_RK, C_RV, C_RG, C_HQ, C_HFF, C_HFB, C_HI, C_HOG, C_GR, C_GH = (
    0, 512, 1024, 2048, 3072, 4096, 5120, 6144, 7168, 8192, 9216)
IN_WIDTH = 10240

VMEM_LIMIT_BYTES = 56 * 1024 * 1024

TM_FFN = 1024
TM_PROJ = 256
TM_MERGE = 1024
SCAN_BLOCK = 128
RET_PACK = 4
RET_UNROLL = 1
HG_PACK = 4
HG_UNROLL = 2
FF_CHUNK = 256
PROJ_CHUNK = 256
LOG2E = 1.4426950408889634
EXP2_CLAMP = 115.0
HG_LEAF = 16
HG_LEVELS = ((4, 16), (5, 32), (6, 64))


def _dot(a, b):
    return jnp.dot(a, b, preferred_element_type=F32)


def _dot_nt(a, b):
    return lax.dot_general(a, b, (((1,), (1,)), ((), ())), preferred_element_type=F32)


def _dot_tn(a, b):
    return lax.dot_general(a, b, (((0,), (0,)), ((), ())), preferred_element_type=F32)


def _sigmoid(x):
    return 1.0 / (1.0 + jnp.exp(-x))


def _rms(x, w):
    ms = jnp.mean(x * x, axis=-1, keepdims=True)
    return x * lax.rsqrt(ms + EPS) * w


def _params(sem):
    return pltpu.CompilerParams(dimension_semantics=sem, vmem_limit_bytes=VMEM_LIMIT_BYTES)


def _mod_kernel(c_ref, w_ref, b_ref, o_ref):
    c = c_ref[...]
    s = (c * _sigmoid(c)).astype(BF16)
    o_ref[...] = _dot(s, w_ref[...].astype(BF16)) + b_ref[...]


def _mod_call(cvec, ada_w, ada_b):
    n = ada_w.shape[1]
    tn = 1024
    return pl.pallas_call(
        _mod_kernel,
        grid=(n // tn,),
        in_specs=[
            pl.BlockSpec((8, D_MODEL), lambda j: (0, 0)),
            pl.BlockSpec((D_MODEL, tn), lambda j: (0, j)),
            pl.BlockSpec((1, tn), lambda j: (0, j)),
        ],
        out_specs=pl.BlockSpec((8, tn), lambda j: (0, j)),
        out_shape=jax.ShapeDtypeStruct((8, n), F32),
        compiler_params=_params(("arbitrary",)),
        name="mod",
    )(cvec, ada_w, ada_b)


def _ffn_kernel(x_ref, mod_ref, nw_ref, fw_ref, w13_ref, w2_ref, o_ref, *, mod_base, norm_row, final):
    y = _rms(x_ref[...], nw_ref[norm_row:norm_row + 1, :])
    sh = mod_ref[mod_base:mod_base + 1, :]
    sc = mod_ref[mod_base + 1:mod_base + 2, :]
    h = (y * (1.0 + sc) + sh).astype(BF16)
    acc = None
    for c0 in range(0, D_FF, FF_CHUNK):
        a = _dot(h, w13_ref[:, c0:c0 + FF_CHUNK])
        b = _dot(h, w13_ref[:, D_FF + c0:D_FF + c0 + FF_CHUNK])
        act = (a * _sigmoid(a) * b).astype(BF16)
        part = _dot(act, w2_ref[c0:c0 + FF_CHUNK, :])
        acc = part if acc is None else acc + part
    g = mod_ref[mod_base + 2:mod_base + 3, :]
    xo = x_ref[...] + 0.5 * g * acc
    if final:
        xo = _rms(xo, fw_ref[...])
    o_ref[...] = xo


def _ffn_call(x, mod3, norm_w, final_w, w13, w2, *, mod_base, norm_row, final, mod_row_fn, tm, name):
    n = x.shape[0]
    kern = functools.partial(_ffn_kernel, mod_base=mod_base, norm_row=norm_row, final=final)
    const = lambda shape: pl.BlockSpec(shape, lambda i: (0, 0), pipeline_mode=pl.Buffered(1))
    return pl.pallas_call(
        kern,
        grid=(n // tm,),
        in_specs=[
            pl.BlockSpec((tm, D_MODEL), lambda i: (i, 0)),
            pl.BlockSpec((None, N_MOD, D_MODEL), lambda i: (mod_row_fn(i, tm), 0, 0)),
            pl.BlockSpec((3, D_MODEL), lambda i: (0, 0)),
            pl.BlockSpec((1, D_MODEL), lambda i: (0, 0)),
            const((D_MODEL, 2 * D_FF)),
            const((D_FF, D_MODEL)),
        ],
        out_specs=pl.BlockSpec((tm, D_MODEL), lambda i: (i, 0)),
        out_shape=jax.ShapeDtypeStruct((n, D_MODEL), F32),
        compiler_params=_params(("parallel",)),
        name=name,
    )(x, mod3, norm_w, final_w, w13, w2)


def _rope(x, cos, sins, first_quarter):
    partner = jnp.where(first_quarter, pltpu.roll(x, 96, 1), pltpu.roll(x, 32, 1))
    return x * cos + partner * sins


def _proj_kernel(*refs, layer, rope):
    it = iter(refs)
    x_ref, mod_ref, nw_ref, lbl_ref, w_ref = (next(it) for _ in range(5))
    cos_ref, sin_ref = (next(it), next(it)) if rope else (None, None)
    qk_ref, rv_ref, rg_ref, hq_ref, g_ref, hi_ref, hog_ref, gr_ref, gh_ref = (next(it) for _ in range(9))
    y = _rms(x_ref[...], nw_ref[1:2, :])
    h = (y * (1.0 + mod_ref[4:5, :]) + mod_ref[3:4, :]).astype(BF16)

    if rope:
        lane = lax.broadcasted_iota(jnp.int32, (x_ref.shape[0], RET_DK), 1)
        first_quarter = (lane & 32) == 0

    def rot(t):
        if not rope:
            return t
        heads = [_rope(t[:, j:j + RET_DK], cos_ref[...], sin_ref[...], first_quarter)
                 for j in range(0, t.shape[1], RET_DK)]
        return jnp.concatenate(heads, axis=1)

    def silu(t):
        return t * _sigmoid(t)

    def log_forget(d):
        def fn(t, c):
            lg = lbl_ref[d, :, c:c + PROJ_CHUNK]
            e = jnp.exp(lg - jnp.max(lg, axis=0, keepdims=True))
            lb = jnp.sum(e[0:layer + 1, :], axis=0, keepdims=True) / jnp.sum(e, axis=0, keepdims=True)
            return jnp.log(lb + (1.0 - lb) * _sigmoid(t)) * LOG2E
        return fn

    parts = (
        (C_RQ, 512, qk_ref, 0, lambda t, c: rot(t)),
        (C_RK, 512, qk_ref, 512, lambda t, c: rot(t * (RET_DK ** -0.5))),
        (C_RV, 1024, rv_ref, 0, lambda t, c: t),
        (C_RG, 1024, rg_ref, 0, lambda t, c: silu(t)),
        (C_HQ, 1024, hq_ref, 0, lambda t, c: silu(t) * (HG_DK ** -0.5)),
        (C_HFF, 1024, g_ref, 0, log_forget(0)),
        (C_HFB, 1024, g_ref, 1024, log_forget(1)),
        (C_HI, 1024, hi_ref, 0, lambda t, c: t),
        (C_HOG, 1024, hog_ref, 0, lambda t, c: silu(t)),
        (C_GR, 1024, gr_ref, 0, lambda t, c: _sigmoid(t)),
        (C_GH, 1024, gh_ref, 0, lambda t, c: _sigmoid(t)),
    )
    for w0, width, o_ref, o0, fn in parts:
        for c in range(0, width, PROJ_CHUNK):
            t = _dot(h, w_ref[:, w0 + c:w0 + c + PROJ_CHUNK])
            o_ref[:, o0 + c:o0 + c + PROJ_CHUNK] = fn(t, c).astype(o_ref.dtype)


def _proj_call(x, mod3, norm_w, lb_logits, w_in, rope_tabs, *, layer, mod_row_fn, tm, name):
    n = x.shape[0]
    slots = lb_logits.shape[1]
    row = lambda w: pl.BlockSpec((tm, w), lambda i: (i, 0))
    outs = [(1024, BF16), (1024, BF16), (1024, BF16), (1024, BF16), (2048, F32),
            (1024, BF16), (1024, BF16), (1024, BF16), (1024, BF16)]
    in_specs = [
        row(D_MODEL),
        pl.BlockSpec((None, N_MOD, D_MODEL), lambda i: (mod_row_fn(i, tm), 0, 0)),
        pl.BlockSpec((3, D_MODEL), lambda i: (0, 0)),
        pl.BlockSpec((2, slots, 1024), lambda i: (0, 0, 0)),
        pl.BlockSpec((D_MODEL, IN_WIDTH), lambda i: (0, 0), pipeline_mode=pl.Buffered(1)),
    ]
    args = [x, mod3, norm_w, lb_logits, w_in]
    if rope_tabs is not None:
        tiles_per_seq = rope_tabs[0].shape[0] // tm
        in_specs += [pl.BlockSpec((tm, RET_DK), lambda i: (i % tiles_per_seq, 0))] * 2
        args += list(rope_tabs)
    return pl.pallas_call(
        functools.partial(_proj_kernel, layer=layer, rope=rope_tabs is not None),
        grid=(n // tm,),
        in_specs=in_specs,
        out_specs=[row(w) for w, _ in outs],
        out_shape=[jax.ShapeDtypeStruct((n, w), dt) for w, dt in outs],
        compiler_params=_params(("parallel",)),
        name=name,
    )(*args)


def _log_sigmoid(x):
    return -(jnp.maximum(-x, 0.0) + jnp.log1p(jnp.exp(-jnp.abs(x))))


def _half_acc_slices(i, n, blk, finalize):
    if finalize:
        starts = ((i - n // 2) * blk, (n - 1 - i) * blk)
    else:
        starts = (i * blk, (n // 2 - 1 - i) * blk)
    return tuple(pl.ds(pl.multiple_of(s, blk), blk) for s in starts)


def _ret_kernel(*refs, L, has_s0, out_state):
    it = iter(refs)
    dec_ref = next(it)
    q_ref, k_ref, v_ref, gate_ref = next(it), next(it), next(it), next(it)
    s0_ref = next(it) if has_s0 else None
    o_ref = next(it)
    st_ref = next(it) if out_state else None
    of_acc, ob_acc, st_scr, cst = next(it), next(it), next(it), next(it)

    C = SCAN_BLOCK
    P = RET_PACK
    n = L // C
    pair = pl.program_id(1)
    ii = lax.broadcasted_iota(jnp.int32, (C, C), 0).astype(F32)
    jj = lax.broadcasted_iota(jnp.int32, (C, C), 1).astype(F32)
    diff = ii - jj

    @pl.when(jnp.logical_and(pl.program_id(0) == 0, pair == 0))
    def _():
        for hd in range(RET_HEADS):
            lgf = _log_sigmoid(jnp.full((C, C), dec_ref[0, hd], F32))
            lgb = _log_sigmoid(jnp.full((C, C), dec_ref[1, hd], F32))
            cst[hd, 0] = (jnp.where(diff >= 0, jnp.exp(jnp.maximum(diff, 0.0) * lgf), 0.0)
                          + jnp.where(diff <= 0, jnp.exp(jnp.maximum(-diff, 0.0) * lgb), 0.0))
            cst[hd, 1] = jnp.exp((ii + 1.0) * lgf)
            cst[hd, 2] = jnp.exp((C - 1.0 - ii) * lgf)
            cst[hd, 3] = jnp.exp((C - ii) * lgb)
            cst[hd, 4] = jnp.exp(ii * lgb)

    cds = []
    for h in range(P):
        hd = pair * P + h
        cds.append([jnp.exp(C * _log_sigmoid(jnp.full((1, RET_DV), dec_ref[d, hd], F32))) for d in range(2)])
        for d in range(2):
            st_scr[2 * h + d] = s0_ref[d, h] if has_s0 else jnp.zeros((RET_DK, RET_DV), F32)

    def step(i, finalize):
        sls = (pl.ds(pl.multiple_of(i * C, C), C), pl.ds(pl.multiple_of((n - 1 - i) * C, C), C))
        acc_sls = _half_acc_slices(i, n, C, finalize)
        chains = []
        for h in range(P):
            ql = slice(h * RET_DK, (h + 1) * RET_DK)
            vl = slice(h * RET_DV, (h + 1) * RET_DV)
            for d in range(2):
                chains.append(dict(h=h, d=d, vl=vl, q=q_ref[sls[d], ql], k=k_ref[sls[d], ql],
                                   v=v_ref[sls[d], vl]))
        for c in chains:
            if c["d"] == 0:
                c["s"] = _dot_nt(c["q"], c["k"])
        for c in chains:
            h, d, v, vl = c["h"], c["d"], c["v"], c["vl"]
            hd = pair * P + h
            st = st_scr[2 * h + d]
            qd = (c["q"].astype(F32) * cst[hd, 1 + 2 * d]).astype(BF16)
            if d == 0:
                p = (c["s"] * cst[hd, 0]).astype(BF16)
                o = _dot(jnp.concatenate([p, qd], axis=1), jnp.concatenate([v, st.astype(BF16)], axis=0))
            else:
                o = _dot(qd, st.astype(BF16))
            kd = (c["k"].astype(F32) * cst[hd, 2 + 2 * d]).astype(BF16)
            st_scr[2 * h + d] = cds[h][d] * st + _dot_tn(kd, v)
            if finalize:
                o = o + (ob_acc, of_acc)[d][acc_sls[d], vl]
                ms = jnp.mean(o * o, axis=-1, keepdims=True)
                o_ref[sls[d], vl] = (o * lax.rsqrt(ms + EPS) * gate_ref[sls[d], vl]).astype(BF16)
            else:
                (of_acc, ob_acc)[d][acc_sls[d], vl] = o

    def first_half(i, carry):
        step(i, False)
        return carry

    def second_half(i, carry):
        step(i, True)
        return carry

    lax.fori_loop(0, n // 2, first_half, 0, unroll=min(RET_UNROLL, n // 2))
    lax.fori_loop(n // 2, n, second_half, 0, unroll=min(RET_UNROLL, n // 2))
    if out_state:
        for h in range(P):
            for d in range(2):
                st_ref[d, h] = st_scr[2 * h + d]


def _ret_call(dec, qk, rv, rg, s0, *, B, L, out_state, name):
    has_s0 = s0 is not None
    H = RET_HEADS
    P = RET_PACK
    st_spec = pl.BlockSpec((None, 2, P, RET_DK, RET_DV), lambda b, h: (b, 0, h, 0, 0))
    in_specs = [
        pl.BlockSpec(memory_space=pltpu.SMEM),
        pl.BlockSpec((None, L, P * RET_DK), lambda b, h: (b, 0, h)),
        pl.BlockSpec((None, L, P * RET_DK), lambda b, h: (b, 0, H // P + h)),
        pl.BlockSpec((None, L, P * RET_DV), lambda b, h: (b, 0, h)),
        pl.BlockSpec((None, L, P * RET_DV), lambda b, h: (b, 0, h)),
    ]
    args = [dec, qk, qk, rv, rg]
    if has_s0:
        in_specs.append(st_spec)
        args.append(s0)
    out_specs = [pl.BlockSpec((None, L, P * RET_DV), lambda b, h: (b, 0, h))]
    out_shape = [jax.ShapeDtypeStruct((B, L, H * RET_DV), BF16)]
    if out_state:
        out_specs.append(st_spec)
        out_shape.append(jax.ShapeDtypeStruct((B, 2, H, RET_DK, RET_DV), F32))
    res = pl.pallas_call(
        functools.partial(_ret_kernel, L=L, has_s0=has_s0, out_state=out_state),
        grid=(B, H // P),
        in_specs=in_specs,
        out_specs=out_specs,
        out_shape=out_shape,
        scratch_shapes=[pltpu.VMEM((L // 2, P * RET_DV), F32), pltpu.VMEM((L // 2, P * RET_DV), F32),
                        pltpu.VMEM((2 * P, RET_DK, RET_DV), F32),
                        pltpu.VMEM((H, 5, SCAN_BLOCK, SCAN_BLOCK), F32)],
        compiler_params=_params(("arbitrary", "arbitrary")),
        name=name,
    )(*args)
    return res


def _gla_blocks(chains, tri_ref, code_ref):
    T = SCAN_BLOCK
    K = HG_DK
    for c in chains:
        g2 = c["g"]
        c["k"] = 1.0 - jnp.exp2(g2)
        hi = g2.astype(BF16)
        lo = (g2 - hi.astype(F32)).astype(BF16)
        tri = tri_ref[int(c["rev"])]
        c["b"] = _dot(tri, hi) + _dot(tri, lo)

    for c in chains:
        bl = c["b"].reshape(T // HG_LEAF, HG_LEAF, K)
        r = HG_LEAF // 2 if c["rev"] else HG_LEAF // 2 - 1
        d = jnp.clip(bl - bl[:, r:r + 1, :], -EXP2_CLAMP, EXP2_CLAMP).reshape(T, K)
        e = jnp.exp2(d)
        p = _dot_nt((c["q"] * e).astype(BF16), (c["k"] / e).astype(BF16))
        c["a"] = jnp.where(code_ref[int(c["rev"])] == 0, p, 0.0)

    for lv, m in HG_LEVELS:
        seg = 2 * m
        ng = T // seg
        for c in chains:
            b3, q3, k3 = (c[n].reshape(ng, seg, K) for n in ("b", "q", "k"))
            if c["rev"]:
                qs, ks, r = slice(0, m), slice(m, seg), m
            else:
                qs, ks, r = slice(m, seg), slice(0, m), m - 1
            ref = b3[:, r:r + 1, :]
            qm = (q3[:, qs, :] * jnp.exp2(b3[:, qs, :] - ref)).reshape(ng * m, K).astype(BF16)
            kz = k3[:, ks, :] * jnp.exp2(ref - b3[:, ks, :])
            zeros = jnp.zeros((ng, m, K), F32)
            kfull = jnp.concatenate([zeros, kz] if c["rev"] else [kz, zeros], axis=1)
            p = _dot_nt(qm, kfull.reshape(T, K).astype(BF16)).reshape(ng, m, T)
            a3 = c["a"].reshape(ng, seg, T)
            c3 = code_ref[int(c["rev"])].reshape(ng, seg, T)
            upd = jnp.where(c3[:, qs, :] == lv, p, a3[:, qs, :])
            parts = [upd, a3[:, ks, :]] if c["rev"] else [a3[:, ks, :], upd]
            c["a"] = jnp.concatenate(parts, axis=1).reshape(T, T)

    out = []
    for c in chains:
        b, q, k, v, st = c["b"], c["q"], c["k"], c["v"], c["st"]
        o = _dot(c["a"].astype(BF16), v) + _dot_nt((q * jnp.exp2(b)).astype(BF16), st.astype(BF16))
        end_row = 0 if c["rev"] else T - 1
        b1 = b.reshape(1, T, K)
        b_end = b1[:, end_row:end_row + 1, :]
        kd = (k * jnp.exp2(b_end - b1).reshape(T, K)).astype(BF16)
        out.append((o, st * jnp.exp2(b_end.reshape(1, K)) + _dot_tn(v, kd)))
    return out


def _hgrn_kernel(*refs, L, has_s0, out_state):
    it = iter(refs)
    q_ref, g_f_ref, g_b_ref, v_ref, gate_ref, nw_ref = (next(it) for _ in range(6))
    s0_ref = next(it) if has_s0 else None
    o_ref = next(it)
    st_ref = next(it) if out_state else None
    of_acc, ob_acc, st_scr, tri_ref, code_ref = (next(it) for _ in range(5))

    T = SCAN_BLOCK
    P = HG_PACK
    n = L // T

    @pl.when(jnp.logical_and(pl.program_id(0) == 0, pl.program_id(1) == 0))
    def _():
        ti = lax.broadcasted_iota(jnp.int32, (T, T), 0)
        si = lax.broadcasted_iota(jnp.int32, (T, T), 1)
        x = ti ^ si
        tri_ref[0] = (si <= ti).astype(BF16)
        tri_ref[1] = (si >= ti).astype(BF16)
        lvl = jnp.where(x < HG_LEAF, 0, 3 + (x >= 16).astype(jnp.int32) + (x >= 32).astype(jnp.int32)
                        + (x >= 64).astype(jnp.int32))
        code_ref[0] = jnp.where(si <= ti, lvl, -1)
        code_ref[1] = jnp.where(si >= ti, lvl, -1)

    for h in range(P):
        for d in range(2):
            st_scr[2 * h + d] = s0_ref[d, h].T if has_s0 else jnp.zeros((HG_DV, HG_DK), F32)

    def step(i, finalize):
        sls = (pl.ds(pl.multiple_of(i * T, T), T), pl.ds(pl.multiple_of((n - 1 - i) * T, T), T))
        acc_sls = _half_acc_slices(i, n, T, finalize)
        chains = []
        for h in range(P):
            lanes = slice(h * HG_DK, (h + 1) * HG_DK)
            for d, g_ref in enumerate((g_f_ref, g_b_ref)):
                chains.append(dict(q=q_ref[sls[d], lanes].astype(F32), g=g_ref[sls[d], lanes],
                                   v=v_ref[sls[d], lanes],
                                   st=st_scr[2 * h + d], rev=bool(d)))
        res = _gla_blocks(chains, tri_ref, code_ref)
        for h in range(P):
            lanes = slice(h * HG_DK, (h + 1) * HG_DK)
            for d in range(2):
                o, st = res[2 * h + d]
                st_scr[2 * h + d] = st
                if finalize:
                    o = o + (ob_acc, of_acc)[d][acc_sls[d], lanes]
                    ms = jnp.mean(o * o, axis=-1, keepdims=True)
                    o = o * lax.rsqrt(ms + EPS) * nw_ref[:, lanes]
                    o_ref[sls[d], lanes] = (o * gate_ref[sls[d], lanes]).astype(BF16)
                else:
                    (of_acc, ob_acc)[d][acc_sls[d], lanes] = o

    def first_half(i, carry):
        step(i, False)
        return carry

    def second_half(i, carry):
        step(i, True)
        return carry

    lax.fori_loop(0, n // 2, first_half, 0, unroll=min(HG_UNROLL, n // 2))
    lax.fori_loop(n // 2, n, second_half, 0, unroll=min(HG_UNROLL, n // 2))
    if out_state:
        for h in range(P):
            for d in range(2):
                st_ref[d, h] = st_scr[2 * h + d].T


def _hgrn_call(hq, g, hi, hog, nw, s0, *, B, L, out_state, name):
    has_s0 = s0 is not None
    H = HG_HEADS
    P = HG_PACK
    W = P * HG_DK
    blk = lambda off: pl.BlockSpec((None, L, W), lambda b, h: (b, 0, off + h))
    st_spec = pl.BlockSpec((None, 2, P, HG_DK, HG_DV), lambda b, h: (b, 0, h, 0, 0))
    in_specs = [blk(0), blk(0), blk(H // P), blk(0), blk(0), pl.BlockSpec((1, W), lambda b, h: (0, h))]
    args = [hq, g, g, hi, hog, nw]
    if has_s0:
        in_specs.append(st_spec)
        args.append(s0)
    out_specs = [blk(0)]
    out_shape = [jax.ShapeDtypeStruct((B, L, H * HG_DV), BF16)]
    if out_state:
        out_specs.append(st_spec)
        out_shape.append(jax.ShapeDtypeStruct((B, 2, H, HG_DK, HG_DV), F32))
    return pl.pallas_call(
        functools.partial(_hgrn_kernel, L=L, has_s0=has_s0, out_state=out_state),
        grid=(B, H // P),
        in_specs=in_specs,
        out_specs=out_specs,
        out_shape=out_shape,
        scratch_shapes=[pltpu.VMEM((L // 2, W), F32), pltpu.VMEM((L // 2, W), F32),
                        pltpu.VMEM((2 * P, HG_DV, HG_DK), F32),
                        pltpu.VMEM((2, SCAN_BLOCK, SCAN_BLOCK), BF16),
                        pltpu.VMEM((2, SCAN_BLOCK, SCAN_BLOCK), jnp.int32)],
        compiler_params=_params(("arbitrary", "arbitrary")),
        name=name,
    )(*args)


def _merge_kernel(x_ref, mod_ref, oret_ref, ohg_ref, gr_ref, gh_ref, wr_ref, wh_ref, wo_ref, o_ref):
    y = gr_ref[...] * _dot(oret_ref[...], wr_ref[...]) + gh_ref[...] * _dot(ohg_ref[...], wh_ref[...])
    out = _dot(y.astype(BF16), wo_ref[...])
    o_ref[...] = x_ref[...] + mod_ref[5:6, :] * out


def _merge_call(x, mod3, oret, ohg, gr, gh, wr, wh, wo, *, mod_row_fn, tm, name):
    n = x.shape[0]
    row = pl.BlockSpec((tm, D_MODEL), lambda i: (i, 0))
    wspec = pl.BlockSpec((D_MODEL, D_MODEL), lambda i: (0, 0), pipeline_mode=pl.Buffered(1))
    return pl.pallas_call(
        _merge_kernel,
        grid=(n // tm,),
        in_specs=[row, pl.BlockSpec((None, N_MOD, D_MODEL), lambda i: (mod_row_fn(i, tm), 0, 0)),
                  row, row, row, row, wspec, wspec, wspec],
        out_specs=row,
        out_shape=jax.ShapeDtypeStruct((n, D_MODEL), F32),
        compiler_params=_params(("parallel",)),
        name=name,
    )(x, mod3, oret, ohg, gr, gh, wr, wh, wo)


def _rope_tables(n_tokens):
    pos = np.arange(n_tokens)
    r = (pos // GRID_W).astype(np.float32)
    cl = (pos % GRID_W).astype(np.float32)
    quarter = RET_DK // 4
    inv_freq = ROPE_BASE ** (-jnp.arange(quarter, dtype=F32) / quarter)
    ang_r = jnp.asarray(r)[:, None] * inv_freq[None, :]
    ang_c = jnp.asarray(cl)[:, None] * inv_freq[None, :]
    cos = jnp.concatenate([jnp.cos(ang_r)] * 2 + [jnp.cos(ang_c)] * 2, axis=-1)
    sins = jnp.concatenate([-jnp.sin(ang_r), jnp.sin(ang_r), -jnp.sin(ang_c), jnp.sin(ang_c)], axis=-1)
    return cos, sins


def _trunk(x, mod3, mod_row_fn, B, L, s0_ret, s0_hg, rope, out_state, wts, tag):
    (norm_w, final_w, ffn1_w13, ffn1_w2, ffn2_w13, ffn2_w2, w_in, dec, lb_logits, hg_nw,
     w_ret, w_hg, w_o, layer) = wts
    x = _ffn_call(x, mod3, norm_w, final_w, ffn1_w13, ffn1_w2, mod_base=0, norm_row=0, final=False,
                  mod_row_fn=mod_row_fn, tm=TM_FFN, name="ffn1_" + tag)
    qk, rv, rg, hq, g, hi, hog, gr, gh = _proj_call(x, mod3, norm_w, lb_logits, w_in,
                                                    _rope_tables(L) if rope else None, layer=layer,
                                                    mod_row_fn=mod_row_fn, tm=TM_PROJ, name="proj_" + tag)
    r3 = lambda t: t.reshape(B, L, t.shape[-1])
    ret = _ret_call(dec, r3(qk), r3(rv), r3(rg), s0_ret, B=B, L=L, out_state=out_state, name="ret_" + tag)
    hg = _hgrn_call(r3(hq), r3(g), r3(hi), r3(hog), hg_nw, s0_hg, B=B, L=L, out_state=out_state,
                    name="hgrn_" + tag)
    x = _merge_call(x, mod3, ret[0].reshape(B * L, -1), hg[0].reshape(B * L, -1), gr, gh, w_ret, w_hg, w_o,
                    mod_row_fn=mod_row_fn, tm=TM_MERGE, name="merge_" + tag)
    y = _ffn_call(x, mod3, norm_w, final_w, ffn2_w13, ffn2_w2, mod_base=6, norm_row=2, final=True,
                  mod_row_fn=mod_row_fn, tm=TM_FFN, name="ffn2_" + tag)
    if out_state:
        return y, ret[1], hg[1]
    return y, None, None


def kernel(x_prompt, x_sample, state_ret, state_hgrn, c, c_ctx, ada_w, ada_b, norm_w, ffn1_w13, ffn1_w2,
           ffn2_w13, ffn2_w2, w_in, ret_decay, hg_lb_logits, hg_norm_w, w_ret_proj, w_hg_proj, w_o,
           final_norm_w):
    depth = ada_w.shape[0]
    assert depth == 1
    l = 0
    bp, lp, _ = x_prompt.shape
    bs, ls, _ = x_sample.shape

    cvec = jnp.concatenate([c_ctx[None, :], c, jnp.zeros((8 - 1 - bs, D_MODEL), F32)], axis=0)
    mod = _mod_call(cvec, ada_w[l], ada_b[l][None, :])
    mod3 = mod.reshape(8, N_MOD, D_MODEL)

    bf = lambda w: w.astype(BF16)
    wts = (norm_w[l], final_norm_w[None, :], bf(ffn1_w13[l]), bf(ffn1_w2[l]), bf(ffn2_w13[l]), bf(ffn2_w2[l]),
           bf(w_in[l]), ret_decay[l], hg_lb_logits, hg_norm_w[l][None, :],
           bf(w_ret_proj[l]), bf(w_hg_proj[l]), bf(w_o[l]), l)

    yp, s_ret, s_hg = _trunk(x_prompt.reshape(bp * lp, D_MODEL), mod3, lambda i, tm: 0, bp, lp, None, None,
                             False, True, wts, "ctx")
    ys, _, _ = _trunk(x_sample.reshape(bs * ls, D_MODEL), mod3, lambda i, tm: 1 + (i * tm) // ls, bs, ls,
                      state_ret[:, l], state_hgrn[:, l], True, False, wts, "lat")
    return (yp.reshape(bp, lp, D_MODEL), ys.reshape(bs, ls, D_MODEL),
            s_ret[:, None], s_hg[:, None])
```

```python
import functools

import numpy as np
import jax
import jax.numpy as jnp
from jax import lax
from jax.experimental import pallas as pl
from jax.experimental.pallas import tpu as pltpu

F32 = jnp.float32
BF16 = jnp.bfloat16

D_MODEL = 1024
GRID_W = 64
RET_HEADS = 4
RET_DK = 128
RET_DV = 256
HG_HEADS = 8
HG_DK = 128
HG_DV = 128
D_FF = 2816
N_MOD = 9
ROPE_BASE = 10000.0
EPS = 1e-6

C_RQ, C_RK, C_RV, C_RG, C_HQ, C_HFF, C_HFB, C_HI, C_HOG, C_GR, C_GH = (
    0, 512, 1024, 2048, 3072, 4096, 5120, 6144, 7168, 8192, 9216)
IN_WIDTH = 10240

VMEM_LIMIT_BYTES = 56 * 1024 * 1024

TM_FFN = 1024
TM_PROJ = 256
TM_MERGE = 1024
SCAN_BLOCK = 128
RET_PACK = 4
RET_UNROLL = 1
HG_PACK = 4
HG_UNROLL = 2
FF_CHUNK = 256
PROJ_CHUNK = 256
LOG2E = 1.4426950408889634
EXP2_CLAMP = 115.0
HG_LEAF = 16
HG_LEVELS = ((4, 16), (5, 32), (6, 64))


def _dot(a, b):
    return jnp.dot(a, b, preferred_element_type=F32)


def _dot_nt(a, b):
    return lax.dot_general(a, b, (((1,), (1,)), ((), ())), preferred_element_type=F32)


def _dot_tn(a, b):
    return lax.dot_general(a, b, (((0,), (0,)), ((), ())), preferred_element_type=F32)


def _sigmoid(x):
    return 1.0 / (1.0 + jnp.exp(-x))


def _rms(x, w):
    ms = jnp.mean(x * x, axis=-1, keepdims=True)
    return x * lax.rsqrt(ms + EPS) * w


def _params(sem):
    return pltpu.CompilerParams(dimension_semantics=sem, vmem_limit_bytes=VMEM_LIMIT_BYTES)


def _mod_kernel(c_ref, w_ref, b_ref, o_ref):
    c = c_ref[...]
    s = (c * _sigmoid(c)).astype(BF16)
    o_ref[...] = _dot(s, w_ref[...].astype(BF16)) + b_ref[...]


def _mod_call(cvec, ada_w, ada_b):
    n = ada_w.shape[1]
    tn = 1024
    return pl.pallas_call(
        _mod_kernel,
        grid=(n // tn,),
        in_specs=[
            pl.BlockSpec((8, D_MODEL), lambda j: (0, 0)),
            pl.BlockSpec((D_MODEL, tn), lambda j: (0, j)),
            pl.BlockSpec((1, tn), lambda j: (0, j)),
        ],
        out_specs=pl.BlockSpec((8, tn), lambda j: (0, j)),
        out_shape=jax.ShapeDtypeStruct((8, n), F32),
        compiler_params=_params(("arbitrary",)),
        name="mod",
    )(cvec, ada_w, ada_b)


def _ffn_kernel(*refs, mod_base, norm_row, final, n_first, two_in, two_out):
    it = iter(refs)
    xa_ref = next(it)
    xb_ref = next(it) if two_in else None
    mod_ref, nw_ref, fw_ref, w13_ref, w2_ref = (next(it) for _ in range(5))
    oa_ref = next(it)
    ob_ref = next(it) if two_out else None
    i = pl.program_id(0)
    x = xa_ref[...]
    if two_in:
        x = jnp.where(i < n_first, x, xb_ref[...])
    y = _rms(x, nw_ref[norm_row:norm_row + 1, :])
    sh = mod_ref[mod_base:mod_base + 1, :]
    sc = mod_ref[mod_base + 1:mod_base + 2, :]
    h = (y * (1.0 + sc) + sh).astype(BF16)
    acc = None
    for c0 in range(0, D_FF, FF_CHUNK):
        a = _dot(h, w13_ref[:, c0:c0 + FF_CHUNK])
        b = _dot(h, w13_ref[:, D_FF + c0:D_FF + c0 + FF_CHUNK])
        act = (a * _sigmoid(a) * b).astype(BF16)
        part = _dot(act, w2_ref[c0:c0 + FF_CHUNK, :])
        acc = part if acc is None else acc + part
    g = mod_ref[mod_base + 2:mod_base + 3, :]
    xo = x + 0.5 * g * acc
    if final:
        xo = _rms(xo, fw_ref[...])
    if two_out:
        @pl.when(i < n_first)
        def _():
            oa_ref[...] = xo

        @pl.when(i >= n_first)
        def _():
            ob_ref[...] = xo
    else:
        oa_ref[...] = xo


def _ffn_call(xs, mod3, norm_w, final_w, w13, w2, *, n_out_split, mod_base, norm_row, final, mod_row_fn, tm,
              name):
    two_in = len(xs) == 2
    two_out = n_out_split is not None
    n = sum(x.shape[0] for x in xs)
    n_first = (xs[0].shape[0] if two_in else n_out_split if two_out else n) // tm
    first = pl.BlockSpec((tm, D_MODEL), lambda i: (jnp.minimum(i, n_first - 1), 0))
    second = pl.BlockSpec((tm, D_MODEL), lambda i: (jnp.maximum(i - n_first, 0), 0))
    whole = pl.BlockSpec((tm, D_MODEL), lambda i: (i, 0))
    kern = functools.partial(_ffn_kernel, mod_base=mod_base, norm_row=norm_row, final=final, n_first=n_first,
                             two_in=two_in, two_out=two_out)
    const = lambda shape: pl.BlockSpec(shape, lambda i: (0, 0), pipeline_mode=pl.Buffered(1))
    if two_out:
        out_specs = [first, second]
        out_shape = [jax.ShapeDtypeStruct((n_out_split, D_MODEL), F32),
                     jax.ShapeDtypeStruct((n - n_out_split, D_MODEL), F32)]
    else:
        out_specs = whole
        out_shape = jax.ShapeDtypeStruct((n, D_MODEL), F32)
    return pl.pallas_call(
        kern,
        grid=(n // tm,),
        in_specs=([first, second] if two_in else [whole]) + [
            pl.BlockSpec((None, N_MOD, D_MODEL), lambda i: (mod_row_fn(i, tm), 0, 0)),
            pl.BlockSpec((3, D_MODEL), lambda i: (0, 0)),
            pl.BlockSpec((1, D_MODEL), lambda i: (0, 0)),
            const((D_MODEL, 2 * D_FF)),
            const((D_FF, D_MODEL)),
        ],
        out_specs=out_specs,
        out_shape=out_shape,
        compiler_params=_params(("arbitrary",)),
        name=name,
    )(*xs, mod3, norm_w, final_w, w13, w2)


def _rope(x, cos, sins, first_quarter):
    partner = jnp.where(first_quarter, pltpu.roll(x, 96, 1), pltpu.roll(x, 32, 1))
    return x * cos + partner * sins


def _proj_kernel(*refs, layer, rope):
    it = iter(refs)
    x_ref, mod_ref, nw_ref, lbl_ref, w_ref = (next(it) for _ in range(5))
    cos_ref, sin_ref = (next(it), next(it)) if rope else (None, None)
    qk_ref, rv_ref, rg_ref, hq_ref, g_ref, hi_ref, hog_ref, gr_ref, gh_ref = (next(it) for _ in range(9))
    y = _rms(x_ref[...], nw_ref[1:2, :])
    h = (y * (1.0 + mod_ref[4:5, :]) + mod_ref[3:4, :]).astype(BF16)

    if rope:
        lane = lax.broadcasted_iota(jnp.int32, (x_ref.shape[0], RET_DK), 1)
        first_quarter = (lane & 32) == 0

    def rot(t):
        if not rope:
            return t
        heads = [_rope(t[:, j:j + RET_DK], cos_ref[...], sin_ref[...], first_quarter)
                 for j in range(0, t.shape[1], RET_DK)]
        return jnp.concatenate(heads, axis=1)

    def silu(t):
        return t * _sigmoid(t)

    def log_forget(d):
        def fn(t, c):
            lg = lbl_ref[d, :, c:c + PROJ_CHUNK]
            e = jnp.exp(lg - jnp.max(lg, axis=0, keepdims=True))
            lb = jnp.sum(e[0:layer + 1, :], axis=0, keepdims=True) / jnp.sum(e, axis=0, keepdims=True)
            return jnp.log(lb + (1.0 - lb) * _sigmoid(t)) * LOG2E
        return fn

    parts = (
        (C_RQ, 512, qk_ref, 0, lambda t, c: rot(t)),
        (C_RK, 512, qk_ref, 512, lambda t, c: rot(t * (RET_DK ** -0.5))),
        (C_RV, 1024, rv_ref, 0, lambda t, c: t),
        (C_RG, 1024, rg_ref, 0, lambda t, c: silu(t)),
        (C_HQ, 1024, hq_ref, 0, lambda t, c: silu(t) * (HG_DK ** -0.5)),
        (C_HFF, 1024, g_ref, 0, log_forget(0)),
        (C_HFB, 1024, g_ref, 1024, log_forget(1)),
        (C_HI, 1024, hi_ref, 0, lambda t, c: t),
        (C_HOG, 1024, hog_ref, 0, lambda t, c: silu(t)),
        (C_GR, 1024, gr_ref, 0, lambda t, c: _sigmoid(t)),
        (C_GH, 1024, gh_ref, 0, lambda t, c: _sigmoid(t)),
    )
    for w0, width, o_ref, o0, fn in parts:
        for c in range(0, width, PROJ_CHUNK):
            t = _dot(h, w_ref[:, w0 + c:w0 + c + PROJ_CHUNK])
            o_ref[:, o0 + c:o0 + c + PROJ_CHUNK] = fn(t, c).astype(o_ref.dtype)


def _proj_call(x, mod3, norm_w, lb_logits, w_in, rope_tabs, rope_blk_fn, *, layer, mod_row_fn, tm, name):
    n = x.shape[0]
    slots = lb_logits.shape[1]
    row = lambda w: pl.BlockSpec((tm, w), lambda i: (i, 0))
    outs = [(1024, BF16), (1024, BF16), (1024, BF16), (1024, BF16), (2048, F32),
            (1024, BF16), (1024, BF16), (1024, BF16), (1024, BF16)]
    in_specs = [
        row(D_MODEL),
        pl.BlockSpec((None, N_MOD, D_MODEL), lambda i: (mod_row_fn(i, tm), 0, 0)),
        pl.BlockSpec((3, D_MODEL), lambda i: (0, 0)),
        pl.BlockSpec((2, slots, 1024), lambda i: (0, 0, 0)),
        pl.BlockSpec((D_MODEL, IN_WIDTH), lambda i: (0, 0), pipeline_mode=pl.Buffered(1)),
    ]
    args = [x, mod3, norm_w, lb_logits, w_in]
    if rope_tabs is not None:
        in_specs += [pl.BlockSpec((tm, RET_DK), lambda i: (rope_blk_fn(i, tm), 0))] * 2
        args += list(rope_tabs)
    return pl.pallas_call(
        functools.partial(_proj_kernel, layer=layer, rope=rope_tabs is not None),
        grid=(n // tm,),
        in_specs=in_specs,
        out_specs=[row(w) for w, _ in outs],
        out_shape=[jax.ShapeDtypeStruct((n, w), dt) for w, dt in outs],
        compiler_params=_params(("parallel",)),
        name=name,
    )(*args)


def _log_sigmoid(x):
    return -(jnp.maximum(-x, 0.0) + jnp.log1p(jnp.exp(-jnp.abs(x))))


def _half_acc_slices(i, n, blk, finalize):
    if finalize:
        starts = ((i - n // 2) * blk, (n - 1 - i) * blk)
    else:
        starts = (i * blk, (n // 2 - 1 - i) * blk)
    return tuple(pl.ds(pl.multiple_of(s, blk), blk) for s in starts)


def _ret_kernel(*refs, L, has_s0, has_prev, out_state):
    it = iter(refs)
    dec_ref = next(it)
    q_ref, k_ref, v_ref, gate_ref = next(it), next(it), next(it), next(it)
    s0_ref = next(it) if has_s0 else None
    if has_prev:
        next(it)
    o_ref = next(it)
    st_ref = next(it) if out_state else None
    of_acc, ob_acc, st_scr, cst = next(it), next(it), next(it), next(it)

    C = SCAN_BLOCK
    P = RET_PACK
    n = L // C
    pair = pl.program_id(1)
    ii = lax.broadcasted_iota(jnp.int32, (C, C), 0).astype(F32)
    jj = lax.broadcasted_iota(jnp.int32, (C, C), 1).astype(F32)
    diff = ii - jj

    @pl.when(jnp.logical_and(pl.program_id(0) == 0, pair == 0))
    def _():
        for hd in range(RET_HEADS):
            lgf = _log_sigmoid(jnp.full((C, C), dec_ref[0, hd], F32))
            lgb = _log_sigmoid(jnp.full((C, C), dec_ref[1, hd], F32))
            cst[hd, 0] = (jnp.where(diff >= 0, jnp.exp(jnp.maximum(diff, 0.0) * lgf), 0.0)
                          + jnp.where(diff <= 0, jnp.exp(jnp.maximum(-diff, 0.0) * lgb), 0.0))
            cst[hd, 1] = jnp.exp((ii + 1.0) * lgf)
            cst[hd, 2] = jnp.exp((C - 1.0 - ii) * lgf)
            cst[hd, 3] = jnp.exp((C - ii) * lgb)
            cst[hd, 4] = jnp.exp(ii * lgb)

    cds = []
    for h in range(P):
        hd = pair * P + h
        cds.append([jnp.exp(C * _log_sigmoid(jnp.full((1, RET_DV), dec_ref[d, hd], F32))) for d in range(2)])
        for d in range(2):
            st_scr[2 * h + d] = s0_ref[d, h] if has_s0 else jnp.zeros((RET_DK, RET_DV), F32)

    def step(i, finalize):
        sls = (pl.ds(pl.multiple_of(i * C, C), C), pl.ds(pl.multiple_of((n - 1 - i) * C, C), C))
        acc_sls = _half_acc_slices(i, n, C, finalize)
        chains = []
        for h in range(P):
            ql = slice(h * RET_DK, (h + 1) * RET_DK)
            vl = slice(h * RET_DV, (h + 1) * RET_DV)
            for d in range(2):
                chains.append(dict(h=h, d=d, vl=vl, q=q_ref[sls[d], ql], k=k_ref[sls[d], ql],
                                   v=v_ref[sls[d], vl]))
        for c in chains:
            if c["d"] == 0:
                c["s"] = _dot_nt(c["q"], c["k"])
        for c in chains:
            h, d, v, vl = c["h"], c["d"], c["v"], c["vl"]
            hd = pair * P + h
            st = st_scr[2 * h + d]
            qd = (c["q"].astype(F32) * cst[hd, 1 + 2 * d]).astype(BF16)
            if d == 0:
                p = (c["s"] * cst[hd, 0]).astype(BF16)
                o = _dot(jnp.concatenate([p, qd], axis=1), jnp.concatenate([v, st.astype(BF16)], axis=0))
            else:
                o = _dot(qd, st.astype(BF16))
            kd = (c["k"].astype(F32) * cst[hd, 2 + 2 * d]).astype(BF16)
            st_scr[2 * h + d] = cds[h][d] * st + _dot_tn(kd, v)
            if finalize:
                o = o + (ob_acc, of_acc)[d][acc_sls[d], vl]
                ms = jnp.mean(o * o, axis=-1, keepdims=True)
                o_ref[sls[d], vl] = (o * lax.rsqrt(ms + EPS) * gate_ref[sls[d], vl]).astype(BF16)
            else:
                (of_acc, ob_acc)[d][acc_sls[d], vl] = o

    def first_half(i, carry):
        step(i, False)
        return carry

    def second_half(i, carry):
        step(i, True)
        return carry

    lax.fori_loop(0, n // 2, first_half, 0, unroll=min(RET_UNROLL, n // 2))
    lax.fori_loop(n // 2, n, second_half, 0, unroll=min(RET_UNROLL, n // 2))
    if out_state:
        for h in range(P):
            for d in range(2):
                st_ref[d, h] = st_scr[2 * h + d]


def _ret_call(dec, qk, rv, rg, s0, o_prev, *, row0, B, L, out_state, name):
    has_s0 = s0 is not None
    has_prev = o_prev is not None
    H = RET_HEADS
    P = RET_PACK
    r0 = row0 // L
    st_spec = pl.BlockSpec((None, 2, P, RET_DK, RET_DV), lambda b, h: (b, 0, h, 0, 0))
    in_specs = [
        pl.BlockSpec(memory_space=pltpu.SMEM),
        pl.BlockSpec((L, P * RET_DK), lambda b, h: (r0 + b, h)),
        pl.BlockSpec((L, P * RET_DK), lambda b, h: (r0 + b, H // P + h)),
        pl.BlockSpec((L, P * RET_DV), lambda b, h: (r0 + b, h)),
        pl.BlockSpec((L, P * RET_DV), lambda b, h: (r0 + b, h)),
    ]
    args = [dec, qk, qk, rv, rg]
    if has_s0:
        in_specs.append(st_spec)
        args.append(s0)
    aliases = {}
    if has_prev:
        aliases = {len(args): 0}
        in_specs.append(pl.BlockSpec(memory_space=pl.ANY))
        args.append(o_prev)
    out_specs = [pl.BlockSpec((L, P * RET_DV), lambda b, h: (r0 + b, h))]
    out_shape = [jax.ShapeDtypeStruct((qk.shape[0], H * RET_DV), BF16)]
    if out_state:
        out_specs.append(st_spec)
        out_shape.append(jax.ShapeDtypeStruct((B, 2, H, RET_DK, RET_DV), F32))
    res = pl.pallas_call(
        functools.partial(_ret_kernel, L=L, has_s0=has_s0, has_prev=has_prev, out_state=out_state),
        grid=(B, H // P),
        in_specs=in_specs,
        out_specs=out_specs,
        out_shape=out_shape,
        input_output_aliases=aliases,
        scratch_shapes=[pltpu.VMEM((L // 2, P * RET_DV), F32), pltpu.VMEM((L // 2, P * RET_DV), F32),
                        pltpu.VMEM((2 * P, RET_DK, RET_DV), F32),
                        pltpu.VMEM((H, 5, SCAN_BLOCK, SCAN_BLOCK), F32)],
        compiler_params=_params(("arbitrary", "arbitrary")),
        name=name,
    )(*args)
    return res


def _gla_blocks(chains, tri_ref, code_ref):
    T = SCAN_BLOCK
    K = HG_DK
    for c in chains:
        g2 = c["g"]
        c["k"] = 1.0 - jnp.exp2(g2)
        hi = g2.astype(BF16)
        lo = (g2 - hi.astype(F32)).astype(BF16)
        tri = tri_ref[int(c["rev"])]
        c["b"] = _dot(tri, hi) + _dot(tri, lo)

    for c in chains:
        bl = c["b"].reshape(T // HG_LEAF, HG_LEAF, K)
        r = HG_LEAF // 2 if c["rev"] else HG_LEAF // 2 - 1
        d = jnp.clip(bl - bl[:, r:r + 1, :], -EXP2_CLAMP, EXP2_CLAMP).reshape(T, K)
        e = jnp.exp2(d)
        p = _dot_nt((c["q"] * e).astype(BF16), (c["k"] / e).astype(BF16))
        c["a"] = jnp.where(code_ref[int(c["rev"])] == 0, p, 0.0)

    for lv, m in HG_LEVELS:
        seg = 2 * m
        ng = T // seg
        for c in chains:
            b3, q3, k3 = (c[n].reshape(ng, seg, K) for n in ("b", "q", "k"))
            if c["rev"]:
                qs, ks, r = slice(0, m), slice(m, seg), m
            else:
                qs, ks, r = slice(m, seg), slice(0, m), m - 1
            ref = b3[:, r:r + 1, :]
            qm = (q3[:, qs, :] * jnp.exp2(b3[:, qs, :] - ref)).reshape(ng * m, K).astype(BF16)
            kz = k3[:, ks, :] * jnp.exp2(ref - b3[:, ks, :])
            zeros = jnp.zeros((ng, m, K), F32)
            kfull = jnp.concatenate([zeros, kz] if c["rev"] else [kz, zeros], axis=1)
            p = _dot_nt(qm, kfull.reshape(T, K).astype(BF16)).reshape(ng, m, T)
            a3 = c["a"].reshape(ng, seg, T)
            c3 = code_ref[int(c["rev"])].reshape(ng, seg, T)
            upd = jnp.where(c3[:, qs, :] == lv, p, a3[:, qs, :])
            parts = [upd, a3[:, ks, :]] if c["rev"] else [a3[:, ks, :], upd]
            c["a"] = jnp.concatenate(parts, axis=1).reshape(T, T)

    out = []
    for c in chains:
        b, q, k, v, st = c["b"], c["q"], c["k"], c["v"], c["st"]
        o = _dot(c["a"].astype(BF16), v) + _dot_nt((q * jnp.exp2(b)).astype(BF16), st.astype(BF16))
        end_row = 0 if c["rev"] else T - 1
        b1 = b.reshape(1, T, K)
        b_end = b1[:, end_row:end_row + 1, :]
        kd = (k * jnp.exp2(b_end - b1).reshape(T, K)).astype(BF16)
        out.append((o, st * jnp.exp2(b_end.reshape(1, K)) + _dot_tn(v, kd)))
    return out


def _hgrn_kernel(*refs, L, has_s0, has_prev, out_state):
    it = iter(refs)
    q_ref, g_f_ref, g_b_ref, v_ref, gate_ref, nw_ref = (next(it) for _ in range(6))
    s0_ref = next(it) if has_s0 else None
    if has_prev:
        next(it)
    o_ref = next(it)
    st_ref = next(it) if out_state else None
    of_acc, ob_acc, st_scr, tri_ref, code_ref = (next(it) for _ in range(5))

    T = SCAN_BLOCK
    P = HG_PACK
    n = L // T

    @pl.when(jnp.logical_and(pl.program_id(0) == 0, pl.program_id(1) == 0))
    def _():
        ti = lax.broadcasted_iota(jnp.int32, (T, T), 0)
        si = lax.broadcasted_iota(jnp.int32, (T, T), 1)
        x = ti ^ si
        tri_ref[0] = (si <= ti).astype(BF16)
        tri_ref[1] = (si >= ti).astype(BF16)
        lvl = jnp.where(x < HG_LEAF, 0, 3 + (x >= 16).astype(jnp.int32) + (x >= 32).astype(jnp.int32)
                        + (x >= 64).astype(jnp.int32))
        code_ref[0] = jnp.where(si <= ti, lvl, -1)
        code_ref[1] = jnp.where(si >= ti, lvl, -1)

    for h in range(P):
        for d in range(2):
            st_scr[2 * h + d] = s0_ref[d, h].T if has_s0 else jnp.zeros((HG_DV, HG_DK), F32)

    def step(i, finalize):
        sls = (pl.ds(pl.multiple_of(i * T, T), T), pl.ds(pl.multiple_of((n - 1 - i) * T, T), T))
        acc_sls = _half_acc_slices(i, n, T, finalize)
        chains = []
        for h in range(P):
            lanes = slice(h * HG_DK, (h + 1) * HG_DK)
            for d, g_ref in enumerate((g_f_ref, g_b_ref)):
                chains.append(dict(q=q_ref[sls[d], lanes].astype(F32), g=g_ref[sls[d], lanes],
                                   v=v_ref[sls[d], lanes],
                                   st=st_scr[2 * h + d], rev=bool(d)))
        res = _gla_blocks(chains, tri_ref, code_ref)
        for h in range(P):
            lanes = slice(h * HG_DK, (h + 1) * HG_DK)
            for d in range(2):
                o, st = res[2 * h + d]
                st_scr[2 * h + d] = st
                if finalize:
                    o = o + (ob_acc, of_acc)[d][acc_sls[d], lanes]
                    ms = jnp.mean(o * o, axis=-1, keepdims=True)
                    o = o * lax.rsqrt(ms + EPS) * nw_ref[:, lanes]
                    o_ref[sls[d], lanes] = (o * gate_ref[sls[d], lanes]).astype(BF16)
                else:
                    (of_acc, ob_acc)[d][acc_sls[d], lanes] = o

    def first_half(i, carry):
        step(i, False)
        return carry

    def second_half(i, carry):
        step(i, True)
        return carry

    lax.fori_loop(0, n // 2, first_half, 0, unroll=min(HG_UNROLL, n // 2))
    lax.fori_loop(n // 2, n, second_half, 0, unroll=min(HG_UNROLL, n // 2))
    if out_state:
        for h in range(P):
            for d in range(2):
                st_ref[d, h] = st_scr[2 * h + d].T


def _hgrn_call(hq, g, hi, hog, nw, s0, o_prev, *, row0, B, L, out_state, name):
    has_s0 = s0 is not None
    has_prev = o_prev is not None
    H = HG_HEADS
    P = HG_PACK
    W = P * HG_DK
    r0 = row0 // L
    blk = lambda off: pl.BlockSpec((L, W), lambda b, h: (r0 + b, off + h))
    st_spec = pl.BlockSpec((None, 2, P, HG_DK, HG_DV), lambda b, h: (b, 0, h, 0, 0))
    in_specs = [blk(0), blk(0), blk(H // P), blk(0), blk(0), pl.BlockSpec((1, W), lambda b, h: (0, h))]
    args = [hq, g, g, hi, hog, nw]
    if has_s0:
        in_specs.append(st_spec)
        args.append(s0)
    aliases = {}
    if has_prev:
        aliases = {len(args): 0}
        in_specs.append(pl.BlockSpec(memory_space=pl.ANY))
        args.append(o_prev)
    out_specs = [blk(0)]
    out_shape = [jax.ShapeDtypeStruct((hq.shape[0], H * HG_DV), BF16)]
    if out_state:
        out_specs.append(st_spec)
        out_shape.append(jax.ShapeDtypeStruct((B, 2, H, HG_DK, HG_DV), F32))
    return pl.pallas_call(
        functools.partial(_hgrn_kernel, L=L, has_s0=has_s0, has_prev=has_prev, out_state=out_state),
        grid=(B, H // P),
        in_specs=in_specs,
        out_specs=out_specs,
        out_shape=out_shape,
        input_output_aliases=aliases,
        scratch_shapes=[pltpu.VMEM((L // 2, W), F32), pltpu.VMEM((L // 2, W), F32),
                        pltpu.VMEM((2 * P, HG_DV, HG_DK), F32),
                        pltpu.VMEM((2, SCAN_BLOCK, SCAN_BLOCK), BF16),
                        pltpu.VMEM((2, SCAN_BLOCK, SCAN_BLOCK), jnp.int32)],
        compiler_params=_params(("arbitrary", "arbitrary")),
        name=name,
    )(*args)


def _merge_kernel(x_ref, mod_ref, oret_ref, ohg_ref, gr_ref, gh_ref, wr_ref, wh_ref, wo_ref, o_ref):
    y = gr_ref[...] * _dot(oret_ref[...], wr_ref[...]) + gh_ref[...] * _dot(ohg_ref[...], wh_ref[...])
    out = _dot(y.astype(BF16), wo_ref[...])
    o_ref[...] = x_ref[...] + mod_ref[5:6, :] * out


def _merge_call(x, mod3, oret, ohg, gr, gh, wr, wh, wo, *, mod_row_fn, tm, name):
    n = x.shape[0]
    row = pl.BlockSpec((tm, D_MODEL), lambda i: (i, 0))
    wspec = pl.BlockSpec((D_MODEL, D_MODEL), lambda i: (0, 0), pipeline_mode=pl.Buffered(1))
    return pl.pallas_call(
        _merge_kernel,
        grid=(n // tm,),
        in_specs=[row, pl.BlockSpec((None, N_MOD, D_MODEL), lambda i: (mod_row_fn(i, tm), 0, 0)),
                  row, row, row, row, wspec, wspec, wspec],
        out_specs=row,
        out_shape=jax.ShapeDtypeStruct((n, D_MODEL), F32),
        compiler_params=_params(("parallel",)),
        name=name,
    )(x, mod3, oret, ohg, gr, gh, wr, wh, wo)


def _rope_tables(n_tokens, n_identity):
    pos = np.arange(n_tokens)
    r = (pos // GRID_W).astype(np.float32)
    cl = (pos % GRID_W).astype(np.float32)
    quarter = RET_DK // 4
    inv_freq = ROPE_BASE ** (-jnp.arange(quarter, dtype=F32) / quarter)
    ang_r = jnp.asarray(r)[:, None] * inv_freq[None, :]
    ang_c = jnp.asarray(cl)[:, None] * inv_freq[None, :]
    cos = jnp.concatenate([jnp.cos(ang_r)] * 2 + [jnp.cos(ang_c)] * 2, axis=-1)
    sins = jnp.concatenate([-jnp.sin(ang_r), jnp.sin(ang_r), -jnp.sin(ang_c), jnp.sin(ang_c)], axis=-1)
    cos = jnp.concatenate([cos, jnp.ones((n_identity, RET_DK), F32)], axis=0)
    sins = jnp.concatenate([sins, jnp.zeros((n_identity, RET_DK), F32)], axis=0)
    return cos, sins


def kernel(x_prompt, x_sample, state_ret, state_hgrn, c, c_ctx, ada_w, ada_b, norm_w, ffn1_w13, ffn1_w2,
           ffn2_w13, ffn2_w2, w_in, ret_decay, hg_lb_logits, hg_norm_w, w_ret_proj, w_hg_proj, w_o,
           final_norm_w):
    depth = ada_w.shape[0]
    assert depth == 1
    l = 0
    bp, lp, _ = x_prompt.shape
    bs, ls, _ = x_sample.shape
    n_ctx = bp * lp
    assert n_ctx % ls == 0 and ls % TM_FFN == 0 and ls % TM_PROJ == 0 and ls % TM_MERGE == 0

    cvec = jnp.concatenate([c_ctx[None, :], c, jnp.zeros((8 - 1 - bs, D_MODEL), F32)], axis=0)
    mod = _mod_call(cvec, ada_w[l], ada_b[l][None, :])
    mod3 = mod.reshape(8, N_MOD, D_MODEL)

    def mod_row(i, tm):
        return jnp.where(i < n_ctx // tm, 0, 1 + (i * tm - n_ctx) // ls)

    def rope_blk(i, tm):
        return jnp.where(i < n_ctx // tm, ls // tm, (i - n_ctx // tm) % (ls // tm))

    bf = lambda w: w.astype(BF16)
    nw, fw = norm_w[l], final_norm_w[None, :]
    dec, hg_nw = ret_decay[l], hg_norm_w[l][None, :]

    x = _ffn_call((x_prompt.reshape(n_ctx, D_MODEL), x_sample.reshape(bs * ls, D_MODEL)), mod3, nw, fw,
                  bf(ffn1_w13[l]), bf(ffn1_w2[l]), n_out_split=None, mod_base=0, norm_row=0, final=False,
                  mod_row_fn=mod_row, tm=TM_FFN, name="ffn1")
    qk, rv, rg, hq, g, hi, hog, gr, gh = _proj_call(x, mod3, nw, hg_lb_logits, bf(w_in[l]),
                                                    _rope_tables(ls, TM_PROJ), rope_blk, layer=l,
                                                    mod_row_fn=mod_row, tm=TM_PROJ, name="proj")
    o_ret, s_ret = _ret_call(dec, qk, rv, rg, None, None, row0=0, B=bp, L=lp, out_state=True, name="ret_ctx")
    o_ret, = _ret_call(dec, qk, rv, rg, state_ret[:, l], o_ret, row0=n_ctx, B=bs, L=ls, out_state=False,
                       name="ret_lat")
    o_hg, s_hg = _hgrn_call(hq, g, hi, hog, hg_nw, None, None, row0=0, B=bp, L=lp, out_state=True,
                            name="hgrn_ctx")
    o_hg, = _hgrn_call(hq, g, hi, hog, hg_nw, state_hgrn[:, l], o_hg, row0=n_ctx, B=bs, L=ls,
                       out_state=False, name="hgrn_lat")
    x = _merge_call(x, mod3, o_ret, o_hg, gr, gh, bf(w_ret_proj[l]), bf(w_hg_proj[l]), bf(w_o[l]),
                    mod_row_fn=mod_row, tm=TM_MERGE, name="merge")
    yp, ys = _ffn_call((x,), mod3, nw, fw, bf(ffn2_w13[l]), bf(ffn2_w2[l]), n_out_split=n_ctx, mod_base=6,
                       norm_row=2, final=True, mod_row_fn=mod_row, tm=TM_FFN, name="ffn2")
    return (yp.reshape(bp, lp, D_MODEL), ys.reshape(bs, ls, D_MODEL), s_ret[:, None], s_hg[:, None])
```

```python
import functools

import numpy as np
import jax
import jax.numpy as jnp
from jax import lax
from jax.experimental import pallas as pl
from jax.experimental.pallas import tpu as pltpu

F32 = jnp.float32
BF16 = jnp.bfloat16

D_MODEL = 1024
GRID_W = 64
RET_HEADS = 4
RET_DK = 128
RET_DV = 256
HG_HEADS = 8
HG_DK = 128
HG_DV = 128
D_FF = 2816
N_MOD = 9
ROPE_BASE = 10000.0
EPS = 1e-6

C_RQ, C_RK, C_RV, C_RG, C_HQ, C_HFF, C_HFB, C_HI, C_HOG, C_GR, C_GH = (
    0, 512, 1024, 2048, 3072, 4096, 5120, 6144, 7168, 8192, 9216)
IN_WIDTH = 10240

VMEM_LIMIT_BYTES = 56 * 1024 * 1024

TM_FFN = 1024
TM_PROJ = 256
TM_MERGE = 1024
SCAN_BLOCK = 128
RET_PACK = 4
RET_UNROLL = 1
HG_PACK = 4
HG_UNROLL = 2
FF_CHUNK = 256
PROJ_CHUNK = 256
LOG2E = 1.4426950408889634
EXP2_CLAMP = 115.0
HG_LEAF = 16
HG_LEVELS = ((4, 16), (5, 32), (6, 64))


def _dot(a, b):
    return jnp.dot(a, b, preferred_element_type=F32)


def _dot_nt(a, b):
    return lax.dot_general(a, b, (((1,), (1,)), ((), ())), preferred_element_type=F32)


def _dot_tn(a, b):
    return lax.dot_general(a, b, (((0,), (0,)), ((), ())), preferred_element_type=F32)


def _sigmoid(x):
    return 1.0 / (1.0 + jnp.exp(-x))


def _rms(x, w):
    ms = jnp.mean(x * x, axis=-1, keepdims=True)
    return x * lax.rsqrt(ms + EPS) * w


def _params(sem):
    return pltpu.CompilerParams(dimension_semantics=sem, vmem_limit_bytes=VMEM_LIMIT_BYTES)


def _stage_weight(step, w_hbm, w_vmem, stage, sem):
    rows = stage.shape[1]
    n_chunks = w_hbm.shape[0] // rows

    def copy(k, slot):
        return pltpu.make_async_copy(w_hbm.at[pl.ds(k * rows, rows), :], stage.at[slot], sem.at[slot])

    @pl.when(step == 0)
    def _():
        copy(0, 0).start()

        def body(k, carry):
            slot = k % 2

            @pl.when(k + 1 < n_chunks)
            def _():
                copy(k + 1, 1 - slot).start()

            copy(k, slot).wait()
            w_vmem[pl.ds(pl.multiple_of(k * rows, rows), rows), :] = stage[slot].astype(BF16)
            return carry

        lax.fori_loop(0, n_chunks, body, 0)


_HBM = pl.BlockSpec(memory_space=pl.ANY)


def _stage_scratch(w, rows):
    r, c = w.shape
    assert r % rows == 0 and rows % 16 == 0
    return [pltpu.VMEM((r, c), BF16), pltpu.VMEM((2, rows, c), F32), pltpu.SemaphoreType.DMA((2,))]


def _mod_kernel(c_ref, w_ref, b_ref, o_ref):
    c = c_ref[...]
    s = (c * _sigmoid(c)).astype(BF16)
    o_ref[...] = _dot(s, w_ref[...].astype(BF16)) + b_ref[...]


def _mod_call(cvec, ada_w, ada_b):
    n = ada_w.shape[1]
    tn = 1024
    return pl.pallas_call(
        _mod_kernel,
        grid=(n // tn,),
        in_specs=[
            pl.BlockSpec((8, D_MODEL), lambda j: (0, 0)),
            pl.BlockSpec((D_MODEL, tn), lambda j: (0, j)),
            pl.BlockSpec((1, tn), lambda j: (0, j)),
        ],
        out_specs=pl.BlockSpec((8, tn), lambda j: (0, j)),
        out_shape=jax.ShapeDtypeStruct((8, n), F32),
        compiler_params=_params(("arbitrary",)),
        name="mod",
    )(cvec, ada_w, ada_b)


def _ffn_kernel(*refs, mod_base, norm_row, final, n_first, two_in, two_out):
    it = iter(refs)
    xa_ref = next(it)
    xb_ref = next(it) if two_in else None
    mod_ref, nw_ref, fw_ref, w13_hbm, w2_hbm = (next(it) for _ in range(5))
    oa_ref = next(it)
    ob_ref = next(it) if two_out else None
    w13_ref, st13, sem13, w2_ref, st2, sem2 = (next(it) for _ in range(6))
    i = pl.program_id(0)
    _stage_weight(i, w13_hbm, w13_ref, st13, sem13)
    _stage_weight(i, w2_hbm, w2_ref, st2, sem2)
    x = xa_ref[...]
    if two_in:
        x = jnp.where(i < n_first, x, xb_ref[...])
    y = _rms(x, nw_ref[norm_row:norm_row + 1, :])
    sh = mod_ref[mod_base:mod_base + 1, :]
    sc = mod_ref[mod_base + 1:mod_base + 2, :]
    h = (y * (1.0 + sc) + sh).astype(BF16)
    acc = None
    for c0 in range(0, D_FF, FF_CHUNK):
        a = _dot(h, w13_ref[:, c0:c0 + FF_CHUNK])
        b = _dot(h, w13_ref[:, D_FF + c0:D_FF + c0 + FF_CHUNK])
        act = (a * _sigmoid(a) * b).astype(BF16)
        part = _dot(act, w2_ref[c0:c0 + FF_CHUNK, :])
        acc = part if acc is None else acc + part
    g = mod_ref[mod_base + 2:mod_base + 3, :]
    xo = x + 0.5 * g * acc
    if final:
        xo = _rms(xo, fw_ref[...])
    if two_out:
        @pl.when(i < n_first)
        def _():
            oa_ref[...] = xo

        @pl.when(i >= n_first)
        def _():
            ob_ref[...] = xo
    else:
        oa_ref[...] = xo


def _ffn_call(xs, mod3, norm_w, final_w, w13, w2, *, n_out_split, mod_base, norm_row, final, mod_row_fn, tm,
              name):
    two_in = len(xs) == 2
    two_out = n_out_split is not None
    n = sum(x.shape[0] for x in xs)
    n_first = (xs[0].shape[0] if two_in else n_out_split if two_out else n) // tm
    first = pl.BlockSpec((tm, D_MODEL), lambda i: (jnp.minimum(i, n_first - 1), 0))
    second = pl.BlockSpec((tm, D_MODEL), lambda i: (jnp.maximum(i - n_first, 0), 0))
    whole = pl.BlockSpec((tm, D_MODEL), lambda i: (i, 0))
    kern = functools.partial(_ffn_kernel, mod_base=mod_base, norm_row=norm_row, final=final, n_first=n_first,
                             two_in=two_in, two_out=two_out)
    if two_out:
        out_specs = [first, second]
        out_shape = [jax.ShapeDtypeStruct((n_out_split, D_MODEL), F32),
                     jax.ShapeDtypeStruct((n - n_out_split, D_MODEL), F32)]
    else:
        out_specs = whole
        out_shape = jax.ShapeDtypeStruct((n, D_MODEL), F32)
    return pl.pallas_call(
        kern,
        grid=(n // tm,),
        in_specs=([first, second] if two_in else [whole]) + [
            pl.BlockSpec((None, N_MOD, D_MODEL), lambda i: (mod_row_fn(i, tm), 0, 0)),
            pl.BlockSpec((3, D_MODEL), lambda i: (0, 0)),
            pl.BlockSpec((1, D_MODEL), lambda i: (0, 0)),
            _HBM,
            _HBM,
        ],
        out_specs=out_specs,
        out_shape=out_shape,
        scratch_shapes=_stage_scratch(w13, 32) + _stage_scratch(w2, 128),
        compiler_params=_params(("arbitrary",)),
        name=name,
    )(*xs, mod3, norm_w, final_w, w13, w2)


def _rope(x, cos, sins, first_quarter):
    partner = jnp.where(first_quarter, pltpu.roll(x, 96, 1), pltpu.roll(x, 32, 1))
    return x * cos + partner * sins


def _proj_kernel(*refs, layer, rope):
    it = iter(refs)
    x_ref, mod_ref, nw_ref, lbl_ref, w_hbm = (next(it) for _ in range(5))
    cos_ref, sin_ref = (next(it), next(it)) if rope else (None, None)
    qk_ref, rv_ref, rg_ref, hq_ref, g_ref, hi_ref, hog_ref, gr_ref, gh_ref = (next(it) for _ in range(9))
    w_ref, w_stage, w_sem = next(it), next(it), next(it)
    _stage_weight(pl.program_id(0), w_hbm, w_ref, w_stage, w_sem)
    y = _rms(x_ref[...], nw_ref[1:2, :])
    h = (y * (1.0 + mod_ref[4:5, :]) + mod_ref[3:4, :]).astype(BF16)

    if rope:
        lane = lax.broadcasted_iota(jnp.int32, (x_ref.shape[0], RET_DK), 1)
        first_quarter = (lane & 32) == 0

    def rot(t):
        if not rope:
            return t
        heads = [_rope(t[:, j:j + RET_DK], cos_ref[...], sin_ref[...], first_quarter)
                 for j in range(0, t.shape[1], RET_DK)]
        return jnp.concatenate(heads, axis=1)

    def silu(t):
        return t * _sigmoid(t)

    def log_forget(d):
        def fn(t, c):
            lg = lbl_ref[d, :, c:c + PROJ_CHUNK]
            e = jnp.exp(lg - jnp.max(lg, axis=0, keepdims=True))
            lb = jnp.sum(e[0:layer + 1, :], axis=0, keepdims=True) / jnp.sum(e, axis=0, keepdims=True)
            return jnp.log(lb + (1.0 - lb) * _sigmoid(t)) * LOG2E
        return fn

    parts = (
        (C_RQ, 512, qk_ref, 0, lambda t, c: rot(t)),
        (C_RK, 512, qk_ref, 512, lambda t, c: rot(t * (RET_DK ** -0.5))),
        (C_RV, 1024, rv_ref, 0, lambda t, c: t),
        (C_RG, 1024, rg_ref, 0, lambda t, c: silu(t)),
        (C_HQ, 1024, hq_ref, 0, lambda t, c: silu(t) * (HG_DK ** -0.5)),
        (C_HFF, 1024, g_ref, 0, log_forget(0)),
        (C_HFB, 1024, g_ref, 1024, log_forget(1)),
        (C_HI, 1024, hi_ref, 0, lambda t, c: t),
        (C_HOG, 1024, hog_ref, 0, lambda t, c: silu(t)),
        (C_GR, 1024, gr_ref, 0, lambda t, c: _sigmoid(t)),
        (C_GH, 1024, gh_ref, 0, lambda t, c: _sigmoid(t)),
    )
    for w0, width, o_ref, o0, fn in parts:
        for c in range(0, width, PROJ_CHUNK):
            t = _dot(h, w_ref[:, w0 + c:w0 + c + PROJ_CHUNK])
            o_ref[:, o0 + c:o0 + c + PROJ_CHUNK] = fn(t, c).astype(o_ref.dtype)


def _proj_call(x, mod3, norm_w, lb_logits, w_in, rope_tabs, rope_blk_fn, *, layer, mod_row_fn, tm, name):
    n = x.shape[0]
    slots = lb_logits.shape[1]
    row = lambda w: pl.BlockSpec((tm, w), lambda i: (i, 0))
    outs = [(1024, BF16), (1024, BF16), (1024, BF16), (1024, BF16), (2048, F32),
            (1024, BF16), (1024, BF16), (1024, BF16), (1024, BF16)]
    in_specs = [
        row(D_MODEL),
        pl.BlockSpec((None, N_MOD, D_MODEL), lambda i: (mod_row_fn(i, tm), 0, 0)),
        pl.BlockSpec((3, D_MODEL), lambda i: (0, 0)),
        pl.BlockSpec((2, slots, 1024), lambda i: (0, 0, 0)),
        _HBM,
    ]
    args = [x, mod3, norm_w, lb_logits, w_in]
    if rope_tabs is not None:
        in_specs += [pl.BlockSpec((tm, RET_DK), lambda i: (rope_blk_fn(i, tm), 0))] * 2
        args += list(rope_tabs)
    return pl.pallas_call(
        functools.partial(_proj_kernel, layer=layer, rope=rope_tabs is not None),
        grid=(n // tm,),
        in_specs=in_specs,
        out_specs=[row(w) for w, _ in outs],
        out_shape=[jax.ShapeDtypeStruct((n, w), dt) for w, dt in outs],
        scratch_shapes=_stage_scratch(w_in, 32),
        compiler_params=_params(("arbitrary",)),
        name=name,
    )(*args)


def _log_sigmoid(x):
    return -(jnp.maximum(-x, 0.0) + jnp.log1p(jnp.exp(-jnp.abs(x))))


def _half_acc_slices(i, n, blk, finalize):
    if finalize:
        starts = ((i - n // 2) * blk, (n - 1 - i) * blk)
    else:
        starts = (i * blk, (n // 2 - 1 - i) * blk)
    return tuple(pl.ds(pl.multiple_of(s, blk), blk) for s in starts)


def _ret_kernel(*refs, L, has_s0, out_state):
    it = iter(refs)
    dec_ref = next(it)
    q_ref, k_ref, v_ref, gate_ref = next(it), next(it), next(it), next(it)
    s0_ref = next(it) if has_s0 else None
    o_ref = next(it)
    st_ref = next(it) if out_state else None
    of_acc, ob_acc, st_scr, cst = next(it), next(it), next(it), next(it)

    C = SCAN_BLOCK
    P = RET_PACK
    n = L // C
    pair = pl.program_id(1)
    ii = lax.broadcasted_iota(jnp.int32, (C, C), 0).astype(F32)
    jj = lax.broadcasted_iota(jnp.int32, (C, C), 1).astype(F32)
    diff = ii - jj

    @pl.when(jnp.logical_and(pl.program_id(0) == 0, pair == 0))
    def _():
        for hd in range(RET_HEADS):
            lgf = _log_sigmoid(jnp.full((C, C), dec_ref[0, hd], F32))
            lgb = _log_sigmoid(jnp.full((C, C), dec_ref[1, hd], F32))
            cst[hd, 0] = (jnp.where(diff >= 0, jnp.exp(jnp.maximum(diff, 0.0) * lgf), 0.0)
                          + jnp.where(diff <= 0, jnp.exp(jnp.maximum(-diff, 0.0) * lgb), 0.0))
            cst[hd, 1] = jnp.exp((ii + 1.0) * lgf)
            cst[hd, 2] = jnp.exp((C - 1.0 - ii) * lgf)
            cst[hd, 3] = jnp.exp((C - ii) * lgb)
            cst[hd, 4] = jnp.exp(ii * lgb)

    cds = []
    for h in range(P):
        hd = pair * P + h
        cds.append([jnp.exp(C * _log_sigmoid(jnp.full((1, RET_DV), dec_ref[d, hd], F32))) for d in range(2)])
        for d in range(2):
            st_scr[2 * h + d] = s0_ref[d, h] if has_s0 else jnp.zeros((RET_DK, RET_DV), F32)

    def step(i, finalize):
        sls = (pl.ds(pl.multiple_of(i * C, C), C), pl.ds(pl.multiple_of((n - 1 - i) * C, C), C))
        acc_sls = _half_acc_slices(i, n, C, finalize)
        chains = []
        for h in range(P):
            ql = slice(h * RET_DK, (h + 1) * RET_DK)
            vl = slice(h * RET_DV, (h + 1) * RET_DV)
            for d in range(2):
                chains.append(dict(h=h, d=d, vl=vl, q=q_ref[sls[d], ql], k=k_ref[sls[d], ql],
                                   v=v_ref[sls[d], vl]))
        for c in chains:
            if c["d"] == 0:
                c["s"] = _dot_nt(c["q"], c["k"])
        for c in chains:
            h, d, v, vl = c["h"], c["d"], c["v"], c["vl"]
            hd = pair * P + h
            st = st_scr[2 * h + d]
            qd = (c["q"].astype(F32) * cst[hd, 1 + 2 * d]).astype(BF16)
            if d == 0:
                p = (c["s"] * cst[hd, 0]).astype(BF16)
                o = _dot(jnp.concatenate([p, qd], axis=1), jnp.concatenate([v, st.astype(BF16)], axis=0))
            else:
                o = _dot(qd, st.astype(BF16))
            kd = (c["k"].astype(F32) * cst[hd, 2 + 2 * d]).astype(BF16)
            st_scr[2 * h + d] = cds[h][d] * st + _dot_tn(kd, v)
            if finalize:
                o = o + (ob_acc, of_acc)[d][acc_sls[d], vl]
                ms = jnp.mean(o * o, axis=-1, keepdims=True)
                o_ref[sls[d], vl] = (o * lax.rsqrt(ms + EPS) * gate_ref[sls[d], vl]).astype(BF16)
            else:
                (of_acc, ob_acc)[d][acc_sls[d], vl] = o

    def first_half(i, carry):
        step(i, False)
        return carry

    def second_half(i, carry):
        step(i, True)
        return carry

    lax.fori_loop(0, n // 2, first_half, 0, unroll=min(RET_UNROLL, n // 2))
    lax.fori_loop(n // 2, n, second_half, 0, unroll=min(RET_UNROLL, n // 2))
    if out_state:
        for h in range(P):
            for d in range(2):
                st_ref[d, h] = st_scr[2 * h + d]


def _ret_call(dec, qk, rv, rg, s0, *, row0, B, L, out_state, name):
    has_s0 = s0 is not None
    H = RET_HEADS
    P = RET_PACK
    r0 = row0 // L
    st_spec = pl.BlockSpec((None, 2, P, RET_DK, RET_DV), lambda b, h: (b, 0, h, 0, 0))
    in_specs = [
        pl.BlockSpec(memory_space=pltpu.SMEM),
        pl.BlockSpec((L, P * RET_DK), lambda b, h: (r0 + b, h)),
        pl.BlockSpec((L, P * RET_DK), lambda b, h: (r0 + b, H // P + h)),
        pl.BlockSpec((L, P * RET_DV), lambda b, h: (r0 + b, h)),
        pl.BlockSpec((L, P * RET_DV), lambda b, h: (r0 + b, h)),
    ]
    args = [dec, qk, qk, rv, rg]
    if has_s0:
        in_specs.append(st_spec)
        args.append(s0)
    out_specs = [pl.BlockSpec((L, P * RET_DV), lambda b, h: (b, h))]
    out_shape = [jax.ShapeDtypeStruct((B * L, H * RET_DV), BF16)]
    if out_state:
        out_specs.append(st_spec)
        out_shape.append(jax.ShapeDtypeStruct((B, 2, H, RET_DK, RET_DV), F32))
    res = pl.pallas_call(
        functools.partial(_ret_kernel, L=L, has_s0=has_s0, out_state=out_state),
        grid=(B, H // P),
        in_specs=in_specs,
        out_specs=out_specs,
        out_shape=out_shape,
        scratch_shapes=[pltpu.VMEM((L // 2, P * RET_DV), F32), pltpu.VMEM((L // 2, P * RET_DV), F32),
                        pltpu.VMEM((2 * P, RET_DK, RET_DV), F32),
                        pltpu.VMEM((H, 5, SCAN_BLOCK, SCAN_BLOCK), F32)],
        compiler_params=_params(("arbitrary", "arbitrary")),
        name=name,
    )(*args)
    return res


def _gla_blocks(chains, tri_ref, code_ref):
    T = SCAN_BLOCK
    K = HG_DK
    for c in chains:
        g2 = c["g"]
        c["k"] = 1.0 - jnp.exp2(g2)
        hi = g2.astype(BF16)
        lo = (g2 - hi.astype(F32)).astype(BF16)
        tri = tri_ref[int(c["rev"])]
        c["b"] = _dot(tri, hi) + _dot(tri, lo)

    for c in chains:
        bl = c["b"].reshape(T // HG_LEAF, HG_LEAF, K)
        r = HG_LEAF // 2 if c["rev"] else HG_LEAF // 2 - 1
        d = jnp.clip(bl - bl[:, r:r + 1, :], -EXP2_CLAMP, EXP2_CLAMP).reshape(T, K)
        e = jnp.exp2(d)
        p = _dot_nt((c["q"] * e).astype(BF16), (c["k"] / e).astype(BF16))
        c["a"] = jnp.where(code_ref[int(c["rev"])] == 0, p, 0.0)

    for lv, m in HG_LEVELS:
        seg = 2 * m
        ng = T // seg
        for c in chains:
            b3, q3, k3 = (c[n].reshape(ng, seg, K) for n in ("b", "q", "k"))
            if c["rev"]:
                qs, ks, r = slice(0, m), slice(m, seg), m
            else:
                qs, ks, r = slice(m, seg), slice(0, m), m - 1
            ref = b3[:, r:r + 1, :]
            qm = (q3[:, qs, :] * jnp.exp2(b3[:, qs, :] - ref)).reshape(ng * m, K).astype(BF16)
            kz = k3[:, ks, :] * jnp.exp2(ref - b3[:, ks, :])
            zeros = jnp.zeros((ng, m, K), F32)
            kfull = jnp.concatenate([zeros, kz] if c["rev"] else [kz, zeros], axis=1)
            p = _dot_nt(qm, kfull.reshape(T, K).astype(BF16)).reshape(ng, m, T)
            a3 = c["a"].reshape(ng, seg, T)
            c3 = code_ref[int(c["rev"])].reshape(ng, seg, T)
            upd = jnp.where(c3[:, qs, :] == lv, p, a3[:, qs, :])
            parts = [upd, a3[:, ks, :]] if c["rev"] else [a3[:, ks, :], upd]
            c["a"] = jnp.concatenate(parts, axis=1).reshape(T, T)

    out = []
    for c in chains:
        b, q, k, v, st = c["b"], c["q"], c["k"], c["v"], c["st"]
        o = _dot(c["a"].astype(BF16), v) + _dot_nt((q * jnp.exp2(b)).astype(BF16), st.astype(BF16))
        end_row = 0 if c["rev"] else T - 1
        b1 = b.reshape(1, T, K)
        b_end = b1[:, end_row:end_row + 1, :]
        kd = (k * jnp.exp2(b_end - b1).reshape(T, K)).astype(BF16)
        out.append((o, st * jnp.exp2(b_end.reshape(1, K)) + _dot_tn(v, kd)))
    return out


def _hgrn_kernel(*refs, L, has_s0, out_state):
    it = iter(refs)
    q_ref, g_f_ref, g_b_ref, v_ref, gate_ref, nw_ref = (next(it) for _ in range(6))
    s0_ref = next(it) if has_s0 else None
    o_ref = next(it)
    st_ref = next(it) if out_state else None
    of_acc, ob_acc, st_scr, tri_ref, code_ref = (next(it) for _ in range(5))

    T = SCAN_BLOCK
    P = HG_PACK
    n = L // T

    @pl.when(jnp.logical_and(pl.program_id(0) == 0, pl.program_id(1) == 0))
    def _():
        ti = lax.broadcasted_iota(jnp.int32, (T, T), 0)
        si = lax.broadcasted_iota(jnp.int32, (T, T), 1)
        x = ti ^ si
        tri_ref[0] = (si <= ti).astype(BF16)
        tri_ref[1] = (si >= ti).astype(BF16)
        lvl = jnp.where(x < HG_LEAF, 0, 3 + (x >= 16).astype(jnp.int32) + (x >= 32).astype(jnp.int32)
                        + (x >= 64).astype(jnp.int32))
        code_ref[0] = jnp.where(si <= ti, lvl, -1)
        code_ref[1] = jnp.where(si >= ti, lvl, -1)

    for h in range(P):
        for d in range(2):
            st_scr[2 * h + d] = s0_ref[d, h].T if has_s0 else jnp.zeros((HG_DV, HG_DK), F32)

    def step(i, finalize):
        sls = (pl.ds(pl.multiple_of(i * T, T), T), pl.ds(pl.multiple_of((n - 1 - i) * T, T), T))
        acc_sls = _half_acc_slices(i, n, T, finalize)
        chains = []
        for h in range(P):
            lanes = slice(h * HG_DK, (h + 1) * HG_DK)
            for d, g_ref in enumerate((g_f_ref, g_b_ref)):
                chains.append(dict(q=q_ref[sls[d], lanes].astype(F32), g=g_ref[sls[d], lanes],
                                   v=v_ref[sls[d], lanes],
                                   st=st_scr[2 * h + d], rev=bool(d)))
        res = _gla_blocks(chains, tri_ref, code_ref)
        for h in range(P):
            lanes = slice(h * HG_DK, (h + 1) * HG_DK)
            for d in range(2):
                o, st = res[2 * h + d]
                st_scr[2 * h + d] = st
                if finalize:
                    o = o + (ob_acc, of_acc)[d][acc_sls[d], lanes]
                    ms = jnp.mean(o * o, axis=-1, keepdims=True)
                    o = o * lax.rsqrt(ms + EPS) * nw_ref[:, lanes]
                    o_ref[sls[d], lanes] = (o * gate_ref[sls[d], lanes]).astype(BF16)
                else:
                    (of_acc, ob_acc)[d][acc_sls[d], lanes] = o

    def first_half(i, carry):
        step(i, False)
        return carry

    def second_half(i, carry):
        step(i, True)
        return carry

    lax.fori_loop(0, n // 2, first_half, 0, unroll=min(HG_UNROLL, n // 2))
    lax.fori_loop(n // 2, n, second_half, 0, unroll=min(HG_UNROLL, n // 2))
    if out_state:
        for h in range(P):
            for d in range(2):
                st_ref[d, h] = st_scr[2 * h + d].T


def _hgrn_call(hq, g, hi, hog, nw, s0, *, row0, B, L, out_state, name):
    has_s0 = s0 is not None
    H = HG_HEADS
    P = HG_PACK
    W = P * HG_DK
    r0 = row0 // L
    blk = lambda off: pl.BlockSpec((L, W), lambda b, h: (r0 + b, off + h))
    st_spec = pl.BlockSpec((None, 2, P, HG_DK, HG_DV), lambda b, h: (b, 0, h, 0, 0))
    in_specs = [blk(0), blk(0), blk(H // P), blk(0), blk(0), pl.BlockSpec((1, W), lambda b, h: (0, h))]
    args = [hq, g, g, hi, hog, nw]
    if has_s0:
        in_specs.append(st_spec)
        args.append(s0)
    out_specs = [pl.BlockSpec((L, W), lambda b, h: (b, h))]
    out_shape = [jax.ShapeDtypeStruct((B * L, H * HG_DV), BF16)]
    if out_state:
        out_specs.append(st_spec)
        out_shape.append(jax.ShapeDtypeStruct((B, 2, H, HG_DK, HG_DV), F32))
    return pl.pallas_call(
        functools.partial(_hgrn_kernel, L=L, has_s0=has_s0, out_state=out_state),
        grid=(B, H // P),
        in_specs=in_specs,
        out_specs=out_specs,
        out_shape=out_shape,
        scratch_shapes=[pltpu.VMEM((L // 2, W), F32), pltpu.VMEM((L // 2, W), F32),
                        pltpu.VMEM((2 * P, HG_DV, HG_DK), F32),
                        pltpu.VMEM((2, SCAN_BLOCK, SCAN_BLOCK), BF16),
                        pltpu.VMEM((2, SCAN_BLOCK, SCAN_BLOCK), jnp.int32)],
        compiler_params=_params(("arbitrary", "arbitrary")),
        name=name,
    )(*args)


def _merge_kernel(x_ref, mod_ref, oret_a, oret_b, ohg_a, ohg_b, gr_ref, gh_ref, wr_hbm, wh_hbm, wo_hbm, o_ref,
                  wr_ref, wh_ref, wo_ref, w_stage, w_sem, *, n_first):
    i = pl.program_id(0)
    for w_hbm, w_ref in ((wr_hbm, wr_ref), (wh_hbm, wh_ref), (wo_hbm, wo_ref)):
        _stage_weight(i, w_hbm, w_ref, w_stage, w_sem)
    oret = jnp.where(i < n_first, oret_a[...], oret_b[...])
    ohg = jnp.where(i < n_first, ohg_a[...], ohg_b[...])
    y = gr_ref[...] * _dot(oret, wr_ref[...]) + gh_ref[...] * _dot(ohg, wh_ref[...])
    out = _dot(y.astype(BF16), wo_ref[...])
    o_ref[...] = x_ref[...] + mod_ref[5:6, :] * out


def _merge_call(x, mod3, orets, ohgs, gr, gh, wr, wh, wo, *, mod_row_fn, tm, name):
    n = x.shape[0]
    n_first = orets[0].shape[0] // tm
    row = pl.BlockSpec((tm, D_MODEL), lambda i: (i, 0))
    first = pl.BlockSpec((tm, D_MODEL), lambda i: (jnp.minimum(i, n_first - 1), 0))
    second = pl.BlockSpec((tm, D_MODEL), lambda i: (jnp.maximum(i - n_first, 0), 0))
    resident, stage, sem = _stage_scratch(wr, 256)
    return pl.pallas_call(
        functools.partial(_merge_kernel, n_first=n_first),
        grid=(n // tm,),
        in_specs=[row, pl.BlockSpec((None, N_MOD, D_MODEL), lambda i: (mod_row_fn(i, tm), 0, 0)),
                  first, second, first, second, row, row, _HBM, _HBM, _HBM],
        out_specs=row,
        out_shape=jax.ShapeDtypeStruct((n, D_MODEL), F32),
        scratch_shapes=[resident, resident, resident, stage, sem],
        compiler_params=_params(("arbitrary",)),
        name=name,
    )(x, mod3, *orets, *ohgs, gr, gh, wr, wh, wo)


def _rope_tables(n_tokens, n_identity):
    pos = np.arange(n_tokens)
    r = (pos // GRID_W).astype(np.float32)
    cl = (pos % GRID_W).astype(np.float32)
    quarter = RET_DK // 4
    inv_freq = ROPE_BASE ** (-jnp.arange(quarter, dtype=F32) / quarter)
    ang_r = jnp.asarray(r)[:, None] * inv_freq[None, :]
    ang_c = jnp.asarray(cl)[:, None] * inv_freq[None, :]
    cos = jnp.concatenate([jnp.cos(ang_r)] * 2 + [jnp.cos(ang_c)] * 2, axis=-1)
    sins = jnp.concatenate([-jnp.sin(ang_r), jnp.sin(ang_r), -jnp.sin(ang_c), jnp.sin(ang_c)], axis=-1)
    cos = jnp.concatenate([cos, jnp.ones((n_identity, RET_DK), F32)], axis=0)
    sins = jnp.concatenate([sins, jnp.zeros((n_identity, RET_DK), F32)], axis=0)
    return cos, sins


def kernel(x_prompt, x_sample, state_ret, state_hgrn, c, c_ctx, ada_w, ada_b, norm_w, ffn1_w13, ffn1_w2,
           ffn2_w13, ffn2_w2, w_in, ret_decay, hg_lb_logits, hg_norm_w, w_ret_proj, w_hg_proj, w_o,
           final_norm_w):
    depth = ada_w.shape[0]
    assert depth == 1
    l = 0
    bp, lp, _ = x_prompt.shape
    bs, ls, _ = x_sample.shape
    n_ctx = bp * lp
    assert n_ctx % ls == 0 and ls % TM_FFN == 0 and ls % TM_PROJ == 0 and ls % TM_MERGE == 0

    cvec = jnp.concatenate([c_ctx[None, :], c, jnp.zeros((8 - 1 - bs, D_MODEL), F32)], axis=0)
    mod = _mod_call(cvec, ada_w[l], ada_b[l][None, :])
    mod3 = mod.reshape(8, N_MOD, D_MODEL)

    def mod_row(i, tm):
        return jnp.where(i < n_ctx // tm, 0, 1 + (i * tm - n_ctx) // ls)

    def rope_blk(i, tm):
        return jnp.where(i < n_ctx // tm, ls // tm, (i - n_ctx // tm) % (ls // tm))

    nw, fw = norm_w[l], final_norm_w[None, :]
    dec, hg_nw = ret_decay[l], hg_norm_w[l][None, :]

    x = _ffn_call((x_prompt.reshape(n_ctx, D_MODEL), x_sample.reshape(bs * ls, D_MODEL)), mod3, nw, fw,
                  ffn1_w13[l], ffn1_w2[l], n_out_split=None, mod_base=0, norm_row=0, final=False,
                  mod_row_fn=mod_row, tm=TM_FFN, name="ffn1")
    qk, rv, rg, hq, g, hi, hog, gr, gh = _proj_call(x, mod3, nw, hg_lb_logits, w_in[l],
                                                    _rope_tables(ls, TM_PROJ), rope_blk, layer=l,
                                                    mod_row_fn=mod_row, tm=TM_PROJ, name="proj")
    ret_ctx, s_ret = _ret_call(dec, qk, rv, rg, None, row0=0, B=bp, L=lp, out_state=True, name="ret_ctx")
    ret_lat, = _ret_call(dec, qk, rv, rg, state_ret[:, l], row0=n_ctx, B=bs, L=ls, out_state=False,
                         name="ret_lat")
    hg_ctx, s_hg = _hgrn_call(hq, g, hi, hog, hg_nw, None, row0=0, B=bp, L=lp, out_state=True,
                              name="hgrn_ctx")
    hg_lat, = _hgrn_call(hq, g, hi, hog, hg_nw, state_hgrn[:, l], row0=n_ctx, B=bs, L=ls, out_state=False,
                         name="hgrn_lat")
    x = _merge_call(x, mod3, (ret_ctx, ret_lat), (hg_ctx, hg_lat), gr, gh, w_ret_proj[l], w_hg_proj[l], w_o[l],
                    mod_row_fn=mod_row, tm=TM_MERGE, name="merge")
    yp, ys = _ffn_call((x,), mod3, nw, fw, ffn2_w13[l], ffn2_w2[l], n_out_split=n_ctx, mod_base=6,
                       norm_row=2, final=True, mod_row_fn=mod_row, tm=TM_FFN, name="ffn2")
    return (yp.reshape(bp, lp, D_MODEL), ys.reshape(bs, ls, D_MODEL), s_ret[:, None], s_hg[:, None])
```

```python
import functools

import numpy as np
import jax
import jax.numpy as jnp
from jax import lax
from jax.experimental import pallas as pl
from jax.experimental.pallas import tpu as pltpu

F32 = jnp.float32
BF16 = jnp.bfloat16

D_MODEL = 1024
GRID_W = 64
RET_HEADS = 4
RET_DK = 128
RET_DV = 256
HG_HEADS = 8
HG_DK = 128
HG_DV = 128
D_FF = 2816
N_MOD = 9
ROPE_BASE = 10000.0
EPS = 1e-6

C_RQ, C_RK, C_RV, C_RG, C_HQ, C_HFF, C_HFB, C_HI, C_HOG, C_GR, C_GH = (
    0, 512, 1024, 2048, 3072, 4096, 5120, 6144, 7168, 8192, 9216)
IN_WIDTH = 10240

VMEM_LIMIT_BYTES = 56 * 1024 * 1024

TM_FFN = 512
TM_PROJ = 256
TM_MERGE = 512
STAGE_SLOTS = 4
SCAN_BLOCK = 128
RET_PACK = 4
RET_UNROLL = 1
HG_PACK = 4
HG_UNROLL = 2
FF_CHUNK = 256
PROJ_CHUNK = 256
LOG2E = 1.4426950408889634
EXP2_CLAMP = 115.0
HG_LEAF = 16
HG_LEVELS = ((4, 16), (5, 32), (6, 64))


def _dot(a, b):
    return jnp.dot(a, b, preferred_element_type=F32)


def _dot_nt(a, b):
    return lax.dot_general(a, b, (((1,), (1,)), ((), ())), preferred_element_type=F32)


def _dot_tn(a, b):
    return lax.dot_general(a, b, (((0,), (0,)), ((), ())), preferred_element_type=F32)


def _sigmoid(x):
    return 1.0 / (1.0 + jnp.exp(-x))


def _rms(x, w):
    ms = jnp.mean(x * x, axis=-1, keepdims=True)
    return x * lax.rsqrt(ms + EPS) * w


def _params(sem):
    return pltpu.CompilerParams(dimension_semantics=sem, vmem_limit_bytes=VMEM_LIMIT_BYTES)


def _stage_weight(step, w_hbm, w_vmem, stage, sem):
    rows = stage.shape[1]
    n_chunks = w_hbm.shape[0] // rows
    ahead = STAGE_SLOTS - 1

    def copy(k):
        slot = k % STAGE_SLOTS
        return pltpu.make_async_copy(w_hbm.at[pl.ds(k * rows, rows), :], stage.at[slot], sem.at[slot])

    @pl.when(step == 0)
    def _():
        for k in range(min(ahead, n_chunks)):
            copy(k).start()

        def body(k, carry):
            @pl.when(k + ahead < n_chunks)
            def _():
                copy(k + ahead).start()

            copy(k).wait()
            w_vmem[pl.ds(pl.multiple_of(k * rows, rows), rows), :] = stage[k % STAGE_SLOTS].astype(BF16)
            return carry

        lax.fori_loop(0, n_chunks, body, 0)


_HBM = pl.BlockSpec(memory_space=pl.ANY)


def _stage_scratch(w, rows):
    r, c = w.shape
    assert r % rows == 0 and rows % 16 == 0
    return [pltpu.VMEM((r, c), BF16), pltpu.VMEM((STAGE_SLOTS, rows, c), F32),
            pltpu.SemaphoreType.DMA((STAGE_SLOTS,))]


def _mod_kernel(c_ref, w_ref, b_ref, o_ref):
    c = c_ref[...]
    s = (c * _sigmoid(c)).astype(BF16)
    o_ref[...] = _dot(s, w_ref[...].astype(BF16)) + b_ref[...]


def _mod_call(cvec, ada_w, ada_b):
    n = ada_w.shape[1]
    tn = 1024
    return pl.pallas_call(
        _mod_kernel,
        grid=(n // tn,),
        in_specs=[
            pl.BlockSpec((8, D_MODEL), lambda j: (0, 0)),
            pl.BlockSpec((D_MODEL, tn), lambda j: (0, j)),
            pl.BlockSpec((1, tn), lambda j: (0, j)),
        ],
        out_specs=pl.BlockSpec((8, tn), lambda j: (0, j)),
        out_shape=jax.ShapeDtypeStruct((8, n), F32),
        compiler_params=_params(("arbitrary",)),
        name="mod",
    )(cvec, ada_w, ada_b)


def _ffn_kernel(*refs, mod_base, norm_row, final, n_first, two_in, two_out):
    it = iter(refs)
    xa_ref = next(it)
    xb_ref = next(it) if two_in else None
    mod_ref, nw_ref, fw_ref, w13_hbm, w2_hbm = (next(it) for _ in range(5))
    oa_ref = next(it)
    ob_ref = next(it) if two_out else None
    w13_ref, st13, sem13, w2_ref, st2, sem2 = (next(it) for _ in range(6))
    i = pl.program_id(0)
    _stage_weight(i, w13_hbm, w13_ref, st13, sem13)
    _stage_weight(i, w2_hbm, w2_ref, st2, sem2)
    x = xa_ref[...]
    if two_in:
        x = jnp.where(i < n_first, x, xb_ref[...])
    y = _rms(x, nw_ref[norm_row:norm_row + 1, :])
    sh = mod_ref[mod_base:mod_base + 1, :]
    sc = mod_ref[mod_base + 1:mod_base + 2, :]
    h = (y * (1.0 + sc) + sh).astype(BF16)
    acc = None
    for c0 in range(0, D_FF, FF_CHUNK):
        a = _dot(h, w13_ref[:, c0:c0 + FF_CHUNK])
        b = _dot(h, w13_ref[:, D_FF + c0:D_FF + c0 + FF_CHUNK])
        act = (a * _sigmoid(a) * b).astype(BF16)
        part = _dot(act, w2_ref[c0:c0 + FF_CHUNK, :])
        acc = part if acc is None else acc + part
    g = mod_ref[mod_base + 2:mod_base + 3, :]
    xo = x + 0.5 * g * acc
    if final:
        xo = _rms(xo, fw_ref[...])
    if two_out:
        @pl.when(i < n_first)
        def _():
            oa_ref[...] = xo

        @pl.when(i >= n_first)
        def _():
            ob_ref[...] = xo
    else:
        oa_ref[...] = xo


def _ffn_call(xs, mod3, norm_w, final_w, w13, w2, *, n_out_split, mod_base, norm_row, final, mod_row_fn, tm,
              name):
    two_in = len(xs) == 2
    two_out = n_out_split is not None
    n = sum(x.shape[0] for x in xs)
    n_first = (xs[0].shape[0] if two_in else n_out_split if two_out else n) // tm
    first = pl.BlockSpec((tm, D_MODEL), lambda i: (jnp.minimum(i, n_first - 1), 0))
    second = pl.BlockSpec((tm, D_MODEL), lambda i: (jnp.maximum(i - n_first, 0), 0))
    whole = pl.BlockSpec((tm, D_MODEL), lambda i: (i, 0))
    kern = functools.partial(_ffn_kernel, mod_base=mod_base, norm_row=norm_row, final=final, n_first=n_first,
                             two_in=two_in, two_out=two_out)
    if two_out:
        out_specs = [first, second]
        out_shape = [jax.ShapeDtypeStruct((n_out_split, D_MODEL), F32),
                     jax.ShapeDtypeStruct((n - n_out_split, D_MODEL), F32)]
    else:
        out_specs = whole
        out_shape = jax.ShapeDtypeStruct((n, D_MODEL), F32)
    return pl.pallas_call(
        kern,
        grid=(n // tm,),
        in_specs=([first, second] if two_in else [whole]) + [
            pl.BlockSpec((None, N_MOD, D_MODEL), lambda i: (mod_row_fn(i, tm), 0, 0)),
            pl.BlockSpec((3, D_MODEL), lambda i: (0, 0)),
            pl.BlockSpec((1, D_MODEL), lambda i: (0, 0)),
            _HBM,
            _HBM,
        ],
        out_specs=out_specs,
        out_shape=out_shape,
        scratch_shapes=_stage_scratch(w13, 64) + _stage_scratch(w2, 256),
        compiler_params=_params(("arbitrary",)),
        name=name,
    )(*xs, mod3, norm_w, final_w, w13, w2)


def _rope(x, cos, sins, first_quarter):
    partner = jnp.where(first_quarter, pltpu.roll(x, 96, 1), pltpu.roll(x, 32, 1))
    return x * cos + partner * sins


def _proj_kernel(*refs, layer, rope):
    it = iter(refs)
    x_ref, mod_ref, nw_ref, lbl_ref, w_hbm = (next(it) for _ in range(5))
    cos_ref, sin_ref = (next(it), next(it)) if rope else (None, None)
    qk_ref, rv_ref, rg_ref, hq_ref, g_ref, hi_ref, hog_ref, gr_ref, gh_ref = (next(it) for _ in range(9))
    w_ref, w_stage, w_sem = next(it), next(it), next(it)
    _stage_weight(pl.program_id(0), w_hbm, w_ref, w_stage, w_sem)
    y = _rms(x_ref[...], nw_ref[1:2, :])
    h = (y * (1.0 + mod_ref[4:5, :]) + mod_ref[3:4, :]).astype(BF16)

    if rope:
        lane = lax.broadcasted_iota(jnp.int32, (x_ref.shape[0], RET_DK), 1)
        first_quarter = (lane & 32) == 0

    def rot(t):
        if not rope:
            return t
        heads = [_rope(t[:, j:j + RET_DK], cos_ref[...], sin_ref[...], first_quarter)
                 for j in range(0, t.shape[1], RET_DK)]
        return jnp.concatenate(heads, axis=1)

    def silu(t):
        return t * _sigmoid(t)

    def log_forget(d):
        def fn(t, c):
            lg = lbl_ref[d, :, c:c + PROJ_CHUNK]
            e = jnp.exp(lg - jnp.max(lg, axis=0, keepdims=True))
            lb = jnp.sum(e[0:layer + 1, :], axis=0, keepdims=True) / jnp.sum(e, axis=0, keepdims=True)
            return jnp.log(lb + (1.0 - lb) * _sigmoid(t)) * LOG2E
        return fn

    parts = (
        (C_RQ, 512, qk_ref, 0, lambda t, c: rot(t)),
        (C_RK, 512, qk_ref, 512, lambda t, c: rot(t * (RET_DK ** -0.5))),
        (C_RV, 1024, rv_ref, 0, lambda t, c: t),
        (C_RG, 1024, rg_ref, 0, lambda t, c: silu(t)),
        (C_HQ, 1024, hq_ref, 0, lambda t, c: silu(t) * (HG_DK ** -0.5)),
        (C_HFF, 1024, g_ref, 0, log_forget(0)),
        (C_HFB, 1024, g_ref, 1024, log_forget(1)),
        (C_HI, 1024, hi_ref, 0, lambda t, c: t),
        (C_HOG, 1024, hog_ref, 0, lambda t, c: silu(t)),
        (C_GR, 1024, gr_ref, 0, lambda t, c: _sigmoid(t)),
        (C_GH, 1024, gh_ref, 0, lambda t, c: _sigmoid(t)),
    )
    for w0, width, o_ref, o0, fn in parts:
        for c in range(0, width, PROJ_CHUNK):
            t = _dot(h, w_ref[:, w0 + c:w0 + c + PROJ_CHUNK])
            o_ref[:, o0 + c:o0 + c + PROJ_CHUNK] = fn(t, c).astype(o_ref.dtype)


def _proj_call(x, mod3, norm_w, lb_logits, w_in, rope_tabs, rope_blk_fn, *, layer, mod_row_fn, tm, name):
    n = x.shape[0]
    slots = lb_logits.shape[1]
    row = lambda w: pl.BlockSpec((tm, w), lambda i: (i, 0))
    outs = [(1024, BF16), (1024, BF16), (1024, BF16), (1024, BF16), (2048, F32),
            (1024, BF16), (1024, BF16), (1024, BF16), (1024, BF16)]
    in_specs = [
        row(D_MODEL),
        pl.BlockSpec((None, N_MOD, D_MODEL), lambda i: (mod_row_fn(i, tm), 0, 0)),
        pl.BlockSpec((3, D_MODEL), lambda i: (0, 0)),
        pl.BlockSpec((2, slots, 1024), lambda i: (0, 0, 0)),
        _HBM,
    ]
    args = [x, mod3, norm_w, lb_logits, w_in]
    if rope_tabs is not None:
        in_specs += [pl.BlockSpec((tm, RET_DK), lambda i: (rope_blk_fn(i, tm), 0))] * 2
        args += list(rope_tabs)
    return pl.pallas_call(
        functools.partial(_proj_kernel, layer=layer, rope=rope_tabs is not None),
        grid=(n // tm,),
        in_specs=in_specs,
        out_specs=[row(w) for w, _ in outs],
        out_shape=[jax.ShapeDtypeStruct((n, w), dt) for w, dt in outs],
        scratch_shapes=_stage_scratch(w_in, 64),
        compiler_params=_params(("arbitrary",)),
        name=name,
    )(*args)


def _log_sigmoid(x):
    return -(jnp.maximum(-x, 0.0) + jnp.log1p(jnp.exp(-jnp.abs(x))))


def _half_acc_slices(i, n, blk, finalize):
    if finalize:
        starts = ((i - n // 2) * blk, (n - 1 - i) * blk)
    else:
        starts = (i * blk, (n // 2 - 1 - i) * blk)
    return tuple(pl.ds(pl.multiple_of(s, blk), blk) for s in starts)


def _ret_kernel(*refs, L, has_s0, out_state):
    it = iter(refs)
    dec_ref = next(it)
    q_ref, k_ref, v_ref, gate_ref = next(it), next(it), next(it), next(it)
    s0_ref = next(it) if has_s0 else None
    o_ref = next(it)
    st_ref = next(it) if out_state else None
    of_acc, ob_acc, st_scr, cst = next(it), next(it), next(it), next(it)

    C = SCAN_BLOCK
    P = RET_PACK
    n = L // C
    pair = pl.program_id(1)
    ii = lax.broadcasted_iota(jnp.int32, (C, C), 0).astype(F32)
    jj = lax.broadcasted_iota(jnp.int32, (C, C), 1).astype(F32)
    diff = ii - jj

    @pl.when(jnp.logical_and(pl.program_id(0) == 0, pair == 0))
    def _():
        for hd in range(RET_HEADS):
            lgf = _log_sigmoid(jnp.full((C, C), dec_ref[0, hd], F32))
            lgb = _log_sigmoid(jnp.full((C, C), dec_ref[1, hd], F32))
            cst[hd, 0] = (jnp.where(diff >= 0, jnp.exp(jnp.maximum(diff, 0.0) * lgf), 0.0)
                          + jnp.where(diff <= 0, jnp.exp(jnp.maximum(-diff, 0.0) * lgb), 0.0))
            cst[hd, 1] = jnp.exp((ii + 1.0) * lgf)
            cst[hd, 2] = jnp.exp((C - 1.0 - ii) * lgf)
            cst[hd, 3] = jnp.exp((C - ii) * lgb)
            cst[hd, 4] = jnp.exp(ii * lgb)

    cds = []
    for h in range(P):
        hd = pair * P + h
        cds.append([jnp.exp(C * _log_sigmoid(jnp.full((1, RET_DV), dec_ref[d, hd], F32))) for d in range(2)])
        for d in range(2):
            st_scr[2 * h + d] = s0_ref[d, h] if has_s0 else jnp.zeros((RET_DK, RET_DV), F32)

    def step(i, finalize):
        sls = (pl.ds(pl.multiple_of(i * C, C), C), pl.ds(pl.multiple_of((n - 1 - i) * C, C), C))
        acc_sls = _half_acc_slices(i, n, C, finalize)
        chains = []
        for h in range(P):
            ql = slice(h * RET_DK, (h + 1) * RET_DK)
            vl = slice(h * RET_DV, (h + 1) * RET_DV)
            for d in range(2):
                chains.append(dict(h=h, d=d, vl=vl, q=q_ref[sls[d], ql], k=k_ref[sls[d], ql],
                                   v=v_ref[sls[d], vl]))
        for c in chains:
            if c["d"] == 0:
                c["s"] = _dot_nt(c["q"], c["k"])
        for c in chains:
            h, d, v, vl = c["h"], c["d"], c["v"], c["vl"]
            hd = pair * P + h
            st = st_scr[2 * h + d]
            qd = (c["q"].astype(F32) * cst[hd, 1 + 2 * d]).astype(BF16)
            if d == 0:
                p = (c["s"] * cst[hd, 0]).astype(BF16)
                o = _dot(jnp.concatenate([p, qd], axis=1), jnp.concatenate([v, st.astype(BF16)], axis=0))
            else:
                o = _dot(qd, st.astype(BF16))
            kd = (c["k"].astype(F32) * cst[hd, 2 + 2 * d]).astype(BF16)
            st_scr[2 * h + d] = cds[h][d] * st + _dot_tn(kd, v)
            if finalize:
                o = o + (ob_acc, of_acc)[d][acc_sls[d], vl]
                ms = jnp.mean(o * o, axis=-1, keepdims=True)
                o_ref[sls[d], vl] = (o * lax.rsqrt(ms + EPS) * gate_ref[sls[d], vl]).astype(BF16)
            else:
                (of_acc, ob_acc)[d][acc_sls[d], vl] = o

    def first_half(i, carry):
        step(i, False)
        return carry

    def second_half(i, carry):
        step(i, True)
        return carry

    lax.fori_loop(0, n // 2, first_half, 0, unroll=min(RET_UNROLL, n // 2))
    lax.fori_loop(n // 2, n, second_half, 0, unroll=min(RET_UNROLL, n // 2))
    if out_state:
        for h in range(P):
            for d in range(2):
                st_ref[d, h] = st_scr[2 * h + d]


def _ret_call(dec, qk, rv, rg, s0, *, row0, B, L, out_state, name):
    has_s0 = s0 is not None
    H = RET_HEADS
    P = RET_PACK
    r0 = row0 // L
    st_spec = pl.BlockSpec((None, 2, P, RET_DK, RET_DV), lambda b, h: (b, 0, h, 0, 0))
    in_specs = [
        pl.BlockSpec(memory_space=pltpu.SMEM),
        pl.BlockSpec((L, P * RET_DK), lambda b, h: (r0 + b, h)),
        pl.BlockSpec((L, P * RET_DK), lambda b, h: (r0 + b, H // P + h)),
        pl.BlockSpec((L, P * RET_DV), lambda b, h: (r0 + b, h)),
        pl.BlockSpec((L, P * RET_DV), lambda b, h: (r0 + b, h)),
    ]
    args = [dec, qk, qk, rv, rg]
    if has_s0:
        in_specs.append(st_spec)
        args.append(s0)
    out_specs = [pl.BlockSpec((L, P * RET_DV), lambda b, h: (b, h))]
    out_shape = [jax.ShapeDtypeStruct((B * L, H * RET_DV), BF16)]
    if out_state:
        out_specs.append(st_spec)
        out_shape.append(jax.ShapeDtypeStruct((B, 2, H, RET_DK, RET_DV), F32))
    res = pl.pallas_call(
        functools.partial(_ret_kernel, L=L, has_s0=has_s0, out_state=out_state),
        grid=(B, H // P),
        in_specs=in_specs,
        out_specs=out_specs,
        out_shape=out_shape,
        scratch_shapes=[pltpu.VMEM((L // 2, P * RET_DV), F32), pltpu.VMEM((L // 2, P * RET_DV), F32),
                        pltpu.VMEM((2 * P, RET_DK, RET_DV), F32),
                        pltpu.VMEM((H, 5, SCAN_BLOCK, SCAN_BLOCK), F32)],
        compiler_params=_params(("arbitrary", "arbitrary")),
        name=name,
    )(*args)
    return res


def _gla_blocks(chains, tri_ref, code_ref):
    T = SCAN_BLOCK
    K = HG_DK
    for c in chains:
        g2 = c["g"]
        c["k"] = 1.0 - jnp.exp2(g2)
        hi = g2.astype(BF16)
        lo = (g2 - hi.astype(F32)).astype(BF16)
        tri = tri_ref[int(c["rev"])]
        c["b"] = _dot(tri, hi) + _dot(tri, lo)

    for c in chains:
        bl = c["b"].reshape(T // HG_LEAF, HG_LEAF, K)
        r = HG_LEAF // 2 if c["rev"] else HG_LEAF // 2 - 1
        d = jnp.clip(bl - bl[:, r:r + 1, :], -EXP2_CLAMP, EXP2_CLAMP).reshape(T, K)
        e = jnp.exp2(d)
        p = _dot_nt((c["q"] * e).astype(BF16), (c["k"] / e).astype(BF16))
        c["a"] = jnp.where(code_ref[int(c["rev"])] == 0, p, 0.0)

    for lv, m in HG_LEVELS:
        seg = 2 * m
        ng = T // seg
        for c in chains:
            b3, q3, k3 = (c[n].reshape(ng, seg, K) for n in ("b", "q", "k"))
            if c["rev"]:
                qs, ks, r = slice(0, m), slice(m, seg), m
            else:
                qs, ks, r = slice(m, seg), slice(0, m), m - 1
            ref = b3[:, r:r + 1, :]
            qm = (q3[:, qs, :] * jnp.exp2(b3[:, qs, :] - ref)).reshape(ng * m, K).astype(BF16)
            kz = k3[:, ks, :] * jnp.exp2(ref - b3[:, ks, :])
            zeros = jnp.zeros((ng, m, K), F32)
            kfull = jnp.concatenate([zeros, kz] if c["rev"] else [kz, zeros], axis=1)
            p = _dot_nt(qm, kfull.reshape(T, K).astype(BF16)).reshape(ng, m, T)
            a3 = c["a"].reshape(ng, seg, T)
            c3 = code_ref[int(c["rev"])].reshape(ng, seg, T)
            upd = jnp.where(c3[:, qs, :] == lv, p, a3[:, qs, :])
            parts = [upd, a3[:, ks, :]] if c["rev"] else [a3[:, ks, :], upd]
            c["a"] = jnp.concatenate(parts, axis=1).reshape(T, T)

    out = []
    for c in chains:
        b, q, k, v, st = c["b"], c["q"], c["k"], c["v"], c["st"]
        o = _dot(c["a"].astype(BF16), v) + _dot_nt((q * jnp.exp2(b)).astype(BF16), st.astype(BF16))
        end_row = 0 if c["rev"] else T - 1
        b1 = b.reshape(1, T, K)
        b_end = b1[:, end_row:end_row + 1, :]
        kd = (k * jnp.exp2(b_end - b1).reshape(T, K)).astype(BF16)
        out.append((o, st * jnp.exp2(b_end.reshape(1, K)) + _dot_tn(v, kd)))
    return out


def _hgrn_kernel(*refs, L, has_s0, out_state):
    it = iter(refs)
    q_ref, g_f_ref, g_b_ref, v_ref, gate_ref, nw_ref = (next(it) for _ in range(6))
    s0_ref = next(it) if has_s0 else None
    o_ref = next(it)
    st_ref = next(it) if out_state else None
    of_acc, ob_acc, st_scr, tri_ref, code_ref = (next(it) for _ in range(5))

    T = SCAN_BLOCK
    P = HG_PACK
    n = L // T

    @pl.when(jnp.logical_and(pl.program_id(0) == 0, pl.program_id(1) == 0))
    def _():
        ti = lax.broadcasted_iota(jnp.int32, (T, T), 0)
        si = lax.broadcasted_iota(jnp.int32, (T, T), 1)
        x = ti ^ si
        tri_ref[0] = (si <= ti).astype(BF16)
        tri_ref[1] = (si >= ti).astype(BF16)
        lvl = jnp.where(x < HG_LEAF, 0, 3 + (x >= 16).astype(jnp.int32) + (x >= 32).astype(jnp.int32)
                        + (x >= 64).astype(jnp.int32))
        code_ref[0] = jnp.where(si <= ti, lvl, -1)
        code_ref[1] = jnp.where(si >= ti, lvl, -1)

    for h in range(P):
        for d in range(2):
            st_scr[2 * h + d] = s0_ref[d, h].T if has_s0 else jnp.zeros((HG_DV, HG_DK), F32)

    def step(i, finalize):
        sls = (pl.ds(pl.multiple_of(i * T, T), T), pl.ds(pl.multiple_of((n - 1 - i) * T, T), T))
        acc_sls = _half_acc_slices(i, n, T, finalize)
        chains = []
        for h in range(P):
            lanes = slice(h * HG_DK, (h + 1) * HG_DK)
            for d, g_ref in enumerate((g_f_ref, g_b_ref)):
                chains.append(dict(q=q_ref[sls[d], lanes].astype(F32), g=g_ref[sls[d], lanes],
                                   v=v_ref[sls[d], lanes],
                                   st=st_scr[2 * h + d], rev=bool(d)))
        res = _gla_blocks(chains, tri_ref, code_ref)
        for h in range(P):
            lanes = slice(h * HG_DK, (h + 1) * HG_DK)
            for d in range(2):
                o, st = res[2 * h + d]
                st_scr[2 * h + d] = st
                if finalize:
                    o = o + (ob_acc, of_acc)[d][acc_sls[d], lanes]
                    ms = jnp.mean(o * o, axis=-1, keepdims=True)
                    o = o * lax.rsqrt(ms + EPS) * nw_ref[:, lanes]
                    o_ref[sls[d], lanes] = (o * gate_ref[sls[d], lanes]).astype(BF16)
                else:
                    (of_acc, ob_acc)[d][acc_sls[d], lanes] = o

    def first_half(i, carry):
        step(i, False)
        return carry

    def second_half(i, carry):
        step(i, True)
        return carry

    lax.fori_loop(0, n // 2, first_half, 0, unroll=min(HG_UNROLL, n // 2))
    lax.fori_loop(n // 2, n, second_half, 0, unroll=min(HG_UNROLL, n // 2))
    if out_state:
        for h in range(P):
            for d in range(2):
                st_ref[d, h] = st_scr[2 * h + d].T


def _hgrn_call(hq, g, hi, hog, nw, s0, *, row0, B, L, out_state, name):
    has_s0 = s0 is not None
    H = HG_HEADS
    P = HG_PACK
    W = P * HG_DK
    r0 = row0 // L
    blk = lambda off: pl.BlockSpec((L, W), lambda b, h: (r0 + b, off + h))
    st_spec = pl.BlockSpec((None, 2, P, HG_DK, HG_DV), lambda b, h: (b, 0, h, 0, 0))
    in_specs = [blk(0), blk(0), blk(H // P), blk(0), blk(0), pl.BlockSpec((1, W), lambda b, h: (0, h))]
    args = [hq, g, g, hi, hog, nw]
    if has_s0:
        in_specs.append(st_spec)
        args.append(s0)
    out_specs = [pl.BlockSpec((L, W), lambda b, h: (b, h))]
    out_shape = [jax.ShapeDtypeStruct((B * L, H * HG_DV), BF16)]
    if out_state:
        out_specs.append(st_spec)
        out_shape.append(jax.ShapeDtypeStruct((B, 2, H, HG_DK, HG_DV), F32))
    return pl.pallas_call(
        functools.partial(_hgrn_kernel, L=L, has_s0=has_s0, out_state=out_state),
        grid=(B, H // P),
        in_specs=in_specs,
        out_specs=out_specs,
        out_shape=out_shape,
        scratch_shapes=[pltpu.VMEM((L // 2, W), F32), pltpu.VMEM((L // 2, W), F32),
                        pltpu.VMEM((2 * P, HG_DV, HG_DK), F32),
                        pltpu.VMEM((2, SCAN_BLOCK, SCAN_BLOCK), BF16),
                        pltpu.VMEM((2, SCAN_BLOCK, SCAN_BLOCK), jnp.int32)],
        compiler_params=_params(("arbitrary", "arbitrary")),
        name=name,
    )(*args)


def _merge_kernel(x_ref, mod_ref, oret_a, oret_b, ohg_a, ohg_b, gr_ref, gh_ref, wr_hbm, wh_hbm, wo_hbm, o_ref,
                  wr_ref, wh_ref, wo_ref, w_stage, w_sem, *, n_first):
    i = pl.program_id(0)
    for w_hbm, w_ref in ((wr_hbm, wr_ref), (wh_hbm, wh_ref), (wo_hbm, wo_ref)):
        _stage_weight(i, w_hbm, w_ref, w_stage, w_sem)
    oret = jnp.where(i < n_first, oret_a[...], oret_b[...])
    ohg = jnp.where(i < n_first, ohg_a[...], ohg_b[...])
    y = gr_ref[...] * _dot(oret, wr_ref[...]) + gh_ref[...] * _dot(ohg, wh_ref[...])
    out = _dot(y.astype(BF16), wo_ref[...])
    o_ref[...] = x_ref[...] + mod_ref[5:6, :] * out


def _merge_call(x, mod3, orets, ohgs, gr, gh, wr, wh, wo, *, mod_row_fn, tm, name):
    n = x.shape[0]
    n_first = orets[0].shape[0] // tm
    row = pl.BlockSpec((tm, D_MODEL), lambda i: (i, 0))
    first = pl.BlockSpec((tm, D_MODEL), lambda i: (jnp.minimum(i, n_first - 1), 0))
    second = pl.BlockSpec((tm, D_MODEL), lambda i: (jnp.maximum(i - n_first, 0), 0))
    resident, stage, sem = _stage_scratch(wr, 256)
    return pl.pallas_call(
        functools.partial(_merge_kernel, n_first=n_first),
        grid=(n // tm,),
        in_specs=[row, pl.BlockSpec((None, N_MOD, D_MODEL), lambda i: (mod_row_fn(i, tm), 0, 0)),
                  first, second, first, second, row, row, _HBM, _HBM, _HBM],
        out_specs=row,
        out_shape=jax.ShapeDtypeStruct((n, D_MODEL), F32),
        scratch_shapes=[resident, resident, resident, stage, sem],
        compiler_params=_params(("arbitrary",)),
        name=name,
    )(x, mod3, *orets, *ohgs, gr, gh, wr, wh, wo)


def _rope_tables(n_tokens, n_identity):
    pos = np.arange(n_tokens)
    r = (pos // GRID_W).astype(np.float32)
    cl = (pos % GRID_W).astype(np.float32)
    quarter = RET_DK // 4
    inv_freq = ROPE_BASE ** (-jnp.arange(quarter, dtype=F32) / quarter)
    ang_r = jnp.asarray(r)[:, None] * inv_freq[None, :]
    ang_c = jnp.asarray(cl)[:, None] * inv_freq[None, :]
    cos = jnp.concatenate([jnp.cos(ang_r)] * 2 + [jnp.cos(ang_c)] * 2, axis=-1)
    sins = jnp.concatenate([-jnp.sin(ang_r), jnp.sin(ang_r), -jnp.sin(ang_c), jnp.sin(ang_c)], axis=-1)
    cos = jnp.concatenate([cos, jnp.ones((n_identity, RET_DK), F32)], axis=0)
    sins = jnp.concatenate([sins, jnp.zeros((n_identity, RET_DK), F32)], axis=0)
    return cos, sins


def kernel(x_prompt, x_sample, state_ret, state_hgrn, c, c_ctx, ada_w, ada_b, norm_w, ffn1_w13, ffn1_w2,
           ffn2_w13, ffn2_w2, w_in, ret_decay, hg_lb_logits, hg_norm_w, w_ret_proj, w_hg_proj, w_o,
           final_norm_w):
    depth = ada_w.shape[0]
    assert depth == 1
    l = 0
    bp, lp, _ = x_prompt.shape
    bs, ls, _ = x_sample.shape
    n_ctx = bp * lp
    assert n_ctx % ls == 0 and ls % TM_FFN == 0 and ls % TM_PROJ == 0 and ls % TM_MERGE == 0

    cvec = jnp.concatenate([c_ctx[None, :], c, jnp.zeros((8 - 1 - bs, D_MODEL), F32)], axis=0)
    mod = _mod_call(cvec, ada_w[l], ada_b[l][None, :])
    mod3 = mod.reshape(8, N_MOD, D_MODEL)

    def mod_row(i, tm):
        return jnp.where(i < n_ctx // tm, 0, 1 + (i * tm - n_ctx) // ls)

    def rope_blk(i, tm):
        return jnp.where(i < n_ctx // tm, ls // tm, (i - n_ctx // tm) % (ls // tm))

    nw, fw = norm_w[l], final_norm_w[None, :]
    dec, hg_nw = ret_decay[l], hg_norm_w[l][None, :]

    x = _ffn_call((x_prompt.reshape(n_ctx, D_MODEL), x_sample.reshape(bs * ls, D_MODEL)), mod3, nw, fw,
                  ffn1_w13[l], ffn1_w2[l], n_out_split=None, mod_base=0, norm_row=0, final=False,
                  mod_row_fn=mod_row, tm=TM_FFN, name="ffn1")
    qk, rv, rg, hq, g, hi, hog, gr, gh = _proj_call(x, mod3, nw, hg_lb_logits, w_in[l],
                                                    _rope_tables(ls, TM_PROJ), rope_blk, layer=l,
                                                    mod_row_fn=mod_row, tm=TM_PROJ, name="proj")
    ret_ctx, s_ret = _ret_call(dec, qk, rv, rg, None, row0=0, B=bp, L=lp, out_state=True, name="ret_ctx")
    ret_lat, = _ret_call(dec, qk, rv, rg, state_ret[:, l], row0=n_ctx, B=bs, L=ls, out_state=False,
                         name="ret_lat")
    hg_ctx, s_hg = _hgrn_call(hq, g, hi, hog, hg_nw, None, row0=0, B=bp, L=lp, out_state=True,
                              name="hgrn_ctx")
    hg_lat, = _hgrn_call(hq, g, hi, hog, hg_nw, state_hgrn[:, l], row0=n_ctx, B=bs, L=ls, out_state=False,
                         name="hgrn_lat")
    x = _merge_call(x, mod3, (ret_ctx, ret_lat), (hg_ctx, hg_lat), gr, gh, w_ret_proj[l], w_hg_proj[l], w_o[l],
                    mod_row_fn=mod_row, tm=TM_MERGE, name="merge")
    yp, ys = _ffn_call((x,), mod3, nw, fw, ffn2_w13[l], ffn2_w2[l], n_out_split=n_ctx, mod_base=6,
                       norm_row=2, final=True, mod_row_fn=mod_row, tm=TM_FFN, name="ffn2")
    return (yp.reshape(bp, lp, D_MODEL), ys.reshape(bs, ls, D_MODEL), s_ret[:, None], s_hg[:, None])
```

```python
import functools

import numpy as np
import jax
import jax.numpy as jnp
from jax import lax
from jax.experimental import pallas as pl
from jax.experimental.pallas import tpu as pltpu

F32 = jnp.float32
BF16 = jnp.bfloat16

D_MODEL = 1024
GRID_W = 64
RET_HEADS = 4
RET_DK = 128
RET_DV = 256
HG_HEADS = 8
HG_DK = 128
HG_DV = 128
D_FF = 2816
N_MOD = 9
ROPE_BASE = 10000.0
EPS = 1e-6

C_RQ, C_RK, C_RV, C_RG, C_HQ, C_HFF, C_HFB, C_HI, C_HOG, C_GR, C_GH = (
    0, 512, 1024, 2048, 3072, 4096, 5120, 6144, 7168, 8192, 9216)
IN_WIDTH = 10240

VMEM_LIMIT_BYTES = 56 * 1024 * 1024

TM_FFN = 512
TM_PROJ = 256
TM_MERGE = 512
STAGE_SLOTS = 4
SCAN_BLOCK = 128
RET_PACK = 4
RET_UNROLL = 1
HG_PACK = 4
HG_UNROLL = 2
FF_CHUNK = 256
PROJ_CHUNK = 256
LOG2E = 1.4426950408889634
EXP2_CLAMP = 115.0
HG_LEAF = 16
HG_LEVELS = ((4, 16), (5, 32), (6, 64))


def _dot(a, b):
    return jnp.dot(a, b, preferred_element_type=F32)


def _dot_nt(a, b):
    return lax.dot_general(a, b, (((1,), (1,)), ((), ())), preferred_element_type=F32)


def _dot_tn(a, b):
    return lax.dot_general(a, b, (((0,), (0,)), ((), ())), preferred_element_type=F32)


def _sigmoid(x):
    return 1.0 / (1.0 + jnp.exp(-x))


def _rms(x, w):
    ms = jnp.mean(x * x, axis=-1, keepdims=True)
    return x * lax.rsqrt(ms + EPS) * w


def _params(sem):
    return pltpu.CompilerParams(dimension_semantics=sem, vmem_limit_bytes=VMEM_LIMIT_BYTES)


def _stage_weight(step, w_hbm, w_vmem, stage, sem):
    rows = stage.shape[1]
    n_chunks = w_hbm.shape[0] // rows
    ahead = STAGE_SLOTS - 1

    def copy(k):
        slot = k % STAGE_SLOTS
        return pltpu.make_async_copy(w_hbm.at[pl.ds(k * rows, rows), :], stage.at[slot], sem.at[slot])

    @pl.when(step == 0)
    def _():
        for k in range(min(ahead, n_chunks)):
            copy(k).start()

        def body(k, carry):
            @pl.when(k + ahead < n_chunks)
            def _():
                copy(k + ahead).start()

            copy(k).wait()
            w_vmem[pl.ds(pl.multiple_of(k * rows, rows), rows), :] = stage[k % STAGE_SLOTS].astype(BF16)
            return carry

        lax.fori_loop(0, n_chunks, body, 0)


_HBM = pl.BlockSpec(memory_space=pl.ANY)


def _stage_scratch(w, rows):
    r, c = w.shape
    assert r % rows == 0 and rows % 16 == 0
    return [pltpu.VMEM((r, c), BF16), pltpu.VMEM((STAGE_SLOTS, rows, c), F32),
            pltpu.SemaphoreType.DMA((STAGE_SLOTS,))]


def _mod_kernel(c_ref, w_ref, b_ref, o_ref):
    c = c_ref[...]
    s = (c * _sigmoid(c)).astype(BF16)
    o_ref[...] = _dot(s, w_ref[...].astype(BF16)) + b_ref[...]


def _mod_call(cvec, ada_w, ada_b):
    n = ada_w.shape[1]
    tn = 2304
    return pl.pallas_call(
        _mod_kernel,
        grid=(n // tn,),
        in_specs=[
            pl.BlockSpec((8, D_MODEL), lambda j: (0, 0)),
            pl.BlockSpec((D_MODEL, tn), lambda j: (0, j)),
            pl.BlockSpec((1, tn), lambda j: (0, j)),
        ],
        out_specs=pl.BlockSpec((8, tn), lambda j: (0, j)),
        out_shape=jax.ShapeDtypeStruct((8, n), F32),
        compiler_params=_params(("arbitrary",)),
        name="mod",
    )(cvec, ada_w, ada_b)


def _ffn_kernel(*refs, mod_base, norm_row, final, n_first, two_in, two_out):
    it = iter(refs)
    xa_ref = next(it)
    xb_ref = next(it) if two_in else None
    mod_ref, nw_ref, fw_ref, w13_hbm, w2_hbm = (next(it) for _ in range(5))
    oa_ref = next(it)
    ob_ref = next(it) if two_out else None
    w13_ref, st13, sem13, w2_ref, st2, sem2 = (next(it) for _ in range(6))
    i = pl.program_id(0)
    _stage_weight(i, w13_hbm, w13_ref, st13, sem13)
    _stage_weight(i, w2_hbm, w2_ref, st2, sem2)
    x = xa_ref[...]
    if two_in:
        x = jnp.where(i < n_first, x, xb_ref[...])
    y = _rms(x, nw_ref[norm_row:norm_row + 1, :])
    sh = mod_ref[mod_base:mod_base + 1, :]
    sc = mod_ref[mod_base + 1:mod_base + 2, :]
    h = (y * (1.0 + sc) + sh).astype(BF16)
    acc = None
    for c0 in range(0, D_FF, FF_CHUNK):
        a = _dot(h, w13_ref[:, c0:c0 + FF_CHUNK])
        b = _dot(h, w13_ref[:, D_FF + c0:D_FF + c0 + FF_CHUNK])
        act = (a * _sigmoid(a) * b).astype(BF16)
        part = _dot(act, w2_ref[c0:c0 + FF_CHUNK, :])
        acc = part if acc is None else acc + part
    g = mod_ref[mod_base + 2:mod_base + 3, :]
    xo = x + 0.5 * g * acc
    if final:
        xo = _rms(xo, fw_ref[...])
    if two_out:
        @pl.when(i < n_first)
        def _():
            oa_ref[...] = xo

        @pl.when(i >= n_first)
        def _():
            ob_ref[...] = xo
    else:
        oa_ref[...] = xo


def _ffn_call(xs, mod3, norm_w, final_w, w13, w2, *, n_out_split, mod_base, norm_row, final, mod_row_fn, tm,
              name):
    two_in = len(xs) == 2
    two_out = n_out_split is not None
    n = sum(x.shape[0] for x in xs)
    n_first = (xs[0].shape[0] if two_in else n_out_split if two_out else n) // tm
    first = pl.BlockSpec((tm, D_MODEL), lambda i: (jnp.minimum(i, n_first - 1), 0))
    second = pl.BlockSpec((tm, D_MODEL), lambda i: (jnp.maximum(i - n_first, 0), 0))
    whole = pl.BlockSpec((tm, D_MODEL), lambda i: (i, 0))
    kern = functools.partial(_ffn_kernel, mod_base=mod_base, norm_row=norm_row, final=final, n_first=n_first,
                             two_in=two_in, two_out=two_out)
    if two_out:
        out_specs = [first, second]
        out_shape = [jax.ShapeDtypeStruct((n_out_split, D_MODEL), F32),
                     jax.ShapeDtypeStruct((n - n_out_split, D_MODEL), F32)]
    else:
        out_specs = whole
        out_shape = jax.ShapeDtypeStruct((n, D_MODEL), F32)
    return pl.pallas_call(
        kern,
        grid=(n // tm,),
        in_specs=([first, second] if two_in else [whole]) + [
            pl.BlockSpec((None, N_MOD, D_MODEL), lambda i: (mod_row_fn(i, tm), 0, 0)),
            pl.BlockSpec((3, D_MODEL), lambda i: (0, 0)),
            pl.BlockSpec((1, D_MODEL), lambda i: (0, 0)),
            _HBM,
            _HBM,
        ],
        out_specs=out_specs,
        out_shape=out_shape,
        scratch_shapes=_stage_scratch(w13, 64) + _stage_scratch(w2, 256),
        compiler_params=_params(("arbitrary",)),
        name=name,
    )(*xs, mod3, norm_w, final_w, w13, w2)


def _rope(x, cos, sins, first_quarter):
    partner = jnp.where(first_quarter, pltpu.roll(x, 96, 1), pltpu.roll(x, 32, 1))
    return x * cos + partner * sins


def _proj_kernel(*refs, layer, rope):
    it = iter(refs)
    x_ref, mod_ref, nw_ref, lbl_ref, w_hbm = (next(it) for _ in range(5))
    cos_ref, sin_ref = (next(it), next(it)) if rope else (None, None)
    qk_ref, rv_ref, rg_ref, hq_ref, g_ref, hi_ref, hog_ref, gr_ref, gh_ref = (next(it) for _ in range(9))
    w_ref, w_stage, w_sem = next(it), next(it), next(it)
    _stage_weight(pl.program_id(0), w_hbm, w_ref, w_stage, w_sem)
    y = _rms(x_ref[...], nw_ref[1:2, :])
    h = (y * (1.0 + mod_ref[4:5, :]) + mod_ref[3:4, :]).astype(BF16)

    if rope:
        lane = lax.broadcasted_iota(jnp.int32, (x_ref.shape[0], RET_DK), 1)
        first_quarter = (lane & 32) == 0

    def rot(t):
        if not rope:
            return t
        heads = [_rope(t[:, j:j + RET_DK], cos_ref[...], sin_ref[...], first_quarter)
                 for j in range(0, t.shape[1], RET_DK)]
        return jnp.concatenate(heads, axis=1)

    def silu(t):
        return t * _sigmoid(t)

    def log_forget(d):
        def fn(t, c):
            lg = lbl_ref[d, :, c:c + PROJ_CHUNK]
            e = jnp.exp(lg - jnp.max(lg, axis=0, keepdims=True))
            lb = jnp.sum(e[0:layer + 1, :], axis=0, keepdims=True) / jnp.sum(e, axis=0, keepdims=True)
            return jnp.log(lb + (1.0 - lb) * _sigmoid(t)) * LOG2E
        return fn

    parts = (
        (C_RQ, 512, qk_ref, 0, lambda t, c: rot(t)),
        (C_RK, 512, qk_ref, 512, lambda t, c: rot(t * (RET_DK ** -0.5))),
        (C_RV, 1024, rv_ref, 0, lambda t, c: t),
        (C_RG, 1024, rg_ref, 0, lambda t, c: silu(t)),
        (C_HQ, 1024, hq_ref, 0, lambda t, c: silu(t) * (HG_DK ** -0.5)),
        (C_HFF, 1024, g_ref, 0, log_forget(0)),
        (C_HFB, 1024, g_ref, 1024, log_forget(1)),
        (C_HI, 1024, hi_ref, 0, lambda t, c: t),
        (C_HOG, 1024, hog_ref, 0, lambda t, c: silu(t)),
        (C_GR, 1024, gr_ref, 0, lambda t, c: _sigmoid(t)),
        (C_GH, 1024, gh_ref, 0, lambda t, c: _sigmoid(t)),
    )
    for w0, width, o_ref, o0, fn in parts:
        for c in range(0, width, PROJ_CHUNK):
            t = _dot(h, w_ref[:, w0 + c:w0 + c + PROJ_CHUNK])
            o_ref[:, o0 + c:o0 + c + PROJ_CHUNK] = fn(t, c).astype(o_ref.dtype)


def _proj_call(x, mod3, norm_w, lb_logits, w_in, rope_tabs, rope_blk_fn, *, layer, mod_row_fn, tm, name):
    n = x.shape[0]
    slots = lb_logits.shape[1]
    row = lambda w: pl.BlockSpec((tm, w), lambda i: (i, 0))
    outs = [(1024, BF16), (1024, BF16), (1024, BF16), (1024, BF16), (2048, F32),
            (1024, BF16), (1024, BF16), (1024, BF16), (1024, BF16)]
    in_specs = [
        row(D_MODEL),
        pl.BlockSpec((None, N_MOD, D_MODEL), lambda i: (mod_row_fn(i, tm), 0, 0)),
        pl.BlockSpec((3, D_MODEL), lambda i: (0, 0)),
        pl.BlockSpec((2, slots, 1024), lambda i: (0, 0, 0)),
        _HBM,
    ]
    args = [x, mod3, norm_w, lb_logits, w_in]
    if rope_tabs is not None:
        in_specs += [pl.BlockSpec((tm, RET_DK), lambda i: (rope_blk_fn(i, tm), 0))] * 2
        args += list(rope_tabs)
    return pl.pallas_call(
        functools.partial(_proj_kernel, layer=layer, rope=rope_tabs is not None),
        grid=(n // tm,),
        in_specs=in_specs,
        out_specs=[row(w) for w, _ in outs],
        out_shape=[jax.ShapeDtypeStruct((n, w), dt) for w, dt in outs],
        scratch_shapes=_stage_scratch(w_in, 64),
        compiler_params=_params(("arbitrary",)),
        name=name,
    )(*args)


def _log_sigmoid(x):
    return -(jnp.maximum(-x, 0.0) + jnp.log1p(jnp.exp(-jnp.abs(x))))


def _half_acc_slices(i, n, blk, finalize):
    if finalize:
        starts = ((i - n // 2) * blk, (n - 1 - i) * blk)
    else:
        starts = (i * blk, (n // 2 - 1 - i) * blk)
    return tuple(pl.ds(pl.multiple_of(s, blk), blk) for s in starts)


def _ret_kernel(*refs, L, has_s0, out_state):
    it = iter(refs)
    dec_ref = next(it)
    q_ref, k_ref, v_ref, gate_ref = next(it), next(it), next(it), next(it)
    s0_ref = next(it) if has_s0 else None
    o_ref = next(it)
    st_ref = next(it) if out_state else None
    of_acc, ob_acc, st_scr, cst = next(it), next(it), next(it), next(it)

    C = SCAN_BLOCK
    P = RET_PACK
    n = L // C
    pair = pl.program_id(1)
    ii = lax.broadcasted_iota(jnp.int32, (C, C), 0).astype(F32)
    jj = lax.broadcasted_iota(jnp.int32, (C, C), 1).astype(F32)
    diff = ii - jj

    @pl.when(jnp.logical_and(pl.program_id(0) == 0, pair == 0))
    def _():
        for hd in range(RET_HEADS):
            lgf = _log_sigmoid(jnp.full((C, C), dec_ref[0, hd], F32))
            lgb = _log_sigmoid(jnp.full((C, C), dec_ref[1, hd], F32))
            cst[hd, 0] = (jnp.where(diff >= 0, jnp.exp(jnp.maximum(diff, 0.0) * lgf), 0.0)
                          + jnp.where(diff <= 0, jnp.exp(jnp.maximum(-diff, 0.0) * lgb), 0.0))
            cst[hd, 1] = jnp.exp((ii + 1.0) * lgf)
            cst[hd, 2] = jnp.exp((C - 1.0 - ii) * lgf)
            cst[hd, 3] = jnp.exp((C - ii) * lgb)
            cst[hd, 4] = jnp.exp(ii * lgb)

    cds = []
    for h in range(P):
        hd = pair * P + h
        cds.append([jnp.exp(C * _log_sigmoid(jnp.full((1, RET_DV), dec_ref[d, hd], F32))) for d in range(2)])
        for d in range(2):
            st_scr[2 * h + d] = s0_ref[d, h] if has_s0 else jnp.zeros((RET_DK, RET_DV), F32)

    def step(i, finalize):
        sls = (pl.ds(pl.multiple_of(i * C, C), C), pl.ds(pl.multiple_of((n - 1 - i) * C, C), C))
        acc_sls = _half_acc_slices(i, n, C, finalize)
        chains = []
        for h in range(P):
            ql = slice(h * RET_DK, (h + 1) * RET_DK)
            vl = slice(h * RET_DV, (h + 1) * RET_DV)
            for d in range(2):
                chains.append(dict(h=h, d=d, vl=vl, q=q_ref[sls[d], ql], k=k_ref[sls[d], ql],
                                   v=v_ref[sls[d], vl]))
        for c in chains:
            if c["d"] == 0:
                c["s"] = _dot_nt(c["q"], c["k"])
        for c in chains:
            h, d, v, vl = c["h"], c["d"], c["v"], c["vl"]
            hd = pair * P + h
            st = st_scr[2 * h + d]
            qd = (c["q"].astype(F32) * cst[hd, 1 + 2 * d]).astype(BF16)
            if d == 0:
                p = (c["s"] * cst[hd, 0]).astype(BF16)
                o = _dot(jnp.concatenate([p, qd], axis=1), jnp.concatenate([v, st.astype(BF16)], axis=0))
            else:
                o = _dot(qd, st.astype(BF16))
            kd = (c["k"].astype(F32) * cst[hd, 2 + 2 * d]).astype(BF16)
            st_scr[2 * h + d] = cds[h][d] * st + _dot_tn(kd, v)
            if finalize:
                o = o + (ob_acc, of_acc)[d][acc_sls[d], vl]
                ms = jnp.mean(o * o, axis=-1, keepdims=True)
                o_ref[sls[d], vl] = (o * lax.rsqrt(ms + EPS) * gate_ref[sls[d], vl]).astype(BF16)
            else:
                (of_acc, ob_acc)[d][acc_sls[d], vl] = o

    def first_half(i, carry):
        step(i, False)
        return carry

    def second_half(i, carry):
        step(i, True)
        return carry

    lax.fori_loop(0, n // 2, first_half, 0, unroll=min(RET_UNROLL, n // 2))
    lax.fori_loop(n // 2, n, second_half, 0, unroll=min(RET_UNROLL, n // 2))
    if out_state:
        for h in range(P):
            for d in range(2):
                st_ref[d, h] = st_scr[2 * h + d]


def _ret_call(dec, qk, rv, rg, s0, *, layer, row0, B, L, out_state, name):
    has_s0 = s0 is not None
    H = RET_HEADS
    P = RET_PACK
    r0 = row0 // L
    st_spec = pl.BlockSpec((None, 2, P, RET_DK, RET_DV), lambda b, h: (b, 0, h, 0, 0))
    in_specs = [
        pl.BlockSpec(memory_space=pltpu.SMEM),
        pl.BlockSpec((L, P * RET_DK), lambda b, h: (r0 + b, h)),
        pl.BlockSpec((L, P * RET_DK), lambda b, h: (r0 + b, H // P + h)),
        pl.BlockSpec((L, P * RET_DV), lambda b, h: (r0 + b, h)),
        pl.BlockSpec((L, P * RET_DV), lambda b, h: (r0 + b, h)),
    ]
    args = [dec, qk, qk, rv, rg]
    if has_s0:
        in_specs.append(pl.BlockSpec((None, None, 2, P, RET_DK, RET_DV), lambda b, h: (b, layer, 0, h, 0, 0)))
        args.append(s0)
    out_specs = [pl.BlockSpec((L, P * RET_DV), lambda b, h: (b, h))]
    out_shape = [jax.ShapeDtypeStruct((B * L, H * RET_DV), BF16)]
    if out_state:
        out_specs.append(st_spec)
        out_shape.append(jax.ShapeDtypeStruct((B, 2, H, RET_DK, RET_DV), F32))
    res = pl.pallas_call(
        functools.partial(_ret_kernel, L=L, has_s0=has_s0, out_state=out_state),
        grid=(B, H // P),
        in_specs=in_specs,
        out_specs=out_specs,
        out_shape=out_shape,
        scratch_shapes=[pltpu.VMEM((L // 2, P * RET_DV), F32), pltpu.VMEM((L // 2, P * RET_DV), F32),
                        pltpu.VMEM((2 * P, RET_DK, RET_DV), F32),
                        pltpu.VMEM((H, 5, SCAN_BLOCK, SCAN_BLOCK), F32)],
        compiler_params=_params(("arbitrary", "arbitrary")),
        name=name,
    )(*args)
    return res


def _gla_blocks(chains, tri_ref, code_ref):
    T = SCAN_BLOCK
    K = HG_DK

    def cumulate(c):
        g2 = c["g"]
        c["k"] = 1.0 - jnp.exp2(g2)
        hi = g2.astype(BF16)
        lo = (g2 - hi.astype(F32)).astype(BF16)
        tri = tri_ref[int(c["rev"])]
        c["b"] = _dot(tri, hi) + _dot(tri, lo)

    def leaf(c):
        bl = c["b"].reshape(T // HG_LEAF, HG_LEAF, K)
        r = HG_LEAF // 2 if c["rev"] else HG_LEAF // 2 - 1
        d = jnp.clip(bl - bl[:, r:r + 1, :], -EXP2_CLAMP, EXP2_CLAMP).reshape(T, K)
        e = jnp.exp2(d)
        p = _dot_nt((c["q"] * e).astype(BF16), (c["k"] / e).astype(BF16))
        c["a"] = jnp.where(code_ref[int(c["rev"])] == 0, p, 0.0)

    def level(c, lv, m):
        seg = 2 * m
        ng = T // seg
        b3, q3, k3 = (c[n].reshape(ng, seg, K) for n in ("b", "q", "k"))
        if c["rev"]:
            qs, ks, r = slice(0, m), slice(m, seg), m
        else:
            qs, ks, r = slice(m, seg), slice(0, m), m - 1
        ref = b3[:, r:r + 1, :]
        qm = (q3[:, qs, :] * jnp.exp2(b3[:, qs, :] - ref)).reshape(ng * m, K).astype(BF16)
        kz = k3[:, ks, :] * jnp.exp2(ref - b3[:, ks, :])
        zeros = jnp.zeros((ng, m, K), F32)
        kfull = jnp.concatenate([zeros, kz] if c["rev"] else [kz, zeros], axis=1)
        p = _dot_nt(qm, kfull.reshape(T, K).astype(BF16)).reshape(ng, m, T)
        a3 = c["a"].reshape(ng, seg, T)
        c3 = code_ref[int(c["rev"])].reshape(ng, seg, T)
        upd = jnp.where(c3[:, qs, :] == lv, p, a3[:, qs, :])
        parts = [upd, a3[:, ks, :]] if c["rev"] else [a3[:, ks, :], upd]
        c["a"] = jnp.concatenate(parts, axis=1).reshape(T, T)

    def finish(c):
        b, q, k, v, st = c["b"], c["q"], c["k"], c["v"], c["st"]
        o = _dot(c["a"].astype(BF16), v) + _dot_nt((q * jnp.exp2(b)).astype(BF16), st.astype(BF16))
        end_row = 0 if c["rev"] else T - 1
        b1 = b.reshape(1, T, K)
        b_end = b1[:, end_row:end_row + 1, :]
        kd = (k * jnp.exp2(b_end - b1).reshape(T, K)).astype(BF16)
        c["out"] = (o, st * jnp.exp2(b_end.reshape(1, K)) + _dot_tn(v, kd))

    for c in chains:
        cumulate(c)
    for c in chains:
        leaf(c)
    for lv, m in HG_LEVELS:
        for c in chains:
            level(c, lv, m)
    for c in chains:
        finish(c)
    return [c["out"] for c in chains]


def _hgrn_kernel(*refs, L, has_s0, out_state):
    it = iter(refs)
    q_ref, g_f_ref, g_b_ref, v_ref, gate_ref, nw_ref = (next(it) for _ in range(6))
    s0_ref = next(it) if has_s0 else None
    o_ref = next(it)
    st_ref = next(it) if out_state else None
    of_acc, ob_acc, st_scr, tri_ref, code_ref = (next(it) for _ in range(5))

    T = SCAN_BLOCK
    P = HG_PACK
    n = L // T

    @pl.when(jnp.logical_and(pl.program_id(0) == 0, pl.program_id(1) == 0))
    def _():
        ti = lax.broadcasted_iota(jnp.int32, (T, T), 0)
        si = lax.broadcasted_iota(jnp.int32, (T, T), 1)
        x = ti ^ si
        tri_ref[0] = (si <= ti).astype(BF16)
        tri_ref[1] = (si >= ti).astype(BF16)
        lvl = jnp.where(x < HG_LEAF, 0, 3 + (x >= 16).astype(jnp.int32) + (x >= 32).astype(jnp.int32)
                        + (x >= 64).astype(jnp.int32))
        code_ref[0] = jnp.where(si <= ti, lvl, -1)
        code_ref[1] = jnp.where(si >= ti, lvl, -1)

    for h in range(P):
        for d in range(2):
            st_scr[2 * h + d] = s0_ref[d, h].T if has_s0 else jnp.zeros((HG_DV, HG_DK), F32)

    def step(i, finalize):
        sls = (pl.ds(pl.multiple_of(i * T, T), T), pl.ds(pl.multiple_of((n - 1 - i) * T, T), T))
        acc_sls = _half_acc_slices(i, n, T, finalize)
        chains = []
        for h in range(P):
            lanes = slice(h * HG_DK, (h + 1) * HG_DK)
            for d, g_ref in enumerate((g_f_ref, g_b_ref)):
                chains.append(dict(q=q_ref[sls[d], lanes].astype(F32), g=g_ref[sls[d], lanes],
                                   v=v_ref[sls[d], lanes],
                                   st=st_scr[2 * h + d], rev=bool(d)))
        res = _gla_blocks(chains, tri_ref, code_ref)
        for h in range(P):
            lanes = slice(h * HG_DK, (h + 1) * HG_DK)
            for d in range(2):
                o, st = res[2 * h + d]
                st_scr[2 * h + d] = st
                if finalize:
                    o = o + (ob_acc, of_acc)[d][acc_sls[d], lanes]
                    ms = jnp.mean(o * o, axis=-1, keepdims=True)
                    o = o * lax.rsqrt(ms + EPS) * nw_ref[:, lanes]
                    o_ref[sls[d], lanes] = (o * gate_ref[sls[d], lanes]).astype(BF16)
                else:
                    (of_acc, ob_acc)[d][acc_sls[d], lanes] = o

    def first_half(i, carry):
        step(i, False)
        return carry

    def second_half(i, carry):
        step(i, True)
        return carry

    lax.fori_loop(0, n // 2, first_half, 0, unroll=min(HG_UNROLL, n // 2))
    lax.fori_loop(n // 2, n, second_half, 0, unroll=min(HG_UNROLL, n // 2))
    if out_state:
        for h in range(P):
            for d in range(2):
                st_ref[d, h] = st_scr[2 * h + d].T


def _hgrn_call(hq, g, hi, hog, nw, s0, *, layer, row0, B, L, out_state, name):
    has_s0 = s0 is not None
    H = HG_HEADS
    P = HG_PACK
    W = P * HG_DK
    r0 = row0 // L
    blk = lambda off: pl.BlockSpec((L, W), lambda b, h: (r0 + b, off + h))
    st_spec = pl.BlockSpec((None, 2, P, HG_DK, HG_DV), lambda b, h: (b, 0, h, 0, 0))
    in_specs = [blk(0), blk(0), blk(H // P), blk(0), blk(0), pl.BlockSpec((1, W), lambda b, h: (0, h))]
    args = [hq, g, g, hi, hog, nw]
    if has_s0:
        in_specs.append(pl.BlockSpec((None, None, 2, P, HG_DK, HG_DV), lambda b, h: (b, layer, 0, h, 0, 0)))
        args.append(s0)
    out_specs = [pl.BlockSpec((L, W), lambda b, h: (b, h))]
    out_shape = [jax.ShapeDtypeStruct((B * L, H * HG_DV), BF16)]
    if out_state:
        out_specs.append(st_spec)
        out_shape.append(jax.ShapeDtypeStruct((B, 2, H, HG_DK, HG_DV), F32))
    return pl.pallas_call(
        functools.partial(_hgrn_kernel, L=L, has_s0=has_s0, out_state=out_state),
        grid=(B, H // P),
        in_specs=in_specs,
        out_specs=out_specs,
        out_shape=out_shape,
        scratch_shapes=[pltpu.VMEM((L // 2, W), F32), pltpu.VMEM((L // 2, W), F32),
                        pltpu.VMEM((2 * P, HG_DV, HG_DK), F32),
                        pltpu.VMEM((2, SCAN_BLOCK, SCAN_BLOCK), BF16),
                        pltpu.VMEM((2, SCAN_BLOCK, SCAN_BLOCK), jnp.int32)],
        compiler_params=_params(("arbitrary", "arbitrary")),
        name=name,
    )(*args)


def _merge_kernel(x_ref, mod_ref, oret_a, oret_b, ohg_a, ohg_b, gr_ref, gh_ref, wr_hbm, wh_hbm, wo_hbm, o_ref,
                  wr_ref, wh_ref, wo_ref, w_stage, w_sem, *, n_first):
    i = pl.program_id(0)
    for w_hbm, w_ref in ((wr_hbm, wr_ref), (wh_hbm, wh_ref), (wo_hbm, wo_ref)):
        _stage_weight(i, w_hbm, w_ref, w_stage, w_sem)
    oret = jnp.where(i < n_first, oret_a[...], oret_b[...])
    ohg = jnp.where(i < n_first, ohg_a[...], ohg_b[...])
    y = gr_ref[...] * _dot(oret, wr_ref[...]) + gh_ref[...] * _dot(ohg, wh_ref[...])
    out = _dot(y.astype(BF16), wo_ref[...])
    o_ref[...] = x_ref[...] + mod_ref[5:6, :] * out


def _merge_call(x, mod3, orets, ohgs, gr, gh, wr, wh, wo, *, mod_row_fn, tm, name):
    n = x.shape[0]
    n_first = orets[0].shape[0] // tm
    row = pl.BlockSpec((tm, D_MODEL), lambda i: (i, 0))
    first = pl.BlockSpec((tm, D_MODEL), lambda i: (jnp.minimum(i, n_first - 1), 0))
    second = pl.BlockSpec((tm, D_MODEL), lambda i: (jnp.maximum(i - n_first, 0), 0))
    resident, stage, sem = _stage_scratch(wr, 256)
    return pl.pallas_call(
        functools.partial(_merge_kernel, n_first=n_first),
        grid=(n // tm,),
        in_specs=[row, pl.BlockSpec((None, N_MOD, D_MODEL), lambda i: (mod_row_fn(i, tm), 0, 0)),
                  first, second, first, second, row, row, _HBM, _HBM, _HBM],
        out_specs=row,
        out_shape=jax.ShapeDtypeStruct((n, D_MODEL), F32),
        scratch_shapes=[resident, resident, resident, stage, sem],
        compiler_params=_params(("arbitrary",)),
        name=name,
    )(x, mod3, *orets, *ohgs, gr, gh, wr, wh, wo)


def _rope_tables(n_tokens, n_identity):
    pos = np.arange(n_tokens)
    r = (pos // GRID_W).astype(np.float32)
    cl = (pos % GRID_W).astype(np.float32)
    quarter = RET_DK // 4
    inv_freq = (np.float32(ROPE_BASE) ** (-np.arange(quarter, dtype=np.float32) / np.float32(quarter)))
    inv_freq = inv_freq.astype(np.float32)
    ang_r = (r[:, None] * inv_freq[None, :]).astype(np.float64)
    ang_c = (cl[:, None] * inv_freq[None, :]).astype(np.float64)
    cos = np.concatenate([np.cos(ang_r)] * 2 + [np.cos(ang_c)] * 2, axis=-1)
    sins = np.concatenate([-np.sin(ang_r), np.sin(ang_r), -np.sin(ang_c), np.sin(ang_c)], axis=-1)
    cos = np.concatenate([cos, np.ones((n_identity, RET_DK))], axis=0).astype(np.float32)
    sins = np.concatenate([sins, np.zeros((n_identity, RET_DK))], axis=0).astype(np.float32)
    return jnp.asarray(cos), jnp.asarray(sins)


def kernel(x_prompt, x_sample, state_ret, state_hgrn, c, c_ctx, ada_w, ada_b, norm_w, ffn1_w13, ffn1_w2,
           ffn2_w13, ffn2_w2, w_in, ret_decay, hg_lb_logits, hg_norm_w, w_ret_proj, w_hg_proj, w_o,
           final_norm_w):
    depth = ada_w.shape[0]
    assert depth == 1
    l = 0
    bp, lp, _ = x_prompt.shape
    bs, ls, _ = x_sample.shape
    n_ctx = bp * lp
    assert n_ctx % ls == 0 and ls % TM_FFN == 0 and ls % TM_PROJ == 0 and ls % TM_MERGE == 0

    cvec = jnp.concatenate([c_ctx[None, :], c, jnp.zeros((8 - 1 - bs, D_MODEL), F32)], axis=0)
    mod = _mod_call(cvec, ada_w[l], ada_b[l][None, :])
    mod3 = mod.reshape(8, N_MOD, D_MODEL)

    def mod_row(i, tm):
        return jnp.where(i < n_ctx // tm, 0, 1 + (i * tm - n_ctx) // ls)

    def rope_blk(i, tm):
        return jnp.where(i < n_ctx // tm, ls // tm, (i - n_ctx // tm) % (ls // tm))

    nw, fw = norm_w[l], final_norm_w[None, :]
    dec, hg_nw = ret_decay[l], hg_norm_w[l][None, :]

    x = _ffn_call((x_prompt.reshape(n_ctx, D_MODEL), x_sample.reshape(bs * ls, D_MODEL)), mod3, nw, fw,
                  ffn1_w13[l], ffn1_w2[l], n_out_split=None, mod_base=0, norm_row=0, final=False,
                  mod_row_fn=mod_row, tm=TM_FFN, name="ffn1")
    qk, rv, rg, hq, g, hi, hog, gr, gh = _proj_call(x, mod3, nw, hg_lb_logits, w_in[l],
                                                    _rope_tables(ls, TM_PROJ), rope_blk, layer=l,
                                                    mod_row_fn=mod_row, tm=TM_PROJ, name="proj")
    ret_ctx, s_ret = _ret_call(dec, qk, rv, rg, None, layer=l, row0=0, B=bp, L=lp, out_state=True,
                               name="ret_ctx")
    ret_lat, = _ret_call(dec, qk, rv, rg, state_ret, layer=l, row0=n_ctx, B=bs, L=ls, out_state=False,
                         name="ret_lat")
    hg_ctx, s_hg = _hgrn_call(hq, g, hi, hog, hg_nw, None, layer=l, row0=0, B=bp, L=lp, out_state=True,
                              name="hgrn_ctx")
    hg_lat, = _hgrn_call(hq, g, hi, hog, hg_nw, state_hgrn, layer=l, row0=n_ctx, B=bs, L=ls,
                         out_state=False, name="hgrn_lat")
    x = _merge_call(x, mod3, (ret_ctx, ret_lat), (hg_ctx, hg_lat), gr, gh, w_ret_proj[l], w_hg_proj[l], w_o[l],
                    mod_row_fn=mod_row, tm=TM_MERGE, name="merge")
    yp, ys = _ffn_call((x,), mod3, nw, fw, ffn2_w13[l], ffn2_w2[l], n_out_split=n_ctx, mod_base=6,
                       norm_row=2, final=True, mod_row_fn=mod_row, tm=TM_FFN, name="ffn2")
    return (yp.reshape(bp, lp, D_MODEL), ys.reshape(bs, ls, D_MODEL), s_ret[:, None], s_hg[:, None])
```

```python
import functools

import numpy as np
import jax
import jax.numpy as jnp
from jax import lax
from jax.experimental import pallas as pl
from jax.experimental.pallas import tpu as pltpu

F32 = jnp.float32
BF16 = jnp.bfloat16

D_MODEL = 1024
GRID_W = 64
RET_HEADS = 4
RET_DK = 128
RET_DV = 256
HG_HEADS = 8
HG_DK = 128
HG_DV = 128
D_FF = 2816
N_MOD = 9
ROPE_BASE = 10000.0
EPS = 1e-6

C_RQ, C_RK, C_RV, C_RG, C_HQ, C_HFF, C_HFB, C_HI, C_HOG, C_GR, C_GH = (
    0, 512, 1024, 2048, 3072, 4096, 5120, 6144, 7168, 8192, 9216)
IN_WIDTH = 10240

VMEM_LIMIT_BYTES = 56 * 1024 * 1024

TM_FFN = 512
TM_PROJ = 256
TM_MERGE = 512
STAGE_SLOTS = 4
SCAN_BLOCK = 128
RET_PACK = 4
RET_UNROLL = 1
HG_PACK_LONG = 4
HG_PACK_SHORT = 8
HG_SHORT_SEQ = 512
HG_UNROLL = 2
FF_CHUNK = 256
PROJ_CHUNK = 256
LOG2E = 1.4426950408889634
EXP2_CLAMP = 115.0
HG_LEAF = 16
HG_LEVELS = ((4, 16), (5, 32), (6, 64))


def _dot(a, b):
    return jnp.dot(a, b, preferred_element_type=F32)


def _dot_nt(a, b):
    return lax.dot_general(a, b, (((1,), (1,)), ((), ())), preferred_element_type=F32)


def _dot_tn(a, b):
    return lax.dot_general(a, b, (((0,), (0,)), ((), ())), preferred_element_type=F32)


def _sigmoid(x):
    return 1.0 / (1.0 + jnp.exp(-x))


def _rms(x, w):
    ms = jnp.mean(x * x, axis=-1, keepdims=True)
    return x * lax.rsqrt(ms + EPS) * w


def _rms_modulate(x, w, shift, scale):
    ms = jnp.mean(x * x, axis=-1, keepdims=True)
    return x * lax.rsqrt(ms + EPS) * (w * (1.0 + scale)) + shift


def _params(sem):
    return pltpu.CompilerParams(dimension_semantics=sem, vmem_limit_bytes=VMEM_LIMIT_BYTES)


def _stage_weight(step, w_hbm, w_vmem, stage, sem):
    rows = stage.shape[1]
    n_chunks = w_hbm.shape[0] // rows
    ahead = STAGE_SLOTS - 1

    def copy(k):
        slot = k % STAGE_SLOTS
        return pltpu.make_async_copy(w_hbm.at[pl.ds(k * rows, rows), :], stage.at[slot], sem.at[slot])

    @pl.when(step == 0)
    def _():
        for k in range(min(ahead, n_chunks)):
            copy(k).start()

        def body(k, carry):
            @pl.when(k + ahead < n_chunks)
            def _():
                copy(k + ahead).start()

            copy(k).wait()
            w_vmem[pl.ds(pl.multiple_of(k * rows, rows), rows), :] = stage[k % STAGE_SLOTS].astype(BF16)
            return carry

        lax.fori_loop(0, n_chunks, body, 0)


_HBM = pl.BlockSpec(memory_space=pl.ANY)


def _stage_scratch(w, rows):
    r, c = w.shape
    assert r % rows == 0 and rows % 16 == 0
    return [pltpu.VMEM((r, c), BF16), pltpu.VMEM((STAGE_SLOTS, rows, c), F32),
            pltpu.SemaphoreType.DMA((STAGE_SLOTS,))]


def _mod_kernel(c_ref, w_ref, b_ref, o_ref):
    c = c_ref[...]
    s = (c * _sigmoid(c)).astype(BF16)
    o_ref[...] = _dot(s, w_ref[...].astype(BF16)) + b_ref[...]


def _mod_call(cvec, ada_w, ada_b):
    n = ada_w.shape[1]
    tn = 2304
    return pl.pallas_call(
        _mod_kernel,
        grid=(n // tn,),
        in_specs=[
            pl.BlockSpec((8, D_MODEL), lambda j: (0, 0)),
            pl.BlockSpec((D_MODEL, tn), lambda j: (0, j)),
            pl.BlockSpec((1, tn), lambda j: (0, j)),
        ],
        out_specs=pl.BlockSpec((8, tn), lambda j: (0, j)),
        out_shape=jax.ShapeDtypeStruct((8, n), F32),
        compiler_params=_params(("arbitrary",)),
        name="mod",
    )(cvec, ada_w, ada_b)


def _ffn_kernel(*refs, mod_base, norm_row, final, n_first, two_in, two_out):
    it = iter(refs)
    xa_ref = next(it)
    xb_ref = next(it) if two_in else None
    mod_ref, nw_ref, fw_ref, w13_hbm, w2_hbm = (next(it) for _ in range(5))
    oa_ref = next(it)
    ob_ref = next(it) if two_out else None
    w13_ref, st13, sem13, w2_ref, st2, sem2 = (next(it) for _ in range(6))
    i = pl.program_id(0)
    _stage_weight(i, w13_hbm, w13_ref, st13, sem13)
    _stage_weight(i, w2_hbm, w2_ref, st2, sem2)
    x = xa_ref[...]
    if two_in:
        x = jnp.where(i < n_first, x, xb_ref[...])
    h = _rms_modulate(x, nw_ref[norm_row:norm_row + 1, :], mod_ref[mod_base:mod_base + 1, :],
                      mod_ref[mod_base + 1:mod_base + 2, :]).astype(BF16)
    acc = None
    for c0 in range(0, D_FF, FF_CHUNK):
        a = _dot(h, w13_ref[:, c0:c0 + FF_CHUNK])
        b = _dot(h, w13_ref[:, D_FF + c0:D_FF + c0 + FF_CHUNK])
        act = (a * _sigmoid(a) * b).astype(BF16)
        part = _dot(act, w2_ref[c0:c0 + FF_CHUNK, :])
        acc = part if acc is None else acc + part
    xo = x + (0.5 * mod_ref[mod_base + 2:mod_base + 3, :]) * acc
    if final:
        xo = _rms(xo, fw_ref[...])
    if two_out:
        @pl.when(i < n_first)
        def _():
            oa_ref[...] = xo

        @pl.when(i >= n_first)
        def _():
            ob_ref[...] = xo
    else:
        oa_ref[...] = xo


def _ffn_call(xs, mod3, norm_w, final_w, w13, w2, *, n_out_split, mod_base, norm_row, final, mod_row_fn, tm,
              name):
    two_in = len(xs) == 2
    two_out = n_out_split is not None
    n = sum(x.shape[0] for x in xs)
    n_first = (xs[0].shape[0] if two_in else n_out_split if two_out else n) // tm
    first = pl.BlockSpec((tm, D_MODEL), lambda i: (jnp.minimum(i, n_first - 1), 0))
    second = pl.BlockSpec((tm, D_MODEL), lambda i: (jnp.maximum(i - n_first, 0), 0))
    whole = pl.BlockSpec((tm, D_MODEL), lambda i: (i, 0))
    kern = functools.partial(_ffn_kernel, mod_base=mod_base, norm_row=norm_row, final=final, n_first=n_first,
                             two_in=two_in, two_out=two_out)
    if two_out:
        out_specs = [first, second]
        out_shape = [jax.ShapeDtypeStruct((n_out_split, D_MODEL), F32),
                     jax.ShapeDtypeStruct((n - n_out_split, D_MODEL), F32)]
    else:
        out_specs = whole
        out_shape = jax.ShapeDtypeStruct((n, D_MODEL), F32)
    return pl.pallas_call(
        kern,
        grid=(n // tm,),
        in_specs=([first, second] if two_in else [whole]) + [
            pl.BlockSpec((None, N_MOD, D_MODEL), lambda i: (mod_row_fn(i, tm), 0, 0)),
            pl.BlockSpec((3, D_MODEL), lambda i: (0, 0)),
            pl.BlockSpec((1, D_MODEL), lambda i: (0, 0)),
            _HBM,
            _HBM,
        ],
        out_specs=out_specs,
        out_shape=out_shape,
        scratch_shapes=_stage_scratch(w13, 64) + _stage_scratch(w2, 256),
        compiler_params=_params(("arbitrary",)),
        name=name,
    )(*xs, mod3, norm_w, final_w, w13, w2)


def _rope(x, cos, sins, first_quarter):
    partner = jnp.where(first_quarter, pltpu.roll(x, 96, 1), pltpu.roll(x, 32, 1))
    return x * cos + partner * sins


def _proj_kernel(*refs, layer, rope):
    it = iter(refs)
    x_ref, mod_ref, nw_ref, lbl_ref, w_hbm = (next(it) for _ in range(5))
    cos_ref, sin_ref = (next(it), next(it)) if rope else (None, None)
    qk_ref, rv_ref, rg_ref, hq_ref, g_ref, hi_ref, hog_ref, gr_ref, gh_ref = (next(it) for _ in range(9))
    w_ref, w_stage, w_sem = next(it), next(it), next(it)
    _stage_weight(pl.program_id(0), w_hbm, w_ref, w_stage, w_sem)
    h = _rms_modulate(x_ref[...], nw_ref[1:2, :], mod_ref[3:4, :], mod_ref[4:5, :]).astype(BF16)

    if rope:
        lane = lax.broadcasted_iota(jnp.int32, (x_ref.shape[0], RET_DK), 1)
        first_quarter = (lane & 32) == 0

    def rot(t):
        if not rope:
            return t
        heads = [_rope(t[:, j:j + RET_DK], cos_ref[...], sin_ref[...], first_quarter)
                 for j in range(0, t.shape[1], RET_DK)]
        return jnp.concatenate(heads, axis=1)

    def silu(t):
        return t * _sigmoid(t)

    def log_forget(d):
        def fn(t, c):
            lg = lbl_ref[d, :, c:c + PROJ_CHUNK]
            e = jnp.exp(lg - jnp.max(lg, axis=0, keepdims=True))
            lb = jnp.sum(e[0:layer + 1, :], axis=0, keepdims=True) / jnp.sum(e, axis=0, keepdims=True)
            return jnp.log(lb + (1.0 - lb) * _sigmoid(t)) * LOG2E
        return fn

    parts = (
        (C_RQ, 512, qk_ref, 0, lambda t, c: rot(t)),
        (C_RK, 512, qk_ref, 512, lambda t, c: rot(t * (RET_DK ** -0.5))),
        (C_RV, 1024, rv_ref, 0, lambda t, c: t),
        (C_RG, 1024, rg_ref, 0, lambda t, c: silu(t)),
        (C_HQ, 1024, hq_ref, 0, lambda t, c: silu(t) * (HG_DK ** -0.5)),
        (C_HFF, 1024, g_ref, 0, log_forget(0)),
        (C_HFB, 1024, g_ref, 1024, log_forget(1)),
        (C_HI, 1024, hi_ref, 0, lambda t, c: t),
        (C_HOG, 1024, hog_ref, 0, lambda t, c: silu(t)),
        (C_GR, 1024, gr_ref, 0, lambda t, c: _sigmoid(t)),
        (C_GH, 1024, gh_ref, 0, lambda t, c: _sigmoid(t)),
    )
    for w0, width, o_ref, o0, fn in parts:
        for c in range(0, width, PROJ_CHUNK):
            t = _dot(h, w_ref[:, w0 + c:w0 + c + PROJ_CHUNK])
            o_ref[:, o0 + c:o0 + c + PROJ_CHUNK] = fn(t, c).astype(o_ref.dtype)


def _proj_call(x, mod3, norm_w, lb_logits, w_in, rope_tabs, rope_blk_fn, *, layer, mod_row_fn, tm, name):
    n = x.shape[0]
    slots = lb_logits.shape[1]
    row = lambda w: pl.BlockSpec((tm, w), lambda i: (i, 0))
    outs = [(1024, BF16), (1024, BF16), (1024, BF16), (1024, BF16), (2048, F32),
            (1024, BF16), (1024, BF16), (1024, BF16), (1024, BF16)]
    in_specs = [
        row(D_MODEL),
        pl.BlockSpec((None, N_MOD, D_MODEL), lambda i: (mod_row_fn(i, tm), 0, 0)),
        pl.BlockSpec((3, D_MODEL), lambda i: (0, 0)),
        pl.BlockSpec((2, slots, 1024), lambda i: (0, 0, 0)),
        _HBM,
    ]
    args = [x, mod3, norm_w, lb_logits, w_in]
    if rope_tabs is not None:
        in_specs += [pl.BlockSpec((tm, RET_DK), lambda i: (rope_blk_fn(i, tm), 0))] * 2
        args += list(rope_tabs)
    return pl.pallas_call(
        functools.partial(_proj_kernel, layer=layer, rope=rope_tabs is not None),
        grid=(n // tm,),
        in_specs=in_specs,
        out_specs=[row(w) for w, _ in outs],
        out_shape=[jax.ShapeDtypeStruct((n, w), dt) for w, dt in outs],
        scratch_shapes=_stage_scratch(w_in, 64),
        compiler_params=_params(("arbitrary",)),
        name=name,
    )(*args)


def _log_sigmoid(x):
    return -(jnp.maximum(-x, 0.0) + jnp.log1p(jnp.exp(-jnp.abs(x))))


def _half_acc_slices(i, n, blk, finalize):
    if finalize:
        starts = ((i - n // 2) * blk, (n - 1 - i) * blk)
    else:
        starts = (i * blk, (n // 2 - 1 - i) * blk)
    return tuple(pl.ds(pl.multiple_of(s, blk), blk) for s in starts)


def _ret_kernel(*refs, L, has_s0, out_state):
    it = iter(refs)
    dec_ref = next(it)
    q_ref, k_ref, v_ref, gate_ref = next(it), next(it), next(it), next(it)
    s0_ref = next(it) if has_s0 else None
    o_ref = next(it)
    st_ref = next(it) if out_state else None
    of_acc, ob_acc, st_scr, cst = next(it), next(it), next(it), next(it)

    C = SCAN_BLOCK
    P = RET_PACK
    n = L // C
    pair = pl.program_id(1)
    ii = lax.broadcasted_iota(jnp.int32, (C, C), 0).astype(F32)
    jj = lax.broadcasted_iota(jnp.int32, (C, C), 1).astype(F32)
    diff = ii - jj

    @pl.when(jnp.logical_and(pl.program_id(0) == 0, pair == 0))
    def _():
        for hd in range(RET_HEADS):
            lgf = _log_sigmoid(jnp.full((C, C), dec_ref[0, hd], F32))
            lgb = _log_sigmoid(jnp.full((C, C), dec_ref[1, hd], F32))
            cst[hd, 0] = (jnp.where(diff >= 0, jnp.exp(jnp.maximum(diff, 0.0) * lgf), 0.0)
                          + jnp.where(diff <= 0, jnp.exp(jnp.maximum(-diff, 0.0) * lgb), 0.0))
            cst[hd, 1] = jnp.exp((ii + 1.0) * lgf)
            cst[hd, 2] = jnp.exp((C - 1.0 - ii) * lgf)
            cst[hd, 3] = jnp.exp((C - ii) * lgb)
            cst[hd, 4] = jnp.exp(ii * lgb)

    cds = []
    for h in range(P):
        hd = pair * P + h
        cds.append([jnp.exp(C * _log_sigmoid(jnp.full((1, RET_DV), dec_ref[d, hd], F32))) for d in range(2)])
        for d in range(2):
            st_scr[2 * h + d] = s0_ref[d, h] if has_s0 else jnp.zeros((RET_DK, RET_DV), F32)

    def step(i, finalize):
        sls = (pl.ds(pl.multiple_of(i * C, C), C), pl.ds(pl.multiple_of((n - 1 - i) * C, C), C))
        acc_sls = _half_acc_slices(i, n, C, finalize)
        chains = []
        for h in range(P):
            ql = slice(h * RET_DK, (h + 1) * RET_DK)
            vl = slice(h * RET_DV, (h + 1) * RET_DV)
            for d in range(2):
                chains.append(dict(h=h, d=d, vl=vl, q=q_ref[sls[d], ql], k=k_ref[sls[d], ql],
                                   v=v_ref[sls[d], vl]))
        for c in chains:
            if c["d"] == 0:
                c["s"] = _dot_nt(c["q"], c["k"])
        for c in chains:
            h, d, v, vl = c["h"], c["d"], c["v"], c["vl"]
            hd = pair * P + h
            st = st_scr[2 * h + d]
            qd = (c["q"].astype(F32) * cst[hd, 1 + 2 * d]).astype(BF16)
            if d == 0:
                p = (c["s"] * cst[hd, 0]).astype(BF16)
                o = _dot(jnp.concatenate([p, qd], axis=1), jnp.concatenate([v, st.astype(BF16)], axis=0))
            else:
                o = _dot(qd, st.astype(BF16))
            kd = (c["k"].astype(F32) * cst[hd, 2 + 2 * d]).astype(BF16)
            st_scr[2 * h + d] = cds[h][d] * st + _dot_tn(kd, v)
            if finalize:
                o = o + (ob_acc, of_acc)[d][acc_sls[d], vl]
                ms = jnp.mean(o * o, axis=-1, keepdims=True)
                o_ref[sls[d], vl] = (o * lax.rsqrt(ms + EPS) * gate_ref[sls[d], vl]).astype(BF16)
            else:
                (of_acc, ob_acc)[d][acc_sls[d], vl] = o

    def first_half(i, carry):
        step(i, False)
        return carry

    def second_half(i, carry):
        step(i, True)
        return carry

    lax.fori_loop(0, n // 2, first_half, 0, unroll=min(RET_UNROLL, n // 2))
    lax.fori_loop(n // 2, n, second_half, 0, unroll=min(RET_UNROLL, n // 2))
    if out_state:
        for h in range(P):
            for d in range(2):
                st_ref[d, h] = st_scr[2 * h + d]


def _ret_call(dec, qk, rv, rg, s0, *, layer, row0, B, L, out_state, name):
    has_s0 = s0 is not None
    H = RET_HEADS
    P = RET_PACK
    r0 = row0 // L
    st_spec = pl.BlockSpec((None, 2, P, RET_DK, RET_DV), lambda b, h: (b, 0, h, 0, 0))
    in_specs = [
        pl.BlockSpec(memory_space=pltpu.SMEM),
        pl.BlockSpec((L, P * RET_DK), lambda b, h: (r0 + b, h)),
        pl.BlockSpec((L, P * RET_DK), lambda b, h: (r0 + b, H // P + h)),
        pl.BlockSpec((L, P * RET_DV), lambda b, h: (r0 + b, h)),
        pl.BlockSpec((L, P * RET_DV), lambda b, h: (r0 + b, h)),
    ]
    args = [dec, qk, qk, rv, rg]
    if has_s0:
        in_specs.append(pl.BlockSpec((None, None, 2, P, RET_DK, RET_DV), lambda b, h: (b, layer, 0, h, 0, 0)))
        args.append(s0)
    out_specs = [pl.BlockSpec((L, P * RET_DV), lambda b, h: (b, h))]
    out_shape = [jax.ShapeDtypeStruct((B * L, H * RET_DV), BF16)]
    if out_state:
        out_specs.append(st_spec)
        out_shape.append(jax.ShapeDtypeStruct((B, 2, H, RET_DK, RET_DV), F32))
    res = pl.pallas_call(
        functools.partial(_ret_kernel, L=L, has_s0=has_s0, out_state=out_state),
        grid=(B, H // P),
        in_specs=in_specs,
        out_specs=out_specs,
        out_shape=out_shape,
        scratch_shapes=[pltpu.VMEM((L // 2, P * RET_DV), F32), pltpu.VMEM((L // 2, P * RET_DV), F32),
                        pltpu.VMEM((2 * P, RET_DK, RET_DV), F32),
                        pltpu.VMEM((H, 5, SCAN_BLOCK, SCAN_BLOCK), F32)],
        compiler_params=_params(("arbitrary", "arbitrary")),
        name=name,
    )(*args)
    return res


def _gla_blocks(chains, tri_ref, code_ref):
    T = SCAN_BLOCK
    K = HG_DK

    def cumulate(c):
        g2 = c["g"]
        c["k"] = 1.0 - jnp.exp2(g2)
        hi = g2.astype(BF16)
        lo = (g2 - hi.astype(F32)).astype(BF16)
        tri = tri_ref[int(c["rev"])]
        c["b"] = _dot(tri, hi) + _dot(tri, lo)

    def leaf(c):
        bl = c["b"].reshape(T // HG_LEAF, HG_LEAF, K)
        r = HG_LEAF // 2 if c["rev"] else HG_LEAF // 2 - 1
        d = jnp.clip(bl - bl[:, r:r + 1, :], -EXP2_CLAMP, EXP2_CLAMP).reshape(T, K)
        e = jnp.exp2(d)
        p = _dot_nt((c["q"] * e).astype(BF16), (c["k"] / e).astype(BF16))
        c["a"] = jnp.where(code_ref[int(c["rev"])] == 0, p, 0.0)

    def level(c, lv, m):
        seg = 2 * m
        ng = T // seg
        b3, q3, k3 = (c[n].reshape(ng, seg, K) for n in ("b", "q", "k"))
        if c["rev"]:
            qs, ks, r = slice(0, m), slice(m, seg), m
        else:
            qs, ks, r = slice(m, seg), slice(0, m), m - 1
        ref = b3[:, r:r + 1, :]
        qm = (q3[:, qs, :] * jnp.exp2(b3[:, qs, :] - ref)).reshape(ng * m, K).astype(BF16)
        kz = k3[:, ks, :] * jnp.exp2(ref - b3[:, ks, :])
        zeros = jnp.zeros((ng, m, K), F32)
        kfull = jnp.concatenate([zeros, kz] if c["rev"] else [kz, zeros], axis=1)
        p = _dot_nt(qm, kfull.reshape(T, K).astype(BF16)).reshape(ng, m, T)
        a3 = c["a"].reshape(ng, seg, T)
        c3 = code_ref[int(c["rev"])].reshape(ng, seg, T)
        upd = jnp.where(c3[:, qs, :] == lv, p, a3[:, qs, :])
        parts = [upd, a3[:, ks, :]] if c["rev"] else [a3[:, ks, :], upd]
        c["a"] = jnp.concatenate(parts, axis=1).reshape(T, T)

    def finish(c):
        b, q, k, v, st = c["b"], c["q"], c["k"], c["v"], c["st"]
        o = _dot(c["a"].astype(BF16), v) + _dot_nt((q * jnp.exp2(b)).astype(BF16), st.astype(BF16))
        end_row = 0 if c["rev"] else T - 1
        b1 = b.reshape(1, T, K)
        b_end = b1[:, end_row:end_row + 1, :]
        kd = (k * jnp.exp2(b_end - b1).reshape(T, K)).astype(BF16)
        c["out"] = (o, st * jnp.exp2(b_end.reshape(1, K)) + _dot_tn(v, kd))

    for c in chains:
        cumulate(c)
    for c in chains:
        leaf(c)
    for lv, m in HG_LEVELS:
        for c in chains:
            level(c, lv, m)
    for c in chains:
        finish(c)
    return [c["out"] for c in chains]


def _hgrn_kernel(*refs, L, P, has_s0, out_state):
    it = iter(refs)
    q_ref, g_f_ref, g_b_ref, v_ref, gate_ref, nw_ref = (next(it) for _ in range(6))
    s0_ref = next(it) if has_s0 else None
    o_ref = next(it)
    st_ref = next(it) if out_state else None
    of_acc, ob_acc, st_scr, tri_ref, code_ref = (next(it) for _ in range(5))

    T = SCAN_BLOCK
    n = L // T

    @pl.when(jnp.logical_and(pl.program_id(0) == 0, pl.program_id(1) == 0))
    def _():
        ti = lax.broadcasted_iota(jnp.int32, (T, T), 0)
        si = lax.broadcasted_iota(jnp.int32, (T, T), 1)
        x = ti ^ si
        tri_ref[0] = (si <= ti).astype(BF16)
        tri_ref[1] = (si >= ti).astype(BF16)
        lvl = jnp.where(x < HG_LEAF, 0, 3 + (x >= 16).astype(jnp.int32) + (x >= 32).astype(jnp.int32)
                        + (x >= 64).astype(jnp.int32))
        code_ref[0] = jnp.where(si <= ti, lvl, -1)
        code_ref[1] = jnp.where(si >= ti, lvl, -1)

    for h in range(P):
        for d in range(2):
            st_scr[2 * h + d] = s0_ref[d, h].T if has_s0 else jnp.zeros((HG_DV, HG_DK), F32)

    def step(i, finalize):
        sls = (pl.ds(pl.multiple_of(i * T, T), T), pl.ds(pl.multiple_of((n - 1 - i) * T, T), T))
        acc_sls = _half_acc_slices(i, n, T, finalize)
        chains = []
        for h in range(P):
            lanes = slice(h * HG_DK, (h + 1) * HG_DK)
            for d, g_ref in enumerate((g_f_ref, g_b_ref)):
                chains.append(dict(q=q_ref[sls[d], lanes].astype(F32), g=g_ref[sls[d], lanes],
                                   v=v_ref[sls[d], lanes],
                                   st=st_scr[2 * h + d], rev=bool(d)))
        res = _gla_blocks(chains, tri_ref, code_ref)
        for h in range(P):
            lanes = slice(h * HG_DK, (h + 1) * HG_DK)
            for d in range(2):
                o, st = res[2 * h + d]
                st_scr[2 * h + d] = st
                if finalize:
                    o = o + (ob_acc, of_acc)[d][acc_sls[d], lanes]
                    ms = jnp.mean(o * o, axis=-1, keepdims=True)
                    o = o * lax.rsqrt(ms + EPS) * nw_ref[:, lanes]
                    o_ref[sls[d], lanes] = (o * gate_ref[sls[d], lanes]).astype(BF16)
                else:
                    (of_acc, ob_acc)[d][acc_sls[d], lanes] = o

    def first_half(i, carry):
        step(i, False)
        return carry

    def second_half(i, carry):
        step(i, True)
        return carry

    lax.fori_loop(0, n // 2, first_half, 0, unroll=min(HG_UNROLL, n // 2))
    lax.fori_loop(n // 2, n, second_half, 0, unroll=min(HG_UNROLL, n // 2))
    if out_state:
        for h in range(P):
            for d in range(2):
                st_ref[d, h] = st_scr[2 * h + d].T


def _hgrn_call(hq, g, hi, hog, nw, s0, *, layer, row0, B, L, out_state, name):
    has_s0 = s0 is not None
    H = HG_HEADS
    P = HG_PACK_SHORT if L <= HG_SHORT_SEQ else HG_PACK_LONG
    W = P * HG_DK
    r0 = row0 // L
    blk = lambda off: pl.BlockSpec((L, W), lambda b, h: (r0 + b, off + h))
    st_spec = pl.BlockSpec((None, 2, P, HG_DK, HG_DV), lambda b, h: (b, 0, h, 0, 0))
    in_specs = [blk(0), blk(0), blk(H // P), blk(0), blk(0), pl.BlockSpec((1, W), lambda b, h: (0, h))]
    args = [hq, g, g, hi, hog, nw]
    if has_s0:
        in_specs.append(pl.BlockSpec((None, None, 2, P, HG_DK, HG_DV), lambda b, h: (b, layer, 0, h, 0, 0)))
        args.append(s0)
    out_specs = [pl.BlockSpec((L, W), lambda b, h: (b, h))]
    out_shape = [jax.ShapeDtypeStruct((B * L, H * HG_DV), BF16)]
    if out_state:
        out_specs.append(st_spec)
        out_shape.append(jax.ShapeDtypeStruct((B, 2, H, HG_DK, HG_DV), F32))
    return pl.pallas_call(
        functools.partial(_hgrn_kernel, L=L, P=P, has_s0=has_s0, out_state=out_state),
        grid=(B, H // P),
        in_specs=in_specs,
        out_specs=out_specs,
        out_shape=out_shape,
        scratch_shapes=[pltpu.VMEM((L // 2, W), F32), pltpu.VMEM((L // 2, W), F32),
                        pltpu.VMEM((2 * P, HG_DV, HG_DK), F32),
                        pltpu.VMEM((2, SCAN_BLOCK, SCAN_BLOCK), BF16),
                        pltpu.VMEM((2, SCAN_BLOCK, SCAN_BLOCK), jnp.int32)],
        compiler_params=_params(("arbitrary", "arbitrary")),
        name=name,
    )(*args)


def _merge_kernel(x_ref, mod_ref, oret_a, oret_b, ohg_a, ohg_b, gr_ref, gh_ref, wr_hbm, wh_hbm, wo_hbm, o_ref,
                  wr_ref, wh_ref, wo_ref, w_stage, w_sem, *, n_first):
    i = pl.program_id(0)
    for w_hbm, w_ref in ((wr_hbm, wr_ref), (wh_hbm, wh_ref), (wo_hbm, wo_ref)):
        _stage_weight(i, w_hbm, w_ref, w_stage, w_sem)
    oret = jnp.where(i < n_first, oret_a[...], oret_b[...])
    ohg = jnp.where(i < n_first, ohg_a[...], ohg_b[...])
    y = gr_ref[...] * _dot(oret, wr_ref[...]) + gh_ref[...] * _dot(ohg, wh_ref[...])
    out = _dot(y.astype(BF16), wo_ref[...])
    o_ref[...] = x_ref[...] + mod_ref[5:6, :] * out


def _merge_call(x, mod3, orets, ohgs, gr, gh, wr, wh, wo, *, mod_row_fn, tm, name):
    n = x.shape[0]
    n_first = orets[0].shape[0] // tm
    row = pl.BlockSpec((tm, D_MODEL), lambda i: (i, 0))
    first = pl.BlockSpec((tm, D_MODEL), lambda i: (jnp.minimum(i, n_first - 1), 0))
    second = pl.BlockSpec((tm, D_MODEL), lambda i: (jnp.maximum(i - n_first, 0), 0))
    resident, stage, sem = _stage_scratch(wr, 256)
    return pl.pallas_call(
        functools.partial(_merge_kernel, n_first=n_first),
        grid=(n // tm,),
        in_specs=[row, pl.BlockSpec((None, N_MOD, D_MODEL), lambda i: (mod_row_fn(i, tm), 0, 0)),
                  first, second, first, second, row, row, _HBM, _HBM, _HBM],
        out_specs=row,
        out_shape=jax.ShapeDtypeStruct((n, D_MODEL), F32),
        scratch_shapes=[resident, resident, resident, stage, sem],
        compiler_params=_params(("arbitrary",)),
        name=name,
    )(x, mod3, *orets, *ohgs, gr, gh, wr, wh, wo)


def _rope_tables(n_tokens, n_identity):
    pos = np.arange(n_tokens)
    r = (pos // GRID_W).astype(np.float32)
    cl = (pos % GRID_W).astype(np.float32)
    quarter = RET_DK // 4
    inv_freq = (np.float32(ROPE_BASE) ** (-np.arange(quarter, dtype=np.float32) / np.float32(quarter)))
    inv_freq = inv_freq.astype(np.float32)
    ang_r = (r[:, None] * inv_freq[None, :]).astype(np.float64)
    ang_c = (cl[:, None] * inv_freq[None, :]).astype(np.float64)
    cos = np.concatenate([np.cos(ang_r)] * 2 + [np.cos(ang_c)] * 2, axis=-1)
    sins = np.concatenate([-np.sin(ang_r), np.sin(ang_r), -np.sin(ang_c), np.sin(ang_c)], axis=-1)
    cos = np.concatenate([cos, np.ones((n_identity, RET_DK))], axis=0).astype(np.float32)
    sins = np.concatenate([sins, np.zeros((n_identity, RET_DK))], axis=0).astype(np.float32)
    return jnp.asarray(cos), jnp.asarray(sins)


def kernel(x_prompt, x_sample, state_ret, state_hgrn, c, c_ctx, ada_w, ada_b, norm_w, ffn1_w13, ffn1_w2,
           ffn2_w13, ffn2_w2, w_in, ret_decay, hg_lb_logits, hg_norm_w, w_ret_proj, w_hg_proj, w_o,
           final_norm_w):
    depth = ada_w.shape[0]
    assert depth == 1
    l = 0
    bp, lp, _ = x_prompt.shape
    bs, ls, _ = x_sample.shape
    n_ctx = bp * lp
    assert n_ctx % ls == 0 and ls % TM_FFN == 0 and ls % TM_PROJ == 0 and ls % TM_MERGE == 0

    cvec = jnp.concatenate([c_ctx[None, :], c, jnp.zeros((8 - 1 - bs, D_MODEL), F32)], axis=0)
    mod = _mod_call(cvec, ada_w[l], ada_b[l][None, :])
    mod3 = mod.reshape(8, N_MOD, D_MODEL)

    def mod_row(i, tm):
        return jnp.where(i < n_ctx // tm, 0, 1 + (i * tm - n_ctx) // ls)

    def rope_blk(i, tm):
        return jnp.where(i < n_ctx // tm, ls // tm, (i - n_ctx // tm) % (ls // tm))

    nw, fw = norm_w[l], final_norm_w[None, :]
    dec, hg_nw = ret_decay[l], hg_norm_w[l][None, :]

    x = _ffn_call((x_prompt.reshape(n_ctx, D_MODEL), x_sample.reshape(bs * ls, D_MODEL)), mod3, nw, fw,
                  ffn1_w13[l], ffn1_w2[l], n_out_split=None, mod_base=0, norm_row=0, final=False,
                  mod_row_fn=mod_row, tm=TM_FFN, name="ffn1")
    qk, rv, rg, hq, g, hi, hog, gr, gh = _proj_call(x, mod3, nw, hg_lb_logits, w_in[l],
                                                    _rope_tables(ls, TM_PROJ), rope_blk, layer=l,
                                                    mod_row_fn=mod_row, tm=TM_PROJ, name="proj")
    ret_ctx, s_ret = _ret_call(dec, qk, rv, rg, None, layer=l, row0=0, B=bp, L=lp, out_state=True,
                               name="ret_ctx")
    ret_lat, = _ret_call(dec, qk, rv, rg, state_ret, layer=l, row0=n_ctx, B=bs, L=ls, out_state=False,
                         name="ret_lat")
    hg_ctx, s_hg = _hgrn_call(hq, g, hi, hog, hg_nw, None, layer=l, row0=0, B=bp, L=lp, out_state=True,
                              name="hgrn_ctx")
    hg_lat, = _hgrn_call(hq, g, hi, hog, hg_nw, state_hgrn, layer=l, row0=n_ctx, B=bs, L=ls,
                         out_state=False, name="hgrn_lat")
    x = _merge_call(x, mod3, (ret_ctx, ret_lat), (hg_ctx, hg_lat), gr, gh, w_ret_proj[l], w_hg_proj[l], w_o[l],
                    mod_row_fn=mod_row, tm=TM_MERGE, name="merge")
    yp, ys = _ffn_call((x,), mod3, nw, fw, ffn2_w13[l], ffn2_w2[l], n_out_split=n_ctx, mod_base=6,
                       norm_row=2, final=True, mod_row_fn=mod_row, tm=TM_FFN, name="ffn2")
    return (yp.reshape(bp, lp, D_MODEL), ys.reshape(bs, ls, D_MODEL), s_ret[:, None], s_hg[:, None])
```

```python
import functools

import numpy as np
import jax
import jax.numpy as jnp
from jax import lax
from jax.experimental import pallas as pl
from jax.experimental.pallas import tpu as pltpu

F32 = jnp.float32
BF16 = jnp.bfloat16

D_MODEL = 1024
GRID_W = 64
RET_HEADS = 4
RET_DK = 128
RET_DV = 256
HG_HEADS = 8
HG_DK = 128
HG_DV = 128
D_FF = 2816
N_MOD = 9
ROPE_BASE = 10000.0
EPS = 1e-6

C_RQ, C_RK, C_RV, C_RG, C_HQ, C_HFF, C_HFB, C_HI, C_HOG, C_GR, C_GH = (
    0, 512, 1024, 2048, 3072, 4096, 5120, 6144, 7168, 8192, 9216)
IN_WIDTH = 10240
P_RQ, P_RK, P_RV, P_RG, P_HQ, P_HI, P_HOG, P_GR, P_GH = 0, 512, 1024, 2048, 3072, 4096, 5120, 6144, 7168
P_WIDTH = 8192

VMEM_LIMIT_BYTES = 56 * 1024 * 1024

TM_FFN = 512
TM_PROJ = 256
TM_MERGE = 512
STAGE_SLOTS = 4
SCAN_BLOCK = 128
RET_PACK = 4
RET_UNROLL = 2
HG_PACK_LONG = 4
HG_PACK_SHORT = 8
HG_SHORT_SEQ = 512
HG_UNROLL = 2
FF_CHUNK = 256
PROJ_CHUNK = 256
LOG2E = 1.4426950408889634
EXP2_CLAMP = 115.0
HG_BLOCK = 128
HG_LEAF = 16
HG_LEVELS = ((4, 16), (5, 32), (6, 64))


def _dot(a, b):
    return jnp.dot(a, b, preferred_element_type=F32)


def _dot_nt(a, b):
    return lax.dot_general(a, b, (((1,), (1,)), ((), ())), preferred_element_type=F32)


def _dot_tn(a, b):
    return lax.dot_general(a, b, (((0,), (0,)), ((), ())), preferred_element_type=F32)


def _sigmoid(x):
    return 0.5 * jnp.tanh(0.5 * x) + 0.5


def _rms(x, w):
    ms = jnp.mean(x * x, axis=-1, keepdims=True)
    return x * lax.rsqrt(ms + EPS) * w


def _rms_modulate(x, w, shift, scale):
    ms = jnp.mean(x * x, axis=-1, keepdims=True)
    return x * lax.rsqrt(ms + EPS) * (w * (1.0 + scale)) + shift


def _params(sem):
    return pltpu.CompilerParams(dimension_semantics=sem, vmem_limit_bytes=VMEM_LIMIT_BYTES)


def _stage_weight(step, w_hbm, w_vmem, stage, sem):
    rows = stage.shape[1]
    n_chunks = w_hbm.shape[0] // rows
    ahead = STAGE_SLOTS - 1

    def copy(k):
        slot = k % STAGE_SLOTS
        return pltpu.make_async_copy(w_hbm.at[pl.ds(k * rows, rows), :], stage.at[slot], sem.at[slot])

    @pl.when(step == 0)
    def _():
        for k in range(min(ahead, n_chunks)):
            copy(k).start()

        def body(k, carry):
            @pl.when(k + ahead < n_chunks)
            def _():
                copy(k + ahead).start()

            copy(k).wait()
            w_vmem[pl.ds(pl.multiple_of(k * rows, rows), rows), :] = stage[k % STAGE_SLOTS].astype(BF16)
            return carry

        lax.fori_loop(0, n_chunks, body, 0)


_HBM = pl.BlockSpec(memory_space=pl.ANY)


def _stage_scratch(w, rows):
    r, c = w.shape
    assert r % rows == 0 and rows % 16 == 0
    return [pltpu.VMEM((r, c), BF16), pltpu.VMEM((STAGE_SLOTS, rows, c), F32),
            pltpu.SemaphoreType.DMA((STAGE_SLOTS,))]


def _mod_kernel(c_ref, w_ref, b_ref, o_ref):
    c = c_ref[...]
    s = (c * _sigmoid(c)).astype(BF16)
    o_ref[...] = _dot(s, w_ref[...].astype(BF16)) + b_ref[...]


def _mod_call(cvec, ada_w, ada_b):
    n = ada_w.shape[1]
    tn = 2304
    return pl.pallas_call(
        _mod_kernel,
        grid=(n // tn,),
        in_specs=[
            pl.BlockSpec((8, D_MODEL), lambda j: (0, 0)),
            pl.BlockSpec((D_MODEL, tn), lambda j: (0, j)),
            pl.BlockSpec((1, tn), lambda j: (0, j)),
        ],
        out_specs=pl.BlockSpec((8, tn), lambda j: (0, j)),
        out_shape=jax.ShapeDtypeStruct((8, n), F32),
        compiler_params=_params(("arbitrary",)),
        name="mod",
    )(cvec, ada_w, ada_b)


def _ffn_kernel(*refs, mod_base, norm_row, final, n_first, two_in, two_out):
    it = iter(refs)
    xa_ref = next(it)
    xb_ref = next(it) if two_in else None
    mod_ref, nw_ref, fw_ref, w13_hbm, w2_hbm = (next(it) for _ in range(5))
    oa_ref = next(it)
    ob_ref = next(it) if two_out else None
    w13_ref, st13, sem13, w2_ref, st2, sem2 = (next(it) for _ in range(6))
    i = pl.program_id(0)
    _stage_weight(i, w13_hbm, w13_ref, st13, sem13)
    _stage_weight(i, w2_hbm, w2_ref, st2, sem2)
    x = xa_ref[...]
    if two_in:
        x = jnp.where(i < n_first, x, xb_ref[...])
    h = _rms_modulate(x, nw_ref[norm_row:norm_row + 1, :], mod_ref[mod_base:mod_base + 1, :],
                      mod_ref[mod_base + 1:mod_base + 2, :]).astype(BF16)
    acc = None
    for c0 in range(0, D_FF, FF_CHUNK):
        a = _dot(h, w13_ref[:, c0:c0 + FF_CHUNK])
        b = _dot(h, w13_ref[:, D_FF + c0:D_FF + c0 + FF_CHUNK])
        act = (a * _sigmoid(a) * b).astype(BF16)
        part = _dot(act, w2_ref[c0:c0 + FF_CHUNK, :])
        acc = part if acc is None else acc + part
    xo = x + (0.5 * mod_ref[mod_base + 2:mod_base + 3, :]) * acc
    if final:
        xo = _rms(xo, fw_ref[...])
    if two_out:
        @pl.when(i < n_first)
        def _():
            oa_ref[...] = xo

        @pl.when(i >= n_first)
        def _():
            ob_ref[...] = xo
    else:
        oa_ref[...] = xo


def _ffn_call(xs, mod3, norm_w, final_w, w13, w2, *, n_out_split, mod_base, norm_row, final, mod_row_fn, tm,
              name):
    two_in = len(xs) == 2
    two_out = n_out_split is not None
    n = sum(x.shape[0] for x in xs)
    n_first = (xs[0].shape[0] if two_in else n_out_split if two_out else n) // tm
    first = pl.BlockSpec((tm, D_MODEL), lambda i: (jnp.minimum(i, n_first - 1), 0))
    second = pl.BlockSpec((tm, D_MODEL), lambda i: (jnp.maximum(i - n_first, 0), 0))
    whole = pl.BlockSpec((tm, D_MODEL), lambda i: (i, 0))
    kern = functools.partial(_ffn_kernel, mod_base=mod_base, norm_row=norm_row, final=final, n_first=n_first,
                             two_in=two_in, two_out=two_out)
    if two_out:
        out_specs = [first, second]
        out_shape = [jax.ShapeDtypeStruct((n_out_split, D_MODEL), F32),
                     jax.ShapeDtypeStruct((n - n_out_split, D_MODEL), F32)]
    else:
        out_specs = whole
        out_shape = jax.ShapeDtypeStruct((n, D_MODEL), F32)
    return pl.pallas_call(
        kern,
        grid=(n // tm,),
        in_specs=([first, second] if two_in else [whole]) + [
            pl.BlockSpec((None, N_MOD, D_MODEL), lambda i: (mod_row_fn(i, tm), 0, 0)),
            pl.BlockSpec((3, D_MODEL), lambda i: (0, 0)),
            pl.BlockSpec((1, D_MODEL), lambda i: (0, 0)),
            _HBM,
            _HBM,
        ],
        out_specs=out_specs,
        out_shape=out_shape,
        scratch_shapes=_stage_scratch(w13, 64) + _stage_scratch(w2, 256),
        compiler_params=_params(("arbitrary",)),
        name=name,
    )(*xs, mod3, norm_w, final_w, w13, w2)


def _rope(x, cos, sins, first_quarter):
    partner = jnp.where(first_quarter, pltpu.roll(x, 96, 1), pltpu.roll(x, 32, 1))
    return x * cos + partner * sins


def _proj_kernel(*refs, layer, rope):
    it = iter(refs)
    x_ref, mod_ref, nw_ref, lbl_ref, w_hbm = (next(it) for _ in range(5))
    cos_ref, sin_ref = (next(it), next(it)) if rope else (None, None)
    p_ref, g_ref = next(it), next(it)
    w_ref, w_stage, w_sem = next(it), next(it), next(it)
    _stage_weight(pl.program_id(0), w_hbm, w_ref, w_stage, w_sem)
    h = _rms_modulate(x_ref[...], nw_ref[1:2, :], mod_ref[3:4, :], mod_ref[4:5, :]).astype(BF16)

    if rope:
        lane = lax.broadcasted_iota(jnp.int32, (x_ref.shape[0], RET_DK), 1)
        first_quarter = (lane & 32) == 0

    def rot(t):
        if not rope:
            return t
        heads = [_rope(t[:, j:j + RET_DK], cos_ref[...], sin_ref[...], first_quarter)
                 for j in range(0, t.shape[1], RET_DK)]
        return jnp.concatenate(heads, axis=1)

    def silu(t):
        return t * _sigmoid(t)

    def log_forget(d):
        def fn(t, c):
            lg = lbl_ref[d, :, c:c + PROJ_CHUNK]
            e = jnp.exp(lg - jnp.max(lg, axis=0, keepdims=True))
            lb = jnp.sum(e[0:layer + 1, :], axis=0, keepdims=True) / jnp.sum(e, axis=0, keepdims=True)
            return jnp.log(lb + (1.0 - lb) * _sigmoid(t)) * LOG2E
        return fn

    parts = (
        (C_RQ, 512, p_ref, P_RQ, lambda t, c: rot(t)),
        (C_RK, 512, p_ref, P_RK, lambda t, c: rot(t * (RET_DK ** -0.5))),
        (C_RV, 1024, p_ref, P_RV, lambda t, c: t),
        (C_RG, 1024, p_ref, P_RG, lambda t, c: silu(t)),
        (C_HQ, 1024, p_ref, P_HQ, lambda t, c: silu(t) * (HG_DK ** -0.5)),
        (C_HFF, 1024, g_ref, 0, log_forget(0)),
        (C_HFB, 1024, g_ref, 1024, log_forget(1)),
        (C_HI, 1024, p_ref, P_HI, lambda t, c: t),
        (C_HOG, 1024, p_ref, P_HOG, lambda t, c: silu(t)),
        (C_GR, 1024, p_ref, P_GR, lambda t, c: _sigmoid(t)),
        (C_GH, 1024, p_ref, P_GH, lambda t, c: _sigmoid(t)),
    )
    for c in range(0, 1024, PROJ_CHUNK):
        for w0, width, o_ref, o0, fn in parts:
            if c < width:
                t = _dot(h, w_ref[:, w0 + c:w0 + c + PROJ_CHUNK])
                o_ref[:, o0 + c:o0 + c + PROJ_CHUNK] = fn(t, c).astype(o_ref.dtype)


def _proj_call(x, mod3, norm_w, lb_logits, w_in, rope_tabs, rope_blk_fn, *, layer, mod_row_fn, tm, name):
    n = x.shape[0]
    slots = lb_logits.shape[1]
    row = lambda w: pl.BlockSpec((tm, w), lambda i: (i, 0))
    outs = [(P_WIDTH, BF16), (2 * HG_HEADS * HG_DK, F32)]
    in_specs = [
        row(D_MODEL),
        pl.BlockSpec((None, N_MOD, D_MODEL), lambda i: (mod_row_fn(i, tm), 0, 0)),
        pl.BlockSpec((3, D_MODEL), lambda i: (0, 0)),
        pl.BlockSpec((2, slots, 1024), lambda i: (0, 0, 0)),
        _HBM,
    ]
    args = [x, mod3, norm_w, lb_logits, w_in]
    if rope_tabs is not None:
        in_specs += [pl.BlockSpec((tm, RET_DK), lambda i: (rope_blk_fn(i, tm), 0))] * 2
        args += list(rope_tabs)
    return pl.pallas_call(
        functools.partial(_proj_kernel, layer=layer, rope=rope_tabs is not None),
        grid=(n // tm,),
        in_specs=in_specs,
        out_specs=[row(w) for w, _ in outs],
        out_shape=[jax.ShapeDtypeStruct((n, w), dt) for w, dt in outs],
        scratch_shapes=_stage_scratch(w_in, 64),
        compiler_params=_params(("arbitrary",)),
        name=name,
    )(*args)


def _log_sigmoid(x):
    return -(jnp.maximum(-x, 0.0) + jnp.log1p(jnp.exp(-jnp.abs(x))))


def _half_acc_slices(i, n, blk, finalize):
    if finalize:
        starts = ((i - n // 2) * blk, (n - 1 - i) * blk)
    else:
        starts = (i * blk, (n // 2 - 1 - i) * blk)
    return tuple(pl.ds(pl.multiple_of(s, blk), blk) for s in starts)


def _ret_kernel(*refs, L, has_s0, out_state):
    it = iter(refs)
    dec_ref = next(it)
    q_ref, k_ref, v_ref, gate_ref = next(it), next(it), next(it), next(it)
    s0_ref = next(it) if has_s0 else None
    o_ref = next(it)
    st_ref = next(it) if out_state else None
    of_acc, ob_acc, st_scr, cst = next(it), next(it), next(it), next(it)

    C = SCAN_BLOCK
    P = RET_PACK
    n = L // C
    pair = pl.program_id(1)
    ii = lax.broadcasted_iota(jnp.int32, (C, C), 0).astype(F32)
    jj = lax.broadcasted_iota(jnp.int32, (C, C), 1).astype(F32)
    diff = ii - jj

    @pl.when(jnp.logical_and(pl.program_id(0) == 0, pair == 0))
    def _():
        for hd in range(RET_HEADS):
            lgf = _log_sigmoid(jnp.full((C, C), dec_ref[0, hd], F32))
            lgb = _log_sigmoid(jnp.full((C, C), dec_ref[1, hd], F32))
            cst[hd, 0] = (jnp.where(diff >= 0, jnp.exp(jnp.maximum(diff, 0.0) * lgf), 0.0)
                          + jnp.where(diff <= 0, jnp.exp(jnp.maximum(-diff, 0.0) * lgb), 0.0))
            cst[hd, 1] = jnp.exp((ii + 1.0) * lgf)
            cst[hd, 2] = jnp.exp((C - 1.0 - ii) * lgf)
            cst[hd, 3] = jnp.exp((C - ii) * lgb)
            cst[hd, 4] = jnp.exp(ii * lgb)

    cds = []
    for h in range(P):
        hd = pair * P + h
        cds.append([jnp.exp(C * _log_sigmoid(jnp.full((1, RET_DV), dec_ref[d, hd], F32))) for d in range(2)])
        for d in range(2):
            st_scr[2 * h + d] = s0_ref[d, h] if has_s0 else jnp.zeros((RET_DK, RET_DV), F32)

    def step(i, finalize):
        sls = (pl.ds(pl.multiple_of(i * C, C), C), pl.ds(pl.multiple_of((n - 1 - i) * C, C), C))
        acc_sls = _half_acc_slices(i, n, C, finalize)
        chains = []
        for h in range(P):
            ql = slice(h * RET_DK, (h + 1) * RET_DK)
            vl = slice(h * RET_DV, (h + 1) * RET_DV)
            for d in range(2):
                chains.append(dict(h=h, d=d, vl=vl, q=q_ref[sls[d], ql], k=k_ref[sls[d], ql],
                                   v=v_ref[sls[d], vl]))
        for c in chains:
            if c["d"] == 0:
                c["s"] = _dot_nt(c["q"], c["k"])
        for c in chains:
            h, d, v, vl = c["h"], c["d"], c["v"], c["vl"]
            hd = pair * P + h
            st = st_scr[2 * h + d]
            qd = (c["q"].astype(F32) * cst[hd, 1 + 2 * d]).astype(BF16)
            if d == 0:
                p = (c["s"] * cst[hd, 0]).astype(BF16)
                o = _dot(jnp.concatenate([p, qd], axis=1), jnp.concatenate([v, st.astype(BF16)], axis=0))
            else:
                o = _dot(qd, st.astype(BF16))
            kd = (c["k"].astype(F32) * cst[hd, 2 + 2 * d]).astype(BF16)
            st_scr[2 * h + d] = cds[h][d] * st + _dot_tn(kd, v)
            if finalize:
                o = o + (ob_acc, of_acc)[d][acc_sls[d], vl]
                ms = jnp.mean(o * o, axis=-1, keepdims=True)
                o_ref[sls[d], vl] = (o * lax.rsqrt(ms + EPS) * gate_ref[sls[d], vl]).astype(BF16)
            else:
                (of_acc, ob_acc)[d][acc_sls[d], vl] = o

    def first_half(i, carry):
        step(i, False)
        return carry

    def second_half(i, carry):
        step(i, True)
        return carry

    lax.fori_loop(0, n // 2, first_half, 0, unroll=min(RET_UNROLL, n // 2))
    lax.fori_loop(n // 2, n, second_half, 0, unroll=min(RET_UNROLL, n // 2))
    if out_state:
        for h in range(P):
            for d in range(2):
                st_ref[d, h] = st_scr[2 * h + d]


def _ret_call(dec, acts, s0, *, layer, row0, B, L, out_state, name):
    has_s0 = s0 is not None
    H = RET_HEADS
    P = RET_PACK
    r0 = row0 // L
    st_spec = pl.BlockSpec((None, 2, P, RET_DK, RET_DV), lambda b, h: (b, 0, h, 0, 0))
    cols = lambda width, off: pl.BlockSpec((L, width), lambda b, h: (r0 + b, off // width + h))
    in_specs = [
        pl.BlockSpec(memory_space=pltpu.SMEM),
        cols(P * RET_DK, P_RQ), cols(P * RET_DK, P_RK), cols(P * RET_DV, P_RV), cols(P * RET_DV, P_RG),
    ]
    args = [dec, acts, acts, acts, acts]
    if has_s0:
        in_specs.append(pl.BlockSpec((None, None, 2, P, RET_DK, RET_DV), lambda b, h: (b, layer, 0, h, 0, 0)))
        args.append(s0)
    out_specs = [pl.BlockSpec((L, P * RET_DV), lambda b, h: (b, h))]
    out_shape = [jax.ShapeDtypeStruct((B * L, H * RET_DV), BF16)]
    if out_state:
        out_specs.append(st_spec)
        out_shape.append(jax.ShapeDtypeStruct((B, 2, H, RET_DK, RET_DV), F32))
    res = pl.pallas_call(
        functools.partial(_ret_kernel, L=L, has_s0=has_s0, out_state=out_state),
        grid=(B, H // P),
        in_specs=in_specs,
        out_specs=out_specs,
        out_shape=out_shape,
        scratch_shapes=[pltpu.VMEM((L // 2, P * RET_DV), F32), pltpu.VMEM((L // 2, P * RET_DV), F32),
                        pltpu.VMEM((2 * P, RET_DK, RET_DV), F32),
                        pltpu.VMEM((H, 5, SCAN_BLOCK, SCAN_BLOCK), F32)],
        compiler_params=_params(("arbitrary", "arbitrary")),
        name=name,
    )(*args)
    return res


def _gla_blocks(chains, tri_ref, code_ref):
    T = HG_BLOCK
    K = HG_DK

    def cumulate(c):
        g2 = c["g"]
        c["k"] = 1.0 - jnp.exp2(g2)
        hi = g2.astype(BF16)
        lo = (g2 - hi.astype(F32)).astype(BF16)
        tri = tri_ref[int(c["rev"])]
        c["b"] = _dot(tri, hi) + _dot(tri, lo)

    def leaf(c):
        bl = c["b"].reshape(T // HG_LEAF, HG_LEAF, K)
        r = HG_LEAF // 2 if c["rev"] else HG_LEAF // 2 - 1
        d = jnp.clip(bl - bl[:, r:r + 1, :], -EXP2_CLAMP, EXP2_CLAMP).reshape(T, K)
        e = jnp.exp2(d)
        p = _dot_nt((c["q"] * e).astype(BF16), (c["k"] / e).astype(BF16))
        c["a"] = jnp.where(code_ref[int(c["rev"])] == 0, p, 0.0)

    def level(c, lv, m):
        seg = 2 * m
        ng = T // seg
        b3, q3, k3 = (c[n].reshape(ng, seg, K) for n in ("b", "q", "k"))
        if c["rev"]:
            qs, ks, r = slice(0, m), slice(m, seg), m
        else:
            qs, ks, r = slice(m, seg), slice(0, m), m - 1
        ref = b3[:, r:r + 1, :]
        qm = (q3[:, qs, :] * jnp.exp2(b3[:, qs, :] - ref)).reshape(ng * m, K).astype(BF16)
        kz = k3[:, ks, :] * jnp.exp2(ref - b3[:, ks, :])
        zeros = jnp.zeros((ng, m, K), F32)
        kfull = jnp.concatenate([zeros, kz] if c["rev"] else [kz, zeros], axis=1)
        p = _dot_nt(qm, kfull.reshape(T, K).astype(BF16)).reshape(ng, m, T)
        a3 = c["a"].reshape(ng, seg, T)
        c3 = code_ref[int(c["rev"])].reshape(ng, seg, T)
        upd = jnp.where(c3[:, qs, :] == lv, p, a3[:, qs, :])
        parts = [upd, a3[:, ks, :]] if c["rev"] else [a3[:, ks, :], upd]
        c["a"] = jnp.concatenate(parts, axis=1).reshape(T, T)

    def finish(c):
        b, q, k, v, st = c["b"], c["q"], c["k"], c["v"], c["st"]
        o = _dot(c["a"].astype(BF16), v) + _dot_nt((q * jnp.exp2(b)).astype(BF16), st.astype(BF16))
        end_row = 0 if c["rev"] else T - 1
        b1 = b.reshape(1, T, K)
        b_end = b1[:, end_row:end_row + 1, :]
        kd = (k * jnp.exp2(b_end - b1).reshape(T, K)).astype(BF16)
        c["out"] = (o, st * jnp.exp2(b_end.reshape(1, K)) + _dot_tn(v, kd))

    for c in chains:
        cumulate(c)
    for c in chains:
        leaf(c)
    for lv, m in HG_LEVELS:
        for c in chains:
            level(c, lv, m)
    for c in chains:
        finish(c)
    return [c["out"] for c in chains]


def _hgrn_kernel(*refs, L, P, has_s0, out_state):
    it = iter(refs)
    q_ref, g_f_ref, g_b_ref, v_ref, gate_ref, nw_ref = (next(it) for _ in range(6))
    s0_ref = next(it) if has_s0 else None
    o_ref = next(it)
    st_ref = next(it) if out_state else None
    of_acc, ob_acc, st_scr, tri_ref, code_ref = (next(it) for _ in range(5))

    T = HG_BLOCK
    n = L // T

    @pl.when(jnp.logical_and(pl.program_id(0) == 0, pl.program_id(1) == 0))
    def _():
        ti = lax.broadcasted_iota(jnp.int32, (T, T), 0)
        si = lax.broadcasted_iota(jnp.int32, (T, T), 1)
        x = ti ^ si
        tri_ref[0] = (si <= ti).astype(BF16)
        tri_ref[1] = (si >= ti).astype(BF16)
        lvl = jnp.where(x < HG_LEAF, 0, HG_LEVELS[0][0] - 1 + sum((x >= m).astype(jnp.int32) for _, m in HG_LEVELS))
        code_ref[0] = jnp.where(si <= ti, lvl, -1)
        code_ref[1] = jnp.where(si >= ti, lvl, -1)

    for h in range(P):
        for d in range(2):
            st_scr[2 * h + d] = s0_ref[d, h].T if has_s0 else jnp.zeros((HG_DV, HG_DK), F32)

    def step(i, finalize):
        sls = (pl.ds(pl.multiple_of(i * T, T), T), pl.ds(pl.multiple_of((n - 1 - i) * T, T), T))
        acc_sls = _half_acc_slices(i, n, T, finalize)
        chains = []
        for h in range(P):
            lanes = slice(h * HG_DK, (h + 1) * HG_DK)
            for d, g_ref in enumerate((g_f_ref, g_b_ref)):
                chains.append(dict(q=q_ref[sls[d], lanes].astype(F32), g=g_ref[sls[d], lanes],
                                   v=v_ref[sls[d], lanes],
                                   st=st_scr[2 * h + d], rev=bool(d)))
        res = _gla_blocks(chains, tri_ref, code_ref)
        for h in range(P):
            lanes = slice(h * HG_DK, (h + 1) * HG_DK)
            for d in range(2):
                o, st = res[2 * h + d]
                st_scr[2 * h + d] = st
                if finalize:
                    o = o + (ob_acc, of_acc)[d][acc_sls[d], lanes]
                    ms = jnp.mean(o * o, axis=-1, keepdims=True)
                    o = o * lax.rsqrt(ms + EPS) * nw_ref[:, lanes]
                    o_ref[sls[d], lanes] = (o * gate_ref[sls[d], lanes]).astype(BF16)
                else:
                    (of_acc, ob_acc)[d][acc_sls[d], lanes] = o

    def first_half(i, carry):
        step(i, False)
        return carry

    def second_half(i, carry):
        step(i, True)
        return carry

    lax.fori_loop(0, n // 2, first_half, 0, unroll=min(HG_UNROLL, n // 2))
    lax.fori_loop(n // 2, n, second_half, 0, unroll=min(HG_UNROLL, n // 2))
    if out_state:
        for h in range(P):
            for d in range(2):
                st_ref[d, h] = st_scr[2 * h + d].T


def _hgrn_call(acts, g, nw, s0, *, layer, row0, B, L, out_state, name):
    has_s0 = s0 is not None
    H = HG_HEADS
    P = HG_PACK_SHORT if L <= HG_SHORT_SEQ else HG_PACK_LONG
    W = P * HG_DK
    r0 = row0 // L
    blk = lambda off: pl.BlockSpec((L, W), lambda b, h: (r0 + b, off // W + h))
    st_spec = pl.BlockSpec((None, 2, P, HG_DK, HG_DV), lambda b, h: (b, 0, h, 0, 0))
    in_specs = [blk(P_HQ), blk(0), blk(H * HG_DK), blk(P_HI), blk(P_HOG),
                pl.BlockSpec((1, W), lambda b, h: (0, h))]
    args = [acts, g, g, acts, acts, nw]
    if has_s0:
        in_specs.append(pl.BlockSpec((None, None, 2, P, HG_DK, HG_DV), lambda b, h: (b, layer, 0, h, 0, 0)))
        args.append(s0)
    out_specs = [pl.BlockSpec((L, W), lambda b, h: (b, h))]
    out_shape = [jax.ShapeDtypeStruct((B * L, H * HG_DV), BF16)]
    if out_state:
        out_specs.append(st_spec)
        out_shape.append(jax.ShapeDtypeStruct((B, 2, H, HG_DK, HG_DV), F32))
    return pl.pallas_call(
        functools.partial(_hgrn_kernel, L=L, P=P, has_s0=has_s0, out_state=out_state),
        grid=(B, H // P),
        in_specs=in_specs,
        out_specs=out_specs,
        out_shape=out_shape,
        scratch_shapes=[pltpu.VMEM((L // 2, W), F32), pltpu.VMEM((L // 2, W), F32),
                        pltpu.VMEM((2 * P, HG_DV, HG_DK), F32),
                        pltpu.VMEM((2, HG_BLOCK, HG_BLOCK), BF16),
                        pltpu.VMEM((2, HG_BLOCK, HG_BLOCK), jnp.int32)],
        compiler_params=_params(("arbitrary", "arbitrary")),
        name=name,
    )(*args)


def _merge_kernel(x_ref, mod_ref, oret_a, oret_b, ohg_a, ohg_b, gr_ref, gh_ref, wr_hbm, wh_hbm, wo_hbm, o_ref,
                  wr_ref, wh_ref, wo_ref, w_stage, w_sem, *, n_first):
    i = pl.program_id(0)
    for w_hbm, w_ref in ((wr_hbm, wr_ref), (wh_hbm, wh_ref), (wo_hbm, wo_ref)):
        _stage_weight(i, w_hbm, w_ref, w_stage, w_sem)
    oret = jnp.where(i < n_first, oret_a[...], oret_b[...])
    ohg = jnp.where(i < n_first, ohg_a[...], ohg_b[...])
    y = gr_ref[...] * _dot(oret, wr_ref[...]) + gh_ref[...] * _dot(ohg, wh_ref[...])
    out = _dot(y.astype(BF16), wo_ref[...])
    o_ref[...] = x_ref[...] + mod_ref[5:6, :] * out


def _merge_call(x, mod3, orets, ohgs, acts, wr, wh, wo, *, mod_row_fn, tm, name):
    n = x.shape[0]
    n_first = orets[0].shape[0] // tm
    row = pl.BlockSpec((tm, D_MODEL), lambda i: (i, 0))
    gate = lambda off: pl.BlockSpec((tm, D_MODEL), lambda i: (i, off // D_MODEL))
    first = pl.BlockSpec((tm, D_MODEL), lambda i: (jnp.minimum(i, n_first - 1), 0))
    second = pl.BlockSpec((tm, D_MODEL), lambda i: (jnp.maximum(i - n_first, 0), 0))
    resident, stage, sem = _stage_scratch(wr, 256)
    return pl.pallas_call(
        functools.partial(_merge_kernel, n_first=n_first),
        grid=(n // tm,),
        in_specs=[row, pl.BlockSpec((None, N_MOD, D_MODEL), lambda i: (mod_row_fn(i, tm), 0, 0)),
                  first, second, first, second, gate(P_GR), gate(P_GH), _HBM, _HBM, _HBM],
        out_specs=row,
        out_shape=jax.ShapeDtypeStruct((n, D_MODEL), F32),
        scratch_shapes=[resident, resident, resident, stage, sem],
        compiler_params=_params(("arbitrary",)),
        name=name,
    )(x, mod3, *orets, *ohgs, acts, acts, wr, wh, wo)


def _rope_tables(n_tokens, n_identity):
    pos = np.arange(n_tokens)
    r = (pos // GRID_W).astype(np.float32)
    cl = (pos % GRID_W).astype(np.float32)
    quarter = RET_DK // 4
    inv_freq = (np.float32(ROPE_BASE) ** (-np.arange(quarter, dtype=np.float32) / np.float32(quarter)))
    inv_freq = inv_freq.astype(np.float32)
    ang_r = (r[:, None] * inv_freq[None, :]).astype(np.float64)
    ang_c = (cl[:, None] * inv_freq[None, :]).astype(np.float64)
    cos = np.concatenate([np.cos(ang_r)] * 2 + [np.cos(ang_c)] * 2, axis=-1)
    sins = np.concatenate([-np.sin(ang_r), np.sin(ang_r), -np.sin(ang_c), np.sin(ang_c)], axis=-1)
    cos = np.concatenate([cos, np.ones((n_identity, RET_DK))], axis=0).astype(np.float32)
    sins = np.concatenate([sins, np.zeros((n_identity, RET_DK))], axis=0).astype(np.float32)
    return jnp.asarray(cos), jnp.asarray(sins)


def kernel(x_prompt, x_sample, state_ret, state_hgrn, c, c_ctx, ada_w, ada_b, norm_w, ffn1_w13, ffn1_w2,
           ffn2_w13, ffn2_w2, w_in, ret_decay, hg_lb_logits, hg_norm_w, w_ret_proj, w_hg_proj, w_o,
           final_norm_w):
    depth = ada_w.shape[0]
    assert depth == 1
    l = 0
    bp, lp, _ = x_prompt.shape
    bs, ls, _ = x_sample.shape
    n_ctx = bp * lp
    assert n_ctx % ls == 0 and ls % TM_FFN == 0 and ls % TM_PROJ == 0 and ls % TM_MERGE == 0

    cvec = jnp.concatenate([c_ctx[None, :], c, jnp.zeros((8 - 1 - bs, D_MODEL), F32)], axis=0)
    mod = _mod_call(cvec, ada_w[l], ada_b[l][None, :])
    mod3 = mod.reshape(8, N_MOD, D_MODEL)

    def mod_row(i, tm):
        return jnp.where(i < n_ctx // tm, 0, 1 + (i * tm - n_ctx) // ls)

    def rope_blk(i, tm):
        return jnp.where(i < n_ctx // tm, ls // tm, (i - n_ctx // tm) % (ls // tm))

    nw, fw = norm_w[l], final_norm_w[None, :]
    dec, hg_nw = ret_decay[l], hg_norm_w[l][None, :]

    x = _ffn_call((x_prompt.reshape(n_ctx, D_MODEL), x_sample.reshape(bs * ls, D_MODEL)), mod3, nw, fw,
                  ffn1_w13[l], ffn1_w2[l], n_out_split=None, mod_base=0, norm_row=0, final=False,
                  mod_row_fn=mod_row, tm=TM_FFN, name="ffn1")
    acts, g = _proj_call(x, mod3, nw, hg_lb_logits, w_in[l], _rope_tables(ls, TM_PROJ), rope_blk, layer=l,
                         mod_row_fn=mod_row, tm=TM_PROJ, name="proj")
    ret_ctx, s_ret = _ret_call(dec, acts, None, layer=l, row0=0, B=bp, L=lp, out_state=True, name="ret_ctx")
    ret_lat, = _ret_call(dec, acts, state_ret, layer=l, row0=n_ctx, B=bs, L=ls, out_state=False,
                         name="ret_lat")
    hg_ctx, s_hg = _hgrn_call(acts, g, hg_nw, None, layer=l, row0=0, B=bp, L=lp, out_state=True,
                              name="hgrn_ctx")
    hg_lat, = _hgrn_call(acts, g, hg_nw, state_hgrn, layer=l, row0=n_ctx, B=bs, L=ls, out_state=False,
                         name="hgrn_lat")
    x = _merge_call(x, mod3, (ret_ctx, ret_lat), (hg_ctx, hg_lat), acts, w_ret_proj[l], w_hg_proj[l], w_o[l],
                    mod_row_fn=mod_row, tm=TM_MERGE, name="merge")
    yp, ys = _ffn_call((x,), mod3, nw, fw, ffn2_w13[l], ffn2_w2[l], n_out_split=n_ctx, mod_base=6,
                       norm_row=2, final=True, mod_row_fn=mod_row, tm=TM_FFN, name="ffn2")
    return (yp.reshape(bp, lp, D_MODEL), ys.reshape(bs, ls, D_MODEL), s_ret[:, None], s_hg[:, None])
```

```python
import functools

import numpy as np
import jax
import jax.numpy as jnp
from jax import lax
from jax.experimental import pallas as pl
from jax.experimental.pallas import tpu as pltpu

F32 = jnp.float32
BF16 = jnp.bfloat16

D_MODEL = 1024
GRID_W = 64
RET_HEADS = 4
RET_DK = 128
RET_DV = 256
HG_HEADS = 8
HG_DK = 128
HG_DV = 128
D_FF = 2816
N_MOD = 9
ROPE_BASE = 10000.0
EPS = 1e-6

C_RQ, C_RK, C_RV, C_RG, C_HQ, C_HFF, C_HFB, C_HI, C_HOG, C_GR, C_GH = (
    0, 512, 1024, 2048, 3072, 4096, 5120, 6144, 7168, 8192, 9216)
IN_WIDTH = 10240
P_RQ, P_RK, P_RV, P_RG, P_HQ, P_HI, P_HOG, P_GR, P_GH = 0, 512, 1024, 2048, 3072, 4096, 5120, 6144, 7168
P_WIDTH = 8192

VMEM_LIMIT_BYTES = 56 * 1024 * 1024

TM_FFN = 512
TM_PROJ = 512
TM_MERGE = 512
STAGE_SLOTS = 4
SCAN_BLOCK = 128
RET_PACK = 4
RET_UNROLL = 2
HG_PACK_LONG = 4
HG_PACK_SHORT = 8
HG_SHORT_SEQ = 512
HG_UNROLL = 2
FF_CHUNK = 256
PROJ_CHUNK = 256
LOG2E = 1.4426950408889634
EXP2_CLAMP = 115.0
HG_BLOCK = 128
HG_LEAF = 16
HG_LEVELS = ((4, 16), (5, 32), (6, 64))


def _dot(a, b):
    return jnp.dot(a, b, preferred_element_type=F32)


def _dot_nt(a, b):
    return lax.dot_general(a, b, (((1,), (1,)), ((), ())), preferred_element_type=F32)


def _dot_tn(a, b):
    return lax.dot_general(a, b, (((0,), (0,)), ((), ())), preferred_element_type=F32)


def _sigmoid(x):
    return 0.5 * jnp.tanh(0.5 * x) + 0.5


def _rms(x, w):
    ms = jnp.mean(x * x, axis=-1, keepdims=True)
    return x * lax.rsqrt(ms + EPS) * w


def _rms_modulate(x, w, shift, scale):
    ms = jnp.mean(x * x, axis=-1, keepdims=True)
    return x * lax.rsqrt(ms + EPS) * (w * (1.0 + scale)) + shift


def _params(sem):
    return pltpu.CompilerParams(dimension_semantics=sem, vmem_limit_bytes=VMEM_LIMIT_BYTES)


def _stage_weight(step, w_hbm, w_vmem, stage, sem):
    rows = stage.shape[1]
    n_chunks = w_hbm.shape[0] // rows
    ahead = STAGE_SLOTS - 1

    def copy(k):
        slot = k % STAGE_SLOTS
        return pltpu.make_async_copy(w_hbm.at[pl.ds(k * rows, rows), :], stage.at[slot], sem.at[slot])

    @pl.when(step == 0)
    def _():
        for k in range(min(ahead, n_chunks)):
            copy(k).start()

        def body(k, carry):
            @pl.when(k + ahead < n_chunks)
            def _():
                copy(k + ahead).start()

            copy(k).wait()
            w_vmem[pl.ds(pl.multiple_of(k * rows, rows), rows), :] = stage[k % STAGE_SLOTS].astype(BF16)
            return carry

        lax.fori_loop(0, n_chunks, body, 0)


_HBM = pl.BlockSpec(memory_space=pl.ANY)


def _stage_scratch(w, rows):
    r, c = w.shape
    assert r % rows == 0 and rows % 16 == 0
    return [pltpu.VMEM((r, c), BF16), pltpu.VMEM((STAGE_SLOTS, rows, c), F32),
            pltpu.SemaphoreType.DMA((STAGE_SLOTS,))]


def _mod_kernel(c_ref, w_ref, b_ref, o_ref):
    c = c_ref[...]
    s = (c * _sigmoid(c)).astype(BF16)
    o_ref[...] = _dot(s, w_ref[...].astype(BF16)) + b_ref[...]


def _mod_call(cvec, ada_w, ada_b):
    n = ada_w.shape[1]
    tn = 2304
    return pl.pallas_call(
        _mod_kernel,
        grid=(n // tn,),
        in_specs=[
            pl.BlockSpec((8, D_MODEL), lambda j: (0, 0)),
            pl.BlockSpec((D_MODEL, tn), lambda j: (0, j)),
            pl.BlockSpec((1, tn), lambda j: (0, j)),
        ],
        out_specs=pl.BlockSpec((8, tn), lambda j: (0, j)),
        out_shape=jax.ShapeDtypeStruct((8, n), F32),
        compiler_params=_params(("arbitrary",)),
        name="mod",
    )(cvec, ada_w, ada_b)


def _ffn_kernel(*refs, mod_base, norm_row, final, n_first, two_in, two_out):
    it = iter(refs)
    xa_ref = next(it)
    xb_ref = next(it) if two_in else None
    mod_ref, nw_ref, fw_ref, w13_hbm, w2_hbm = (next(it) for _ in range(5))
    oa_ref = next(it)
    ob_ref = next(it) if two_out else None
    w13_ref, st13, sem13, w2_ref, st2, sem2 = (next(it) for _ in range(6))
    i = pl.program_id(0)
    _stage_weight(i, w13_hbm, w13_ref, st13, sem13)
    _stage_weight(i, w2_hbm, w2_ref, st2, sem2)
    x = xa_ref[...]
    if two_in:
        x = jnp.where(i < n_first, x, xb_ref[...])
    h = _rms_modulate(x, nw_ref[norm_row:norm_row + 1, :], mod_ref[mod_base:mod_base + 1, :],
                      mod_ref[mod_base + 1:mod_base + 2, :]).astype(BF16)
    acc = None
    for c0 in range(0, D_FF, FF_CHUNK):
        a = _dot(h, w13_ref[:, c0:c0 + FF_CHUNK])
        b = _dot(h, w13_ref[:, D_FF + c0:D_FF + c0 + FF_CHUNK])
        act = (a * _sigmoid(a) * b).astype(BF16)
        part = _dot(act, w2_ref[c0:c0 + FF_CHUNK, :])
        acc = part if acc is None else acc + part
    xo = x + (0.5 * mod_ref[mod_base + 2:mod_base + 3, :]) * acc
    if final:
        xo = _rms(xo, fw_ref[...])
    if two_out:
        @pl.when(i < n_first)
        def _():
            oa_ref[...] = xo

        @pl.when(i >= n_first)
        def _():
            ob_ref[...] = xo
    else:
        oa_ref[...] = xo


def _ffn_call(xs, mod3, norm_w, final_w, w13, w2, *, n_out_split, mod_base, norm_row, final, mod_row_fn, tm,
              name):
    two_in = len(xs) == 2
    two_out = n_out_split is not None
    n = sum(x.shape[0] for x in xs)
    n_first = (xs[0].shape[0] if two_in else n_out_split if two_out else n) // tm
    first = pl.BlockSpec((tm, D_MODEL), lambda i: (jnp.minimum(i, n_first - 1), 0))
    second = pl.BlockSpec((tm, D_MODEL), lambda i: (jnp.maximum(i - n_first, 0), 0))
    whole = pl.BlockSpec((tm, D_MODEL), lambda i: (i, 0))
    kern = functools.partial(_ffn_kernel, mod_base=mod_base, norm_row=norm_row, final=final, n_first=n_first,
                             two_in=two_in, two_out=two_out)
    if two_out:
        out_specs = [first, second]
        out_shape = [jax.ShapeDtypeStruct((n_out_split, D_MODEL), F32),
                     jax.ShapeDtypeStruct((n - n_out_split, D_MODEL), F32)]
    else:
        out_specs = whole
        out_shape = jax.ShapeDtypeStruct((n, D_MODEL), F32)
    return pl.pallas_call(
        kern,
        grid=(n // tm,),
        in_specs=([first, second] if two_in else [whole]) + [
            pl.BlockSpec((None, N_MOD, D_MODEL), lambda i: (mod_row_fn(i, tm), 0, 0)),
            pl.BlockSpec((3, D_MODEL), lambda i: (0, 0)),
            pl.BlockSpec((1, D_MODEL), lambda i: (0, 0)),
            _HBM,
            _HBM,
        ],
        out_specs=out_specs,
        out_shape=out_shape,
        scratch_shapes=_stage_scratch(w13, 64) + _stage_scratch(w2, 256),
        compiler_params=_params(("arbitrary",)),
        name=name,
    )(*xs, mod3, norm_w, final_w, w13, w2)


def _rope(x, cos, sins, first_quarter):
    partner = jnp.where(first_quarter, pltpu.roll(x, 96, 1), pltpu.roll(x, 32, 1))
    return x * cos + partner * sins


def _proj_kernel(*refs, layer, rope):
    it = iter(refs)
    x_ref, mod_ref, nw_ref, lbl_ref, w_hbm = (next(it) for _ in range(5))
    cos_ref, sin_ref = (next(it), next(it)) if rope else (None, None)
    p_ref, g_ref = next(it), next(it)
    w_ref, w_stage, w_sem = next(it), next(it), next(it)
    _stage_weight(pl.program_id(0), w_hbm, w_ref, w_stage, w_sem)
    h = _rms_modulate(x_ref[...], nw_ref[1:2, :], mod_ref[3:4, :], mod_ref[4:5, :]).astype(BF16)

    if rope:
        lane = lax.broadcasted_iota(jnp.int32, (x_ref.shape[0], RET_DK), 1)
        first_quarter = (lane & 32) == 0

    def rot(t):
        if not rope:
            return t
        heads = [_rope(t[:, j:j + RET_DK], cos_ref[...], sin_ref[...], first_quarter)
                 for j in range(0, t.shape[1], RET_DK)]
        return jnp.concatenate(heads, axis=1)

    def silu(t):
        return t * _sigmoid(t)

    def log_forget(d):
        def fn(t, c):
            lg = lbl_ref[d, :, c:c + PROJ_CHUNK]
            e = jnp.exp(lg - jnp.max(lg, axis=0, keepdims=True))
            lb = jnp.sum(e[0:layer + 1, :], axis=0, keepdims=True) / jnp.sum(e, axis=0, keepdims=True)
            return jnp.log(lb + (1.0 - lb) * _sigmoid(t)) * LOG2E
        return fn

    parts = (
        (C_RQ, 512, p_ref, P_RQ, lambda t, c: rot(t)),
        (C_RK, 512, p_ref, P_RK, lambda t, c: rot(t * (RET_DK ** -0.5))),
        (C_RV, 1024, p_ref, P_RV, lambda t, c: t),
        (C_RG, 1024, p_ref, P_RG, lambda t, c: silu(t)),
        (C_HQ, 1024, p_ref, P_HQ, lambda t, c: silu(t) * (HG_DK ** -0.5)),
        (C_HFF, 1024, g_ref, 0, log_forget(0)),
        (C_HFB, 1024, g_ref, 1024, log_forget(1)),
        (C_HI, 1024, p_ref, P_HI, lambda t, c: t),
        (C_HOG, 1024, p_ref, P_HOG, lambda t, c: silu(t)),
        (C_GR, 1024, p_ref, P_GR, lambda t, c: _sigmoid(t)),
        (C_GH, 1024, p_ref, P_GH, lambda t, c: _sigmoid(t)),
    )
    for c in range(0, 1024, PROJ_CHUNK):
        for w0, width, o_ref, o0, fn in parts:
            if c < width:
                t = _dot(h, w_ref[:, w0 + c:w0 + c + PROJ_CHUNK])
                o_ref[:, o0 + c:o0 + c + PROJ_CHUNK] = fn(t, c).astype(o_ref.dtype)


def _proj_call(x, mod3, norm_w, lb_logits, w_in, rope_tabs, rope_blk_fn, *, layer, mod_row_fn, tm, name):
    n = x.shape[0]
    slots = lb_logits.shape[1]
    row = lambda w: pl.BlockSpec((tm, w), lambda i: (i, 0))
    outs = [(P_WIDTH, BF16), (2 * HG_HEADS * HG_DK, F32)]
    in_specs = [
        row(D_MODEL),
        pl.BlockSpec((None, N_MOD, D_MODEL), lambda i: (mod_row_fn(i, tm), 0, 0)),
        pl.BlockSpec((3, D_MODEL), lambda i: (0, 0)),
        pl.BlockSpec((2, slots, 1024), lambda i: (0, 0, 0)),
        _HBM,
    ]
    args = [x, mod3, norm_w, lb_logits, w_in]
    if rope_tabs is not None:
        in_specs += [pl.BlockSpec((tm, RET_DK), lambda i: (rope_blk_fn(i, tm), 0))] * 2
        args += list(rope_tabs)
    return pl.pallas_call(
        functools.partial(_proj_kernel, layer=layer, rope=rope_tabs is not None),
        grid=(n // tm,),
        in_specs=in_specs,
        out_specs=[row(w) for w, _ in outs],
        out_shape=[jax.ShapeDtypeStruct((n, w), dt) for w, dt in outs],
        scratch_shapes=_stage_scratch(w_in, 16),
        compiler_params=_params(("arbitrary",)),
        name=name,
    )(*args)


def _log_sigmoid(x):
    return -(jnp.maximum(-x, 0.0) + jnp.log1p(jnp.exp(-jnp.abs(x))))


def _half_acc_slices(i, n, blk, finalize):
    if finalize:
        starts = ((i - n // 2) * blk, (n - 1 - i) * blk)
    else:
        starts = (i * blk, (n // 2 - 1 - i) * blk)
    return tuple(pl.ds(pl.multiple_of(s, blk), blk) for s in starts)


def _ret_kernel(*refs, L, has_s0, out_state):
    it = iter(refs)
    dec_ref = next(it)
    q_ref, k_ref, v_ref, gate_ref = next(it), next(it), next(it), next(it)
    s0_ref = next(it) if has_s0 else None
    o_ref = next(it)
    st_ref = next(it) if out_state else None
    of_acc, ob_acc, st_scr, cst = next(it), next(it), next(it), next(it)

    C = SCAN_BLOCK
    P = RET_PACK
    n = L // C
    pair = pl.program_id(1)
    ii = lax.broadcasted_iota(jnp.int32, (C, C), 0).astype(F32)
    jj = lax.broadcasted_iota(jnp.int32, (C, C), 1).astype(F32)
    diff = ii - jj

    @pl.when(jnp.logical_and(pl.program_id(0) == 0, pair == 0))
    def _():
        for hd in range(RET_HEADS):
            lgf = _log_sigmoid(jnp.full((C, C), dec_ref[0, hd], F32))
            lgb = _log_sigmoid(jnp.full((C, C), dec_ref[1, hd], F32))
            cst[hd, 0] = (jnp.where(diff >= 0, jnp.exp(jnp.maximum(diff, 0.0) * lgf), 0.0)
                          + jnp.where(diff <= 0, jnp.exp(jnp.maximum(-diff, 0.0) * lgb), 0.0))
            cst[hd, 1] = jnp.exp((ii + 1.0) * lgf)
            cst[hd, 2] = jnp.exp((C - 1.0 - ii) * lgf)
            cst[hd, 3] = jnp.exp((C - ii) * lgb)
            cst[hd, 4] = jnp.exp(ii * lgb)

    cds = []
    for h in range(P):
        hd = pair * P + h
        cds.append([jnp.exp(C * _log_sigmoid(jnp.full((1, RET_DV), dec_ref[d, hd], F32))) for d in range(2)])
        for d in range(2):
            st_scr[2 * h + d] = s0_ref[d, h] if has_s0 else jnp.zeros((RET_DK, RET_DV), F32)

    def step(i, finalize):
        sls = (pl.ds(pl.multiple_of(i * C, C), C), pl.ds(pl.multiple_of((n - 1 - i) * C, C), C))
        acc_sls = _half_acc_slices(i, n, C, finalize)
        chains = []
        for h in range(P):
            ql = slice(h * RET_DK, (h + 1) * RET_DK)
            vl = slice(h * RET_DV, (h + 1) * RET_DV)
            for d in range(2):
                chains.append(dict(h=h, d=d, vl=vl, q=q_ref[sls[d], ql], k=k_ref[sls[d], ql],
                                   v=v_ref[sls[d], vl]))
        for c in chains:
            if c["d"] == 0:
                c["s"] = _dot_nt(c["q"], c["k"])
        for c in chains:
            h, d, v, vl = c["h"], c["d"], c["v"], c["vl"]
            hd = pair * P + h
            st = st_scr[2 * h + d]
            qd = (c["q"].astype(F32) * cst[hd, 1 + 2 * d]).astype(BF16)
            if d == 0:
                p = (c["s"] * cst[hd, 0]).astype(BF16)
                o = _dot(jnp.concatenate([p, qd], axis=1), jnp.concatenate([v, st.astype(BF16)], axis=0))
            else:
                o = _dot(qd, st.astype(BF16))
            kd = (c["k"].astype(F32) * cst[hd, 2 + 2 * d]).astype(BF16)
            st_scr[2 * h + d] = cds[h][d] * st + _dot_tn(kd, v)
            if finalize:
                o = o + (ob_acc, of_acc)[d][acc_sls[d], vl]
                ms = jnp.mean(o * o, axis=-1, keepdims=True)
                o_ref[sls[d], vl] = (o * lax.rsqrt(ms + EPS) * gate_ref[sls[d], vl]).astype(BF16)
            else:
                (of_acc, ob_acc)[d][acc_sls[d], vl] = o

    def first_half(i, carry):
        step(i, False)
        return carry

    def second_half(i, carry):
        step(i, True)
        return carry

    lax.fori_loop(0, n // 2, first_half, 0, unroll=min(RET_UNROLL, n // 2))
    lax.fori_loop(n // 2, n, second_half, 0, unroll=min(RET_UNROLL, n // 2))
    if out_state:
        for h in range(P):
            for d in range(2):
                st_ref[d, h] = st_scr[2 * h + d]


def _ret_call(dec, acts, s0, *, layer, row0, B, L, out_state, name):
    has_s0 = s0 is not None
    H = RET_HEADS
    P = RET_PACK
    r0 = row0 // L
    st_spec = pl.BlockSpec((None, 2, P, RET_DK, RET_DV), lambda b, h: (b, 0, h, 0, 0))
    cols = lambda width, off: pl.BlockSpec((L, width), lambda b, h: (r0 + b, off // width + h))
    in_specs = [
        pl.BlockSpec(memory_space=pltpu.SMEM),
        cols(P * RET_DK, P_RQ), cols(P * RET_DK, P_RK), cols(P * RET_DV, P_RV), cols(P * RET_DV, P_RG),
    ]
    args = [dec, acts, acts, acts, acts]
    if has_s0:
        in_specs.append(pl.BlockSpec((None, None, 2, P, RET_DK, RET_DV), lambda b, h: (b, layer, 0, h, 0, 0)))
        args.append(s0)
    out_specs = [pl.BlockSpec((L, P * RET_DV), lambda b, h: (b, h))]
    out_shape = [jax.ShapeDtypeStruct((B * L, H * RET_DV), BF16)]
    if out_state:
        out_specs.append(st_spec)
        out_shape.append(jax.ShapeDtypeStruct((B, 2, H, RET_DK, RET_DV), F32))
    res = pl.pallas_call(
        functools.partial(_ret_kernel, L=L, has_s0=has_s0, out_state=out_state),
        grid=(B, H // P),
        in_specs=in_specs,
        out_specs=out_specs,
        out_shape=out_shape,
        scratch_shapes=[pltpu.VMEM((L // 2, P * RET_DV), F32), pltpu.VMEM((L // 2, P * RET_DV), F32),
                        pltpu.VMEM((2 * P, RET_DK, RET_DV), F32),
                        pltpu.VMEM((H, 5, SCAN_BLOCK, SCAN_BLOCK), F32)],
        compiler_params=_params(("arbitrary", "arbitrary")),
        name=name,
    )(*args)
    return res


def _gla_blocks(chains, tri_ref, code_ref):
    T = HG_BLOCK
    K = HG_DK

    def cumulate(c):
        g2 = c["g"]
        c["k"] = 1.0 - jnp.exp2(g2)
        hi = g2.astype(BF16)
        lo = (g2 - hi.astype(F32)).astype(BF16)
        tri = tri_ref[int(c["rev"])]
        c["b"] = _dot(tri, hi) + _dot(tri, lo)

    def leaf(c):
        bl = c["b"].reshape(T // HG_LEAF, HG_LEAF, K)
        r = HG_LEAF // 2 if c["rev"] else HG_LEAF // 2 - 1
        d = jnp.clip(bl - bl[:, r:r + 1, :], -EXP2_CLAMP, EXP2_CLAMP).reshape(T, K)
        e = jnp.exp2(d)
        p = _dot_nt((c["q"] * e).astype(BF16), (c["k"] / e).astype(BF16))
        c["a"] = jnp.where(code_ref[int(c["rev"])] == 0, p, 0.0)

    def level(c, lv, m):
        seg = 2 * m
        ng = T // seg
        b3, q3, k3 = (c[n].reshape(ng, seg, K) for n in ("b", "q", "k"))
        if c["rev"]:
            qs, ks, r = slice(0, m), slice(m, seg), m
        else:
            qs, ks, r = slice(m, seg), slice(0, m), m - 1
        ref = b3[:, r:r + 1, :]
        qm = (q3[:, qs, :] * jnp.exp2(b3[:, qs, :] - ref)).reshape(ng * m, K).astype(BF16)
        kz = k3[:, ks, :] * jnp.exp2(ref - b3[:, ks, :])
        zeros = jnp.zeros((ng, m, K), F32)
        kfull = jnp.concatenate([zeros, kz] if c["rev"] else [kz, zeros], axis=1)
        p = _dot_nt(qm, kfull.reshape(T, K).astype(BF16)).reshape(ng, m, T)
        a3 = c["a"].reshape(ng, seg, T)
        c3 = code_ref[int(c["rev"])].reshape(ng, seg, T)
        upd = jnp.where(c3[:, qs, :] == lv, p, a3[:, qs, :])
        parts = [upd, a3[:, ks, :]] if c["rev"] else [a3[:, ks, :], upd]
        c["a"] = jnp.concatenate(parts, axis=1).reshape(T, T)

    def finish(c):
        b, q, k, v, st = c["b"], c["q"], c["k"], c["v"], c["st"]
        o = _dot(c["a"].astype(BF16), v) + _dot_nt((q * jnp.exp2(b)).astype(BF16), st.astype(BF16))
        end_row = 0 if c["rev"] else T - 1
        b1 = b.reshape(1, T, K)
        b_end = b1[:, end_row:end_row + 1, :]
        kd = (k * jnp.exp2(b_end - b1).reshape(T, K)).astype(BF16)
        c["out"] = (o, st * jnp.exp2(b_end.reshape(1, K)) + _dot_tn(v, kd))

    for c in chains:
        cumulate(c)
    for c in chains:
        leaf(c)
    for lv, m in HG_LEVELS:
        for c in chains:
            level(c, lv, m)
    for c in chains:
        finish(c)
    return [c["out"] for c in chains]


def _hgrn_kernel(*refs, L, P, has_s0, out_state):
    it = iter(refs)
    q_ref, g_f_ref, g_b_ref, v_ref, gate_ref, nw_ref = (next(it) for _ in range(6))
    s0_ref = next(it) if has_s0 else None
    o_ref = next(it)
    st_ref = next(it) if out_state else None
    of_acc, ob_acc, st_scr, tri_ref, code_ref = (next(it) for _ in range(5))

    T = HG_BLOCK
    n = L // T

    @pl.when(jnp.logical_and(pl.program_id(0) == 0, pl.program_id(1) == 0))
    def _():
        ti = lax.broadcasted_iota(jnp.int32, (T, T), 0)
        si = lax.broadcasted_iota(jnp.int32, (T, T), 1)
        x = ti ^ si
        tri_ref[0] = (si <= ti).astype(BF16)
        tri_ref[1] = (si >= ti).astype(BF16)
        lvl = jnp.where(x < HG_LEAF, 0, HG_LEVELS[0][0] - 1 + sum((x >= m).astype(jnp.int32) for _, m in HG_LEVELS))
        code_ref[0] = jnp.where(si <= ti, lvl, -1)
        code_ref[1] = jnp.where(si >= ti, lvl, -1)

    for h in range(P):
        for d in range(2):
            st_scr[2 * h + d] = s0_ref[d, h].T if has_s0 else jnp.zeros((HG_DV, HG_DK), F32)

    def step(i, finalize):
        sls = (pl.ds(pl.multiple_of(i * T, T), T), pl.ds(pl.multiple_of((n - 1 - i) * T, T), T))
        acc_sls = _half_acc_slices(i, n, T, finalize)
        chains = []
        for h in range(P):
            lanes = slice(h * HG_DK, (h + 1) * HG_DK)
            for d, g_ref in enumerate((g_f_ref, g_b_ref)):
                chains.append(dict(q=q_ref[sls[d], lanes].astype(F32), g=g_ref[sls[d], lanes],
                                   v=v_ref[sls[d], lanes],
                                   st=st_scr[2 * h + d], rev=bool(d)))
        res = _gla_blocks(chains, tri_ref, code_ref)
        for h in range(P):
            lanes = slice(h * HG_DK, (h + 1) * HG_DK)
            for d in range(2):
                o, st = res[2 * h + d]
                st_scr[2 * h + d] = st
                if finalize:
                    o = o + (ob_acc, of_acc)[d][acc_sls[d], lanes]
                    ms = jnp.mean(o * o, axis=-1, keepdims=True)
                    o = o * lax.rsqrt(ms + EPS) * nw_ref[:, lanes]
                    o_ref[sls[d], lanes] = (o * gate_ref[sls[d], lanes]).astype(BF16)
                else:
                    (of_acc, ob_acc)[d][acc_sls[d], lanes] = o

    def first_half(i, carry):
        step(i, False)
        return carry

    def second_half(i, carry):
        step(i, True)
        return carry

    lax.fori_loop(0, n // 2, first_half, 0, unroll=min(HG_UNROLL, n // 2))
    lax.fori_loop(n // 2, n, second_half, 0, unroll=min(HG_UNROLL, n // 2))
    if out_state:
        for h in range(P):
            for d in range(2):
                st_ref[d, h] = st_scr[2 * h + d].T


def _hgrn_call(acts, g, nw, s0, *, layer, row0, B, L, out_state, name):
    has_s0 = s0 is not None
    H = HG_HEADS
    P = HG_PACK_SHORT if L <= HG_SHORT_SEQ else HG_PACK_LONG
    W = P * HG_DK
    r0 = row0 // L
    blk = lambda off: pl.BlockSpec((L, W), lambda b, h: (r0 + b, off // W + h))
    st_spec = pl.BlockSpec((None, 2, P, HG_DK, HG_DV), lambda b, h: (b, 0, h, 0, 0))
    in_specs = [blk(P_HQ), blk(0), blk(H * HG_DK), blk(P_HI), blk(P_HOG),
                pl.BlockSpec((1, W), lambda b, h: (0, h))]
    args = [acts, g, g, acts, acts, nw]
    if has_s0:
        in_specs.append(pl.BlockSpec((None, None, 2, P, HG_DK, HG_DV), lambda b, h: (b, layer, 0, h, 0, 0)))
        args.append(s0)
    out_specs = [pl.BlockSpec((L, W), lambda b, h: (b, h))]
    out_shape = [jax.ShapeDtypeStruct((B * L, H * HG_DV), BF16)]
    if out_state:
        out_specs.append(st_spec)
        out_shape.append(jax.ShapeDtypeStruct((B, 2, H, HG_DK, HG_DV), F32))
    return pl.pallas_call(
        functools.partial(_hgrn_kernel, L=L, P=P, has_s0=has_s0, out_state=out_state),
        grid=(B, H // P),
        in_specs=in_specs,
        out_specs=out_specs,
        out_shape=out_shape,
        scratch_shapes=[pltpu.VMEM((L // 2, W), F32), pltpu.VMEM((L // 2, W), F32),
                        pltpu.VMEM((2 * P, HG_DV, HG_DK), F32),
                        pltpu.VMEM((2, HG_BLOCK, HG_BLOCK), BF16),
                        pltpu.VMEM((2, HG_BLOCK, HG_BLOCK), jnp.int32)],
        compiler_params=_params(("arbitrary", "arbitrary")),
        name=name,
    )(*args)


def _merge_kernel(x_ref, mod_ref, oret_a, oret_b, ohg_a, ohg_b, gr_ref, gh_ref, wr_hbm, wh_hbm, wo_hbm, o_ref,
                  wr_ref, wh_ref, wo_ref, w_stage, w_sem, *, n_first):
    i = pl.program_id(0)
    for w_hbm, w_ref in ((wr_hbm, wr_ref), (wh_hbm, wh_ref), (wo_hbm, wo_ref)):
        _stage_weight(i, w_hbm, w_ref, w_stage, w_sem)
    oret = jnp.where(i < n_first, oret_a[...], oret_b[...])
    ohg = jnp.where(i < n_first, ohg_a[...], ohg_b[...])
    y = gr_ref[...] * _dot(oret, wr_ref[...]) + gh_ref[...] * _dot(ohg, wh_ref[...])
    out = _dot(y.astype(BF16), wo_ref[...])
    o_ref[...] = x_ref[...] + mod_ref[5:6, :] * out


def _merge_call(x, mod3, orets, ohgs, acts, wr, wh, wo, *, mod_row_fn, tm, name):
    n = x.shape[0]
    n_first = orets[0].shape[0] // tm
    row = pl.BlockSpec((tm, D_MODEL), lambda i: (i, 0))
    gate = lambda off: pl.BlockSpec((tm, D_MODEL), lambda i: (i, off // D_MODEL))
    first = pl.BlockSpec((tm, D_MODEL), lambda i: (jnp.minimum(i, n_first - 1), 0))
    second = pl.BlockSpec((tm, D_MODEL), lambda i: (jnp.maximum(i - n_first, 0), 0))
    resident, stage, sem = _stage_scratch(wr, 256)
    return pl.pallas_call(
        functools.partial(_merge_kernel, n_first=n_first),
        grid=(n // tm,),
        in_specs=[row, pl.BlockSpec((None, N_MOD, D_MODEL), lambda i: (mod_row_fn(i, tm), 0, 0)),
                  first, second, first, second, gate(P_GR), gate(P_GH), _HBM, _HBM, _HBM],
        out_specs=row,
        out_shape=jax.ShapeDtypeStruct((n, D_MODEL), F32),
        scratch_shapes=[resident, resident, resident, stage, sem],
        compiler_params=_params(("arbitrary",)),
        name=name,
    )(x, mod3, *orets, *ohgs, acts, acts, wr, wh, wo)


def _rope_tables(n_tokens, n_identity):
    pos = np.arange(n_tokens)
    r = (pos // GRID_W).astype(np.float32)
    cl = (pos % GRID_W).astype(np.float32)
    quarter = RET_DK // 4
    inv_freq = (np.float32(ROPE_BASE) ** (-np.arange(quarter, dtype=np.float32) / np.float32(quarter)))
    inv_freq = inv_freq.astype(np.float32)
    ang_r = (r[:, None] * inv_freq[None, :]).astype(np.float64)
    ang_c = (cl[:, None] * inv_freq[None, :]).astype(np.float64)
    cos = np.concatenate([np.cos(ang_r)] * 2 + [np.cos(ang_c)] * 2, axis=-1)
    sins = np.concatenate([-np.sin(ang_r), np.sin(ang_r), -np.sin(ang_c), np.sin(ang_c)], axis=-1)
    cos = np.concatenate([cos, np.ones((n_identity, RET_DK))], axis=0).astype(np.float32)
    sins = np.concatenate([sins, np.zeros((n_identity, RET_DK))], axis=0).astype(np.float32)
    return jnp.asarray(cos), jnp.asarray(sins)


def kernel(x_prompt, x_sample, state_ret, state_hgrn, c, c_ctx, ada_w, ada_b, norm_w, ffn1_w13, ffn1_w2,
           ffn2_w13, ffn2_w2, w_in, ret_decay, hg_lb_logits, hg_norm_w, w_ret_proj, w_hg_proj, w_o,
           final_norm_w):
    depth = ada_w.shape[0]
    assert depth == 1
    l = 0
    bp, lp, _ = x_prompt.shape
    bs, ls, _ = x_sample.shape
    n_ctx = bp * lp
    assert n_ctx % ls == 0 and ls % TM_FFN == 0 and ls % TM_PROJ == 0 and ls % TM_MERGE == 0

    cvec = jnp.concatenate([c_ctx[None, :], c, jnp.zeros((8 - 1 - bs, D_MODEL), F32)], axis=0)
    mod = _mod_call(cvec, ada_w[l], ada_b[l][None, :])
    mod3 = mod.reshape(8, N_MOD, D_MODEL)

    def mod_row(i, tm):
        return jnp.where(i < n_ctx // tm, 0, 1 + (i * tm - n_ctx) // ls)

    def rope_blk(i, tm):
        return jnp.where(i < n_ctx // tm, ls // tm, (i - n_ctx // tm) % (ls // tm))

    nw, fw = norm_w[l], final_norm_w[None, :]
    dec, hg_nw = ret_decay[l], hg_norm_w[l][None, :]

    x = _ffn_call((x_prompt.reshape(n_ctx, D_MODEL), x_sample.reshape(bs * ls, D_MODEL)), mod3, nw, fw,
                  ffn1_w13[l], ffn1_w2[l], n_out_split=None, mod_base=0, norm_row=0, final=False,
                  mod_row_fn=mod_row, tm=TM_FFN, name="ffn1")
    acts, g = _proj_call(x, mod3, nw, hg_lb_logits, w_in[l], _rope_tables(ls, TM_PROJ), rope_blk, layer=l,
                         mod_row_fn=mod_row, tm=TM_PROJ, name="proj")
    ret_ctx, s_ret = _ret_call(dec, acts, None, layer=l, row0=0, B=bp, L=lp, out_state=True, name="ret_ctx")
    ret_lat, = _ret_call(dec, acts, state_ret, layer=l, row0=n_ctx, B=bs, L=ls, out_state=False,
                         name="ret_lat")
    hg_ctx, s_hg = _hgrn_call(acts, g, hg_nw, None, layer=l, row0=0, B=bp, L=lp, out_state=True,
                              name="hgrn_ctx")
    hg_lat, = _hgrn_call(acts, g, hg_nw, state_hgrn, layer=l, row0=n_ctx, B=bs, L=ls, out_state=False,
                         name="hgrn_lat")
    x = _merge_call(x, mod3, (ret_ctx, ret_lat), (hg_ctx, hg_lat), acts, w_ret_proj[l], w_hg_proj[l], w_o[l],
                    mod_row_fn=mod_row, tm=TM_MERGE, name="merge")
    yp, ys = _ffn_call((x,), mod3, nw, fw, ffn2_w13[l], ffn2_w2[l], n_out_split=n_ctx, mod_base=6,
                       norm_row=2, final=True, mod_row_fn=mod_row, tm=TM_FFN, name="ffn2")
    return (yp.reshape(bp, lp, D_MODEL), ys.reshape(bs, ls, D_MODEL), s_ret[:, None], s_hg[:, None])
```

```python
import functools

import numpy as np
import jax
import jax.numpy as jnp
from jax import lax
from jax.experimental import pallas as pl
from jax.experimental.pallas import tpu as pltpu

F32 = jnp.float32
BF16 = jnp.bfloat16

D_MODEL = 1024
GRID_W = 64
RET_HEADS = 4
RET_DK = 128
RET_DV = 256
HG_HEADS = 8
HG_DK = 128
HG_DV = 128
D_FF = 2816
N_MOD = 9
ROPE_BASE = 10000.0
EPS = 1e-6

C_RQ, C_RK, C_RV, C_RG, C_HQ, C_HFF, C_HFB, C_HI, C_HOG, C_GR, C_GH = (
    0, 512, 1024, 2048, 3072, 4096, 5120, 6144, 7168, 8192, 9216)
IN_WIDTH = 10240
P_RQ, P_RK, P_RV, P_RG, P_HQ, P_HI, P_HOG, P_GR, P_GH = 0, 512, 1024, 2048, 3072, 4096, 5120, 6144, 7168
P_WIDTH = 8192

VMEM_LIMIT_BYTES = 56 * 1024 * 1024

TM_FFN = 512
TM_PROJ = 512
TM_MERGE = 512
STAGE_SLOTS = 4
SCAN_BLOCK = 128
RET_PACK = 4
RET_UNROLL = 2
HG_PACK_LONG = 4
HG_PACK_SHORT = 8
HG_SHORT_SEQ = 512
HG_UNROLL = 2
FF_CHUNK = 256
PROJ_CHUNK = 256
LOG2E = 1.4426950408889634
EXP2_CLAMP = 115.0
HG_BLOCK = 128
HG_LEAF = 16
HG_LEVELS = ((4, 16), (5, 32), (6, 64))


def _dot(a, b):
    return jnp.dot(a, b, preferred_element_type=F32)


def _dot_nt(a, b):
    return lax.dot_general(a, b, (((1,), (1,)), ((), ())), preferred_element_type=F32)


def _dot_tn(a, b):
    return lax.dot_general(a, b, (((0,), (0,)), ((), ())), preferred_element_type=F32)


def _sigmoid(x):
    return 0.5 * jnp.tanh(0.5 * x) + 0.5


def _rms(x, w):
    ms = jnp.mean(x * x, axis=-1, keepdims=True)
    return x * lax.rsqrt(ms + EPS) * w


def _rms_modulate(x, w, shift, scale):
    ms = jnp.mean(x * x, axis=-1, keepdims=True)
    return x * lax.rsqrt(ms + EPS) * (w * (1.0 + scale)) + shift


def _params(sem):
    return pltpu.CompilerParams(dimension_semantics=sem, vmem_limit_bytes=VMEM_LIMIT_BYTES)


def _stage_weight(step, w_hbm, w_vmem, rows):
    r, c = w_hbm.shape
    assert r % rows == 0 and rows % 16 == 0
    n_chunks = r // rows
    ahead = STAGE_SLOTS - 1

    def run(stage, sem):
        def copy(k):
            slot = k % STAGE_SLOTS
            return pltpu.make_async_copy(w_hbm.at[pl.ds(k * rows, rows), :], stage.at[slot], sem.at[slot])

        for k in range(min(ahead, n_chunks)):
            copy(k).start()

        def body(k, carry):
            @pl.when(k + ahead < n_chunks)
            def _():
                copy(k + ahead).start()

            copy(k).wait()
            w_vmem[pl.ds(pl.multiple_of(k * rows, rows), rows), :] = stage[k % STAGE_SLOTS].astype(BF16)
            return carry

        lax.fori_loop(0, n_chunks, body, 0)

    @pl.when(step == 0)
    def _():
        pl.run_scoped(run, pltpu.VMEM((STAGE_SLOTS, rows, c), F32), pltpu.SemaphoreType.DMA((STAGE_SLOTS,)))


_HBM = pl.BlockSpec(memory_space=pl.ANY)


def _bf16_copy(w):
    return pltpu.VMEM(w.shape, BF16)


def _mod_kernel(c_ref, w_ref, b_ref, o_ref):
    c = c_ref[...]
    s = (c * _sigmoid(c)).astype(BF16)
    o_ref[...] = _dot(s, w_ref[...].astype(BF16)) + b_ref[...]


def _mod_call(cvec, ada_w, ada_b):
    n = ada_w.shape[1]
    tn = 2304
    return pl.pallas_call(
        _mod_kernel,
        grid=(n // tn,),
        in_specs=[
            pl.BlockSpec((8, D_MODEL), lambda j: (0, 0)),
            pl.BlockSpec((D_MODEL, tn), lambda j: (0, j)),
            pl.BlockSpec((1, tn), lambda j: (0, j)),
        ],
        out_specs=pl.BlockSpec((8, tn), lambda j: (0, j)),
        out_shape=jax.ShapeDtypeStruct((8, n), F32),
        compiler_params=_params(("arbitrary",)),
        name="mod",
    )(cvec, ada_w, ada_b)


def _ffn_kernel(*refs, mod_base, norm_row, final, n_first, two_in, two_out):
    it = iter(refs)
    xa_ref = next(it)
    xb_ref = next(it) if two_in else None
    mod_ref, nw_ref, fw_ref, w13_hbm, w2_hbm = (next(it) for _ in range(5))
    oa_ref = next(it)
    ob_ref = next(it) if two_out else None
    w13_ref, w2_ref = next(it), next(it)
    i = pl.program_id(0)
    _stage_weight(i, w13_hbm, w13_ref, 64)
    _stage_weight(i, w2_hbm, w2_ref, 256)
    x = xa_ref[...]
    if two_in:
        x = jnp.where(i < n_first, x, xb_ref[...])
    h = _rms_modulate(x, nw_ref[norm_row:norm_row + 1, :], mod_ref[mod_base:mod_base + 1, :],
                      mod_ref[mod_base + 1:mod_base + 2, :]).astype(BF16)
    acc = None
    for c0 in range(0, D_FF, FF_CHUNK):
        a = _dot(h, w13_ref[:, c0:c0 + FF_CHUNK])
        b = _dot(h, w13_ref[:, D_FF + c0:D_FF + c0 + FF_CHUNK])
        act = (a * _sigmoid(a) * b).astype(BF16)
        part = _dot(act, w2_ref[c0:c0 + FF_CHUNK, :])
        acc = part if acc is None else acc + part
    xo = x + (0.5 * mod_ref[mod_base + 2:mod_base + 3, :]) * acc
    if final:
        xo = _rms(xo, fw_ref[...])
    if two_out:
        @pl.when(i < n_first)
        def _():
            oa_ref[...] = xo

        @pl.when(i >= n_first)
        def _():
            ob_ref[...] = xo
    else:
        oa_ref[...] = xo


def _ffn_call(xs, mod3, norm_w, final_w, w13, w2, *, n_out_split, mod_base, norm_row, final, mod_row_fn, tm,
              name):
    two_in = len(xs) == 2
    two_out = n_out_split is not None
    n = sum(x.shape[0] for x in xs)
    n_first = (xs[0].shape[0] if two_in else n_out_split if two_out else n) // tm
    first = pl.BlockSpec((tm, D_MODEL), lambda i: (jnp.minimum(i, n_first - 1), 0))
    second = pl.BlockSpec((tm, D_MODEL), lambda i: (jnp.maximum(i - n_first, 0), 0))
    whole = pl.BlockSpec((tm, D_MODEL), lambda i: (i, 0))
    kern = functools.partial(_ffn_kernel, mod_base=mod_base, norm_row=norm_row, final=final, n_first=n_first,
                             two_in=two_in, two_out=two_out)
    if two_out:
        out_specs = [first, second]
        out_shape = [jax.ShapeDtypeStruct((n_out_split, D_MODEL), F32),
                     jax.ShapeDtypeStruct((n - n_out_split, D_MODEL), F32)]
    else:
        out_specs = whole
        out_shape = jax.ShapeDtypeStruct((n, D_MODEL), F32)
    return pl.pallas_call(
        kern,
        grid=(n // tm,),
        in_specs=([first, second] if two_in else [whole]) + [
            pl.BlockSpec((None, N_MOD, D_MODEL), lambda i: (mod_row_fn(i, tm), 0, 0)),
            pl.BlockSpec((3, D_MODEL), lambda i: (0, 0)),
            pl.BlockSpec((1, D_MODEL), lambda i: (0, 0)),
            _HBM,
            _HBM,
        ],
        out_specs=out_specs,
        out_shape=out_shape,
        scratch_shapes=[_bf16_copy(w13), _bf16_copy(w2)],
        compiler_params=_params(("arbitrary",)),
        name=name,
    )(*xs, mod3, norm_w, final_w, w13, w2)


def _rope(x, cos, sins, first_quarter):
    partner = jnp.where(first_quarter, pltpu.roll(x, 96, 1), pltpu.roll(x, 32, 1))
    return x * cos + partner * sins


def _proj_kernel(*refs, layer, rope):
    it = iter(refs)
    x_ref, mod_ref, nw_ref, lbl_ref, w_hbm = (next(it) for _ in range(5))
    cos_ref, sin_ref = (next(it), next(it)) if rope else (None, None)
    p_ref, g_ref = next(it), next(it)
    w_ref = next(it)
    _stage_weight(pl.program_id(0), w_hbm, w_ref, 16)
    h = _rms_modulate(x_ref[...], nw_ref[1:2, :], mod_ref[3:4, :], mod_ref[4:5, :]).astype(BF16)

    if rope:
        lane = lax.broadcasted_iota(jnp.int32, (x_ref.shape[0], RET_DK), 1)
        first_quarter = (lane & 32) == 0

    def rot(t):
        if not rope:
            return t
        heads = [_rope(t[:, j:j + RET_DK], cos_ref[...], sin_ref[...], first_quarter)
                 for j in range(0, t.shape[1], RET_DK)]
        return jnp.concatenate(heads, axis=1)

    def silu(t):
        return t * _sigmoid(t)

    def log_forget(d):
        def fn(t, c):
            lg = lbl_ref[d, :, c:c + PROJ_CHUNK]
            e = jnp.exp(lg - jnp.max(lg, axis=0, keepdims=True))
            lb = jnp.sum(e[0:layer + 1, :], axis=0, keepdims=True) / jnp.sum(e, axis=0, keepdims=True)
            return jnp.log(lb + (1.0 - lb) * _sigmoid(t)) * LOG2E
        return fn

    parts = (
        (C_RQ, 512, p_ref, P_RQ, lambda t, c: rot(t)),
        (C_RK, 512, p_ref, P_RK, lambda t, c: rot(t * (RET_DK ** -0.5))),
        (C_RV, 1024, p_ref, P_RV, lambda t, c: t),
        (C_RG, 1024, p_ref, P_RG, lambda t, c: silu(t)),
        (C_HQ, 1024, p_ref, P_HQ, lambda t, c: silu(t) * (HG_DK ** -0.5)),
        (C_HFF, 1024, g_ref, 0, log_forget(0)),
        (C_HFB, 1024, g_ref, 1024, log_forget(1)),
        (C_HI, 1024, p_ref, P_HI, lambda t, c: t),
        (C_HOG, 1024, p_ref, P_HOG, lambda t, c: silu(t)),
        (C_GR, 1024, p_ref, P_GR, lambda t, c: _sigmoid(t)),
        (C_GH, 1024, p_ref, P_GH, lambda t, c: _sigmoid(t)),
    )
    for c in range(0, 1024, PROJ_CHUNK):
        for w0, width, o_ref, o0, fn in parts:
            if c < width:
                t = _dot(h, w_ref[:, w0 + c:w0 + c + PROJ_CHUNK])
                o_ref[:, o0 + c:o0 + c + PROJ_CHUNK] = fn(t, c).astype(o_ref.dtype)


def _proj_call(x, mod3, norm_w, lb_logits, w_in, rope_tabs, rope_blk_fn, *, layer, mod_row_fn, tm, name):
    n = x.shape[0]
    slots = lb_logits.shape[1]
    row = lambda w: pl.BlockSpec((tm, w), lambda i: (i, 0))
    outs = [(P_WIDTH, BF16), (2 * HG_HEADS * HG_DK, F32)]
    in_specs = [
        row(D_MODEL),
        pl.BlockSpec((None, N_MOD, D_MODEL), lambda i: (mod_row_fn(i, tm), 0, 0)),
        pl.BlockSpec((3, D_MODEL), lambda i: (0, 0)),
        pl.BlockSpec((2, slots, 1024), lambda i: (0, 0, 0)),
        _HBM,
    ]
    args = [x, mod3, norm_w, lb_logits, w_in]
    if rope_tabs is not None:
        in_specs += [pl.BlockSpec((tm, RET_DK), lambda i: (rope_blk_fn(i, tm), 0))] * 2
        args += list(rope_tabs)
    return pl.pallas_call(
        functools.partial(_proj_kernel, layer=layer, rope=rope_tabs is not None),
        grid=(n // tm,),
        in_specs=in_specs,
        out_specs=[row(w) for w, _ in outs],
        out_shape=[jax.ShapeDtypeStruct((n, w), dt) for w, dt in outs],
        scratch_shapes=[_bf16_copy(w_in)],
        compiler_params=_params(("arbitrary",)),
        name=name,
    )(*args)


def _log_sigmoid(x):
    return -(jnp.maximum(-x, 0.0) + jnp.log1p(jnp.exp(-jnp.abs(x))))


def _half_acc_slices(i, n, blk, finalize):
    if finalize:
        starts = ((i - n // 2) * blk, (n - 1 - i) * blk)
    else:
        starts = (i * blk, (n // 2 - 1 - i) * blk)
    return tuple(pl.ds(pl.multiple_of(s, blk), blk) for s in starts)


def _ret_kernel(*refs, L, has_s0, out_state):
    it = iter(refs)
    dec_ref = next(it)
    q_ref, k_ref, v_ref, gate_ref = next(it), next(it), next(it), next(it)
    s0_ref = next(it) if has_s0 else None
    o_ref = next(it)
    st_ref = next(it) if out_state else None
    of_acc, ob_acc, st_scr, cst = next(it), next(it), next(it), next(it)

    C = SCAN_BLOCK
    P = RET_PACK
    n = L // C
    pair = pl.program_id(1)
    ii = lax.broadcasted_iota(jnp.int32, (C, C), 0).astype(F32)
    jj = lax.broadcasted_iota(jnp.int32, (C, C), 1).astype(F32)
    diff = ii - jj

    @pl.when(jnp.logical_and(pl.program_id(0) == 0, pair == 0))
    def _():
        for hd in range(RET_HEADS):
            lgf = _log_sigmoid(jnp.full((C, C), dec_ref[0, hd], F32))
            lgb = _log_sigmoid(jnp.full((C, C), dec_ref[1, hd], F32))
            cst[hd, 0] = (jnp.where(diff >= 0, jnp.exp(jnp.maximum(diff, 0.0) * lgf), 0.0)
                          + jnp.where(diff <= 0, jnp.exp(jnp.maximum(-diff, 0.0) * lgb), 0.0))
            cst[hd, 1] = jnp.exp((ii + 1.0) * lgf)
            cst[hd, 2] = jnp.exp((C - 1.0 - ii) * lgf)
            cst[hd, 3] = jnp.exp((C - ii) * lgb)
            cst[hd, 4] = jnp.exp(ii * lgb)

    cds = []
    for h in range(P):
        hd = pair * P + h
        cds.append([jnp.exp(C * _log_sigmoid(jnp.full((1, RET_DV), dec_ref[d, hd], F32))) for d in range(2)])
        for d in range(2):
            st_scr[2 * h + d] = s0_ref[d, h] if has_s0 else jnp.zeros((RET_DK, RET_DV), F32)

    def step(i, finalize):
        sls = (pl.ds(pl.multiple_of(i * C, C), C), pl.ds(pl.multiple_of((n - 1 - i) * C, C), C))
        acc_sls = _half_acc_slices(i, n, C, finalize)
        chains = []
        for h in range(P):
            ql = slice(h * RET_DK, (h + 1) * RET_DK)
            vl = slice(h * RET_DV, (h + 1) * RET_DV)
            for d in range(2):
                chains.append(dict(h=h, d=d, vl=vl, q=q_ref[sls[d], ql], k=k_ref[sls[d], ql],
                                   v=v_ref[sls[d], vl]))
        for c in chains:
            if c["d"] == 0:
                c["s"] = _dot_nt(c["q"], c["k"])
        for c in chains:
            h, d, v, vl = c["h"], c["d"], c["v"], c["vl"]
            hd = pair * P + h
            st = st_scr[2 * h + d]
            qd = (c["q"].astype(F32) * cst[hd, 1 + 2 * d]).astype(BF16)
            if d == 0:
                p = (c["s"] * cst[hd, 0]).astype(BF16)
                o = _dot(jnp.concatenate([p, qd], axis=1), jnp.concatenate([v, st.astype(BF16)], axis=0))
            else:
                o = _dot(qd, st.astype(BF16))
            kd = (c["k"].astype(F32) * cst[hd, 2 + 2 * d]).astype(BF16)
            st_scr[2 * h + d] = cds[h][d] * st + _dot_tn(kd, v)
            if finalize:
                o = o + (ob_acc, of_acc)[d][acc_sls[d], vl]
                ms = jnp.mean(o * o, axis=-1, keepdims=True)
                o_ref[sls[d], vl] = (o * lax.rsqrt(ms + EPS) * gate_ref[sls[d], vl]).astype(BF16)
            else:
                (of_acc, ob_acc)[d][acc_sls[d], vl] = o

    def first_half(i, carry):
        step(i, False)
        return carry

    def second_half(i, carry):
        step(i, True)
        return carry

    lax.fori_loop(0, n // 2, first_half, 0, unroll=min(RET_UNROLL, n // 2))
    lax.fori_loop(n // 2, n, second_half, 0, unroll=min(RET_UNROLL, n // 2))
    if out_state:
        for h in range(P):
            for d in range(2):
                st_ref[d, h] = st_scr[2 * h + d]


def _ret_call(dec, acts, s0, *, layer, row0, B, L, out_state, name):
    has_s0 = s0 is not None
    H = RET_HEADS
    P = RET_PACK
    r0 = row0 // L
    st_spec = pl.BlockSpec((None, 2, P, RET_DK, RET_DV), lambda b, h: (b, 0, h, 0, 0))
    cols = lambda width, off: pl.BlockSpec((L, width), lambda b, h: (r0 + b, off // width + h))
    in_specs = [
        pl.BlockSpec(memory_space=pltpu.SMEM),
        cols(P * RET_DK, P_RQ), cols(P * RET_DK, P_RK), cols(P * RET_DV, P_RV), cols(P * RET_DV, P_RG),
    ]
    args = [dec, acts, acts, acts, acts]
    if has_s0:
        in_specs.append(pl.BlockSpec((None, None, 2, P, RET_DK, RET_DV), lambda b, h: (b, layer, 0, h, 0, 0)))
        args.append(s0)
    out_specs = [pl.BlockSpec((L, P * RET_DV), lambda b, h: (b, h))]
    out_shape = [jax.ShapeDtypeStruct((B * L, H * RET_DV), BF16)]
    if out_state:
        out_specs.append(st_spec)
        out_shape.append(jax.ShapeDtypeStruct((B, 2, H, RET_DK, RET_DV), F32))
    res = pl.pallas_call(
        functools.partial(_ret_kernel, L=L, has_s0=has_s0, out_state=out_state),
        grid=(B, H // P),
        in_specs=in_specs,
        out_specs=out_specs,
        out_shape=out_shape,
        scratch_shapes=[pltpu.VMEM((L // 2, P * RET_DV), F32), pltpu.VMEM((L // 2, P * RET_DV), F32),
                        pltpu.VMEM((2 * P, RET_DK, RET_DV), F32),
                        pltpu.VMEM((H, 5, SCAN_BLOCK, SCAN_BLOCK), F32)],
        compiler_params=_params(("arbitrary", "arbitrary")),
        name=name,
    )(*args)
    return res


def _gla_blocks(chains, tri_ref, code_ref):
    T = HG_BLOCK
    K = HG_DK

    def cumulate(c):
        g2 = c["g"]
        c["k"] = 1.0 - jnp.exp2(g2)
        hi = g2.astype(BF16)
        lo = (g2 - hi.astype(F32)).astype(BF16)
        tri = tri_ref[int(c["rev"])]
        c["b"] = _dot(tri, hi) + _dot(tri, lo)

    def leaf(c):
        bl = c["b"].reshape(T // HG_LEAF, HG_LEAF, K)
        r = HG_LEAF // 2 if c["rev"] else HG_LEAF // 2 - 1
        d = jnp.clip(bl - bl[:, r:r + 1, :], -EXP2_CLAMP, EXP2_CLAMP).reshape(T, K)
        e = jnp.exp2(d)
        p = _dot_nt((c["q"] * e).astype(BF16), (c["k"] / e).astype(BF16))
        c["a"] = jnp.where(code_ref[int(c["rev"])] == 0, p, 0.0)

    def level(c, lv, m):
        seg = 2 * m
        ng = T // seg
        b3, q3, k3 = (c[n].reshape(ng, seg, K) for n in ("b", "q", "k"))
        if c["rev"]:
            qs, ks, r = slice(0, m), slice(m, seg), m
        else:
            qs, ks, r = slice(m, seg), slice(0, m), m - 1
        ref = b3[:, r:r + 1, :]
        qm = (q3[:, qs, :] * jnp.exp2(b3[:, qs, :] - ref)).reshape(ng * m, K).astype(BF16)
        kz = k3[:, ks, :] * jnp.exp2(ref - b3[:, ks, :])
        zeros = jnp.zeros((ng, m, K), F32)
        kfull = jnp.concatenate([zeros, kz] if c["rev"] else [kz, zeros], axis=1)
        p = _dot_nt(qm, kfull.reshape(T, K).astype(BF16)).reshape(ng, m, T)
        a3 = c["a"].reshape(ng, seg, T)
        c3 = code_ref[int(c["rev"])].reshape(ng, seg, T)
        upd = jnp.where(c3[:, qs, :] == lv, p, a3[:, qs, :])
        parts = [upd, a3[:, ks, :]] if c["rev"] else [a3[:, ks, :], upd]
        c["a"] = jnp.concatenate(parts, axis=1).reshape(T, T)

    def finish(c):
        b, q, k, v, st = c["b"], c["q"], c["k"], c["v"], c["st"]
        o = _dot(c["a"].astype(BF16), v) + _dot_nt((q * jnp.exp2(b)).astype(BF16), st.astype(BF16))
        end_row = 0 if c["rev"] else T - 1
        b1 = b.reshape(1, T, K)
        b_end = b1[:, end_row:end_row + 1, :]
        kd = (k * jnp.exp2(b_end - b1).reshape(T, K)).astype(BF16)
        c["out"] = (o, st * jnp.exp2(b_end.reshape(1, K)) + _dot_tn(v, kd))

    for c in chains:
        cumulate(c)
    for c in chains:
        leaf(c)
    for lv, m in HG_LEVELS:
        for c in chains:
            level(c, lv, m)
    for c in chains:
        finish(c)
    return [c["out"] for c in chains]


def _hgrn_kernel(*refs, L, P, has_s0, out_state):
    it = iter(refs)
    q_ref, g_f_ref, g_b_ref, v_ref, gate_ref, nw_ref = (next(it) for _ in range(6))
    s0_ref = next(it) if has_s0 else None
    o_ref = next(it)
    st_ref = next(it) if out_state else None
    of_acc, ob_acc, st_scr, tri_ref, code_ref = (next(it) for _ in range(5))

    T = HG_BLOCK
    n = L // T

    @pl.when(jnp.logical_and(pl.program_id(0) == 0, pl.program_id(1) == 0))
    def _():
        ti = lax.broadcasted_iota(jnp.int32, (T, T), 0)
        si = lax.broadcasted_iota(jnp.int32, (T, T), 1)
        x = ti ^ si
        tri_ref[0] = (si <= ti).astype(BF16)
        tri_ref[1] = (si >= ti).astype(BF16)
        lvl = jnp.where(x < HG_LEAF, 0, HG_LEVELS[0][0] - 1 + sum((x >= m).astype(jnp.int32) for _, m in HG_LEVELS))
        code_ref[0] = jnp.where(si <= ti, lvl, -1)
        code_ref[1] = jnp.where(si >= ti, lvl, -1)

    for h in range(P):
        for d in range(2):
            st_scr[2 * h + d] = s0_ref[d, h].T if has_s0 else jnp.zeros((HG_DV, HG_DK), F32)

    def step(i, finalize):
        sls = (pl.ds(pl.multiple_of(i * T, T), T), pl.ds(pl.multiple_of((n - 1 - i) * T, T), T))
        acc_sls = _half_acc_slices(i, n, T, finalize)
        chains = []
        for h in range(P):
            lanes = slice(h * HG_DK, (h + 1) * HG_DK)
            for d, g_ref in enumerate((g_f_ref, g_b_ref)):
                chains.append(dict(q=q_ref[sls[d], lanes].astype(F32), g=g_ref[sls[d], lanes],
                                   v=v_ref[sls[d], lanes],
                                   st=st_scr[2 * h + d], rev=bool(d)))
        res = _gla_blocks(chains, tri_ref, code_ref)
        for h in range(P):
            lanes = slice(h * HG_DK, (h + 1) * HG_DK)
            for d in range(2):
                o, st = res[2 * h + d]
                st_scr[2 * h + d] = st
                if finalize:
                    o = o + (ob_acc, of_acc)[d][acc_sls[d], lanes]
                    ms = jnp.mean(o * o, axis=-1, keepdims=True)
                    o = o * lax.rsqrt(ms + EPS) * nw_ref[:, lanes]
                    o_ref[sls[d], lanes] = (o * gate_ref[sls[d], lanes]).astype(BF16)
                else:
                    (of_acc, ob_acc)[d][acc_sls[d], lanes] = o

    def first_half(i, carry):
        step(i, False)
        return carry

    def second_half(i, carry):
        step(i, True)
        return carry

    lax.fori_loop(0, n // 2, first_half, 0, unroll=min(HG_UNROLL, n // 2))
    lax.fori_loop(n // 2, n, second_half, 0, unroll=min(HG_UNROLL, n // 2))
    if out_state:
        for h in range(P):
            for d in range(2):
                st_ref[d, h] = st_scr[2 * h + d].T


def _hgrn_call(acts, g, nw, s0, *, layer, row0, B, L, out_state, name):
    has_s0 = s0 is not None
    H = HG_HEADS
    P = HG_PACK_SHORT if L <= HG_SHORT_SEQ else HG_PACK_LONG
    W = P * HG_DK
    r0 = row0 // L
    blk = lambda off: pl.BlockSpec((L, W), lambda b, h: (r0 + b, off // W + h))
    st_spec = pl.BlockSpec((None, 2, P, HG_DK, HG_DV), lambda b, h: (b, 0, h, 0, 0))
    in_specs = [blk(P_HQ), blk(0), blk(H * HG_DK), blk(P_HI), blk(P_HOG),
                pl.BlockSpec((1, W), lambda b, h: (0, h))]
    args = [acts, g, g, acts, acts, nw]
    if has_s0:
        in_specs.append(pl.BlockSpec((None, None, 2, P, HG_DK, HG_DV), lambda b, h: (b, layer, 0, h, 0, 0)))
        args.append(s0)
    out_specs = [pl.BlockSpec((L, W), lambda b, h: (b, h))]
    out_shape = [jax.ShapeDtypeStruct((B * L, H * HG_DV), BF16)]
    if out_state:
        out_specs.append(st_spec)
        out_shape.append(jax.ShapeDtypeStruct((B, 2, H, HG_DK, HG_DV), F32))
    return pl.pallas_call(
        functools.partial(_hgrn_kernel, L=L, P=P, has_s0=has_s0, out_state=out_state),
        grid=(B, H // P),
        in_specs=in_specs,
        out_specs=out_specs,
        out_shape=out_shape,
        scratch_shapes=[pltpu.VMEM((L // 2, W), F32), pltpu.VMEM((L // 2, W), F32),
                        pltpu.VMEM((2 * P, HG_DV, HG_DK), F32),
                        pltpu.VMEM((2, HG_BLOCK, HG_BLOCK), BF16),
                        pltpu.VMEM((2, HG_BLOCK, HG_BLOCK), jnp.int32)],
        compiler_params=_params(("arbitrary", "arbitrary")),
        name=name,
    )(*args)


def _merge_kernel(x_ref, mod_ref, oret_a, oret_b, ohg_a, ohg_b, gr_ref, gh_ref, wr_hbm, wh_hbm, wo_hbm, o_ref,
                  wr_ref, wh_ref, wo_ref, *, n_first):
    i = pl.program_id(0)
    for w_hbm, w_ref in ((wr_hbm, wr_ref), (wh_hbm, wh_ref), (wo_hbm, wo_ref)):
        _stage_weight(i, w_hbm, w_ref, 256)
    oret = jnp.where(i < n_first, oret_a[...], oret_b[...])
    ohg = jnp.where(i < n_first, ohg_a[...], ohg_b[...])
    y = gr_ref[...] * _dot(oret, wr_ref[...]) + gh_ref[...] * _dot(ohg, wh_ref[...])
    out = _dot(y.astype(BF16), wo_ref[...])
    o_ref[...] = x_ref[...] + mod_ref[5:6, :] * out


def _merge_call(x, mod3, orets, ohgs, acts, wr, wh, wo, *, mod_row_fn, tm, name):
    n = x.shape[0]
    n_first = orets[0].shape[0] // tm
    row = pl.BlockSpec((tm, D_MODEL), lambda i: (i, 0))
    gate = lambda off: pl.BlockSpec((tm, D_MODEL), lambda i: (i, off // D_MODEL))
    first = pl.BlockSpec((tm, D_MODEL), lambda i: (jnp.minimum(i, n_first - 1), 0))
    second = pl.BlockSpec((tm, D_MODEL), lambda i: (jnp.maximum(i - n_first, 0), 0))
    return pl.pallas_call(
        functools.partial(_merge_kernel, n_first=n_first),
        grid=(n // tm,),
        in_specs=[row, pl.BlockSpec((None, N_MOD, D_MODEL), lambda i: (mod_row_fn(i, tm), 0, 0)),
                  first, second, first, second, gate(P_GR), gate(P_GH), _HBM, _HBM, _HBM],
        out_specs=row,
        out_shape=jax.ShapeDtypeStruct((n, D_MODEL), F32),
        scratch_shapes=[_bf16_copy(wr), _bf16_copy(wh), _bf16_copy(wo)],
        compiler_params=_params(("arbitrary",)),
        name=name,
    )(x, mod3, *orets, *ohgs, acts, acts, wr, wh, wo)


def _rope_tables(n_tokens, n_identity):
    pos = np.arange(n_tokens)
    r = (pos // GRID_W).astype(np.float32)
    cl = (pos % GRID_W).astype(np.float32)
    quarter = RET_DK // 4
    inv_freq = (np.float32(ROPE_BASE) ** (-np.arange(quarter, dtype=np.float32) / np.float32(quarter)))
    inv_freq = inv_freq.astype(np.float32)
    ang_r = (r[:, None] * inv_freq[None, :]).astype(np.float64)
    ang_c = (cl[:, None] * inv_freq[None, :]).astype(np.float64)
    cos = np.concatenate([np.cos(ang_r)] * 2 + [np.cos(ang_c)] * 2, axis=-1)
    sins = np.concatenate([-np.sin(ang_r), np.sin(ang_r), -np.sin(ang_c), np.sin(ang_c)], axis=-1)
    cos = np.concatenate([cos, np.ones((n_identity, RET_DK))], axis=0).astype(np.float32)
    sins = np.concatenate([sins, np.zeros((n_identity, RET_DK))], axis=0).astype(np.float32)
    return jnp.asarray(cos), jnp.asarray(sins)


def kernel(x_prompt, x_sample, state_ret, state_hgrn, c, c_ctx, ada_w, ada_b, norm_w, ffn1_w13, ffn1_w2,
           ffn2_w13, ffn2_w2, w_in, ret_decay, hg_lb_logits, hg_norm_w, w_ret_proj, w_hg_proj, w_o,
           final_norm_w):
    depth = ada_w.shape[0]
    assert depth == 1
    l = 0
    bp, lp, _ = x_prompt.shape
    bs, ls, _ = x_sample.shape
    n_ctx = bp * lp
    assert n_ctx % ls == 0 and ls % TM_FFN == 0 and ls % TM_PROJ == 0 and ls % TM_MERGE == 0

    cvec = jnp.concatenate([c_ctx[None, :], c, jnp.zeros((8 - 1 - bs, D_MODEL), F32)], axis=0)
    mod = _mod_call(cvec, ada_w[l], ada_b[l][None, :])
    mod3 = mod.reshape(8, N_MOD, D_MODEL)

    def mod_row(i, tm):
        return jnp.where(i < n_ctx // tm, 0, 1 + (i * tm - n_ctx) // ls)

    def rope_blk(i, tm):
        return jnp.where(i < n_ctx // tm, ls // tm, (i - n_ctx // tm) % (ls // tm))

    nw, fw = norm_w[l], final_norm_w[None, :]
    dec, hg_nw = ret_decay[l], hg_norm_w[l][None, :]

    x = _ffn_call((x_prompt.reshape(n_ctx, D_MODEL), x_sample.reshape(bs * ls, D_MODEL)), mod3, nw, fw,
                  ffn1_w13[l], ffn1_w2[l], n_out_split=None, mod_base=0, norm_row=0, final=False,
                  mod_row_fn=mod_row, tm=TM_FFN, name="ffn1")
    acts, g = _proj_call(x, mod3, nw, hg_lb_logits, w_in[l], _rope_tables(ls, TM_PROJ), rope_blk, layer=l,
                         mod_row_fn=mod_row, tm=TM_PROJ, name="proj")
    ret_ctx, s_ret = _ret_call(dec, acts, None, layer=l, row0=0, B=bp, L=lp, out_state=True, name="ret_ctx")
    ret_lat, = _ret_call(dec, acts, state_ret, layer=l, row0=n_ctx, B=bs, L=ls, out_state=False,
                         name="ret_lat")
    hg_ctx, s_hg = _hgrn_call(acts, g, hg_nw, None, layer=l, row0=0, B=bp, L=lp, out_state=True,
                              name="hgrn_ctx")
    hg_lat, = _hgrn_call(acts, g, hg_nw, state_hgrn, layer=l, row0=n_ctx, B=bs, L=ls, out_state=False,
                         name="hgrn_lat")
    x = _merge_call(x, mod3, (ret_ctx, ret_lat), (hg_ctx, hg_lat), acts, w_ret_proj[l], w_hg_proj[l], w_o[l],
                    mod_row_fn=mod_row, tm=TM_MERGE, name="merge")
    yp, ys = _ffn_call((x,), mod3, nw, fw, ffn2_w13[l], ffn2_w2[l], n_out_split=n_ctx, mod_base=6,
                       norm_row=2, final=True, mod_row_fn=mod_row, tm=TM_FFN, name="ffn2")
    return (yp.reshape(bp, lp, D_MODEL), ys.reshape(bs, ls, D_MODEL), s_ret[:, None], s_hg[:, None])
```

```python
import functools

import numpy as np
import jax
import jax.numpy as jnp
from jax import lax
from jax.experimental import pallas as pl
from jax.experimental.pallas import tpu as pltpu

F32 = jnp.float32
BF16 = jnp.bfloat16

D_MODEL = 1024
GRID_W = 64
RET_HEADS = 4
RET_DK = 128
RET_DV = 256
HG_HEADS = 8
HG_DK = 128
HG_DV = 128
D_FF = 2816
N_MOD = 9
ROPE_BASE = 10000.0
EPS = 1e-6

C_RQ, C_RK, C_RV, C_RG, C_HQ, C_HFF, C_HFB, C_HI, C_HOG, C_GR, C_GH = (
    0, 512, 1024, 2048, 3072, 4096, 5120, 6144, 7168, 8192, 9216)
IN_WIDTH = 10240
P_RQ, P_RK, P_RV, P_RG, P_HQ, P_HI, P_HOG, P_GR, P_GH = 0, 512, 1024, 2048, 3072, 4096, 5120, 6144, 7168
P_WIDTH = 8192

VMEM_LIMIT_BYTES = 56 * 1024 * 1024

TM_FFN = 512
TM_PROJ = 512
TM_MERGE = 512
STAGE_SLOTS = 4
STAGE_ROWS_W13, STAGE_ROWS_W2, STAGE_ROWS_W_IN, STAGE_ROWS_MERGE = 128, 256, 16, 256
MOD_TILE = 2304
SCAN_BLOCK = 128
RET_PACK = 4
RET_UNROLL = 2
HG_PACK_LONG = 4
HG_PACK_SHORT = 8
HG_SHORT_SEQ = 512
HG_UNROLL = 2
FF_CHUNK = 256
PROJ_CHUNK = 256
LOG2E = 1.4426950408889634
EXP2_CLAMP = 115.0
HG_BLOCK = 128
HG_LEAF = 16
HG_LEVELS = ((4, 16), (5, 32), (6, 64))


def _dot(a, b):
    return jnp.dot(a, b, preferred_element_type=F32)


def _dot_nt(a, b):
    return lax.dot_general(a, b, (((1,), (1,)), ((), ())), preferred_element_type=F32)


def _dot_tn(a, b):
    return lax.dot_general(a, b, (((0,), (0,)), ((), ())), preferred_element_type=F32)


def _sigmoid(x):
    return 0.5 * jnp.tanh(0.5 * x) + 0.5


def _rms(x, w):
    ms = jnp.mean(x * x, axis=-1, keepdims=True)
    return x * lax.rsqrt(ms + EPS) * w


def _rms_modulate(x, w, shift, scale):
    ms = jnp.mean(x * x, axis=-1, keepdims=True)
    return x * lax.rsqrt(ms + EPS) * (w * (1.0 + scale)) + shift


def _params(sem):
    return pltpu.CompilerParams(dimension_semantics=sem, vmem_limit_bytes=VMEM_LIMIT_BYTES)


def _stage_weight(step, w_hbm, w_vmem, rows):
    r, c = w_hbm.shape
    assert r % rows == 0 and rows % 16 == 0
    n_chunks = r // rows
    ahead = STAGE_SLOTS - 1

    def run(stage, sem):
        def copy(k):
            slot = k % STAGE_SLOTS
            return pltpu.make_async_copy(w_hbm.at[pl.ds(k * rows, rows), :], stage.at[slot], sem.at[slot])

        for k in range(min(ahead, n_chunks)):
            copy(k).start()

        def body(k, carry):
            @pl.when(k + ahead < n_chunks)
            def _():
                copy(k + ahead).start()

            copy(k).wait()
            w_vmem[pl.ds(pl.multiple_of(k * rows, rows), rows), :] = stage[k % STAGE_SLOTS].astype(BF16)
            return carry

        lax.fori_loop(0, n_chunks, body, 0)

    @pl.when(step == 0)
    def _():
        pl.run_scoped(run, pltpu.VMEM((STAGE_SLOTS, rows, c), F32), pltpu.SemaphoreType.DMA((STAGE_SLOTS,)))


_HBM = pl.BlockSpec(memory_space=pl.ANY)


def _bf16_copy(w):
    return pltpu.VMEM(w.shape, BF16)


def _mod_kernel(c_ref, w_ref, b_ref, o_ref):
    c = c_ref[...]
    s = (c * _sigmoid(c)).astype(BF16)
    o_ref[...] = _dot(s, w_ref[...].astype(BF16)) + b_ref[...]


def _mod_call(cvec, ada_w, ada_b):
    n = ada_w.shape[1]
    tn = MOD_TILE
    return pl.pallas_call(
        _mod_kernel,
        grid=(n // tn,),
        in_specs=[
            pl.BlockSpec((8, D_MODEL), lambda j: (0, 0)),
            pl.BlockSpec((D_MODEL, tn), lambda j: (0, j)),
            pl.BlockSpec((1, tn), lambda j: (0, j)),
        ],
        out_specs=pl.BlockSpec((8, tn), lambda j: (0, j)),
        out_shape=jax.ShapeDtypeStruct((8, n), F32),
        compiler_params=_params(("arbitrary",)),
        name="mod",
    )(cvec, ada_w, ada_b)


def _ffn_kernel(*refs, mod_base, norm_row, final, n_first, two_in, two_out):
    it = iter(refs)
    xa_ref = next(it)
    xb_ref = next(it) if two_in else None
    mod_ref, nw_ref, fw_ref, w13_hbm, w2_hbm = (next(it) for _ in range(5))
    oa_ref = next(it)
    ob_ref = next(it) if two_out else None
    w13_ref, w2_ref = next(it), next(it)
    i = pl.program_id(0)
    _stage_weight(i, w13_hbm, w13_ref, STAGE_ROWS_W13)
    _stage_weight(i, w2_hbm, w2_ref, STAGE_ROWS_W2)
    x = xa_ref[...]
    if two_in:
        x = jnp.where(i < n_first, x, xb_ref[...])
    h = _rms_modulate(x, nw_ref[norm_row:norm_row + 1, :], mod_ref[mod_base:mod_base + 1, :],
                      mod_ref[mod_base + 1:mod_base + 2, :]).astype(BF16)
    acc = None
    for c0 in range(0, D_FF, FF_CHUNK):
        a = _dot(h, w13_ref[:, c0:c0 + FF_CHUNK])
        b = _dot(h, w13_ref[:, D_FF + c0:D_FF + c0 + FF_CHUNK])
        act = (a * _sigmoid(a) * b).astype(BF16)
        part = _dot(act, w2_ref[c0:c0 + FF_CHUNK, :])
        acc = part if acc is None else acc + part
    xo = x + (0.5 * mod_ref[mod_base + 2:mod_base + 3, :]) * acc
    if final:
        xo = _rms(xo, fw_ref[...])
    if two_out:
        @pl.when(i < n_first)
        def _():
            oa_ref[...] = xo

        @pl.when(i >= n_first)
        def _():
            ob_ref[...] = xo
    else:
        oa_ref[...] = xo


def _ffn_call(xs, mod3, norm_w, final_w, w13, w2, *, n_out_split, mod_base, norm_row, final, mod_row_fn, tm,
              name):
    two_in = len(xs) == 2
    two_out = n_out_split is not None
    n = sum(x.shape[0] for x in xs)
    n_first = (xs[0].shape[0] if two_in else n_out_split if two_out else n) // tm
    first = pl.BlockSpec((tm, D_MODEL), lambda i: (jnp.minimum(i, n_first - 1), 0))
    second = pl.BlockSpec((tm, D_MODEL), lambda i: (jnp.maximum(i - n_first, 0), 0))
    whole = pl.BlockSpec((tm, D_MODEL), lambda i: (i, 0))
    kern = functools.partial(_ffn_kernel, mod_base=mod_base, norm_row=norm_row, final=final, n_first=n_first,
                             two_in=two_in, two_out=two_out)
    if two_out:
        out_specs = [first, second]
        out_shape = [jax.ShapeDtypeStruct((n_out_split, D_MODEL), F32),
                     jax.ShapeDtypeStruct((n - n_out_split, D_MODEL), F32)]
    else:
        out_specs = whole
        out_shape = jax.ShapeDtypeStruct((n, D_MODEL), F32)
    return pl.pallas_call(
        kern,
        grid=(n // tm,),
        in_specs=([first, second] if two_in else [whole]) + [
            pl.BlockSpec((None, N_MOD, D_MODEL), lambda i: (mod_row_fn(i, tm), 0, 0)),
            pl.BlockSpec((3, D_MODEL), lambda i: (0, 0)),
            pl.BlockSpec((1, D_MODEL), lambda i: (0, 0)),
            _HBM,
            _HBM,
        ],
        out_specs=out_specs,
        out_shape=out_shape,
        scratch_shapes=[_bf16_copy(w13), _bf16_copy(w2)],
        compiler_params=_params(("arbitrary",)),
        name=name,
    )(*xs, mod3, norm_w, final_w, w13, w2)


def _rope(x, cos, sins, first_quarter):
    partner = jnp.where(first_quarter, pltpu.roll(x, 96, 1), pltpu.roll(x, 32, 1))
    return x * cos + partner * sins


def _proj_kernel(*refs, layer, rope):
    it = iter(refs)
    x_ref, mod_ref, nw_ref, lbl_ref, w_hbm = (next(it) for _ in range(5))
    cos_ref, sin_ref = (next(it), next(it)) if rope else (None, None)
    p_ref, g_ref = next(it), next(it)
    w_ref = next(it)
    _stage_weight(pl.program_id(0), w_hbm, w_ref, STAGE_ROWS_W_IN)
    h = _rms_modulate(x_ref[...], nw_ref[1:2, :], mod_ref[3:4, :], mod_ref[4:5, :]).astype(BF16)

    if rope:
        lane = lax.broadcasted_iota(jnp.int32, (x_ref.shape[0], RET_DK), 1)
        first_quarter = (lane & 32) == 0

    def rot(t):
        if not rope:
            return t
        heads = [_rope(t[:, j:j + RET_DK], cos_ref[...], sin_ref[...], first_quarter)
                 for j in range(0, t.shape[1], RET_DK)]
        return jnp.concatenate(heads, axis=1)

    def silu(t):
        return t * _sigmoid(t)

    def log_forget(d):
        def fn(t, c):
            lg = lbl_ref[d, :, c:c + PROJ_CHUNK]
            e = jnp.exp(lg - jnp.max(lg, axis=0, keepdims=True))
            lb = jnp.sum(e[0:layer + 1, :], axis=0, keepdims=True) / jnp.sum(e, axis=0, keepdims=True)
            return jnp.log(lb + (1.0 - lb) * _sigmoid(t)) * LOG2E
        return fn

    parts = (
        (C_RQ, 512, p_ref, P_RQ, lambda t, c: rot(t)),
        (C_RK, 512, p_ref, P_RK, lambda t, c: rot(t * (RET_DK ** -0.5))),
        (C_RV, 1024, p_ref, P_RV, lambda t, c: t),
        (C_RG, 1024, p_ref, P_RG, lambda t, c: silu(t)),
        (C_HQ, 1024, p_ref, P_HQ, lambda t, c: silu(t) * (HG_DK ** -0.5)),
        (C_HFF, 1024, g_ref, 0, log_forget(0)),
        (C_HFB, 1024, g_ref, 1024, log_forget(1)),
        (C_HI, 1024, p_ref, P_HI, lambda t, c: t),
        (C_HOG, 1024, p_ref, P_HOG, lambda t, c: silu(t)),
        (C_GR, 1024, p_ref, P_GR, lambda t, c: _sigmoid(t)),
        (C_GH, 1024, p_ref, P_GH, lambda t, c: _sigmoid(t)),
    )
    for c in range(0, 1024, PROJ_CHUNK):
        for w0, width, o_ref, o0, fn in parts:
            if c < width:
                t = _dot(h, w_ref[:, w0 + c:w0 + c + PROJ_CHUNK])
                o_ref[:, o0 + c:o0 + c + PROJ_CHUNK] = fn(t, c).astype(o_ref.dtype)


def _proj_call(x, mod3, norm_w, lb_logits, w_in, rope_tabs, rope_blk_fn, *, layer, mod_row_fn, tm, name):
    n = x.shape[0]
    slots = lb_logits.shape[1]
    row = lambda w: pl.BlockSpec((tm, w), lambda i: (i, 0))
    outs = [(P_WIDTH, BF16), (2 * HG_HEADS * HG_DK, F32)]
    in_specs = [
        row(D_MODEL),
        pl.BlockSpec((None, N_MOD, D_MODEL), lambda i: (mod_row_fn(i, tm), 0, 0)),
        pl.BlockSpec((3, D_MODEL), lambda i: (0, 0)),
        pl.BlockSpec((2, slots, 1024), lambda i: (0, 0, 0)),
        _HBM,
    ]
    args = [x, mod3, norm_w, lb_logits, w_in]
    if rope_tabs is not None:
        in_specs += [pl.BlockSpec((tm, RET_DK), lambda i: (rope_blk_fn(i, tm), 0))] * 2
        args += list(rope_tabs)
    return pl.pallas_call(
        functools.partial(_proj_kernel, layer=layer, rope=rope_tabs is not None),
        grid=(n // tm,),
        in_specs=in_specs,
        out_specs=[row(w) for w, _ in outs],
        out_shape=[jax.ShapeDtypeStruct((n, w), dt) for w, dt in outs],
        scratch_shapes=[_bf16_copy(w_in)],
        compiler_params=_params(("arbitrary",)),
        name=name,
    )(*args)


def _log_sigmoid(x):
    return -(jnp.maximum(-x, 0.0) + jnp.log1p(jnp.exp(-jnp.abs(x))))


def _half_acc_slices(i, n, blk, finalize):
    if finalize:
        starts = ((i - n // 2) * blk, (n - 1 - i) * blk)
    else:
        starts = (i * blk, (n // 2 - 1 - i) * blk)
    return tuple(pl.ds(pl.multiple_of(s, blk), blk) for s in starts)


def _ret_kernel(*refs, L, has_s0, out_state):
    it = iter(refs)
    dec_ref = next(it)
    q_ref, k_ref, v_ref, gate_ref = next(it), next(it), next(it), next(it)
    s0_ref = next(it) if has_s0 else None
    o_ref = next(it)
    st_ref = next(it) if out_state else None
    of_acc, ob_acc, st_scr, cst = next(it), next(it), next(it), next(it)

    C = SCAN_BLOCK
    P = RET_PACK
    n = L // C
    pair = pl.program_id(1)
    ii = lax.broadcasted_iota(jnp.int32, (C, C), 0).astype(F32)
    jj = lax.broadcasted_iota(jnp.int32, (C, C), 1).astype(F32)
    diff = ii - jj

    @pl.when(jnp.logical_and(pl.program_id(0) == 0, pair == 0))
    def _():
        for hd in range(RET_HEADS):
            lgf = _log_sigmoid(jnp.full((C, C), dec_ref[0, hd], F32))
            lgb = _log_sigmoid(jnp.full((C, C), dec_ref[1, hd], F32))
            cst[hd, 0] = (jnp.where(diff >= 0, jnp.exp(jnp.maximum(diff, 0.0) * lgf), 0.0)
                          + jnp.where(diff <= 0, jnp.exp(jnp.maximum(-diff, 0.0) * lgb), 0.0))
            cst[hd, 1] = jnp.exp((ii + 1.0) * lgf)
            cst[hd, 2] = jnp.exp((C - 1.0 - ii) * lgf)
            cst[hd, 3] = jnp.exp((C - ii) * lgb)
            cst[hd, 4] = jnp.exp(ii * lgb)

    cds = []
    for h in range(P):
        hd = pair * P + h
        cds.append([jnp.exp(C * _log_sigmoid(jnp.full((1, RET_DV), dec_ref[d, hd], F32))) for d in range(2)])
        for d in range(2):
            st_scr[2 * h + d] = s0_ref[d, h] if has_s0 else jnp.zeros((RET_DK, RET_DV), F32)

    def step(i, finalize):
        sls = (pl.ds(pl.multiple_of(i * C, C), C), pl.ds(pl.multiple_of((n - 1 - i) * C, C), C))
        acc_sls = _half_acc_slices(i, n, C, finalize)
        chains = []
        for h in range(P):
            ql = slice(h * RET_DK, (h + 1) * RET_DK)
            vl = slice(h * RET_DV, (h + 1) * RET_DV)
            for d in range(2):
                chains.append(dict(h=h, d=d, vl=vl, q=q_ref[sls[d], ql], k=k_ref[sls[d], ql],
                                   v=v_ref[sls[d], vl]))
        for c in chains:
            if c["d"] == 0:
                c["s"] = _dot_nt(c["q"], c["k"])
        for c in chains:
            h, d, v, vl = c["h"], c["d"], c["v"], c["vl"]
            hd = pair * P + h
            st = st_scr[2 * h + d]
            qd = (c["q"].astype(F32) * cst[hd, 1 + 2 * d]).astype(BF16)
            if d == 0:
                p = (c["s"] * cst[hd, 0]).astype(BF16)
                o = _dot(jnp.concatenate([p, qd], axis=1), jnp.concatenate([v, st.astype(BF16)], axis=0))
            else:
                o = _dot(qd, st.astype(BF16))
            kd = (c["k"].astype(F32) * cst[hd, 2 + 2 * d]).astype(BF16)
            st_scr[2 * h + d] = cds[h][d] * st + _dot_tn(kd, v)
            if finalize:
                o = o + (ob_acc, of_acc)[d][acc_sls[d], vl]
                ms = jnp.mean(o * o, axis=-1, keepdims=True)
                o_ref[sls[d], vl] = (o * lax.rsqrt(ms + EPS) * gate_ref[sls[d], vl]).astype(BF16)
            else:
                (of_acc, ob_acc)[d][acc_sls[d], vl] = o

    def first_half(i, carry):
        step(i, False)
        return carry

    def second_half(i, carry):
        step(i, True)
        return carry

    lax.fori_loop(0, n // 2, first_half, 0, unroll=min(RET_UNROLL, n // 2))
    lax.fori_loop(n // 2, n, second_half, 0, unroll=min(RET_UNROLL, n // 2))
    if out_state:
        for h in range(P):
            for d in range(2):
                st_ref[d, h] = st_scr[2 * h + d]


def _ret_call(dec, acts, s0, *, layer, row0, B, L, out_state, name):
    has_s0 = s0 is not None
    H = RET_HEADS
    P = RET_PACK
    r0 = row0 // L
    st_spec = pl.BlockSpec((None, 2, P, RET_DK, RET_DV), lambda b, h: (b, 0, h, 0, 0))
    cols = lambda width, off: pl.BlockSpec((L, width), lambda b, h: (r0 + b, off // width + h))
    in_specs = [
        pl.BlockSpec(memory_space=pltpu.SMEM),
        cols(P * RET_DK, P_RQ), cols(P * RET_DK, P_RK), cols(P * RET_DV, P_RV), cols(P * RET_DV, P_RG),
    ]
    args = [dec, acts, acts, acts, acts]
    if has_s0:
        in_specs.append(pl.BlockSpec((None, None, 2, P, RET_DK, RET_DV), lambda b, h: (b, layer, 0, h, 0, 0)))
        args.append(s0)
    out_specs = [pl.BlockSpec((L, P * RET_DV), lambda b, h: (b, h))]
    out_shape = [jax.ShapeDtypeStruct((B * L, H * RET_DV), BF16)]
    if out_state:
        out_specs.append(st_spec)
        out_shape.append(jax.ShapeDtypeStruct((B, 2, H, RET_DK, RET_DV), F32))
    res = pl.pallas_call(
        functools.partial(_ret_kernel, L=L, has_s0=has_s0, out_state=out_state),
        grid=(B, H // P),
        in_specs=in_specs,
        out_specs=out_specs,
        out_shape=out_shape,
        scratch_shapes=[pltpu.VMEM((L // 2, P * RET_DV), F32), pltpu.VMEM((L // 2, P * RET_DV), F32),
                        pltpu.VMEM((2 * P, RET_DK, RET_DV), F32),
                        pltpu.VMEM((H, 5, SCAN_BLOCK, SCAN_BLOCK), F32)],
        compiler_params=_params(("arbitrary", "arbitrary")),
        name=name,
    )(*args)
    return res


def _gla_blocks(chains, tri_ref, code_ref):
    T = HG_BLOCK
    K = HG_DK

    def cumulate(c):
        g2 = c["g"]
        c["k"] = 1.0 - jnp.exp2(g2)
        hi = g2.astype(BF16)
        lo = (g2 - hi.astype(F32)).astype(BF16)
        tri = tri_ref[int(c["rev"])]
        c["b"] = _dot(tri, hi) + _dot(tri, lo)

    def leaf(c):
        bl = c["b"].reshape(T // HG_LEAF, HG_LEAF, K)
        r = HG_LEAF // 2 if c["rev"] else HG_LEAF // 2 - 1
        d = jnp.clip(bl - bl[:, r:r + 1, :], -EXP2_CLAMP, EXP2_CLAMP).reshape(T, K)
        e = jnp.exp2(d)
        p = _dot_nt((c["q"] * e).astype(BF16), (c["k"] / e).astype(BF16))
        c["a"] = jnp.where(code_ref[int(c["rev"])] == 0, p, 0.0)

    def level(c, lv, m):
        seg = 2 * m
        ng = T // seg
        b3, q3, k3 = (c[n].reshape(ng, seg, K) for n in ("b", "q", "k"))
        if c["rev"]:
            qs, ks, r = slice(0, m), slice(m, seg), m
        else:
            qs, ks, r = slice(m, seg), slice(0, m), m - 1
        ref = b3[:, r:r + 1, :]
        qm = (q3[:, qs, :] * jnp.exp2(b3[:, qs, :] - ref)).reshape(ng * m, K).astype(BF16)
        kz = k3[:, ks, :] * jnp.exp2(ref - b3[:, ks, :])
        zeros = jnp.zeros((ng, m, K), F32)
        kfull = jnp.concatenate([zeros, kz] if c["rev"] else [kz, zeros], axis=1)
        p = _dot_nt(qm, kfull.reshape(T, K).astype(BF16)).reshape(ng, m, T)
        a3 = c["a"].reshape(ng, seg, T)
        c3 = code_ref[int(c["rev"])].reshape(ng, seg, T)
        upd = jnp.where(c3[:, qs, :] == lv, p, a3[:, qs, :])
        parts = [upd, a3[:, ks, :]] if c["rev"] else [a3[:, ks, :], upd]
        c["a"] = jnp.concatenate(parts, axis=1).reshape(T, T)

    def finish(c):
        b, q, k, v, st = c["b"], c["q"], c["k"], c["v"], c["st"]
        o = _dot(c["a"].astype(BF16), v) + _dot_nt((q * jnp.exp2(b)).astype(BF16), st.astype(BF16))
        end_row = 0 if c["rev"] else T - 1
        b1 = b.reshape(1, T, K)
        b_end = b1[:, end_row:end_row + 1, :]
        kd = (k * jnp.exp2(b_end - b1).reshape(T, K)).astype(BF16)
        c["out"] = (o, st * jnp.exp2(b_end.reshape(1, K)) + _dot_tn(v, kd))

    for c in chains:
        cumulate(c)
    for c in chains:
        leaf(c)
    for lv, m in HG_LEVELS:
        for c in chains:
            level(c, lv, m)
    for c in chains:
        finish(c)
    return [c["out"] for c in chains]


def _hgrn_kernel(*refs, L, P, has_s0, out_state):
    it = iter(refs)
    q_ref, g_f_ref, g_b_ref, v_ref, gate_ref, nw_ref = (next(it) for _ in range(6))
    s0_ref = next(it) if has_s0 else None
    o_ref = next(it)
    st_ref = next(it) if out_state else None
    of_acc, ob_acc, st_scr, tri_ref, code_ref = (next(it) for _ in range(5))

    T = HG_BLOCK
    n = L // T

    @pl.when(jnp.logical_and(pl.program_id(0) == 0, pl.program_id(1) == 0))
    def _():
        ti = lax.broadcasted_iota(jnp.int32, (T, T), 0)
        si = lax.broadcasted_iota(jnp.int32, (T, T), 1)
        x = ti ^ si
        tri_ref[0] = (si <= ti).astype(BF16)
        tri_ref[1] = (si >= ti).astype(BF16)
        lvl = jnp.where(x < HG_LEAF, 0, HG_LEVELS[0][0] - 1 + sum((x >= m).astype(jnp.int32) for _, m in HG_LEVELS))
        code_ref[0] = jnp.where(si <= ti, lvl, -1)
        code_ref[1] = jnp.where(si >= ti, lvl, -1)

    for h in range(P):
        for d in range(2):
            st_scr[2 * h + d] = s0_ref[d, h].T if has_s0 else jnp.zeros((HG_DV, HG_DK), F32)

    def step(i, finalize):
        sls = (pl.ds(pl.multiple_of(i * T, T), T), pl.ds(pl.multiple_of((n - 1 - i) * T, T), T))
        acc_sls = _half_acc_slices(i, n, T, finalize)
        chains = []
        for h in range(P):
            lanes = slice(h * HG_DK, (h + 1) * HG_DK)
            for d, g_ref in enumerate((g_f_ref, g_b_ref)):
                chains.append(dict(q=q_ref[sls[d], lanes].astype(F32), g=g_ref[sls[d], lanes],
                                   v=v_ref[sls[d], lanes],
                                   st=st_scr[2 * h + d], rev=bool(d)))
        res = _gla_blocks(chains, tri_ref, code_ref)
        for h in range(P):
            lanes = slice(h * HG_DK, (h + 1) * HG_DK)
            for d in range(2):
                o, st = res[2 * h + d]
                st_scr[2 * h + d] = st
                if finalize:
                    o = o + (ob_acc, of_acc)[d][acc_sls[d], lanes]
                    ms = jnp.mean(o * o, axis=-1, keepdims=True)
                    o = o * lax.rsqrt(ms + EPS) * nw_ref[:, lanes]
                    o_ref[sls[d], lanes] = (o * gate_ref[sls[d], lanes]).astype(BF16)
                else:
                    (of_acc, ob_acc)[d][acc_sls[d], lanes] = o

    def first_half(i, carry):
        step(i, False)
        return carry

    def second_half(i, carry):
        step(i, True)
        return carry

    lax.fori_loop(0, n // 2, first_half, 0, unroll=min(HG_UNROLL, n // 2))
    lax.fori_loop(n // 2, n, second_half, 0, unroll=min(HG_UNROLL, n // 2))
    if out_state:
        for h in range(P):
            for d in range(2):
                st_ref[d, h] = st_scr[2 * h + d].T


def _hgrn_call(acts, g, nw, s0, *, layer, row0, B, L, out_state, name):
    has_s0 = s0 is not None
    H = HG_HEADS
    P = HG_PACK_SHORT if L <= HG_SHORT_SEQ else HG_PACK_LONG
    W = P * HG_DK
    r0 = row0 // L
    blk = lambda off: pl.BlockSpec((L, W), lambda b, h: (r0 + b, off // W + h))
    st_spec = pl.BlockSpec((None, 2, P, HG_DK, HG_DV), lambda b, h: (b, 0, h, 0, 0))
    in_specs = [blk(P_HQ), blk(0), blk(H * HG_DK), blk(P_HI), blk(P_HOG),
                pl.BlockSpec((1, W), lambda b, h: (0, h))]
    args = [acts, g, g, acts, acts, nw]
    if has_s0:
        in_specs.append(pl.BlockSpec((None, None, 2, P, HG_DK, HG_DV), lambda b, h: (b, layer, 0, h, 0, 0)))
        args.append(s0)
    out_specs = [pl.BlockSpec((L, W), lambda b, h: (b, h))]
    out_shape = [jax.ShapeDtypeStruct((B * L, H * HG_DV), BF16)]
    if out_state:
        out_specs.append(st_spec)
        out_shape.append(jax.ShapeDtypeStruct((B, 2, H, HG_DK, HG_DV), F32))
    return pl.pallas_call(
        functools.partial(_hgrn_kernel, L=L, P=P, has_s0=has_s0, out_state=out_state),
        grid=(B, H // P),
        in_specs=in_specs,
        out_specs=out_specs,
        out_shape=out_shape,
        scratch_shapes=[pltpu.VMEM((L // 2, W), F32), pltpu.VMEM((L // 2, W), F32),
                        pltpu.VMEM((2 * P, HG_DV, HG_DK), F32),
                        pltpu.VMEM((2, HG_BLOCK, HG_BLOCK), BF16),
                        pltpu.VMEM((2, HG_BLOCK, HG_BLOCK), jnp.int32)],
        compiler_params=_params(("arbitrary", "arbitrary")),
        name=name,
    )(*args)


def _merge_kernel(x_ref, mod_ref, oret_a, oret_b, ohg_a, ohg_b, gr_ref, gh_ref, wr_hbm, wh_hbm, wo_hbm, o_ref,
                  wr_ref, wh_ref, wo_ref, *, n_first):
    i = pl.program_id(0)
    for w_hbm, w_ref in ((wr_hbm, wr_ref), (wh_hbm, wh_ref), (wo_hbm, wo_ref)):
        _stage_weight(i, w_hbm, w_ref, STAGE_ROWS_MERGE)
    oret = jnp.where(i < n_first, oret_a[...], oret_b[...])
    ohg = jnp.where(i < n_first, ohg_a[...], ohg_b[...])
    y = gr_ref[...] * _dot(oret, wr_ref[...]) + gh_ref[...] * _dot(ohg, wh_ref[...])
    out = _dot(y.astype(BF16), wo_ref[...])
    o_ref[...] = x_ref[...] + mod_ref[5:6, :] * out


def _merge_call(x, mod3, orets, ohgs, acts, wr, wh, wo, *, mod_row_fn, tm, name):
    n = x.shape[0]
    n_first = orets[0].shape[0] // tm
    row = pl.BlockSpec((tm, D_MODEL), lambda i: (i, 0))
    gate = lambda off: pl.BlockSpec((tm, D_MODEL), lambda i: (i, off // D_MODEL))
    first = pl.BlockSpec((tm, D_MODEL), lambda i: (jnp.minimum(i, n_first - 1), 0))
    second = pl.BlockSpec((tm, D_MODEL), lambda i: (jnp.maximum(i - n_first, 0), 0))
    return pl.pallas_call(
        functools.partial(_merge_kernel, n_first=n_first),
        grid=(n // tm,),
        in_specs=[row, pl.BlockSpec((None, N_MOD, D_MODEL), lambda i: (mod_row_fn(i, tm), 0, 0)),
                  first, second, first, second, gate(P_GR), gate(P_GH), _HBM, _HBM, _HBM],
        out_specs=row,
        out_shape=jax.ShapeDtypeStruct((n, D_MODEL), F32),
        scratch_shapes=[_bf16_copy(wr), _bf16_copy(wh), _bf16_copy(wo)],
        compiler_params=_params(("arbitrary",)),
        name=name,
    )(x, mod3, *orets, *ohgs, acts, acts, wr, wh, wo)


def _rope_tables(n_tokens, n_identity):
    pos = np.arange(n_tokens)
    r = (pos // GRID_W).astype(np.float32)
    cl = (pos % GRID_W).astype(np.float32)
    quarter = RET_DK // 4
    inv_freq = (np.float32(ROPE_BASE) ** (-np.arange(quarter, dtype=np.float32) / np.float32(quarter)))
    inv_freq = inv_freq.astype(np.float32)
    ang_r = (r[:, None] * inv_freq[None, :]).astype(np.float64)
    ang_c = (cl[:, None] * inv_freq[None, :]).astype(np.float64)
    cos = np.concatenate([np.cos(ang_r)] * 2 + [np.cos(ang_c)] * 2, axis=-1)
    sins = np.concatenate([-np.sin(ang_r), np.sin(ang_r), -np.sin(ang_c), np.sin(ang_c)], axis=-1)
    cos = np.concatenate([cos, np.ones((n_identity, RET_DK))], axis=0).astype(np.float32)
    sins = np.concatenate([sins, np.zeros((n_identity, RET_DK))], axis=0).astype(np.float32)
    return jnp.asarray(cos), jnp.asarray(sins)


def kernel(x_prompt, x_sample, state_ret, state_hgrn, c, c_ctx, ada_w, ada_b, norm_w, ffn1_w13, ffn1_w2,
           ffn2_w13, ffn2_w2, w_in, ret_decay, hg_lb_logits, hg_norm_w, w_ret_proj, w_hg_proj, w_o,
           final_norm_w):
    depth = ada_w.shape[0]
    assert depth == 1
    l = 0
    bp, lp, _ = x_prompt.shape
    bs, ls, _ = x_sample.shape
    n_ctx = bp * lp
    assert n_ctx % ls == 0 and ls % TM_FFN == 0 and ls % TM_PROJ == 0 and ls % TM_MERGE == 0

    cvec = jnp.concatenate([c_ctx[None, :], c, jnp.zeros((8 - 1 - bs, D_MODEL), F32)], axis=0)
    mod = _mod_call(cvec, ada_w[l], ada_b[l][None, :])
    mod3 = mod.reshape(8, N_MOD, D_MODEL)

    def mod_row(i, tm):
        return jnp.where(i < n_ctx // tm, 0, 1 + (i * tm - n_ctx) // ls)

    def rope_blk(i, tm):
        return jnp.where(i < n_ctx // tm, ls // tm, (i - n_ctx // tm) % (ls // tm))

    nw, fw = norm_w[l], final_norm_w[None, :]
    dec, hg_nw = ret_decay[l], hg_norm_w[l][None, :]

    x = _ffn_call((x_prompt.reshape(n_ctx, D_MODEL), x_sample.reshape(bs * ls, D_MODEL)), mod3, nw, fw,
                  ffn1_w13[l], ffn1_w2[l], n_out_split=None, mod_base=0, norm_row=0, final=False,
                  mod_row_fn=mod_row, tm=TM_FFN, name="ffn1")
    acts, g = _proj_call(x, mod3, nw, hg_lb_logits, w_in[l], _rope_tables(ls, TM_PROJ), rope_blk, layer=l,
                         mod_row_fn=mod_row, tm=TM_PROJ, name="proj")
    ret_ctx, s_ret = _ret_call(dec, acts, None, layer=l, row0=0, B=bp, L=lp, out_state=True, name="ret_ctx")
    ret_lat, = _ret_call(dec, acts, state_ret, layer=l, row0=n_ctx, B=bs, L=ls, out_state=False,
                         name="ret_lat")
    hg_ctx, s_hg = _hgrn_call(acts, g, hg_nw, None, layer=l, row0=0, B=bp, L=lp, out_state=True,
                              name="hgrn_ctx")
    hg_lat, = _hgrn_call(acts, g, hg_nw, state_hgrn, layer=l, row0=n_ctx, B=bs, L=ls, out_state=False,
                         name="hgrn_lat")
    x = _merge_call(x, mod3, (ret_ctx, ret_lat), (hg_ctx, hg_lat), acts, w_ret_proj[l], w_hg_proj[l], w_o[l],
                    mod_row_fn=mod_row, tm=TM_MERGE, name="merge")
    yp, ys = _ffn_call((x,), mod3, nw, fw, ffn2_w13[l], ffn2_w2[l], n_out_split=n_ctx, mod_base=6,
                       norm_row=2, final=True, mod_row_fn=mod_row, tm=TM_FFN, name="ffn2")
    return (yp.reshape(bp, lp, D_MODEL), ys.reshape(bs, ls, D_MODEL), s_ret[:, None], s_hg[:, None])
```

```python
import functools

import numpy as np
import jax
import jax.numpy as jnp
from jax import lax
from jax.experimental import pallas as pl
from jax.experimental.pallas import tpu as pltpu

F32 = jnp.float32
BF16 = jnp.bfloat16

D_MODEL = 1024
GRID_W = 64
RET_HEADS = 4
RET_DK = 128
RET_DV = 256
HG_HEADS = 8
HG_DK = 128
HG_DV = 128
D_FF = 2816
N_MOD = 9
ROPE_BASE = 10000.0
EPS = 1e-6

C_RQ, C_RK, C_RV, C_RG, C_HQ, C_HFF, C_HFB, C_HI, C_HOG, C_GR, C_GH = (
    0, 512, 1024, 2048, 3072, 4096, 5120, 6144, 7168, 8192, 9216)
IN_WIDTH = 10240
P_RQ, P_RK, P_RV, P_RG, P_HQ, P_HI, P_HOG, P_GR, P_GH = 0, 512, 1024, 2048, 3072, 4096, 5120, 6144, 7168
P_WIDTH = 8192

VMEM_LIMIT_BYTES = 56 * 1024 * 1024

TM_FFN = 512
TM_PROJ = 512
TM_MERGE = 512
STAGE_SLOTS = 4
STAGE_ROWS_W13, STAGE_ROWS_W2, STAGE_ROWS_W_IN, STAGE_ROWS_MERGE = 128, 256, 16, 256
MOD_TILE = 2304
SCAN_BLOCK = 128
RET_PACK = 4
SCAN_UNROLL = 2
HG_PACK_LONG = 4
HG_PACK_SHORT = 8
HG_SHORT_SEQ = 512
FF_CHUNK = 256
PROJ_CHUNK = 256
LOG2E = 1.4426950408889634
EXP2_CLAMP = 115.0
HG_BLOCK = 128
HG_LEAF = 16
HG_LEVELS = ((4, 16), (5, 32), (6, 64))


def _dot(a, b):
    return jnp.dot(a, b, preferred_element_type=F32)


def _dot_nt(a, b):
    return lax.dot_general(a, b, (((1,), (1,)), ((), ())), preferred_element_type=F32)


def _dot_tn(a, b):
    return lax.dot_general(a, b, (((0,), (0,)), ((), ())), preferred_element_type=F32)


def _sigmoid(x):
    return 0.5 * jnp.tanh(0.5 * x) + 0.5


def _rms(x, w):
    ms = jnp.mean(x * x, axis=-1, keepdims=True)
    return x * lax.rsqrt(ms + EPS) * w


def _rms_modulate(x, w, shift, scale):
    ms = jnp.mean(x * x, axis=-1, keepdims=True)
    return x * lax.rsqrt(ms + EPS) * (w * (1.0 + scale)) + shift


def _params(sem):
    return pltpu.CompilerParams(dimension_semantics=sem, vmem_limit_bytes=VMEM_LIMIT_BYTES)


def _stage_weight(step, w_hbm, w_vmem, rows):
    r, c = w_hbm.shape
    assert r % rows == 0 and rows % 16 == 0
    n_chunks = r // rows
    ahead = STAGE_SLOTS - 1

    def run(stage, sem):
        def copy(k):
            slot = k % STAGE_SLOTS
            return pltpu.make_async_copy(w_hbm.at[pl.ds(k * rows, rows), :], stage.at[slot], sem.at[slot])

        for k in range(min(ahead, n_chunks)):
            copy(k).start()

        def body(k, carry):
            @pl.when(k + ahead < n_chunks)
            def _():
                copy(k + ahead).start()

            copy(k).wait()
            w_vmem[pl.ds(pl.multiple_of(k * rows, rows), rows), :] = stage[k % STAGE_SLOTS].astype(BF16)
            return carry

        lax.fori_loop(0, n_chunks, body, 0)

    @pl.when(step == 0)
    def _():
        pl.run_scoped(run, pltpu.VMEM((STAGE_SLOTS, rows, c), F32), pltpu.SemaphoreType.DMA((STAGE_SLOTS,)))


_HBM = pl.BlockSpec(memory_space=pl.ANY)


def _bf16_copy(w):
    return pltpu.VMEM(w.shape, BF16)


def _mod_kernel(c_ref, w_ref, b_ref, o_ref):
    c = c_ref[...]
    s = (c * _sigmoid(c)).astype(BF16)
    o_ref[...] = _dot(s, w_ref[...].astype(BF16)) + b_ref[...]


def _mod_call(cvec, ada_w, ada_b):
    n = ada_w.shape[1]
    tn = MOD_TILE
    return pl.pallas_call(
        _mod_kernel,
        grid=(n // tn,),
        in_specs=[
            pl.BlockSpec((8, D_MODEL), lambda j: (0, 0)),
            pl.BlockSpec((D_MODEL, tn), lambda j: (0, j)),
            pl.BlockSpec((1, tn), lambda j: (0, j)),
        ],
        out_specs=pl.BlockSpec((8, tn), lambda j: (0, j)),
        out_shape=jax.ShapeDtypeStruct((8, n), F32),
        compiler_params=_params(("arbitrary",)),
        name="mod",
    )(cvec, ada_w, ada_b)


def _ffn_kernel(*refs, mod_base, norm_row, final, n_first, two_in, two_out):
    it = iter(refs)
    xa_ref = next(it)
    xb_ref = next(it) if two_in else None
    mod_ref, nw_ref, fw_ref, w13_hbm, w2_hbm = (next(it) for _ in range(5))
    oa_ref = next(it)
    ob_ref = next(it) if two_out else None
    w13_ref, w2_ref = next(it), next(it)
    i = pl.program_id(0)
    _stage_weight(i, w13_hbm, w13_ref, STAGE_ROWS_W13)
    _stage_weight(i, w2_hbm, w2_ref, STAGE_ROWS_W2)
    x = xa_ref[...]
    if two_in:
        x = jnp.where(i < n_first, x, xb_ref[...])
    h = _rms_modulate(x, nw_ref[norm_row:norm_row + 1, :], mod_ref[mod_base:mod_base + 1, :],
                      mod_ref[mod_base + 1:mod_base + 2, :]).astype(BF16)
    acc = None
    for c0 in range(0, D_FF, FF_CHUNK):
        a = _dot(h, w13_ref[:, c0:c0 + FF_CHUNK])
        b = _dot(h, w13_ref[:, D_FF + c0:D_FF + c0 + FF_CHUNK])
        act = (a * _sigmoid(a) * b).astype(BF16)
        part = _dot(act, w2_ref[c0:c0 + FF_CHUNK, :])
        acc = part if acc is None else acc + part
    xo = x + (0.5 * mod_ref[mod_base + 2:mod_base + 3, :]) * acc
    if final:
        xo = _rms(xo, fw_ref[...])
    if two_out:
        @pl.when(i < n_first)
        def _():
            oa_ref[...] = xo

        @pl.when(i >= n_first)
        def _():
            ob_ref[...] = xo
    else:
        oa_ref[...] = xo


def _ffn_call(xs, mod3, norm_w, final_w, w13, w2, *, n_out_split, mod_base, norm_row, final, mod_row_fn, tm,
              name):
    two_in = len(xs) == 2
    two_out = n_out_split is not None
    n = sum(x.shape[0] for x in xs)
    n_first = (xs[0].shape[0] if two_in else n_out_split if two_out else n) // tm
    first = pl.BlockSpec((tm, D_MODEL), lambda i: (jnp.minimum(i, n_first - 1), 0))
    second = pl.BlockSpec((tm, D_MODEL), lambda i: (jnp.maximum(i - n_first, 0), 0))
    whole = pl.BlockSpec((tm, D_MODEL), lambda i: (i, 0))
    kern = functools.partial(_ffn_kernel, mod_base=mod_base, norm_row=norm_row, final=final, n_first=n_first,
                             two_in=two_in, two_out=two_out)
    if two_out:
        out_specs = [first, second]
        out_shape = [jax.ShapeDtypeStruct((n_out_split, D_MODEL), F32),
                     jax.ShapeDtypeStruct((n - n_out_split, D_MODEL), F32)]
    else:
        out_specs = whole
        out_shape = jax.ShapeDtypeStruct((n, D_MODEL), F32)
    return pl.pallas_call(
        kern,
        grid=(n // tm,),
        in_specs=([first, second] if two_in else [whole]) + [
            pl.BlockSpec((None, N_MOD, D_MODEL), lambda i: (mod_row_fn(i, tm), 0, 0)),
            pl.BlockSpec((3, D_MODEL), lambda i: (0, 0)),
            pl.BlockSpec((1, D_MODEL), lambda i: (0, 0)),
            _HBM,
            _HBM,
        ],
        out_specs=out_specs,
        out_shape=out_shape,
        scratch_shapes=[_bf16_copy(w13), _bf16_copy(w2)],
        compiler_params=_params(("arbitrary",)),
        name=name,
    )(*xs, mod3, norm_w, final_w, w13, w2)


def _rope(x, cos, sins, first_quarter):
    partner = jnp.where(first_quarter, pltpu.roll(x, 96, 1), pltpu.roll(x, 32, 1))
    return x * cos + partner * sins


def _proj_kernel(*refs, layer, rope):
    it = iter(refs)
    x_ref, mod_ref, nw_ref, lbl_ref, w_hbm = (next(it) for _ in range(5))
    cos_ref, sin_ref = (next(it), next(it)) if rope else (None, None)
    p_ref, g_ref = next(it), next(it)
    w_ref = next(it)
    _stage_weight(pl.program_id(0), w_hbm, w_ref, STAGE_ROWS_W_IN)
    h = _rms_modulate(x_ref[...], nw_ref[1:2, :], mod_ref[3:4, :], mod_ref[4:5, :]).astype(BF16)

    if rope:
        lane = lax.broadcasted_iota(jnp.int32, (x_ref.shape[0], RET_DK), 1)
        first_quarter = (lane & 32) == 0

    def rot(t):
        if not rope:
            return t
        heads = [_rope(t[:, j:j + RET_DK], cos_ref[...], sin_ref[...], first_quarter)
                 for j in range(0, t.shape[1], RET_DK)]
        return jnp.concatenate(heads, axis=1)

    def silu(t):
        return t * _sigmoid(t)

    def log_forget(d):
        def fn(t, c):
            lg = lbl_ref[d, :, c:c + PROJ_CHUNK]
            e = jnp.exp(lg - jnp.max(lg, axis=0, keepdims=True))
            lb = jnp.sum(e[0:layer + 1, :], axis=0, keepdims=True) / jnp.sum(e, axis=0, keepdims=True)
            return jnp.log(lb + (1.0 - lb) * _sigmoid(t)) * LOG2E
        return fn

    parts = (
        (C_RQ, 512, p_ref, P_RQ, lambda t, c: rot(t)),
        (C_RK, 512, p_ref, P_RK, lambda t, c: rot(t * (RET_DK ** -0.5))),
        (C_RV, 1024, p_ref, P_RV, lambda t, c: t),
        (C_RG, 1024, p_ref, P_RG, lambda t, c: silu(t)),
        (C_HQ, 1024, p_ref, P_HQ, lambda t, c: silu(t) * (HG_DK ** -0.5)),
        (C_HFF, 1024, g_ref, 0, log_forget(0)),
        (C_HFB, 1024, g_ref, 1024, log_forget(1)),
        (C_HI, 1024, p_ref, P_HI, lambda t, c: t),
        (C_HOG, 1024, p_ref, P_HOG, lambda t, c: silu(t)),
        (C_GR, 1024, p_ref, P_GR, lambda t, c: _sigmoid(t)),
        (C_GH, 1024, p_ref, P_GH, lambda t, c: _sigmoid(t)),
    )
    for c in range(0, 1024, PROJ_CHUNK):
        for w0, width, o_ref, o0, fn in parts:
            if c < width:
                t = _dot(h, w_ref[:, w0 + c:w0 + c + PROJ_CHUNK])
                o_ref[:, o0 + c:o0 + c + PROJ_CHUNK] = fn(t, c).astype(o_ref.dtype)


def _proj_call(x, mod3, norm_w, lb_logits, w_in, rope_tabs, rope_blk_fn, *, layer, mod_row_fn, tm, name):
    n = x.shape[0]
    slots = lb_logits.shape[1]
    row = lambda w: pl.BlockSpec((tm, w), lambda i: (i, 0))
    outs = [(P_WIDTH, BF16), (2 * HG_HEADS * HG_DK, F32)]
    in_specs = [
        row(D_MODEL),
        pl.BlockSpec((None, N_MOD, D_MODEL), lambda i: (mod_row_fn(i, tm), 0, 0)),
        pl.BlockSpec((3, D_MODEL), lambda i: (0, 0)),
        pl.BlockSpec((2, slots, 1024), lambda i: (0, 0, 0)),
        _HBM,
    ]
    args = [x, mod3, norm_w, lb_logits, w_in]
    if rope_tabs is not None:
        in_specs += [pl.BlockSpec((tm, RET_DK), lambda i: (rope_blk_fn(i, tm), 0))] * 2
        args += list(rope_tabs)
    return pl.pallas_call(
        functools.partial(_proj_kernel, layer=layer, rope=rope_tabs is not None),
        grid=(n // tm,),
        in_specs=in_specs,
        out_specs=[row(w) for w, _ in outs],
        out_shape=[jax.ShapeDtypeStruct((n, w), dt) for w, dt in outs],
        scratch_shapes=[_bf16_copy(w_in)],
        compiler_params=_params(("arbitrary",)),
        name=name,
    )(*args)


def _log_sigmoid(x):
    return -(jnp.maximum(-x, 0.0) + jnp.log1p(jnp.exp(-jnp.abs(x))))


def _half_acc_slices(i, n, blk, finalize):
    if finalize:
        starts = ((i - n // 2) * blk, (n - 1 - i) * blk)
    else:
        starts = (i * blk, (n // 2 - 1 - i) * blk)
    return tuple(pl.ds(pl.multiple_of(s, blk), blk) for s in starts)


def _run_scans(scans):
    n = scans[0][0]
    assert all(s[0] == n for s in scans)

    def half(finalize):
        def body(i, carry):
            for _, step, _ in scans:
                step(i, finalize)
            return carry
        return body

    lax.fori_loop(0, n // 2, half(False), 0, unroll=min(SCAN_UNROLL, n // 2))
    lax.fori_loop(n // 2, n, half(True), 0, unroll=min(SCAN_UNROLL, n // 2))
    for _, _, finish in scans:
        finish()


def _scan_kernel(*refs, parts):
    ins = [n for _, n, _, _ in parts]
    outs = [n for _, _, n, _ in parts]
    offs = [0, sum(ins), sum(ins) + sum(outs)]
    scans = []
    for setup, n_in, n_out, n_scr in parts:
        pick = [refs[o:o + c] for o, c in zip(offs, (n_in, n_out, n_scr))]
        offs = [o + c for o, c in zip(offs, (n_in, n_out, n_scr))]
        scans.append(setup(*pick))
    _run_scans(scans)


def _ret_setup(in_refs, out_refs, scratch, *, L, has_s0, out_state):
    it = iter(in_refs)
    dec_ref = next(it)
    q_ref, k_ref, v_ref, gate_ref = next(it), next(it), next(it), next(it)
    s0_ref = next(it) if has_s0 else None
    o_ref = out_refs[0]
    st_ref = out_refs[1] if out_state else None
    of_acc, ob_acc, st_scr, cst = scratch

    C = SCAN_BLOCK
    P = RET_PACK
    n = L // C
    pair = pl.program_id(1)
    ii = lax.broadcasted_iota(jnp.int32, (C, C), 0).astype(F32)
    jj = lax.broadcasted_iota(jnp.int32, (C, C), 1).astype(F32)
    diff = ii - jj

    @pl.when(jnp.logical_and(pl.program_id(0) == 0, pair == 0))
    def _():
        for hd in range(RET_HEADS):
            lgf = _log_sigmoid(jnp.full((C, C), dec_ref[0, hd], F32))
            lgb = _log_sigmoid(jnp.full((C, C), dec_ref[1, hd], F32))
            cst[hd, 0] = (jnp.where(diff >= 0, jnp.exp(jnp.maximum(diff, 0.0) * lgf), 0.0)
                          + jnp.where(diff <= 0, jnp.exp(jnp.maximum(-diff, 0.0) * lgb), 0.0))
            cst[hd, 1] = jnp.exp((ii + 1.0) * lgf)
            cst[hd, 2] = jnp.exp((C - 1.0 - ii) * lgf)
            cst[hd, 3] = jnp.exp((C - ii) * lgb)
            cst[hd, 4] = jnp.exp(ii * lgb)

    cds = []
    for h in range(P):
        hd = pair * P + h
        cds.append([jnp.exp(C * _log_sigmoid(jnp.full((1, RET_DV), dec_ref[d, hd], F32))) for d in range(2)])
        for d in range(2):
            st_scr[2 * h + d] = s0_ref[d, h] if has_s0 else jnp.zeros((RET_DK, RET_DV), F32)

    def step(i, finalize):
        sls = (pl.ds(pl.multiple_of(i * C, C), C), pl.ds(pl.multiple_of((n - 1 - i) * C, C), C))
        acc_sls = _half_acc_slices(i, n, C, finalize)
        chains = []
        for h in range(P):
            ql = slice(h * RET_DK, (h + 1) * RET_DK)
            vl = slice(h * RET_DV, (h + 1) * RET_DV)
            for d in range(2):
                chains.append(dict(h=h, d=d, vl=vl, q=q_ref[sls[d], ql], k=k_ref[sls[d], ql],
                                   v=v_ref[sls[d], vl]))
        for c in chains:
            if c["d"] == 0:
                c["s"] = _dot_nt(c["q"], c["k"])
        for c in chains:
            h, d, v, vl = c["h"], c["d"], c["v"], c["vl"]
            hd = pair * P + h
            st = st_scr[2 * h + d]
            qd = (c["q"].astype(F32) * cst[hd, 1 + 2 * d]).astype(BF16)
            if d == 0:
                p = (c["s"] * cst[hd, 0]).astype(BF16)
                o = _dot(jnp.concatenate([p, qd], axis=1), jnp.concatenate([v, st.astype(BF16)], axis=0))
            else:
                o = _dot(qd, st.astype(BF16))
            kd = (c["k"].astype(F32) * cst[hd, 2 + 2 * d]).astype(BF16)
            st_scr[2 * h + d] = cds[h][d] * st + _dot_tn(kd, v)
            if finalize:
                o = o + (ob_acc, of_acc)[d][acc_sls[d], vl]
                ms = jnp.mean(o * o, axis=-1, keepdims=True)
                o_ref[sls[d], vl] = (o * lax.rsqrt(ms + EPS) * gate_ref[sls[d], vl]).astype(BF16)
            else:
                (of_acc, ob_acc)[d][acc_sls[d], vl] = o

    def finish():
        if out_state:
            for h in range(P):
                for d in range(2):
                    st_ref[d, h] = st_scr[2 * h + d]

    return n, step, finish


def _ret_part(dec, acts, s0, *, layer, row0, B, L, out_state):
    has_s0 = s0 is not None
    H = RET_HEADS
    P = RET_PACK
    r0 = row0 // L
    st_spec = pl.BlockSpec((None, 2, P, RET_DK, RET_DV), lambda b, h: (b, 0, h, 0, 0))
    cols = lambda width, off: pl.BlockSpec((L, width), lambda b, h: (r0 + b, off // width + h))
    in_specs = [
        pl.BlockSpec(memory_space=pltpu.SMEM),
        cols(P * RET_DK, P_RQ), cols(P * RET_DK, P_RK), cols(P * RET_DV, P_RV), cols(P * RET_DV, P_RG),
    ]
    args = [dec, acts, acts, acts, acts]
    if has_s0:
        in_specs.append(pl.BlockSpec((None, None, 2, P, RET_DK, RET_DV), lambda b, h: (b, layer, 0, h, 0, 0)))
        args.append(s0)
    out_specs = [pl.BlockSpec((L, P * RET_DV), lambda b, h: (b, h))]
    out_shape = [jax.ShapeDtypeStruct((B * L, H * RET_DV), BF16)]
    if out_state:
        out_specs.append(st_spec)
        out_shape.append(jax.ShapeDtypeStruct((B, 2, H, RET_DK, RET_DV), F32))
    scratch = [pltpu.VMEM((L // 2, P * RET_DV), F32), pltpu.VMEM((L // 2, P * RET_DV), F32),
               pltpu.VMEM((2 * P, RET_DK, RET_DV), F32), pltpu.VMEM((H, 5, SCAN_BLOCK, SCAN_BLOCK), F32)]
    return dict(setup=functools.partial(_ret_setup, L=L, has_s0=has_s0, out_state=out_state),
                grid=(B, H // P), in_specs=in_specs, args=args, out_specs=out_specs, out_shape=out_shape,
                scratch=scratch)


def _scan_call(parts, name):
    grid = parts[0]["grid"]
    assert all(p["grid"] == grid for p in parts)
    kern = functools.partial(_scan_kernel, parts=[(p["setup"], len(p["in_specs"]), len(p["out_specs"]),
                                                   len(p["scratch"])) for p in parts])
    cat = lambda key: [x for p in parts for x in p[key]]
    res = pl.pallas_call(
        kern,
        grid=grid,
        in_specs=cat("in_specs"),
        out_specs=cat("out_specs"),
        out_shape=cat("out_shape"),
        scratch_shapes=cat("scratch"),
        compiler_params=_params(("arbitrary", "arbitrary")),
        name=name,
    )(*cat("args"))
    out, k = [], 0
    for p in parts:
        out.append(res[k:k + len(p["out_specs"])])
        k += len(p["out_specs"])
    return out


def _gla_blocks(chains, tri_ref, code_ref):
    T = HG_BLOCK
    K = HG_DK

    def cumulate(c):
        g2 = c["g"]
        c["k"] = 1.0 - jnp.exp2(g2)
        hi = g2.astype(BF16)
        lo = (g2 - hi.astype(F32)).astype(BF16)
        tri = tri_ref[int(c["rev"])]
        c["b"] = _dot(tri, hi) + _dot(tri, lo)

    def leaf(c):
        bl = c["b"].reshape(T // HG_LEAF, HG_LEAF, K)
        r = HG_LEAF // 2 if c["rev"] else HG_LEAF // 2 - 1
        d = jnp.clip(bl - bl[:, r:r + 1, :], -EXP2_CLAMP, EXP2_CLAMP).reshape(T, K)
        e = jnp.exp2(d)
        p = _dot_nt((c["q"] * e).astype(BF16), (c["k"] / e).astype(BF16))
        c["a"] = jnp.where(code_ref[int(c["rev"])] == 0, p, 0.0)

    def level(c, lv, m):
        seg = 2 * m
        ng = T // seg
        b3, q3, k3 = (c[n].reshape(ng, seg, K) for n in ("b", "q", "k"))
        if c["rev"]:
            qs, ks, r = slice(0, m), slice(m, seg), m
        else:
            qs, ks, r = slice(m, seg), slice(0, m), m - 1
        ref = b3[:, r:r + 1, :]
        qm = (q3[:, qs, :] * jnp.exp2(b3[:, qs, :] - ref)).reshape(ng * m, K).astype(BF16)
        kz = k3[:, ks, :] * jnp.exp2(ref - b3[:, ks, :])
        zeros = jnp.zeros((ng, m, K), F32)
        kfull = jnp.concatenate([zeros, kz] if c["rev"] else [kz, zeros], axis=1)
        p = _dot_nt(qm, kfull.reshape(T, K).astype(BF16)).reshape(ng, m, T)
        a3 = c["a"].reshape(ng, seg, T)
        c3 = code_ref[int(c["rev"])].reshape(ng, seg, T)
        upd = jnp.where(c3[:, qs, :] == lv, p, a3[:, qs, :])
        parts = [upd, a3[:, ks, :]] if c["rev"] else [a3[:, ks, :], upd]
        c["a"] = jnp.concatenate(parts, axis=1).reshape(T, T)

    def finish(c):
        b, q, k, v, st = c["b"], c["q"], c["k"], c["v"], c["st"]
        o = _dot(c["a"].astype(BF16), v) + _dot_nt((q * jnp.exp2(b)).astype(BF16), st.astype(BF16))
        end_row = 0 if c["rev"] else T - 1
        b1 = b.reshape(1, T, K)
        b_end = b1[:, end_row:end_row + 1, :]
        kd = (k * jnp.exp2(b_end - b1).reshape(T, K)).astype(BF16)
        c["out"] = (o, st * jnp.exp2(b_end.reshape(1, K)) + _dot_tn(v, kd))

    for c in chains:
        cumulate(c)
    for c in chains:
        leaf(c)
    for lv, m in HG_LEVELS:
        for c in chains:
            level(c, lv, m)
    for c in chains:
        finish(c)
    return [c["out"] for c in chains]


def _hgrn_setup(in_refs, out_refs, scratch, *, L, P, has_s0, out_state):
    it = iter(in_refs)
    q_ref, g_f_ref, g_b_ref, v_ref, gate_ref, nw_ref = (next(it) for _ in range(6))
    s0_ref = next(it) if has_s0 else None
    o_ref = out_refs[0]
    st_ref = out_refs[1] if out_state else None
    of_acc, ob_acc, st_scr, tri_ref, code_ref = scratch

    T = HG_BLOCK
    n = L // T

    @pl.when(jnp.logical_and(pl.program_id(0) == 0, pl.program_id(1) == 0))
    def _():
        ti = lax.broadcasted_iota(jnp.int32, (T, T), 0)
        si = lax.broadcasted_iota(jnp.int32, (T, T), 1)
        x = ti ^ si
        tri_ref[0] = (si <= ti).astype(BF16)
        tri_ref[1] = (si >= ti).astype(BF16)
        lvl = jnp.where(x < HG_LEAF, 0, HG_LEVELS[0][0] - 1 + sum((x >= m).astype(jnp.int32) for _, m in HG_LEVELS))
        code_ref[0] = jnp.where(si <= ti, lvl, -1)
        code_ref[1] = jnp.where(si >= ti, lvl, -1)

    for h in range(P):
        for d in range(2):
            st_scr[2 * h + d] = s0_ref[d, h].T if has_s0 else jnp.zeros((HG_DV, HG_DK), F32)

    def step(i, finalize):
        sls = (pl.ds(pl.multiple_of(i * T, T), T), pl.ds(pl.multiple_of((n - 1 - i) * T, T), T))
        acc_sls = _half_acc_slices(i, n, T, finalize)
        chains = []
        for h in range(P):
            lanes = slice(h * HG_DK, (h + 1) * HG_DK)
            for d, g_ref in enumerate((g_f_ref, g_b_ref)):
                chains.append(dict(q=q_ref[sls[d], lanes].astype(F32), g=g_ref[sls[d], lanes],
                                   v=v_ref[sls[d], lanes],
                                   st=st_scr[2 * h + d], rev=bool(d)))
        res = _gla_blocks(chains, tri_ref, code_ref)
        for h in range(P):
            lanes = slice(h * HG_DK, (h + 1) * HG_DK)
            for d in range(2):
                o, st = res[2 * h + d]
                st_scr[2 * h + d] = st
                if finalize:
                    o = o + (ob_acc, of_acc)[d][acc_sls[d], lanes]
                    ms = jnp.mean(o * o, axis=-1, keepdims=True)
                    o = o * lax.rsqrt(ms + EPS) * nw_ref[:, lanes]
                    o_ref[sls[d], lanes] = (o * gate_ref[sls[d], lanes]).astype(BF16)
                else:
                    (of_acc, ob_acc)[d][acc_sls[d], lanes] = o

    def finish():
        if out_state:
            for h in range(P):
                for d in range(2):
                    st_ref[d, h] = st_scr[2 * h + d].T

    return n, step, finish


def _hgrn_part(acts, g, nw, s0, *, layer, row0, B, L, out_state):
    has_s0 = s0 is not None
    H = HG_HEADS
    P = HG_PACK_SHORT if L <= HG_SHORT_SEQ else HG_PACK_LONG
    W = P * HG_DK
    r0 = row0 // L
    blk = lambda off: pl.BlockSpec((L, W), lambda b, h: (r0 + b, off // W + h))
    st_spec = pl.BlockSpec((None, 2, P, HG_DK, HG_DV), lambda b, h: (b, 0, h, 0, 0))
    in_specs = [blk(P_HQ), blk(0), blk(H * HG_DK), blk(P_HI), blk(P_HOG),
                pl.BlockSpec((1, W), lambda b, h: (0, h))]
    args = [acts, g, g, acts, acts, nw]
    if has_s0:
        in_specs.append(pl.BlockSpec((None, None, 2, P, HG_DK, HG_DV), lambda b, h: (b, layer, 0, h, 0, 0)))
        args.append(s0)
    out_specs = [pl.BlockSpec((L, W), lambda b, h: (b, h))]
    out_shape = [jax.ShapeDtypeStruct((B * L, H * HG_DV), BF16)]
    if out_state:
        out_specs.append(st_spec)
        out_shape.append(jax.ShapeDtypeStruct((B, 2, H, HG_DK, HG_DV), F32))
    scratch = [pltpu.VMEM((L // 2, W), F32), pltpu.VMEM((L // 2, W), F32),
               pltpu.VMEM((2 * P, HG_DV, HG_DK), F32), pltpu.VMEM((2, HG_BLOCK, HG_BLOCK), BF16),
               pltpu.VMEM((2, HG_BLOCK, HG_BLOCK), jnp.int32)]
    return dict(setup=functools.partial(_hgrn_setup, L=L, P=P, has_s0=has_s0, out_state=out_state),
                grid=(B, H // P), in_specs=in_specs, args=args, out_specs=out_specs, out_shape=out_shape,
                scratch=scratch)


def _merge_kernel(x_ref, mod_ref, oret_a, oret_b, ohg_a, ohg_b, gr_ref, gh_ref, wr_hbm, wh_hbm, wo_hbm, o_ref,
                  wr_ref, wh_ref, wo_ref, *, n_first):
    i = pl.program_id(0)
    for w_hbm, w_ref in ((wr_hbm, wr_ref), (wh_hbm, wh_ref), (wo_hbm, wo_ref)):
        _stage_weight(i, w_hbm, w_ref, STAGE_ROWS_MERGE)
    oret = jnp.where(i < n_first, oret_a[...], oret_b[...])
    ohg = jnp.where(i < n_first, ohg_a[...], ohg_b[...])
    y = gr_ref[...] * _dot(oret, wr_ref[...]) + gh_ref[...] * _dot(ohg, wh_ref[...])
    out = _dot(y.astype(BF16), wo_ref[...])
    o_ref[...] = x_ref[...] + mod_ref[5:6, :] * out


def _merge_call(x, mod3, orets, ohgs, acts, wr, wh, wo, *, mod_row_fn, tm, name):
    n = x.shape[0]
    n_first = orets[0].shape[0] // tm
    row = pl.BlockSpec((tm, D_MODEL), lambda i: (i, 0))
    gate = lambda off: pl.BlockSpec((tm, D_MODEL), lambda i: (i, off // D_MODEL))
    first = pl.BlockSpec((tm, D_MODEL), lambda i: (jnp.minimum(i, n_first - 1), 0))
    second = pl.BlockSpec((tm, D_MODEL), lambda i: (jnp.maximum(i - n_first, 0), 0))
    return pl.pallas_call(
        functools.partial(_merge_kernel, n_first=n_first),
        grid=(n // tm,),
        in_specs=[row, pl.BlockSpec((None, N_MOD, D_MODEL), lambda i: (mod_row_fn(i, tm), 0, 0)),
                  first, second, first, second, gate(P_GR), gate(P_GH), _HBM, _HBM, _HBM],
        out_specs=row,
        out_shape=jax.ShapeDtypeStruct((n, D_MODEL), F32),
        scratch_shapes=[_bf16_copy(wr), _bf16_copy(wh), _bf16_copy(wo)],
        compiler_params=_params(("arbitrary",)),
        name=name,
    )(x, mod3, *orets, *ohgs, acts, acts, wr, wh, wo)


def _rope_tables(n_tokens, n_identity):
    pos = np.arange(n_tokens)
    r = (pos // GRID_W).astype(np.float32)
    cl = (pos % GRID_W).astype(np.float32)
    quarter = RET_DK // 4
    inv_freq = (np.float32(ROPE_BASE) ** (-np.arange(quarter, dtype=np.float32) / np.float32(quarter)))
    inv_freq = inv_freq.astype(np.float32)
    ang_r = (r[:, None] * inv_freq[None, :]).astype(np.float64)
    ang_c = (cl[:, None] * inv_freq[None, :]).astype(np.float64)
    cos = np.concatenate([np.cos(ang_r)] * 2 + [np.cos(ang_c)] * 2, axis=-1)
    sins = np.concatenate([-np.sin(ang_r), np.sin(ang_r), -np.sin(ang_c), np.sin(ang_c)], axis=-1)
    cos = np.concatenate([cos, np.ones((n_identity, RET_DK))], axis=0).astype(np.float32)
    sins = np.concatenate([sins, np.zeros((n_identity, RET_DK))], axis=0).astype(np.float32)
    return jnp.asarray(cos), jnp.asarray(sins)


def kernel(x_prompt, x_sample, state_ret, state_hgrn, c, c_ctx, ada_w, ada_b, norm_w, ffn1_w13, ffn1_w2,
           ffn2_w13, ffn2_w2, w_in, ret_decay, hg_lb_logits, hg_norm_w, w_ret_proj, w_hg_proj, w_o,
           final_norm_w):
    depth = ada_w.shape[0]
    assert depth == 1
    l = 0
    bp, lp, _ = x_prompt.shape
    bs, ls, _ = x_sample.shape
    n_ctx = bp * lp
    assert n_ctx % ls == 0 and ls % TM_FFN == 0 and ls % TM_PROJ == 0 and ls % TM_MERGE == 0

    cvec = jnp.concatenate([c_ctx[None, :], c, jnp.zeros((8 - 1 - bs, D_MODEL), F32)], axis=0)
    mod = _mod_call(cvec, ada_w[l], ada_b[l][None, :])
    mod3 = mod.reshape(8, N_MOD, D_MODEL)

    def mod_row(i, tm):
        return jnp.where(i < n_ctx // tm, 0, 1 + (i * tm - n_ctx) // ls)

    def rope_blk(i, tm):
        return jnp.where(i < n_ctx // tm, ls // tm, (i - n_ctx // tm) % (ls // tm))

    nw, fw = norm_w[l], final_norm_w[None, :]
    dec, hg_nw = ret_decay[l], hg_norm_w[l][None, :]

    x = _ffn_call((x_prompt.reshape(n_ctx, D_MODEL), x_sample.reshape(bs * ls, D_MODEL)), mod3, nw, fw,
                  ffn1_w13[l], ffn1_w2[l], n_out_split=None, mod_base=0, norm_row=0, final=False,
                  mod_row_fn=mod_row, tm=TM_FFN, name="ffn1")
    acts, g = _proj_call(x, mod3, nw, hg_lb_logits, w_in[l], _rope_tables(ls, TM_PROJ), rope_blk, layer=l,
                         mod_row_fn=mod_row, tm=TM_PROJ, name="proj")
    (ret_ctx, s_ret), (hg_ctx, s_hg) = _scan_call(
        [_ret_part(dec, acts, None, layer=l, row0=0, B=bp, L=lp, out_state=True),
         _hgrn_part(acts, g, hg_nw, None, layer=l, row0=0, B=bp, L=lp, out_state=True)], "scan_ctx")
    (ret_lat,), = _scan_call([_ret_part(dec, acts, state_ret, layer=l, row0=n_ctx, B=bs, L=ls, out_state=False)],
                             "ret_lat")
    (hg_lat,), = _scan_call([_hgrn_part(acts, g, hg_nw, state_hgrn, layer=l, row0=n_ctx, B=bs, L=ls,
                                        out_state=False)], "hgrn_lat")
    x = _merge_call(x, mod3, (ret_ctx, ret_lat), (hg_ctx, hg_lat), acts, w_ret_proj[l], w_hg_proj[l], w_o[l],
                    mod_row_fn=mod_row, tm=TM_MERGE, name="merge")
    yp, ys = _ffn_call((x,), mod3, nw, fw, ffn2_w13[l], ffn2_w2[l], n_out_split=n_ctx, mod_base=6,
                       norm_row=2, final=True, mod_row_fn=mod_row, tm=TM_FFN, name="ffn2")
    return (yp.reshape(bp, lp, D_MODEL), ys.reshape(bs, ls, D_MODEL), s_ret[:, None], s_hg[:, None])
```

```python
import functools

import numpy as np
import jax
import jax.numpy as jnp
from jax import lax
from jax.experimental import pallas as pl
from jax.experimental.pallas import tpu as pltpu

F32 = jnp.float32
BF16 = jnp.bfloat16

D_MODEL = 1024
GRID_W = 64
RET_HEADS = 4
RET_DK = 128
RET_DV = 256
HG_HEADS = 8
HG_DK = 128
HG_DV = 128
D_FF = 2816
N_MOD = 9
ROPE_BASE = 10000.0
EPS = 1e-6

C_RQ, C_RK, C_RV, C_RG, C_HQ, C_HFF, C_HFB, C_HI, C_HOG, C_GR, C_GH = (
    0, 512, 1024, 2048, 3072, 4096, 5120, 6144, 7168, 8192, 9216)
IN_WIDTH = 10240
P_RQ, P_RK, P_RV, P_RG, P_HQ, P_HI, P_HOG, P_GR, P_GH = 0, 512, 1024, 2048, 3072, 4096, 5120, 6144, 7168
P_WIDTH = 8192

VMEM_LIMIT_BYTES = 56 * 1024 * 1024

TM_FFN = 512
TM_PROJ = 512
TM_MERGE = 512
STAGE_SLOTS = 4
STAGE_ROWS_W13, STAGE_ROWS_W2, STAGE_ROWS_W_IN, STAGE_ROWS_MERGE = 128, 256, 16, 512
MOD_TILE = 3072
SCAN_BLOCK = 128
RET_PACK = 4
SCAN_UNROLL = 4
HG_PACK_LONG = 4
HG_PACK_SHORT = 8
HG_SHORT_SEQ = 512
FF_CHUNK = 256
PROJ_CHUNK = 256
LOG2E = 1.4426950408889634
EXP2_CLAMP = 115.0
HG_BLOCK = 128
HG_LEAF = 16
HG_LEVELS = ((4, 16), (5, 32), (6, 64))


def _dot(a, b):
    return jnp.dot(a, b, preferred_element_type=F32)


def _dot_nt(a, b):
    return lax.dot_general(a, b, (((1,), (1,)), ((), ())), preferred_element_type=F32)


def _dot_tn(a, b):
    return lax.dot_general(a, b, (((0,), (0,)), ((), ())), preferred_element_type=F32)


def _sigmoid(x):
    return 0.5 * jnp.tanh(0.5 * x) + 0.5


def _rms(x, w):
    ms = jnp.mean(x * x, axis=-1, keepdims=True)
    return x * lax.rsqrt(ms + EPS) * w


def _rms_modulate(x, w, shift, scale):
    ms = jnp.mean(x * x, axis=-1, keepdims=True)
    return x * lax.rsqrt(ms + EPS) * (w * (1.0 + scale)) + shift


def _params(sem):
    return pltpu.CompilerParams(dimension_semantics=sem, vmem_limit_bytes=VMEM_LIMIT_BYTES)


def _stage_weight(step, w_hbm, w_vmem, rows):
    r, c = w_hbm.shape
    assert r % rows == 0 and rows % 16 == 0
    n_chunks = r // rows
    ahead = STAGE_SLOTS - 1

    def run(stage, sem):
        def copy(k):
            slot = k % STAGE_SLOTS
            return pltpu.make_async_copy(w_hbm.at[pl.ds(k * rows, rows), :], stage.at[slot], sem.at[slot])

        for k in range(min(ahead, n_chunks)):
            copy(k).start()

        def body(k, carry):
            @pl.when(k + ahead < n_chunks)
            def _():
                copy(k + ahead).start()

            copy(k).wait()
            w_vmem[pl.ds(pl.multiple_of(k * rows, rows), rows), :] = stage[k % STAGE_SLOTS].astype(BF16)
            return carry

        lax.fori_loop(0, n_chunks, body, 0)

    @pl.when(step == 0)
    def _():
        pl.run_scoped(run, pltpu.VMEM((STAGE_SLOTS, rows, c), F32), pltpu.SemaphoreType.DMA((STAGE_SLOTS,)))


_HBM = pl.BlockSpec(memory_space=pl.ANY)


def _bf16_copy(w):
    return pltpu.VMEM(w.shape, BF16)


def _mod_kernel(c_ref, w_ref, b_ref, o_ref):
    c = c_ref[...]
    s = (c * _sigmoid(c)).astype(BF16)
    o_ref[...] = _dot(s, w_ref[...].astype(BF16)) + b_ref[...]


def _mod_call(cvec, ada_w, ada_b):
    n = ada_w.shape[1]
    tn = MOD_TILE
    return pl.pallas_call(
        _mod_kernel,
        grid=(n // tn,),
        in_specs=[
            pl.BlockSpec((8, D_MODEL), lambda j: (0, 0)),
            pl.BlockSpec((D_MODEL, tn), lambda j: (0, j)),
            pl.BlockSpec((1, tn), lambda j: (0, j)),
        ],
        out_specs=pl.BlockSpec((8, tn), lambda j: (0, j)),
        out_shape=jax.ShapeDtypeStruct((8, n), F32),
        compiler_params=_params(("arbitrary",)),
        name="mod",
    )(cvec, ada_w, ada_b)


def _ffn_kernel(*refs, mod_base, norm_row, final, n_first, two_in, two_out):
    it = iter(refs)
    xa_ref = next(it)
    xb_ref = next(it) if two_in else None
    mod_ref, nw_ref, fw_ref, w13_hbm, w2_hbm = (next(it) for _ in range(5))
    oa_ref = next(it)
    ob_ref = next(it) if two_out else None
    w13_ref, w2_ref = next(it), next(it)
    i = pl.program_id(0)
    _stage_weight(i, w13_hbm, w13_ref, STAGE_ROWS_W13)
    _stage_weight(i, w2_hbm, w2_ref, STAGE_ROWS_W2)
    x = xa_ref[...]
    if two_in:
        x = jnp.where(i < n_first, x, xb_ref[...])
    h = _rms_modulate(x, nw_ref[norm_row:norm_row + 1, :], mod_ref[mod_base:mod_base + 1, :],
                      mod_ref[mod_base + 1:mod_base + 2, :]).astype(BF16)
    acc = None
    for c0 in range(0, D_FF, FF_CHUNK):
        a = _dot(h, w13_ref[:, c0:c0 + FF_CHUNK])
        b = _dot(h, w13_ref[:, D_FF + c0:D_FF + c0 + FF_CHUNK])
        act = (a * _sigmoid(a) * b).astype(BF16)
        part = _dot(act, w2_ref[c0:c0 + FF_CHUNK, :])
        acc = part if acc is None else acc + part
    xo = x + (0.5 * mod_ref[mod_base + 2:mod_base + 3, :]) * acc
    if final:
        xo = _rms(xo, fw_ref[...])
    if two_out:
        @pl.when(i < n_first)
        def _():
            oa_ref[...] = xo

        @pl.when(i >= n_first)
        def _():
            ob_ref[...] = xo
    else:
        oa_ref[...] = xo


def _ffn_call(xs, mod3, norm_w, final_w, w13, w2, *, n_out_split, mod_base, norm_row, final, mod_row_fn, tm,
              name):
    two_in = len(xs) == 2
    two_out = n_out_split is not None
    n = sum(x.shape[0] for x in xs)
    n_first = (xs[0].shape[0] if two_in else n_out_split if two_out else n) // tm
    first = pl.BlockSpec((tm, D_MODEL), lambda i: (jnp.minimum(i, n_first - 1), 0))
    second = pl.BlockSpec((tm, D_MODEL), lambda i: (jnp.maximum(i - n_first, 0), 0))
    whole = pl.BlockSpec((tm, D_MODEL), lambda i: (i, 0))
    kern = functools.partial(_ffn_kernel, mod_base=mod_base, norm_row=norm_row, final=final, n_first=n_first,
                             two_in=two_in, two_out=two_out)
    if two_out:
        out_specs = [first, second]
        out_shape = [jax.ShapeDtypeStruct((n_out_split, D_MODEL), F32),
                     jax.ShapeDtypeStruct((n - n_out_split, D_MODEL), F32)]
    else:
        out_specs = whole
        out_shape = jax.ShapeDtypeStruct((n, D_MODEL), F32)
    return pl.pallas_call(
        kern,
        grid=(n // tm,),
        in_specs=([first, second] if two_in else [whole]) + [
            pl.BlockSpec((None, N_MOD, D_MODEL), lambda i: (mod_row_fn(i, tm), 0, 0)),
            pl.BlockSpec((3, D_MODEL), lambda i: (0, 0)),
            pl.BlockSpec((1, D_MODEL), lambda i: (0, 0)),
            _HBM,
            _HBM,
        ],
        out_specs=out_specs,
        out_shape=out_shape,
        scratch_shapes=[_bf16_copy(w13), _bf16_copy(w2)],
        compiler_params=_params(("arbitrary",)),
        name=name,
    )(*xs, mod3, norm_w, final_w, w13, w2)


def _rope(x, cos, sins, first_quarter):
    partner = jnp.where(first_quarter, pltpu.roll(x, 96, 1), pltpu.roll(x, 32, 1))
    return x * cos + partner * sins


def _proj_kernel(*refs, layer, rope):
    it = iter(refs)
    x_ref, mod_ref, nw_ref, lbl_ref, w_hbm = (next(it) for _ in range(5))
    cos_ref, sin_ref = (next(it), next(it)) if rope else (None, None)
    p_ref, g_ref = next(it), next(it)
    w_ref = next(it)
    _stage_weight(pl.program_id(0), w_hbm, w_ref, STAGE_ROWS_W_IN)
    h = _rms_modulate(x_ref[...], nw_ref[1:2, :], mod_ref[3:4, :], mod_ref[4:5, :]).astype(BF16)

    if rope:
        lane = lax.broadcasted_iota(jnp.int32, (x_ref.shape[0], RET_DK), 1)
        first_quarter = (lane & 32) == 0

    def rot(t):
        if not rope:
            return t
        heads = [_rope(t[:, j:j + RET_DK], cos_ref[...], sin_ref[...], first_quarter)
                 for j in range(0, t.shape[1], RET_DK)]
        return jnp.concatenate(heads, axis=1)

    def silu(t):
        return t * _sigmoid(t)

    def log_forget(d):
        def fn(t, c):
            lg = lbl_ref[d, :, c:c + PROJ_CHUNK]
            e = jnp.exp(lg - jnp.max(lg, axis=0, keepdims=True))
            lb = jnp.sum(e[0:layer + 1, :], axis=0, keepdims=True) / jnp.sum(e, axis=0, keepdims=True)
            return jnp.log(lb + (1.0 - lb) * _sigmoid(t)) * LOG2E
        return fn

    parts = (
        (C_RQ, 512, p_ref, P_RQ, lambda t, c: rot(t)),
        (C_RK, 512, p_ref, P_RK, lambda t, c: rot(t * (RET_DK ** -0.5))),
        (C_RV, 1024, p_ref, P_RV, lambda t, c: t),
        (C_RG, 1024, p_ref, P_RG, lambda t, c: silu(t)),
        (C_HQ, 1024, p_ref, P_HQ, lambda t, c: silu(t) * (HG_DK ** -0.5)),
        (C_HFF, 1024, g_ref, 0, log_forget(0)),
        (C_HFB, 1024, g_ref, 1024, log_forget(1)),
        (C_HI, 1024, p_ref, P_HI, lambda t, c: t),
        (C_HOG, 1024, p_ref, P_HOG, lambda t, c: silu(t)),
        (C_GR, 1024, p_ref, P_GR, lambda t, c: _sigmoid(t)),
        (C_GH, 1024, p_ref, P_GH, lambda t, c: _sigmoid(t)),
    )
    for c in range(0, 1024, PROJ_CHUNK):
        for w0, width, o_ref, o0, fn in parts:
            if c < width:
                t = _dot(h, w_ref[:, w0 + c:w0 + c + PROJ_CHUNK])
                o_ref[:, o0 + c:o0 + c + PROJ_CHUNK] = fn(t, c).astype(o_ref.dtype)


def _proj_call(x, mod3, norm_w, lb_logits, w_in, rope_tabs, rope_blk_fn, *, layer, mod_row_fn, tm, name):
    n = x.shape[0]
    slots = lb_logits.shape[1]
    row = lambda w: pl.BlockSpec((tm, w), lambda i: (i, 0))
    outs = [(P_WIDTH, BF16), (2 * HG_HEADS * HG_DK, F32)]
    in_specs = [
        row(D_MODEL),
        pl.BlockSpec((None, N_MOD, D_MODEL), lambda i: (mod_row_fn(i, tm), 0, 0)),
        pl.BlockSpec((3, D_MODEL), lambda i: (0, 0)),
        pl.BlockSpec((2, slots, 1024), lambda i: (0, 0, 0)),
        _HBM,
    ]
    args = [x, mod3, norm_w, lb_logits, w_in]
    if rope_tabs is not None:
        in_specs += [pl.BlockSpec((tm, RET_DK), lambda i: (rope_blk_fn(i, tm), 0))] * 2
        args += list(rope_tabs)
    return pl.pallas_call(
        functools.partial(_proj_kernel, layer=layer, rope=rope_tabs is not None),
        grid=(n // tm,),
        in_specs=in_specs,
        out_specs=[row(w) for w, _ in outs],
        out_shape=[jax.ShapeDtypeStruct((n, w), dt) for w, dt in outs],
        scratch_shapes=[_bf16_copy(w_in)],
        compiler_params=_params(("arbitrary",)),
        name=name,
    )(*args)


def _log_sigmoid(x):
    return -(jnp.maximum(-x, 0.0) + jnp.log1p(jnp.exp(-jnp.abs(x))))


def _half_acc_slices(i, n, blk, finalize):
    if finalize:
        starts = ((i - n // 2) * blk, (n - 1 - i) * blk)
    else:
        starts = (i * blk, (n // 2 - 1 - i) * blk)
    return tuple(pl.ds(pl.multiple_of(s, blk), blk) for s in starts)


def _run_scans(scans):
    n = scans[0][0]
    assert all(s[0] == n for s in scans)

    def half(finalize):
        def body(i, carry):
            for _, step, _ in scans:
                step(i, finalize)
            return carry
        return body

    lax.fori_loop(0, n // 2, half(False), 0, unroll=min(SCAN_UNROLL, n // 2))
    lax.fori_loop(n // 2, n, half(True), 0, unroll=min(SCAN_UNROLL, n // 2))
    for _, _, finish in scans:
        finish()


def _scan_kernel(*refs, parts):
    ins = [n for _, n, _, _ in parts]
    outs = [n for _, _, n, _ in parts]
    offs = [0, sum(ins), sum(ins) + sum(outs)]
    scans = []
    for setup, n_in, n_out, n_scr in parts:
        pick = [refs[o:o + c] for o, c in zip(offs, (n_in, n_out, n_scr))]
        offs = [o + c for o, c in zip(offs, (n_in, n_out, n_scr))]
        scans.append(setup(*pick))
    _run_scans(scans)


def _ret_setup(in_refs, out_refs, scratch, *, L, has_s0, out_state):
    it = iter(in_refs)
    dec_ref = next(it)
    q_ref, k_ref, v_ref, gate_ref = next(it), next(it), next(it), next(it)
    s0_ref = next(it) if has_s0 else None
    o_ref = out_refs[0]
    st_ref = out_refs[1] if out_state else None
    of_acc, ob_acc, st_scr, cst = scratch

    C = SCAN_BLOCK
    P = RET_PACK
    n = L // C
    pair = pl.program_id(1)
    ii = lax.broadcasted_iota(jnp.int32, (C, C), 0).astype(F32)
    jj = lax.broadcasted_iota(jnp.int32, (C, C), 1).astype(F32)
    diff = ii - jj

    @pl.when(jnp.logical_and(pl.program_id(0) == 0, pair == 0))
    def _():
        for hd in range(RET_HEADS):
            lgf = _log_sigmoid(jnp.full((C, C), dec_ref[0, hd], F32))
            lgb = _log_sigmoid(jnp.full((C, C), dec_ref[1, hd], F32))
            cst[hd, 0] = (jnp.where(diff >= 0, jnp.exp(jnp.maximum(diff, 0.0) * lgf), 0.0)
                          + jnp.where(diff <= 0, jnp.exp(jnp.maximum(-diff, 0.0) * lgb), 0.0))
            cst[hd, 1] = jnp.exp((ii + 1.0) * lgf)
            cst[hd, 2] = jnp.exp((C - 1.0 - ii) * lgf)
            cst[hd, 3] = jnp.exp((C - ii) * lgb)
            cst[hd, 4] = jnp.exp(ii * lgb)

    cds = []
    for h in range(P):
        hd = pair * P + h
        cds.append([jnp.exp(C * _log_sigmoid(jnp.full((1, RET_DV), dec_ref[d, hd], F32))) for d in range(2)])
        for d in range(2):
            st_scr[2 * h + d] = s0_ref[d, h] if has_s0 else jnp.zeros((RET_DK, RET_DV), F32)

    def step(i, finalize):
        sls = (pl.ds(pl.multiple_of(i * C, C), C), pl.ds(pl.multiple_of((n - 1 - i) * C, C), C))
        acc_sls = _half_acc_slices(i, n, C, finalize)
        chains = []
        for h in range(P):
            ql = slice(h * RET_DK, (h + 1) * RET_DK)
            vl = slice(h * RET_DV, (h + 1) * RET_DV)
            for d in range(2):
                chains.append(dict(h=h, d=d, vl=vl, q=q_ref[sls[d], ql], k=k_ref[sls[d], ql],
                                   v=v_ref[sls[d], vl]))
        for c in chains:
            if c["d"] == 0:
                c["s"] = _dot_nt(c["q"], c["k"])
        for c in chains:
            h, d, v, vl = c["h"], c["d"], c["v"], c["vl"]
            hd = pair * P + h
            st = st_scr[2 * h + d]
            qd = (c["q"].astype(F32) * cst[hd, 1 + 2 * d]).astype(BF16)
            if d == 0:
                p = (c["s"] * cst[hd, 0]).astype(BF16)
                o = _dot(jnp.concatenate([p, qd], axis=1), jnp.concatenate([v, st.astype(BF16)], axis=0))
            else:
                o = _dot(qd, st.astype(BF16))
            kd = (c["k"].astype(F32) * cst[hd, 2 + 2 * d]).astype(BF16)
            st_scr[2 * h + d] = cds[h][d] * st + _dot_tn(kd, v)
            if finalize:
                o = o + (ob_acc, of_acc)[d][acc_sls[d], vl]
                ms = jnp.mean(o * o, axis=-1, keepdims=True)
                o_ref[sls[d], vl] = (o * lax.rsqrt(ms + EPS) * gate_ref[sls[d], vl]).astype(BF16)
            else:
                (of_acc, ob_acc)[d][acc_sls[d], vl] = o

    def finish():
        if out_state:
            for h in range(P):
                for d in range(2):
                    st_ref[d, h] = st_scr[2 * h + d]

    return n, step, finish


def _ret_part(dec, acts, s0, *, layer, row0, B, L, out_state):
    has_s0 = s0 is not None
    H = RET_HEADS
    P = RET_PACK
    r0 = row0 // L
    st_spec = pl.BlockSpec((None, 2, P, RET_DK, RET_DV), lambda b, h: (b, 0, h, 0, 0))
    cols = lambda width, off: pl.BlockSpec((L, width), lambda b, h: (r0 + b, off // width + h))
    in_specs = [
        pl.BlockSpec(memory_space=pltpu.SMEM),
        cols(P * RET_DK, P_RQ), cols(P * RET_DK, P_RK), cols(P * RET_DV, P_RV), cols(P * RET_DV, P_RG),
    ]
    args = [dec, acts, acts, acts, acts]
    if has_s0:
        in_specs.append(pl.BlockSpec((None, None, 2, P, RET_DK, RET_DV), lambda b, h: (b, layer, 0, h, 0, 0)))
        args.append(s0)
    out_specs = [pl.BlockSpec((L, P * RET_DV), lambda b, h: (b, h))]
    out_shape = [jax.ShapeDtypeStruct((B * L, H * RET_DV), BF16)]
    if out_state:
        out_specs.append(st_spec)
        out_shape.append(jax.ShapeDtypeStruct((B, 2, H, RET_DK, RET_DV), F32))
    scratch = [pltpu.VMEM((L // 2, P * RET_DV), F32), pltpu.VMEM((L // 2, P * RET_DV), F32),
               pltpu.VMEM((2 * P, RET_DK, RET_DV), F32), pltpu.VMEM((H, 5, SCAN_BLOCK, SCAN_BLOCK), F32)]
    return dict(setup=functools.partial(_ret_setup, L=L, has_s0=has_s0, out_state=out_state),
                grid=(B, H // P), in_specs=in_specs, args=args, out_specs=out_specs, out_shape=out_shape,
                scratch=scratch)


def _scan_call(parts, name):
    grid = parts[0]["grid"]
    assert all(p["grid"] == grid for p in parts)
    kern = functools.partial(_scan_kernel, parts=[(p["setup"], len(p["in_specs"]), len(p["out_specs"]),
                                                   len(p["scratch"])) for p in parts])
    cat = lambda key: [x for p in parts for x in p[key]]
    res = pl.pallas_call(
        kern,
        grid=grid,
        in_specs=cat("in_specs"),
        out_specs=cat("out_specs"),
        out_shape=cat("out_shape"),
        scratch_shapes=cat("scratch"),
        compiler_params=_params(("arbitrary", "arbitrary")),
        name=name,
    )(*cat("args"))
    out, k = [], 0
    for p in parts:
        out.append(res[k:k + len(p["out_specs"])])
        k += len(p["out_specs"])
    return out


def _gla_blocks(chains, tri_ref, code_ref):
    T = HG_BLOCK
    K = HG_DK

    def cumulate(c):
        g2 = c["g"]
        c["k"] = 1.0 - jnp.exp2(g2)
        hi = g2.astype(BF16)
        lo = (g2 - hi.astype(F32)).astype(BF16)
        tri = tri_ref[int(c["rev"])]
        c["b"] = _dot(tri, hi) + _dot(tri, lo)

    def leaf(c):
        bl = c["b"].reshape(T // HG_LEAF, HG_LEAF, K)
        r = HG_LEAF // 2 if c["rev"] else HG_LEAF // 2 - 1
        d = jnp.clip(bl - bl[:, r:r + 1, :], -EXP2_CLAMP, EXP2_CLAMP).reshape(T, K)
        e = jnp.exp2(d)
        p = _dot_nt((c["q"] * e).astype(BF16), (c["k"] / e).astype(BF16))
        c["a"] = jnp.where(code_ref[int(c["rev"])] == 0, p, 0.0)

    def level(c, lv, m):
        seg = 2 * m
        ng = T // seg
        b3, q3, k3 = (c[n].reshape(ng, seg, K) for n in ("b", "q", "k"))
        if c["rev"]:
            qs, ks, r = slice(0, m), slice(m, seg), m
        else:
            qs, ks, r = slice(m, seg), slice(0, m), m - 1
        ref = b3[:, r:r + 1, :]
        qm = (q3[:, qs, :] * jnp.exp2(b3[:, qs, :] - ref)).reshape(ng * m, K).astype(BF16)
        kz = k3[:, ks, :] * jnp.exp2(ref - b3[:, ks, :])
        zeros = jnp.zeros((ng, m, K), F32)
        kfull = jnp.concatenate([zeros, kz] if c["rev"] else [kz, zeros], axis=1)
        p = _dot_nt(qm, kfull.reshape(T, K).astype(BF16)).reshape(ng, m, T)
        a3 = c["a"].reshape(ng, seg, T)
        c3 = code_ref[int(c["rev"])].reshape(ng, seg, T)
        upd = jnp.where(c3[:, qs, :] == lv, p, a3[:, qs, :])
        parts = [upd, a3[:, ks, :]] if c["rev"] else [a3[:, ks, :], upd]
        c["a"] = jnp.concatenate(parts, axis=1).reshape(T, T)

    def finish(c):
        b, q, k, v, st = c["b"], c["q"], c["k"], c["v"], c["st"]
        o = _dot(c["a"].astype(BF16), v) + _dot_nt((q * jnp.exp2(b)).astype(BF16), st.astype(BF16))
        end_row = 0 if c["rev"] else T - 1
        b1 = b.reshape(1, T, K)
        b_end = b1[:, end_row:end_row + 1, :]
        kd = (k * jnp.exp2(b_end - b1).reshape(T, K)).astype(BF16)
        c["out"] = (o, st * jnp.exp2(b_end.reshape(1, K)) + _dot_tn(v, kd))

    for c in chains:
        cumulate(c)
    for c in chains:
        leaf(c)
    for lv, m in HG_LEVELS:
        for c in chains:
            level(c, lv, m)
    for c in chains:
        finish(c)
    return [c["out"] for c in chains]


def _hgrn_setup(in_refs, out_refs, scratch, *, L, P, has_s0, out_state):
    it = iter(in_refs)
    q_ref, g_f_ref, g_b_ref, v_ref, gate_ref, nw_ref = (next(it) for _ in range(6))
    s0_ref = next(it) if has_s0 else None
    o_ref = out_refs[0]
    st_ref = out_refs[1] if out_state else None
    of_acc, ob_acc, st_scr, tri_ref, code_ref = scratch

    T = HG_BLOCK
    n = L // T

    @pl.when(jnp.logical_and(pl.program_id(0) == 0, pl.program_id(1) == 0))
    def _():
        ti = lax.broadcasted_iota(jnp.int32, (T, T), 0)
        si = lax.broadcasted_iota(jnp.int32, (T, T), 1)
        x = ti ^ si
        tri_ref[0] = (si <= ti).astype(BF16)
        tri_ref[1] = (si >= ti).astype(BF16)
        lvl = jnp.where(x < HG_LEAF, 0, HG_LEVELS[0][0] - 1 + sum((x >= m).astype(jnp.int32) for _, m in HG_LEVELS))
        code_ref[0] = jnp.where(si <= ti, lvl, -1)
        code_ref[1] = jnp.where(si >= ti, lvl, -1)

    for h in range(P):
        for d in range(2):
            st_scr[2 * h + d] = s0_ref[d, h].T if has_s0 else jnp.zeros((HG_DV, HG_DK), F32)

    def step(i, finalize):
        sls = (pl.ds(pl.multiple_of(i * T, T), T), pl.ds(pl.multiple_of((n - 1 - i) * T, T), T))
        acc_sls = _half_acc_slices(i, n, T, finalize)
        chains = []
        for h in range(P):
            lanes = slice(h * HG_DK, (h + 1) * HG_DK)
            for d, g_ref in enumerate((g_f_ref, g_b_ref)):
                chains.append(dict(q=q_ref[sls[d], lanes].astype(F32), g=g_ref[sls[d], lanes],
                                   v=v_ref[sls[d], lanes],
                                   st=st_scr[2 * h + d], rev=bool(d)))
        res = _gla_blocks(chains, tri_ref, code_ref)
        for h in range(P):
            lanes = slice(h * HG_DK, (h + 1) * HG_DK)
            for d in range(2):
                o, st = res[2 * h + d]
                st_scr[2 * h + d] = st
                if finalize:
                    o = o + (ob_acc, of_acc)[d][acc_sls[d], lanes]
                    ms = jnp.mean(o * o, axis=-1, keepdims=True)
                    o = o * lax.rsqrt(ms + EPS) * nw_ref[:, lanes]
                    o_ref[sls[d], lanes] = (o * gate_ref[sls[d], lanes]).astype(BF16)
                else:
                    (of_acc, ob_acc)[d][acc_sls[d], lanes] = o

    def finish():
        if out_state:
            for h in range(P):
                for d in range(2):
                    st_ref[d, h] = st_scr[2 * h + d].T

    return n, step, finish


def _hgrn_part(acts, g, nw, s0, *, layer, row0, B, L, out_state):
    has_s0 = s0 is not None
    H = HG_HEADS
    P = HG_PACK_SHORT if L <= HG_SHORT_SEQ else HG_PACK_LONG
    W = P * HG_DK
    r0 = row0 // L
    blk = lambda off: pl.BlockSpec((L, W), lambda b, h: (r0 + b, off // W + h))
    st_spec = pl.BlockSpec((None, 2, P, HG_DK, HG_DV), lambda b, h: (b, 0, h, 0, 0))
    in_specs = [blk(P_HQ), blk(0), blk(H * HG_DK), blk(P_HI), blk(P_HOG),
                pl.BlockSpec((1, W), lambda b, h: (0, h))]
    args = [acts, g, g, acts, acts, nw]
    if has_s0:
        in_specs.append(pl.BlockSpec((None, None, 2, P, HG_DK, HG_DV), lambda b, h: (b, layer, 0, h, 0, 0)))
        args.append(s0)
    out_specs = [pl.BlockSpec((L, W), lambda b, h: (b, h))]
    out_shape = [jax.ShapeDtypeStruct((B * L, H * HG_DV), BF16)]
    if out_state:
        out_specs.append(st_spec)
        out_shape.append(jax.ShapeDtypeStruct((B, 2, H, HG_DK, HG_DV), F32))
    scratch = [pltpu.VMEM((L // 2, W), F32), pltpu.VMEM((L // 2, W), F32),
               pltpu.VMEM((2 * P, HG_DV, HG_DK), F32), pltpu.VMEM((2, HG_BLOCK, HG_BLOCK), BF16),
               pltpu.VMEM((2, HG_BLOCK, HG_BLOCK), jnp.int32)]
    return dict(setup=functools.partial(_hgrn_setup, L=L, P=P, has_s0=has_s0, out_state=out_state),
                grid=(B, H // P), in_specs=in_specs, args=args, out_specs=out_specs, out_shape=out_shape,
                scratch=scratch)


def _merge_kernel(x_ref, mod_ref, oret_a, oret_b, ohg_a, ohg_b, gr_ref, gh_ref, wr_hbm, wh_hbm, wo_hbm, o_ref,
                  wr_ref, wh_ref, wo_ref, *, n_first):
    i = pl.program_id(0)
    for w_hbm, w_ref in ((wr_hbm, wr_ref), (wh_hbm, wh_ref), (wo_hbm, wo_ref)):
        _stage_weight(i, w_hbm, w_ref, STAGE_ROWS_MERGE)
    oret = jnp.where(i < n_first, oret_a[...], oret_b[...])
    ohg = jnp.where(i < n_first, ohg_a[...], ohg_b[...])
    y = gr_ref[...] * _dot(oret, wr_ref[...]) + gh_ref[...] * _dot(ohg, wh_ref[...])
    out = _dot(y.astype(BF16), wo_ref[...])
    o_ref[...] = x_ref[...] + mod_ref[5:6, :] * out


def _merge_call(x, mod3, orets, ohgs, acts, wr, wh, wo, *, mod_row_fn, tm, name):
    n = x.shape[0]
    n_first = orets[0].shape[0] // tm
    row = pl.BlockSpec((tm, D_MODEL), lambda i: (i, 0))
    gate = lambda off: pl.BlockSpec((tm, D_MODEL), lambda i: (i, off // D_MODEL))
    first = pl.BlockSpec((tm, D_MODEL), lambda i: (jnp.minimum(i, n_first - 1), 0))
    second = pl.BlockSpec((tm, D_MODEL), lambda i: (jnp.maximum(i - n_first, 0), 0))
    return pl.pallas_call(
        functools.partial(_merge_kernel, n_first=n_first),
        grid=(n // tm,),
        in_specs=[row, pl.BlockSpec((None, N_MOD, D_MODEL), lambda i: (mod_row_fn(i, tm), 0, 0)),
                  first, second, first, second, gate(P_GR), gate(P_GH), _HBM, _HBM, _HBM],
        out_specs=row,
        out_shape=jax.ShapeDtypeStruct((n, D_MODEL), F32),
        scratch_shapes=[_bf16_copy(wr), _bf16_copy(wh), _bf16_copy(wo)],
        compiler_params=_params(("arbitrary",)),
        name=name,
    )(x, mod3, *orets, *ohgs, acts, acts, wr, wh, wo)


def _rope_tables(n_tokens, n_identity):
    pos = np.arange(n_tokens)
    r = (pos // GRID_W).astype(np.float32)
    cl = (pos % GRID_W).astype(np.float32)
    quarter = RET_DK // 4
    inv_freq = (np.float32(ROPE_BASE) ** (-np.arange(quarter, dtype=np.float32) / np.float32(quarter)))
    inv_freq = inv_freq.astype(np.float32)
    ang_r = (r[:, None] * inv_freq[None, :]).astype(np.float64)
    ang_c = (cl[:, None] * inv_freq[None, :]).astype(np.float64)
    cos = np.concatenate([np.cos(ang_r)] * 2 + [np.cos(ang_c)] * 2, axis=-1)
    sins = np.concatenate([-np.sin(ang_r), np.sin(ang_r), -np.sin(ang_c), np.sin(ang_c)], axis=-1)
    cos = np.concatenate([cos, np.ones((n_identity, RET_DK))], axis=0).astype(np.float32)
    sins = np.concatenate([sins, np.zeros((n_identity, RET_DK))], axis=0).astype(np.float32)
    return jnp.asarray(cos), jnp.asarray(sins)


def kernel(x_prompt, x_sample, state_ret, state_hgrn, c, c_ctx, ada_w, ada_b, norm_w, ffn1_w13, ffn1_w2,
           ffn2_w13, ffn2_w2, w_in, ret_decay, hg_lb_logits, hg_norm_w, w_ret_proj, w_hg_proj, w_o,
           final_norm_w):
    depth = ada_w.shape[0]
    assert depth == 1
    l = 0
    bp, lp, _ = x_prompt.shape
    bs, ls, _ = x_sample.shape
    n_ctx = bp * lp
    assert n_ctx % ls == 0 and ls % TM_FFN == 0 and ls % TM_PROJ == 0 and ls % TM_MERGE == 0

    cvec = jnp.concatenate([c_ctx[None, :], c, jnp.zeros((8 - 1 - bs, D_MODEL), F32)], axis=0)
    mod = _mod_call(cvec, ada_w[l], ada_b[l][None, :])
    mod3 = mod.reshape(8, N_MOD, D_MODEL)

    def mod_row(i, tm):
        return jnp.where(i < n_ctx // tm, 0, 1 + (i * tm - n_ctx) // ls)

    def rope_blk(i, tm):
        return jnp.where(i < n_ctx // tm, ls // tm, (i - n_ctx // tm) % (ls // tm))

    nw, fw = norm_w[l], final_norm_w[None, :]
    dec, hg_nw = ret_decay[l], hg_norm_w[l][None, :]

    x = _ffn_call((x_prompt.reshape(n_ctx, D_MODEL), x_sample.reshape(bs * ls, D_MODEL)), mod3, nw, fw,
                  ffn1_w13[l], ffn1_w2[l], n_out_split=None, mod_base=0, norm_row=0, final=False,
                  mod_row_fn=mod_row, tm=TM_FFN, name="ffn1")
    acts, g = _proj_call(x, mod3, nw, hg_lb_logits, w_in[l], _rope_tables(ls, TM_PROJ), rope_blk, layer=l,
                         mod_row_fn=mod_row, tm=TM_PROJ, name="proj")
    (ret_ctx, s_ret), (hg_ctx, s_hg) = _scan_call(
        [_ret_part(dec, acts, None, layer=l, row0=0, B=bp, L=lp, out_state=True),
         _hgrn_part(acts, g, hg_nw, None, layer=l, row0=0, B=bp, L=lp, out_state=True)], "scan_ctx")
    (ret_lat,), = _scan_call([_ret_part(dec, acts, state_ret, layer=l, row0=n_ctx, B=bs, L=ls, out_state=False)],
                             "ret_lat")
    (hg_lat,), = _scan_call([_hgrn_part(acts, g, hg_nw, state_hgrn, layer=l, row0=n_ctx, B=bs, L=ls,
                                        out_state=False)], "hgrn_lat")
    x = _merge_call(x, mod3, (ret_ctx, ret_lat), (hg_ctx, hg_lat), acts, w_ret_proj[l], w_hg_proj[l], w_o[l],
                    mod_row_fn=mod_row, tm=TM_MERGE, name="merge")
    yp, ys = _ffn_call((x,), mod3, nw, fw, ffn2_w13[l], ffn2_w2[l], n_out_split=n_ctx, mod_base=6,
                       norm_row=2, final=True, mod_row_fn=mod_row, tm=TM_FFN, name="ffn2")
    return (yp.reshape(bp, lp, D_MODEL), ys.reshape(bs, ls, D_MODEL), s_ret[:, None], s_hg[:, None])
```

```python
import functools

import numpy as np
import jax
import jax.numpy as jnp
from jax import lax
from jax.experimental import pallas as pl
from jax.experimental.pallas import tpu as pltpu

F32 = jnp.float32
BF16 = jnp.bfloat16

D_MODEL = 1024
GRID_W = 64
RET_HEADS = 4
RET_DK = 128
RET_DV = 256
HG_HEADS = 8
HG_DK = 128
HG_DV = 128
D_FF = 2816
N_MOD = 9
ROPE_BASE = 10000.0
EPS = 1e-6

C_RQ, C_RK, C_RV, C_RG, C_HQ, C_HFF, C_HFB, C_HI, C_HOG, C_GR, C_GH = (
    0, 512, 1024, 2048, 3072, 4096, 5120, 6144, 7168, 8192, 9216)
IN_WIDTH = 10240
P_RQ, P_RK, P_RV, P_RG, P_HQ, P_HI, P_HOG, P_GR, P_GH = 0, 512, 1024, 2048, 3072, 4096, 5120, 6144, 7168
P_WIDTH = 8192

VMEM_LIMIT_BYTES = 56 * 1024 * 1024

TM_FFN = 512
TM_PROJ = 512
TM_MERGE = 512
STAGE_SLOTS = 4
STAGE_ROWS_W13, STAGE_ROWS_W2, STAGE_ROWS_W_IN, STAGE_ROWS_MERGE = 128, 256, 16, 512
MOD_TILE = 2304
SCAN_BLOCK = 128
RET_PACK = 4
SCAN_UNROLL = 8
HG_PACK_LONG = 4
HG_PACK_SHORT = 8
HG_SHORT_SEQ = 512
FF_CHUNK = 256
PROJ_CHUNK = 256
LOG2E = 1.4426950408889634
EXP2_CLAMP = 115.0
HG_BLOCK = 128
HG_LEAF = 16
HG_LEVELS = ((4, 16), (5, 32), (6, 64))


def _dot(a, b):
    return jnp.dot(a, b, preferred_element_type=F32)


def _dot_nt(a, b):
    return lax.dot_general(a, b, (((1,), (1,)), ((), ())), preferred_element_type=F32)


def _dot_tn(a, b):
    return lax.dot_general(a, b, (((0,), (0,)), ((), ())), preferred_element_type=F32)


def _sigmoid(x):
    return 0.5 * jnp.tanh(0.5 * x) + 0.5


def _rms(x, w):
    ms = jnp.mean(x * x, axis=-1, keepdims=True)
    return x * lax.rsqrt(ms + EPS) * w


def _rms_modulate(x, w, shift, scale):
    ms = jnp.mean(x * x, axis=-1, keepdims=True)
    return x * lax.rsqrt(ms + EPS) * (w * (1.0 + scale)) + shift


def _params(sem):
    return pltpu.CompilerParams(dimension_semantics=sem, vmem_limit_bytes=VMEM_LIMIT_BYTES)


def _stage_weight(step, w_hbm, w_vmem, rows):
    r, c = w_hbm.shape
    assert r % rows == 0 and rows % 16 == 0
    n_chunks = r // rows
    ahead = STAGE_SLOTS - 1

    def run(stage, sem):
        def copy(k):
            slot = k % STAGE_SLOTS
            return pltpu.make_async_copy(w_hbm.at[pl.ds(k * rows, rows), :], stage.at[slot], sem.at[slot])

        for k in range(min(ahead, n_chunks)):
            copy(k).start()

        def body(k, carry):
            @pl.when(k + ahead < n_chunks)
            def _():
                copy(k + ahead).start()

            copy(k).wait()
            w_vmem[pl.ds(pl.multiple_of(k * rows, rows), rows), :] = stage[k % STAGE_SLOTS].astype(BF16)
            return carry

        lax.fori_loop(0, n_chunks, body, 0)

    @pl.when(step == 0)
    def _():
        pl.run_scoped(run, pltpu.VMEM((STAGE_SLOTS, rows, c), F32), pltpu.SemaphoreType.DMA((STAGE_SLOTS,)))


_HBM = pl.BlockSpec(memory_space=pl.ANY)


def _bf16_copy(w):
    return pltpu.VMEM(w.shape, BF16)


def _mod_kernel(c_ref, w_ref, b_ref, o_ref):
    c = c_ref[...]
    s = (c * _sigmoid(c)).astype(BF16)
    o_ref[...] = _dot(s, w_ref[...].astype(BF16)) + b_ref[...]


def _mod_call(cvec, ada_w, ada_b):
    n = ada_w.shape[1]
    tn = MOD_TILE
    return pl.pallas_call(
        _mod_kernel,
        grid=(n // tn,),
        in_specs=[
            pl.BlockSpec((8, D_MODEL), lambda j: (0, 0)),
            pl.BlockSpec((D_MODEL, tn), lambda j: (0, j)),
            pl.BlockSpec((1, tn), lambda j: (0, j)),
        ],
        out_specs=pl.BlockSpec((8, tn), lambda j: (0, j)),
        out_shape=jax.ShapeDtypeStruct((8, n), F32),
        compiler_params=_params(("arbitrary",)),
        name="mod",
    )(cvec, ada_w, ada_b)


def _ffn_kernel(*refs, mod_base, norm_row, final, n_first, two_in, two_out):
    it = iter(refs)
    xa_ref = next(it)
    xb_ref = next(it) if two_in else None
    mod_ref, nw_ref, fw_ref, w13_hbm, w2_hbm = (next(it) for _ in range(5))
    oa_ref = next(it)
    ob_ref = next(it) if two_out else None
    w13_ref, w2_ref = next(it), next(it)
    i = pl.program_id(0)
    _stage_weight(i, w13_hbm, w13_ref, STAGE_ROWS_W13)
    _stage_weight(i, w2_hbm, w2_ref, STAGE_ROWS_W2)
    x = xa_ref[...]
    if two_in:
        x = jnp.where(i < n_first, x, xb_ref[...])
    h = _rms_modulate(x, nw_ref[norm_row:norm_row + 1, :], mod_ref[mod_base:mod_base + 1, :],
                      mod_ref[mod_base + 1:mod_base + 2, :]).astype(BF16)
    acc = None
    for c0 in range(0, D_FF, FF_CHUNK):
        a = _dot(h, w13_ref[:, c0:c0 + FF_CHUNK])
        b = _dot(h, w13_ref[:, D_FF + c0:D_FF + c0 + FF_CHUNK])
        act = (a * _sigmoid(a) * b).astype(BF16)
        part = _dot(act, w2_ref[c0:c0 + FF_CHUNK, :])
        acc = part if acc is None else acc + part
    xo = x + (0.5 * mod_ref[mod_base + 2:mod_base + 3, :]) * acc
    if final:
        xo = _rms(xo, fw_ref[...])
    if two_out:
        @pl.when(i < n_first)
        def _():
            oa_ref[...] = xo

        @pl.when(i >= n_first)
        def _():
            ob_ref[...] = xo
    else:
        oa_ref[...] = xo


def _ffn_call(xs, mod3, norm_w, final_w, w13, w2, *, n_out_split, mod_base, norm_row, final, mod_row_fn, tm,
              name):
    two_in = len(xs) == 2
    two_out = n_out_split is not None
    n = sum(x.shape[0] for x in xs)
    n_first = (xs[0].shape[0] if two_in else n_out_split if two_out else n) // tm
    first = pl.BlockSpec((tm, D_MODEL), lambda i: (jnp.minimum(i, n_first - 1), 0))
    second = pl.BlockSpec((tm, D_MODEL), lambda i: (jnp.maximum(i - n_first, 0), 0))
    whole = pl.BlockSpec((tm, D_MODEL), lambda i: (i, 0))
    kern = functools.partial(_ffn_kernel, mod_base=mod_base, norm_row=norm_row, final=final, n_first=n_first,
                             two_in=two_in, two_out=two_out)
    if two_out:
        out_specs = [first, second]
        out_shape = [jax.ShapeDtypeStruct((n_out_split, D_MODEL), F32),
                     jax.ShapeDtypeStruct((n - n_out_split, D_MODEL), F32)]
    else:
        out_specs = whole
        out_shape = jax.ShapeDtypeStruct((n, D_MODEL), F32)
    return pl.pallas_call(
        kern,
        grid=(n // tm,),
        in_specs=([first, second] if two_in else [whole]) + [
            pl.BlockSpec((None, N_MOD, D_MODEL), lambda i: (mod_row_fn(i, tm), 0, 0)),
            pl.BlockSpec((3, D_MODEL), lambda i: (0, 0)),
            pl.BlockSpec((1, D_MODEL), lambda i: (0, 0)),
            _HBM,
            _HBM,
        ],
        out_specs=out_specs,
        out_shape=out_shape,
        scratch_shapes=[_bf16_copy(w13), _bf16_copy(w2)],
        compiler_params=_params(("arbitrary",)),
        name=name,
    )(*xs, mod3, norm_w, final_w, w13, w2)


def _rope(x, cos, sins, first_quarter):
    partner = jnp.where(first_quarter, pltpu.roll(x, 96, 1), pltpu.roll(x, 32, 1))
    return x * cos + partner * sins


def _proj_kernel(*refs, layer, rope):
    it = iter(refs)
    x_ref, mod_ref, nw_ref, lbl_ref, w_hbm = (next(it) for _ in range(5))
    cos_ref, sin_ref = (next(it), next(it)) if rope else (None, None)
    p_ref, g_ref = next(it), next(it)
    w_ref = next(it)
    _stage_weight(pl.program_id(0), w_hbm, w_ref, STAGE_ROWS_W_IN)
    h = _rms_modulate(x_ref[...], nw_ref[1:2, :], mod_ref[3:4, :], mod_ref[4:5, :]).astype(BF16)

    if rope:
        lane = lax.broadcasted_iota(jnp.int32, (x_ref.shape[0], RET_DK), 1)
        first_quarter = (lane & 32) == 0

    def rot(t):
        if not rope:
            return t
        heads = [_rope(t[:, j:j + RET_DK], cos_ref[...], sin_ref[...], first_quarter)
                 for j in range(0, t.shape[1], RET_DK)]
        return jnp.concatenate(heads, axis=1)

    def silu(t):
        return t * _sigmoid(t)

    def log_forget(d):
        def fn(t, c):
            lg = lbl_ref[d, :, c:c + PROJ_CHUNK]
            e = jnp.exp(lg - jnp.max(lg, axis=0, keepdims=True))
            lb = jnp.sum(e[0:layer + 1, :], axis=0, keepdims=True) / jnp.sum(e, axis=0, keepdims=True)
            return jnp.log(lb + (1.0 - lb) * _sigmoid(t)) * LOG2E
        return fn

    parts = (
        (C_RQ, 512, p_ref, P_RQ, lambda t, c: rot(t)),
        (C_RK, 512, p_ref, P_RK, lambda t, c: rot(t * (RET_DK ** -0.5))),
        (C_RV, 1024, p_ref, P_RV, lambda t, c: t),
        (C_RG, 1024, p_ref, P_RG, lambda t, c: silu(t)),
        (C_HQ, 1024, p_ref, P_HQ, lambda t, c: silu(t) * (HG_DK ** -0.5)),
        (C_HFF, 1024, g_ref, 0, log_forget(0)),
        (C_HFB, 1024, g_ref, 1024, log_forget(1)),
        (C_HI, 1024, p_ref, P_HI, lambda t, c: t),
        (C_HOG, 1024, p_ref, P_HOG, lambda t, c: silu(t)),
        (C_GR, 1024, p_ref, P_GR, lambda t, c: _sigmoid(t)),
        (C_GH, 1024, p_ref, P_GH, lambda t, c: _sigmoid(t)),
    )
    for c in range(0, 1024, PROJ_CHUNK):
        for w0, width, o_ref, o0, fn in parts:
            if c < width:
                t = _dot(h, w_ref[:, w0 + c:w0 + c + PROJ_CHUNK])
                o_ref[:, o0 + c:o0 + c + PROJ_CHUNK] = fn(t, c).astype(o_ref.dtype)


def _proj_call(x, mod3, norm_w, lb_logits, w_in, rope_tabs, rope_blk_fn, *, layer, mod_row_fn, tm, name):
    n = x.shape[0]
    slots = lb_logits.shape[1]
    row = lambda w: pl.BlockSpec((tm, w), lambda i: (i, 0))
    outs = [(P_WIDTH, BF16), (2 * HG_HEADS * HG_DK, F32)]
    in_specs = [
        row(D_MODEL),
        pl.BlockSpec((None, N_MOD, D_MODEL), lambda i: (mod_row_fn(i, tm), 0, 0)),
        pl.BlockSpec((3, D_MODEL), lambda i: (0, 0)),
        pl.BlockSpec((2, slots, 1024), lambda i: (0, 0, 0)),
        _HBM,
    ]
    args = [x, mod3, norm_w, lb_logits, w_in]
    if rope_tabs is not None:
        in_specs += [pl.BlockSpec((tm, RET_DK), lambda i: (rope_blk_fn(i, tm), 0))] * 2
        args += list(rope_tabs)
    return pl.pallas_call(
        functools.partial(_proj_kernel, layer=layer, rope=rope_tabs is not None),
        grid=(n // tm,),
        in_specs=in_specs,
        out_specs=[row(w) for w, _ in outs],
        out_shape=[jax.ShapeDtypeStruct((n, w), dt) for w, dt in outs],
        scratch_shapes=[_bf16_copy(w_in)],
        compiler_params=_params(("arbitrary",)),
        name=name,
    )(*args)


def _log_sigmoid(x):
    return -(jnp.maximum(-x, 0.0) + jnp.log1p(jnp.exp(-jnp.abs(x))))


def _half_acc_slices(i, n, blk, finalize):
    if finalize:
        starts = ((i - n // 2) * blk, (n - 1 - i) * blk)
    else:
        starts = (i * blk, (n // 2 - 1 - i) * blk)
    return tuple(pl.ds(pl.multiple_of(s, blk), blk) for s in starts)


def _run_scans(scans):
    n = scans[0][0]
    assert all(s[0] == n for s in scans)

    def half(finalize):
        def body(i, carry):
            for _, step, _ in scans:
                step(i, finalize)
            return carry
        return body

    lax.fori_loop(0, n // 2, half(False), 0, unroll=min(SCAN_UNROLL, n // 2))
    lax.fori_loop(n // 2, n, half(True), 0, unroll=min(SCAN_UNROLL, n // 2))
    for _, _, finish in scans:
        finish()


def _scan_kernel(*refs, parts):
    ins = [n for _, n, _, _ in parts]
    outs = [n for _, _, n, _ in parts]
    offs = [0, sum(ins), sum(ins) + sum(outs)]
    scans = []
    for setup, n_in, n_out, n_scr in parts:
        pick = [refs[o:o + c] for o, c in zip(offs, (n_in, n_out, n_scr))]
        offs = [o + c for o, c in zip(offs, (n_in, n_out, n_scr))]
        scans.append(setup(*pick))
    _run_scans(scans)


def _ret_setup(in_refs, out_refs, scratch, *, L, has_s0, out_state):
    it = iter(in_refs)
    dec_ref = next(it)
    q_ref, k_ref, v_ref, gate_ref = next(it), next(it), next(it), next(it)
    s0_ref = next(it) if has_s0 else None
    o_ref = out_refs[0]
    st_ref = out_refs[1] if out_state else None
    of_acc, ob_acc, st_scr, cst = scratch

    C = SCAN_BLOCK
    P = RET_PACK
    n = L // C
    pair = pl.program_id(1)
    ii = lax.broadcasted_iota(jnp.int32, (C, C), 0).astype(F32)
    jj = lax.broadcasted_iota(jnp.int32, (C, C), 1).astype(F32)
    diff = ii - jj

    @pl.when(jnp.logical_and(pl.program_id(0) == 0, pair == 0))
    def _():
        for hd in range(RET_HEADS):
            lgf = _log_sigmoid(jnp.full((C, C), dec_ref[0, hd], F32))
            lgb = _log_sigmoid(jnp.full((C, C), dec_ref[1, hd], F32))
            cst[hd, 0] = (jnp.where(diff >= 0, jnp.exp(jnp.maximum(diff, 0.0) * lgf), 0.0)
                          + jnp.where(diff <= 0, jnp.exp(jnp.maximum(-diff, 0.0) * lgb), 0.0))
            cst[hd, 1] = jnp.exp((ii + 1.0) * lgf)
            cst[hd, 2] = jnp.exp((C - 1.0 - ii) * lgf)
            cst[hd, 3] = jnp.exp((C - ii) * lgb)
            cst[hd, 4] = jnp.exp(ii * lgb)

    cds = []
    for h in range(P):
        hd = pair * P + h
        cds.append([jnp.exp(C * _log_sigmoid(jnp.full((1, RET_DV), dec_ref[d, hd], F32))) for d in range(2)])
        for d in range(2):
            st_scr[2 * h + d] = s0_ref[d, h] if has_s0 else jnp.zeros((RET_DK, RET_DV), F32)

    def step(i, finalize):
        sls = (pl.ds(pl.multiple_of(i * C, C), C), pl.ds(pl.multiple_of((n - 1 - i) * C, C), C))
        acc_sls = _half_acc_slices(i, n, C, finalize)
        chains = []
        for h in range(P):
            ql = slice(h * RET_DK, (h + 1) * RET_DK)
            vl = slice(h * RET_DV, (h + 1) * RET_DV)
            for d in range(2):
                chains.append(dict(h=h, d=d, vl=vl, q=q_ref[sls[d], ql], k=k_ref[sls[d], ql],
                                   v=v_ref[sls[d], vl]))
        for c in chains:
            if c["d"] == 0:
                c["s"] = _dot_nt(c["q"], c["k"])
        for c in chains:
            h, d, v, vl = c["h"], c["d"], c["v"], c["vl"]
            hd = pair * P + h
            st = st_scr[2 * h + d]
            qd = (c["q"].astype(F32) * cst[hd, 1 + 2 * d]).astype(BF16)
            if d == 0:
                p = (c["s"] * cst[hd, 0]).astype(BF16)
                o = _dot(jnp.concatenate([p, qd], axis=1), jnp.concatenate([v, st.astype(BF16)], axis=0))
            else:
                o = _dot(qd, st.astype(BF16))
            kd = (c["k"].astype(F32) * cst[hd, 2 + 2 * d]).astype(BF16)
            st_scr[2 * h + d] = cds[h][d] * st + _dot_tn(kd, v)
            if finalize:
                o = o + (ob_acc, of_acc)[d][acc_sls[d], vl]
                ms = jnp.mean(o * o, axis=-1, keepdims=True)
                o_ref[sls[d], vl] = (o * lax.rsqrt(ms + EPS) * gate_ref[sls[d], vl]).astype(BF16)
            else:
                (of_acc, ob_acc)[d][acc_sls[d], vl] = o

    def finish():
        if out_state:
            for h in range(P):
                for d in range(2):
                    st_ref[d, h] = st_scr[2 * h + d]

    return n, step, finish


def _ret_part(dec, acts, s0, *, layer, row0, B, L, out_state):
    has_s0 = s0 is not None
    H = RET_HEADS
    P = RET_PACK
    r0 = row0 // L
    st_spec = pl.BlockSpec((None, 2, P, RET_DK, RET_DV), lambda b, h: (b, 0, h, 0, 0))
    cols = lambda width, off: pl.BlockSpec((L, width), lambda b, h: (r0 + b, off // width + h))
    in_specs = [
        pl.BlockSpec(memory_space=pltpu.SMEM),
        cols(P * RET_DK, P_RQ), cols(P * RET_DK, P_RK), cols(P * RET_DV, P_RV), cols(P * RET_DV, P_RG),
    ]
    args = [dec, acts, acts, acts, acts]
    if has_s0:
        in_specs.append(pl.BlockSpec((None, None, 2, P, RET_DK, RET_DV), lambda b, h: (b, layer, 0, h, 0, 0)))
        args.append(s0)
    out_specs = [pl.BlockSpec((L, P * RET_DV), lambda b, h: (b, h))]
    out_shape = [jax.ShapeDtypeStruct((B * L, H * RET_DV), BF16)]
    if out_state:
        out_specs.append(st_spec)
        out_shape.append(jax.ShapeDtypeStruct((B, 2, H, RET_DK, RET_DV), F32))
    scratch = [pltpu.VMEM((L // 2, P * RET_DV), F32), pltpu.VMEM((L // 2, P * RET_DV), F32),
               pltpu.VMEM((2 * P, RET_DK, RET_DV), F32), pltpu.VMEM((H, 5, SCAN_BLOCK, SCAN_BLOCK), F32)]
    return dict(setup=functools.partial(_ret_setup, L=L, has_s0=has_s0, out_state=out_state),
                grid=(B, H // P), in_specs=in_specs, args=args, out_specs=out_specs, out_shape=out_shape,
                scratch=scratch)


def _scan_call(parts, name):
    grid = parts[0]["grid"]
    assert all(p["grid"] == grid for p in parts)
    kern = functools.partial(_scan_kernel, parts=[(p["setup"], len(p["in_specs"]), len(p["out_specs"]),
                                                   len(p["scratch"])) for p in parts])
    cat = lambda key: [x for p in parts for x in p[key]]
    res = pl.pallas_call(
        kern,
        grid=grid,
        in_specs=cat("in_specs"),
        out_specs=cat("out_specs"),
        out_shape=cat("out_shape"),
        scratch_shapes=cat("scratch"),
        compiler_params=_params(("arbitrary", "arbitrary")),
        name=name,
    )(*cat("args"))
    out, k = [], 0
    for p in parts:
        out.append(res[k:k + len(p["out_specs"])])
        k += len(p["out_specs"])
    return out


def _gla_blocks(chains, tri_ref, code_ref):
    T = HG_BLOCK
    K = HG_DK

    def cumulate(c):
        g2 = c["g"]
        c["k"] = 1.0 - jnp.exp2(g2)
        hi = g2.astype(BF16)
        lo = (g2 - hi.astype(F32)).astype(BF16)
        tri = tri_ref[int(c["rev"])]
        c["b"] = _dot(tri, hi) + _dot(tri, lo)

    def leaf(c):
        bl = c["b"].reshape(T // HG_LEAF, HG_LEAF, K)
        r = HG_LEAF // 2 if c["rev"] else HG_LEAF // 2 - 1
        d = jnp.clip(bl - bl[:, r:r + 1, :], -EXP2_CLAMP, EXP2_CLAMP).reshape(T, K)
        e = jnp.exp2(d)
        p = _dot_nt((c["q"] * e).astype(BF16), (c["k"] / e).astype(BF16))
        c["a"] = jnp.where(code_ref[int(c["rev"])] == 0, p, 0.0)

    def level(c, lv, m):
        seg = 2 * m
        ng = T // seg
        b3, q3, k3 = (c[n].reshape(ng, seg, K) for n in ("b", "q", "k"))
        if c["rev"]:
            qs, ks, r = slice(0, m), slice(m, seg), m
        else:
            qs, ks, r = slice(m, seg), slice(0, m), m - 1
        ref = b3[:, r:r + 1, :]
        qm = (q3[:, qs, :] * jnp.exp2(b3[:, qs, :] - ref)).reshape(ng * m, K).astype(BF16)
        kz = k3[:, ks, :] * jnp.exp2(ref - b3[:, ks, :])
        zeros = jnp.zeros((ng, m, K), F32)
        kfull = jnp.concatenate([zeros, kz] if c["rev"] else [kz, zeros], axis=1)
        p = _dot_nt(qm, kfull.reshape(T, K).astype(BF16)).reshape(ng, m, T)
        a3 = c["a"].reshape(ng, seg, T)
        c3 = code_ref[int(c["rev"])].reshape(ng, seg, T)
        upd = jnp.where(c3[:, qs, :] == lv, p, a3[:, qs, :])
        parts = [upd, a3[:, ks, :]] if c["rev"] else [a3[:, ks, :], upd]
        c["a"] = jnp.concatenate(parts, axis=1).reshape(T, T)

    def finish(c):
        b, q, k, v, st = c["b"], c["q"], c["k"], c["v"], c["st"]
        o = _dot(c["a"].astype(BF16), v) + _dot_nt((q * jnp.exp2(b)).astype(BF16), st.astype(BF16))
        end_row = 0 if c["rev"] else T - 1
        b1 = b.reshape(1, T, K)
        b_end = b1[:, end_row:end_row + 1, :]
        kd = (k * jnp.exp2(b_end - b1).reshape(T, K)).astype(BF16)
        c["out"] = (o, st * jnp.exp2(b_end.reshape(1, K)) + _dot_tn(v, kd))

    for c in chains:
        cumulate(c)
    for c in chains:
        leaf(c)
    for lv, m in HG_LEVELS:
        for c in chains:
            level(c, lv, m)
    for c in chains:
        finish(c)
    return [c["out"] for c in chains]


def _hgrn_setup(in_refs, out_refs, scratch, *, L, P, has_s0, out_state):
    it = iter(in_refs)
    q_ref, g_f_ref, g_b_ref, v_ref, gate_ref, nw_ref = (next(it) for _ in range(6))
    s0_ref = next(it) if has_s0 else None
    o_ref = out_refs[0]
    st_ref = out_refs[1] if out_state else None
    of_acc, ob_acc, st_scr, tri_ref, code_ref = scratch

    T = HG_BLOCK
    n = L // T

    @pl.when(jnp.logical_and(pl.program_id(0) == 0, pl.program_id(1) == 0))
    def _():
        ti = lax.broadcasted_iota(jnp.int32, (T, T), 0)
        si = lax.broadcasted_iota(jnp.int32, (T, T), 1)
        x = ti ^ si
        tri_ref[0] = (si <= ti).astype(BF16)
        tri_ref[1] = (si >= ti).astype(BF16)
        lvl = jnp.where(x < HG_LEAF, 0, HG_LEVELS[0][0] - 1 + sum((x >= m).astype(jnp.int32) for _, m in HG_LEVELS))
        code_ref[0] = jnp.where(si <= ti, lvl, -1)
        code_ref[1] = jnp.where(si >= ti, lvl, -1)

    for h in range(P):
        for d in range(2):
            st_scr[2 * h + d] = s0_ref[d, h].T if has_s0 else jnp.zeros((HG_DV, HG_DK), F32)

    def step(i, finalize):
        sls = (pl.ds(pl.multiple_of(i * T, T), T), pl.ds(pl.multiple_of((n - 1 - i) * T, T), T))
        acc_sls = _half_acc_slices(i, n, T, finalize)
        chains = []
        for h in range(P):
            lanes = slice(h * HG_DK, (h + 1) * HG_DK)
            for d, g_ref in enumerate((g_f_ref, g_b_ref)):
                chains.append(dict(q=q_ref[sls[d], lanes].astype(F32), g=g_ref[sls[d], lanes],
                                   v=v_ref[sls[d], lanes],
                                   st=st_scr[2 * h + d], rev=bool(d)))
        res = _gla_blocks(chains, tri_ref, code_ref)
        for h in range(P):
            lanes = slice(h * HG_DK, (h + 1) * HG_DK)
            for d in range(2):
                o, st = res[2 * h + d]
                st_scr[2 * h + d] = st
                if finalize:
                    o = o + (ob_acc, of_acc)[d][acc_sls[d], lanes]
                    ms = jnp.mean(o * o, axis=-1, keepdims=True)
                    o = o * lax.rsqrt(ms + EPS) * nw_ref[:, lanes]
                    o_ref[sls[d], lanes] = (o * gate_ref[sls[d], lanes]).astype(BF16)
                else:
                    (of_acc, ob_acc)[d][acc_sls[d], lanes] = o

    def finish():
        if out_state:
            for h in range(P):
                for d in range(2):
                    st_ref[d, h] = st_scr[2 * h + d].T

    return n, step, finish


def _hgrn_part(acts, g, nw, s0, *, layer, row0, B, L, out_state):
    has_s0 = s0 is not None
    H = HG_HEADS
    P = HG_PACK_SHORT if L <= HG_SHORT_SEQ else HG_PACK_LONG
    W = P * HG_DK
    r0 = row0 // L
    blk = lambda off: pl.BlockSpec((L, W), lambda b, h: (r0 + b, off // W + h))
    st_spec = pl.BlockSpec((None, 2, P, HG_DK, HG_DV), lambda b, h: (b, 0, h, 0, 0))
    in_specs = [blk(P_HQ), blk(0), blk(H * HG_DK), blk(P_HI), blk(P_HOG),
                pl.BlockSpec((1, W), lambda b, h: (0, h))]
    args = [acts, g, g, acts, acts, nw]
    if has_s0:
        in_specs.append(pl.BlockSpec((None, None, 2, P, HG_DK, HG_DV), lambda b, h: (b, layer, 0, h, 0, 0)))
        args.append(s0)
    out_specs = [pl.BlockSpec((L, W), lambda b, h: (b, h))]
    out_shape = [jax.ShapeDtypeStruct((B * L, H * HG_DV), BF16)]
    if out_state:
        out_specs.append(st_spec)
        out_shape.append(jax.ShapeDtypeStruct((B, 2, H, HG_DK, HG_DV), F32))
    scratch = [pltpu.VMEM((L // 2, W), F32), pltpu.VMEM((L // 2, W), F32),
               pltpu.VMEM((2 * P, HG_DV, HG_DK), F32), pltpu.VMEM((2, HG_BLOCK, HG_BLOCK), BF16),
               pltpu.VMEM((2, HG_BLOCK, HG_BLOCK), jnp.int32)]
    return dict(setup=functools.partial(_hgrn_setup, L=L, P=P, has_s0=has_s0, out_state=out_state),
                grid=(B, H // P), in_specs=in_specs, args=args, out_specs=out_specs, out_shape=out_shape,
                scratch=scratch)


def _merge_kernel(x_ref, mod_ref, oret_a, oret_b, ohg_a, ohg_b, gr_ref, gh_ref, wr_hbm, wh_hbm, wo_hbm, o_ref,
                  wr_ref, wh_ref, wo_ref, *, n_first):
    i = pl.program_id(0)
    for w_hbm, w_ref in ((wr_hbm, wr_ref), (wh_hbm, wh_ref), (wo_hbm, wo_ref)):
        _stage_weight(i, w_hbm, w_ref, STAGE_ROWS_MERGE)
    oret = jnp.where(i < n_first, oret_a[...], oret_b[...])
    ohg = jnp.where(i < n_first, ohg_a[...], ohg_b[...])
    y = gr_ref[...] * _dot(oret, wr_ref[...]) + gh_ref[...] * _dot(ohg, wh_ref[...])
    out = _dot(y.astype(BF16), wo_ref[...])
    o_ref[...] = x_ref[...] + mod_ref[5:6, :] * out


def _merge_call(x, mod3, orets, ohgs, acts, wr, wh, wo, *, mod_row_fn, tm, name):
    n = x.shape[0]
    n_first = orets[0].shape[0] // tm
    row = pl.BlockSpec((tm, D_MODEL), lambda i: (i, 0))
    gate = lambda off: pl.BlockSpec((tm, D_MODEL), lambda i: (i, off // D_MODEL))
    first = pl.BlockSpec((tm, D_MODEL), lambda i: (jnp.minimum(i, n_first - 1), 0))
    second = pl.BlockSpec((tm, D_MODEL), lambda i: (jnp.maximum(i - n_first, 0), 0))
    return pl.pallas_call(
        functools.partial(_merge_kernel, n_first=n_first),
        grid=(n // tm,),
        in_specs=[row, pl.BlockSpec((None, N_MOD, D_MODEL), lambda i: (mod_row_fn(i, tm), 0, 0)),
                  first, second, first, second, gate(P_GR), gate(P_GH), _HBM, _HBM, _HBM],
        out_specs=row,
        out_shape=jax.ShapeDtypeStruct((n, D_MODEL), F32),
        scratch_shapes=[_bf16_copy(wr), _bf16_copy(wh), _bf16_copy(wo)],
        compiler_params=_params(("arbitrary",)),
        name=name,
    )(x, mod3, *orets, *ohgs, acts, acts, wr, wh, wo)


def _rope_tables(n_tokens, n_identity):
    pos = np.arange(n_tokens)
    r = (pos // GRID_W).astype(np.float32)
    cl = (pos % GRID_W).astype(np.float32)
    quarter = RET_DK // 4
    inv_freq = (np.float32(ROPE_BASE) ** (-np.arange(quarter, dtype=np.float32) / np.float32(quarter)))
    inv_freq = inv_freq.astype(np.float32)
    ang_r = (r[:, None] * inv_freq[None, :]).astype(np.float64)
    ang_c = (cl[:, None] * inv_freq[None, :]).astype(np.float64)
    cos = np.concatenate([np.cos(ang_r)] * 2 + [np.cos(ang_c)] * 2, axis=-1)
    sins = np.concatenate([-np.sin(ang_r), np.sin(ang_r), -np.sin(ang_c), np.sin(ang_c)], axis=-1)
    cos = np.concatenate([cos, np.ones((n_identity, RET_DK))], axis=0).astype(np.float32)
    sins = np.concatenate([sins, np.zeros((n_identity, RET_DK))], axis=0).astype(np.float32)
    return jnp.asarray(cos), jnp.asarray(sins)


def kernel(x_prompt, x_sample, state_ret, state_hgrn, c, c_ctx, ada_w, ada_b, norm_w, ffn1_w13, ffn1_w2,
           ffn2_w13, ffn2_w2, w_in, ret_decay, hg_lb_logits, hg_norm_w, w_ret_proj, w_hg_proj, w_o,
           final_norm_w):
    depth = ada_w.shape[0]
    assert depth == 1
    l = 0
    bp, lp, _ = x_prompt.shape
    bs, ls, _ = x_sample.shape
    n_ctx = bp * lp
    assert n_ctx % ls == 0 and ls % TM_FFN == 0 and ls % TM_PROJ == 0 and ls % TM_MERGE == 0

    cvec = jnp.concatenate([c_ctx[None, :], c, jnp.zeros((8 - 1 - bs, D_MODEL), F32)], axis=0)
    mod = _mod_call(cvec, ada_w[l], ada_b[l][None, :])
    mod3 = mod.reshape(8, N_MOD, D_MODEL)

    def mod_row(i, tm):
        return jnp.where(i < n_ctx // tm, 0, 1 + (i * tm - n_ctx) // ls)

    def rope_blk(i, tm):
        return jnp.where(i < n_ctx // tm, ls // tm, (i - n_ctx // tm) % (ls // tm))

    nw, fw = norm_w[l], final_norm_w[None, :]
    dec, hg_nw = ret_decay[l], hg_norm_w[l][None, :]

    x = _ffn_call((x_prompt.reshape(n_ctx, D_MODEL), x_sample.reshape(bs * ls, D_MODEL)), mod3, nw, fw,
                  ffn1_w13[l], ffn1_w2[l], n_out_split=None, mod_base=0, norm_row=0, final=False,
                  mod_row_fn=mod_row, tm=TM_FFN, name="ffn1")
    acts, g = _proj_call(x, mod3, nw, hg_lb_logits, w_in[l], _rope_tables(ls, TM_PROJ), rope_blk, layer=l,
                         mod_row_fn=mod_row, tm=TM_PROJ, name="proj")
    (ret_ctx, s_ret), (hg_ctx, s_hg) = _scan_call(
        [_ret_part(dec, acts, None, layer=l, row0=0, B=bp, L=lp, out_state=True),
         _hgrn_part(acts, g, hg_nw, None, layer=l, row0=0, B=bp, L=lp, out_state=True)], "scan_ctx")
    (ret_lat,), = _scan_call([_ret_part(dec, acts, state_ret, layer=l, row0=n_ctx, B=bs, L=ls, out_state=False)],
                             "ret_lat")
    (hg_lat,), = _scan_call([_hgrn_part(acts, g, hg_nw, state_hgrn, layer=l, row0=n_ctx, B=bs, L=ls,
                                        out_state=False)], "hgrn_lat")
    x = _merge_call(x, mod3, (ret_ctx, ret_lat), (hg_ctx, hg_lat), acts, w_ret_proj[l], w_hg_proj[l], w_o[l],
                    mod_row_fn=mod_row, tm=TM_MERGE, name="merge")
    yp, ys = _ffn_call((x,), mod3, nw, fw, ffn2_w13[l], ffn2_w2[l], n_out_split=n_ctx, mod_base=6,
                       norm_row=2, final=True, mod_row_fn=mod_row, tm=TM_FFN, name="ffn2")
    return (yp.reshape(bp, lp, D_MODEL), ys.reshape(bs, ls, D_MODEL), s_ret[:, None], s_hg[:, None])
```

```python
import functools

import numpy as np
import jax
import jax.numpy as jnp
from jax import lax
from jax.experimental import pallas as pl
from jax.experimental.pallas import tpu as pltpu

F32 = jnp.float32
BF16 = jnp.bfloat16

D_MODEL = 1024
GRID_W = 64
RET_HEADS = 4
RET_DK = 128
RET_DV = 256
HG_HEADS = 8
HG_DK = 128
HG_DV = 128
D_FF = 2816
N_MOD = 9
ROPE_BASE = 10000.0
EPS = 1e-6

C_RQ, C_RK, C_RV, C_RG, C_HQ, C_HFF, C_HFB, C_HI, C_HOG, C_GR, C_GH = (
    0, 512, 1024, 2048, 3072, 4096, 5120, 6144, 7168, 8192, 9216)
IN_WIDTH = 10240
P_RQ, P_RK, P_RV, P_RG, P_HQ, P_HI, P_HOG, P_GR, P_GH = 0, 512, 1024, 2048, 3072, 4096, 5120, 6144, 7168
P_WIDTH = 8192

VMEM_LIMIT_BYTES = 56 * 1024 * 1024

TM_FFN = 512
TM_PROJ = 512
TM_MERGE = 512
STAGE_SLOTS = 4
STAGE_ROWS_W13, STAGE_ROWS_W2, STAGE_ROWS_W_IN, STAGE_ROWS_MERGE = 128, 256, 16, 512
MOD_TILE = 2304
SCAN_BLOCK = 128
RET_PACK = 4
SCAN_UNROLL = 8
HG_PACK_LONG = 4
HG_PACK_SHORT = 8
HG_SHORT_SEQ = 512
FF_CHUNK = 256
PROJ_CHUNK = 256
LOG2E = 1.4426950408889634
EXP2_CLAMP = 115.0
HG_BLOCK = 128
HG_LEAF = 16
HG_LEVELS = ((4, 16), (5, 32), (6, 64))


def _dot(a, b):
    return jnp.dot(a, b, preferred_element_type=F32)


def _dot_nt(a, b):
    return lax.dot_general(a, b, (((1,), (1,)), ((), ())), preferred_element_type=F32)


def _dot_tn(a, b):
    return lax.dot_general(a, b, (((0,), (0,)), ((), ())), preferred_element_type=F32)


def _sigmoid(x):
    return 0.5 * jnp.tanh(0.5 * x) + 0.5


def _rms(x, w):
    ms = jnp.mean(x * x, axis=-1, keepdims=True)
    return x * lax.rsqrt(ms + EPS) * w


def _rms_modulate(x, w, shift, scale):
    ms = jnp.mean(x * x, axis=-1, keepdims=True)
    return x * lax.rsqrt(ms + EPS) * (w * (1.0 + scale)) + shift


def _params(sem):
    return pltpu.CompilerParams(dimension_semantics=sem, vmem_limit_bytes=VMEM_LIMIT_BYTES)


def _stage_weight(step, w_hbm, w_vmem, rows):
    r, c = w_hbm.shape
    assert r % rows == 0 and rows % 16 == 0
    n_chunks = r // rows
    ahead = STAGE_SLOTS - 1

    def run(stage, sem):
        def copy(k):
            slot = k % STAGE_SLOTS
            return pltpu.make_async_copy(w_hbm.at[pl.ds(k * rows, rows), :], stage.at[slot], sem.at[slot])

        for k in range(min(ahead, n_chunks)):
            copy(k).start()

        def body(k, carry):
            @pl.when(k + ahead < n_chunks)
            def _():
                copy(k + ahead).start()

            copy(k).wait()
            w_vmem[pl.ds(pl.multiple_of(k * rows, rows), rows), :] = stage[k % STAGE_SLOTS].astype(BF16)
            return carry

        lax.fori_loop(0, n_chunks, body, 0)

    @pl.when(step == 0)
    def _():
        pl.run_scoped(run, pltpu.VMEM((STAGE_SLOTS, rows, c), F32), pltpu.SemaphoreType.DMA((STAGE_SLOTS,)))


_HBM = pl.BlockSpec(memory_space=pl.ANY)


def _bf16_copy(w):
    return pltpu.VMEM(w.shape, BF16)


def _mod_kernel(c_ref, w_ref, b_ref, o_ref):
    c = c_ref[...]
    s = (c * _sigmoid(c)).astype(BF16)
    o_ref[...] = _dot(s, w_ref[...].astype(BF16)) + b_ref[...]


def _mod_call(cvec, ada_w, ada_b):
    n = ada_w.shape[1]
    tn = MOD_TILE
    return pl.pallas_call(
        _mod_kernel,
        grid=(n // tn,),
        in_specs=[
            pl.BlockSpec((8, D_MODEL), lambda j: (0, 0)),
            pl.BlockSpec((D_MODEL, tn), lambda j: (0, j)),
            pl.BlockSpec((1, tn), lambda j: (0, j)),
        ],
        out_specs=pl.BlockSpec((8, tn), lambda j: (0, j)),
        out_shape=jax.ShapeDtypeStruct((8, n), F32),
        compiler_params=_params(("arbitrary",)),
        name="mod",
    )(cvec, ada_w, ada_b)


def _ffn_kernel(*refs, mod_base, norm_row, final, n_first, two_in, two_out):
    it = iter(refs)
    xa_ref = next(it)
    xb_ref = next(it) if two_in else None
    mod_ref, nw_ref, fw_ref, w13_hbm, w2_hbm = (next(it) for _ in range(5))
    oa_ref = next(it)
    ob_ref = next(it) if two_out else None
    w13_ref, w2_ref = next(it), next(it)
    i = pl.program_id(0)
    x = xa_ref[...]
    if two_in:
        x = jnp.where(i < n_first, x, xb_ref[...])
    h = _rms_modulate(x, nw_ref[norm_row:norm_row + 1, :], mod_ref[mod_base:mod_base + 1, :],
                      mod_ref[mod_base + 1:mod_base + 2, :]).astype(BF16)
    n_chunks = D_FF // FF_CHUNK

    def chunk(k, acc):
        c0 = k * FF_CHUNK
        a = _dot(h, w13_ref[:, c0:c0 + FF_CHUNK])
        b = _dot(h, w13_ref[:, D_FF + c0:D_FF + c0 + FF_CHUNK])
        act = (a * _sigmoid(a) * b).astype(BF16)
        part = _dot(act, w2_ref[c0:c0 + FF_CHUNK, :])
        return part if acc is None else acc + part

    def finish(acc):
        xo = x + (0.5 * mod_ref[mod_base + 2:mod_base + 3, :]) * acc
        if final:
            xo = _rms(xo, fw_ref[...])
        if two_out:
            @pl.when(i < n_first)
            def _():
                oa_ref[...] = xo

            @pl.when(i >= n_first)
            def _():
                ob_ref[...] = xo
        else:
            oa_ref[...] = xo

    @pl.when(i == 0)
    def _():
        ahead = STAGE_SLOTS - 1

        def run(sa, sb, s2, sem):
            def copies(k):
                slot, c0 = k % STAGE_SLOTS, k * FF_CHUNK
                return (pltpu.make_async_copy(w13_hbm.at[:, pl.ds(c0, FF_CHUNK)], sa.at[slot], sem.at[0, slot]),
                        pltpu.make_async_copy(w13_hbm.at[:, pl.ds(D_FF + c0, FF_CHUNK)], sb.at[slot],
                                              sem.at[1, slot]),
                        pltpu.make_async_copy(w2_hbm.at[pl.ds(c0, FF_CHUNK), :], s2.at[slot], sem.at[2, slot]))

            for k in range(min(ahead, n_chunks)):
                for cp in copies(k):
                    cp.start()
            acc = None
            for k in range(n_chunks):
                if k + ahead < n_chunks:
                    for cp in copies(k + ahead):
                        cp.start()
                for cp in copies(k):
                    cp.wait()
                slot, c0 = k % STAGE_SLOTS, k * FF_CHUNK
                w13_ref[:, c0:c0 + FF_CHUNK] = sa[slot].astype(BF16)
                w13_ref[:, D_FF + c0:D_FF + c0 + FF_CHUNK] = sb[slot].astype(BF16)
                w2_ref[c0:c0 + FF_CHUNK, :] = s2[slot].astype(BF16)
                acc = chunk(k, acc)
            finish(acc)

        pl.run_scoped(run, pltpu.VMEM((STAGE_SLOTS, D_MODEL, FF_CHUNK), F32),
                      pltpu.VMEM((STAGE_SLOTS, D_MODEL, FF_CHUNK), F32),
                      pltpu.VMEM((STAGE_SLOTS, FF_CHUNK, D_MODEL), F32),
                      pltpu.SemaphoreType.DMA((3, STAGE_SLOTS)))

    @pl.when(i > 0)
    def _():
        acc = None
        for k in range(n_chunks):
            acc = chunk(k, acc)
        finish(acc)


def _ffn_call(xs, mod3, norm_w, final_w, w13, w2, *, n_out_split, mod_base, norm_row, final, mod_row_fn, tm,
              name):
    two_in = len(xs) == 2
    two_out = n_out_split is not None
    n = sum(x.shape[0] for x in xs)
    n_first = (xs[0].shape[0] if two_in else n_out_split if two_out else n) // tm
    first = pl.BlockSpec((tm, D_MODEL), lambda i: (jnp.minimum(i, n_first - 1), 0))
    second = pl.BlockSpec((tm, D_MODEL), lambda i: (jnp.maximum(i - n_first, 0), 0))
    whole = pl.BlockSpec((tm, D_MODEL), lambda i: (i, 0))
    kern = functools.partial(_ffn_kernel, mod_base=mod_base, norm_row=norm_row, final=final, n_first=n_first,
                             two_in=two_in, two_out=two_out)
    if two_out:
        out_specs = [first, second]
        out_shape = [jax.ShapeDtypeStruct((n_out_split, D_MODEL), F32),
                     jax.ShapeDtypeStruct((n - n_out_split, D_MODEL), F32)]
    else:
        out_specs = whole
        out_shape = jax.ShapeDtypeStruct((n, D_MODEL), F32)
    return pl.pallas_call(
        kern,
        grid=(n // tm,),
        in_specs=([first, second] if two_in else [whole]) + [
            pl.BlockSpec((None, N_MOD, D_MODEL), lambda i: (mod_row_fn(i, tm), 0, 0)),
            pl.BlockSpec((3, D_MODEL), lambda i: (0, 0)),
            pl.BlockSpec((1, D_MODEL), lambda i: (0, 0)),
            _HBM,
            _HBM,
        ],
        out_specs=out_specs,
        out_shape=out_shape,
        scratch_shapes=[_bf16_copy(w13), _bf16_copy(w2)],
        compiler_params=_params(("arbitrary",)),
        name=name,
    )(*xs, mod3, norm_w, final_w, w13, w2)


def _rope(x, cos, sins, first_quarter):
    partner = jnp.where(first_quarter, pltpu.roll(x, 96, 1), pltpu.roll(x, 32, 1))
    return x * cos + partner * sins


def _proj_kernel(*refs, layer, rope):
    it = iter(refs)
    x_ref, mod_ref, nw_ref, lbl_ref, w_hbm = (next(it) for _ in range(5))
    cos_ref, sin_ref = (next(it), next(it)) if rope else (None, None)
    p_ref, g_ref = next(it), next(it)
    w_ref = next(it)
    _stage_weight(pl.program_id(0), w_hbm, w_ref, STAGE_ROWS_W_IN)
    h = _rms_modulate(x_ref[...], nw_ref[1:2, :], mod_ref[3:4, :], mod_ref[4:5, :]).astype(BF16)

    if rope:
        lane = lax.broadcasted_iota(jnp.int32, (x_ref.shape[0], RET_DK), 1)
        first_quarter = (lane & 32) == 0

    def rot(t):
        if not rope:
            return t
        heads = [_rope(t[:, j:j + RET_DK], cos_ref[...], sin_ref[...], first_quarter)
                 for j in range(0, t.shape[1], RET_DK)]
        return jnp.concatenate(heads, axis=1)

    def silu(t):
        return t * _sigmoid(t)

    def log_forget(d):
        def fn(t, c):
            lg = lbl_ref[d, :, c:c + PROJ_CHUNK]
            e = jnp.exp(lg - jnp.max(lg, axis=0, keepdims=True))
            lb = jnp.sum(e[0:layer + 1, :], axis=0, keepdims=True) / jnp.sum(e, axis=0, keepdims=True)
            return jnp.log(lb + (1.0 - lb) * _sigmoid(t)) * LOG2E
        return fn

    parts = (
        (C_RQ, 512, p_ref, P_RQ, lambda t, c: rot(t)),
        (C_RK, 512, p_ref, P_RK, lambda t, c: rot(t * (RET_DK ** -0.5))),
        (C_RV, 1024, p_ref, P_RV, lambda t, c: t),
        (C_RG, 1024, p_ref, P_RG, lambda t, c: silu(t)),
        (C_HQ, 1024, p_ref, P_HQ, lambda t, c: silu(t) * (HG_DK ** -0.5)),
        (C_HFF, 1024, g_ref, 0, log_forget(0)),
        (C_HFB, 1024, g_ref, 1024, log_forget(1)),
        (C_HI, 1024, p_ref, P_HI, lambda t, c: t),
        (C_HOG, 1024, p_ref, P_HOG, lambda t, c: silu(t)),
        (C_GR, 1024, p_ref, P_GR, lambda t, c: _sigmoid(t)),
        (C_GH, 1024, p_ref, P_GH, lambda t, c: _sigmoid(t)),
    )
    for c in range(0, 1024, PROJ_CHUNK):
        for w0, width, o_ref, o0, fn in parts:
            if c < width:
                t = _dot(h, w_ref[:, w0 + c:w0 + c + PROJ_CHUNK])
                o_ref[:, o0 + c:o0 + c + PROJ_CHUNK] = fn(t, c).astype(o_ref.dtype)


def _proj_call(x, mod3, norm_w, lb_logits, w_in, rope_tabs, rope_blk_fn, *, layer, mod_row_fn, tm, name):
    n = x.shape[0]
    slots = lb_logits.shape[1]
    row = lambda w: pl.BlockSpec((tm, w), lambda i: (i, 0))
    outs = [(P_WIDTH, BF16), (2 * HG_HEADS * HG_DK, F32)]
    in_specs = [
        row(D_MODEL),
        pl.BlockSpec((None, N_MOD, D_MODEL), lambda i: (mod_row_fn(i, tm), 0, 0)),
        pl.BlockSpec((3, D_MODEL), lambda i: (0, 0)),
        pl.BlockSpec((2, slots, 1024), lambda i: (0, 0, 0)),
        _HBM,
    ]
    args = [x, mod3, norm_w, lb_logits, w_in]
    if rope_tabs is not None:
        in_specs += [pl.BlockSpec((tm, RET_DK), lambda i: (rope_blk_fn(i, tm), 0))] * 2
        args += list(rope_tabs)
    return pl.pallas_call(
        functools.partial(_proj_kernel, layer=layer, rope=rope_tabs is not None),
        grid=(n // tm,),
        in_specs=in_specs,
        out_specs=[row(w) for w, _ in outs],
        out_shape=[jax.ShapeDtypeStruct((n, w), dt) for w, dt in outs],
        scratch_shapes=[_bf16_copy(w_in)],
        compiler_params=_params(("arbitrary",)),
        name=name,
    )(*args)


def _log_sigmoid(x):
    return -(jnp.maximum(-x, 0.0) + jnp.log1p(jnp.exp(-jnp.abs(x))))


def _half_acc_slices(i, n, blk, finalize):
    if finalize:
        starts = ((i - n // 2) * blk, (n - 1 - i) * blk)
    else:
        starts = (i * blk, (n // 2 - 1 - i) * blk)
    return tuple(pl.ds(pl.multiple_of(s, blk), blk) for s in starts)


def _run_scans(scans):
    n = scans[0][0]
    assert all(s[0] == n for s in scans)

    def half(finalize):
        def body(i, carry):
            for _, step, _ in scans:
                step(i, finalize)
            return carry
        return body

    lax.fori_loop(0, n // 2, half(False), 0, unroll=min(SCAN_UNROLL, n // 2))
    lax.fori_loop(n // 2, n, half(True), 0, unroll=min(SCAN_UNROLL, n // 2))
    for _, _, finish in scans:
        finish()


def _scan_kernel(*refs, parts):
    ins = [n for _, n, _, _ in parts]
    outs = [n for _, _, n, _ in parts]
    offs = [0, sum(ins), sum(ins) + sum(outs)]
    scans = []
    for setup, n_in, n_out, n_scr in parts:
        pick = [refs[o:o + c] for o, c in zip(offs, (n_in, n_out, n_scr))]
        offs = [o + c for o, c in zip(offs, (n_in, n_out, n_scr))]
        scans.append(setup(*pick))
    _run_scans(scans)


def _ret_setup(in_refs, out_refs, scratch, *, L, has_s0, out_state):
    it = iter(in_refs)
    dec_ref = next(it)
    q_ref, k_ref, v_ref, gate_ref = next(it), next(it), next(it), next(it)
    s0_ref = next(it) if has_s0 else None
    o_ref = out_refs[0]
    st_ref = out_refs[1] if out_state else None
    of_acc, ob_acc, st_scr, cst = scratch

    C = SCAN_BLOCK
    P = RET_PACK
    n = L // C
    pair = pl.program_id(1)
    ii = lax.broadcasted_iota(jnp.int32, (C, C), 0).astype(F32)
    jj = lax.broadcasted_iota(jnp.int32, (C, C), 1).astype(F32)
    diff = ii - jj

    @pl.when(jnp.logical_and(pl.program_id(0) == 0, pair == 0))
    def _():
        for hd in range(RET_HEADS):
            lgf = _log_sigmoid(jnp.full((C, C), dec_ref[0, hd], F32))
            lgb = _log_sigmoid(jnp.full((C, C), dec_ref[1, hd], F32))
            cst[hd, 0] = (jnp.where(diff >= 0, jnp.exp(jnp.maximum(diff, 0.0) * lgf), 0.0)
                          + jnp.where(diff <= 0, jnp.exp(jnp.maximum(-diff, 0.0) * lgb), 0.0))
            cst[hd, 1] = jnp.exp((ii + 1.0) * lgf)
            cst[hd, 2] = jnp.exp((C - 1.0 - ii) * lgf)
            cst[hd, 3] = jnp.exp((C - ii) * lgb)
            cst[hd, 4] = jnp.exp(ii * lgb)

    cds = []
    for h in range(P):
        hd = pair * P + h
        cds.append([jnp.exp(C * _log_sigmoid(jnp.full((1, RET_DV), dec_ref[d, hd], F32))) for d in range(2)])
        for d in range(2):
            st_scr[2 * h + d] = s0_ref[d, h] if has_s0 else jnp.zeros((RET_DK, RET_DV), F32)

    def step(i, finalize):
        sls = (pl.ds(pl.multiple_of(i * C, C), C), pl.ds(pl.multiple_of((n - 1 - i) * C, C), C))
        acc_sls = _half_acc_slices(i, n, C, finalize)
        chains = []
        for h in range(P):
            ql = slice(h * RET_DK, (h + 1) * RET_DK)
            vl = slice(h * RET_DV, (h + 1) * RET_DV)
            for d in range(2):
                chains.append(dict(h=h, d=d, vl=vl, q=q_ref[sls[d], ql], k=k_ref[sls[d], ql],
                                   v=v_ref[sls[d], vl]))
        for c in chains:
            if c["d"] == 0:
                c["s"] = _dot_nt(c["q"], c["k"])
        for c in chains:
            h, d, v, vl = c["h"], c["d"], c["v"], c["vl"]
            hd = pair * P + h
            st = st_scr[2 * h + d]
            qd = (c["q"].astype(F32) * cst[hd, 1 + 2 * d]).astype(BF16)
            if d == 0:
                p = (c["s"] * cst[hd, 0]).astype(BF16)
                o = _dot(jnp.concatenate([p, qd], axis=1), jnp.concatenate([v, st.astype(BF16)], axis=0))
            else:
                o = _dot(qd, st.astype(BF16))
            kd = (c["k"].astype(F32) * cst[hd, 2 + 2 * d]).astype(BF16)
            st_scr[2 * h + d] = cds[h][d] * st + _dot_tn(kd, v)
            if finalize:
                o = o + (ob_acc, of_acc)[d][acc_sls[d], vl]
                ms = jnp.mean(o * o, axis=-1, keepdims=True)
                o_ref[sls[d], vl] = (o * lax.rsqrt(ms + EPS) * gate_ref[sls[d], vl]).astype(BF16)
            else:
                (of_acc, ob_acc)[d][acc_sls[d], vl] = o

    def finish():
        if out_state:
            for h in range(P):
                for d in range(2):
                    st_ref[d, h] = st_scr[2 * h + d]

    return n, step, finish


def _ret_part(dec, acts, s0, *, layer, row0, B, L, out_state):
    has_s0 = s0 is not None
    H = RET_HEADS
    P = RET_PACK
    r0 = row0 // L
    st_spec = pl.BlockSpec((None, 2, P, RET_DK, RET_DV), lambda b, h: (b, 0, h, 0, 0))
    cols = lambda width, off: pl.BlockSpec((L, width), lambda b, h: (r0 + b, off // width + h))
    in_specs = [
        pl.BlockSpec(memory_space=pltpu.SMEM),
        cols(P * RET_DK, P_RQ), cols(P * RET_DK, P_RK), cols(P * RET_DV, P_RV), cols(P * RET_DV, P_RG),
    ]
    args = [dec, acts, acts, acts, acts]
    if has_s0:
        in_specs.append(pl.BlockSpec((None, None, 2, P, RET_DK, RET_DV), lambda b, h: (b, layer, 0, h, 0, 0)))
        args.append(s0)
    out_specs = [pl.BlockSpec((L, P * RET_DV), lambda b, h: (b, h))]
    out_shape = [jax.ShapeDtypeStruct((B * L, H * RET_DV), BF16)]
    if out_state:
        out_specs.append(st_spec)
        out_shape.append(jax.ShapeDtypeStruct((B, 2, H, RET_DK, RET_DV), F32))
    scratch = [pltpu.VMEM((L // 2, P * RET_DV), F32), pltpu.VMEM((L // 2, P * RET_DV), F32),
               pltpu.VMEM((2 * P, RET_DK, RET_DV), F32), pltpu.VMEM((H, 5, SCAN_BLOCK, SCAN_BLOCK), F32)]
    return dict(setup=functools.partial(_ret_setup, L=L, has_s0=has_s0, out_state=out_state),
                grid=(B, H // P), in_specs=in_specs, args=args, out_specs=out_specs, out_shape=out_shape,
                scratch=scratch)


def _scan_call(parts, name):
    grid = parts[0]["grid"]
    assert all(p["grid"] == grid for p in parts)
    kern = functools.partial(_scan_kernel, parts=[(p["setup"], len(p["in_specs"]), len(p["out_specs"]),
                                                   len(p["scratch"])) for p in parts])
    cat = lambda key: [x for p in parts for x in p[key]]
    res = pl.pallas_call(
        kern,
        grid=grid,
        in_specs=cat("in_specs"),
        out_specs=cat("out_specs"),
        out_shape=cat("out_shape"),
        scratch_shapes=cat("scratch"),
        compiler_params=_params(("arbitrary", "arbitrary")),
        name=name,
    )(*cat("args"))
    out, k = [], 0
    for p in parts:
        out.append(res[k:k + len(p["out_specs"])])
        k += len(p["out_specs"])
    return out


def _gla_blocks(chains, tri_ref, code_ref):
    T = HG_BLOCK
    K = HG_DK

    def cumulate(c):
        g2 = c["g"]
        c["k"] = 1.0 - jnp.exp2(g2)
        hi = g2.astype(BF16)
        lo = (g2 - hi.astype(F32)).astype(BF16)
        tri = tri_ref[int(c["rev"])]
        c["b"] = _dot(tri, hi) + _dot(tri, lo)

    def leaf(c):
        bl = c["b"].reshape(T // HG_LEAF, HG_LEAF, K)
        r = HG_LEAF // 2 if c["rev"] else HG_LEAF // 2 - 1
        d = jnp.clip(bl - bl[:, r:r + 1, :], -EXP2_CLAMP, EXP2_CLAMP).reshape(T, K)
        e = jnp.exp2(d)
        p = _dot_nt((c["q"] * e).astype(BF16), (c["k"] / e).astype(BF16))
        c["a"] = jnp.where(code_ref[int(c["rev"])] == 0, p, 0.0)

    def level(c, lv, m):
        seg = 2 * m
        ng = T // seg
        b3, q3, k3 = (c[n].reshape(ng, seg, K) for n in ("b", "q", "k"))
        if c["rev"]:
            qs, ks, r = slice(0, m), slice(m, seg), m
        else:
            qs, ks, r = slice(m, seg), slice(0, m), m - 1
        ref = b3[:, r:r + 1, :]
        qm = (q3[:, qs, :] * jnp.exp2(b3[:, qs, :] - ref)).reshape(ng * m, K).astype(BF16)
        kz = k3[:, ks, :] * jnp.exp2(ref - b3[:, ks, :])
        zeros = jnp.zeros((ng, m, K), F32)
        kfull = jnp.concatenate([zeros, kz] if c["rev"] else [kz, zeros], axis=1)
        p = _dot_nt(qm, kfull.reshape(T, K).astype(BF16)).reshape(ng, m, T)
        a3 = c["a"].reshape(ng, seg, T)
        c3 = code_ref[int(c["rev"])].reshape(ng, seg, T)
        upd = jnp.where(c3[:, qs, :] == lv, p, a3[:, qs, :])
        parts = [upd, a3[:, ks, :]] if c["rev"] else [a3[:, ks, :], upd]
        c["a"] = jnp.concatenate(parts, axis=1).reshape(T, T)

    def finish(c):
        b, q, k, v, st = c["b"], c["q"], c["k"], c["v"], c["st"]
        o = _dot(c["a"].astype(BF16), v) + _dot_nt((q * jnp.exp2(b)).astype(BF16), st.astype(BF16))
        end_row = 0 if c["rev"] else T - 1
        b1 = b.reshape(1, T, K)
        b_end = b1[:, end_row:end_row + 1, :]
        kd = (k * jnp.exp2(b_end - b1).reshape(T, K)).astype(BF16)
        c["out"] = (o, st * jnp.exp2(b_end.reshape(1, K)) + _dot_tn(v, kd))

    for c in chains:
        cumulate(c)
    for c in chains:
        leaf(c)
    for lv, m in HG_LEVELS:
        for c in chains:
            level(c, lv, m)
    for c in chains:
        finish(c)
    return [c["out"] for c in chains]


def _hgrn_setup(in_refs, out_refs, scratch, *, L, P, has_s0, out_state):
    it = iter(in_refs)
    q_ref, g_f_ref, g_b_ref, v_ref, gate_ref, nw_ref = (next(it) for _ in range(6))
    s0_ref = next(it) if has_s0 else None
    o_ref = out_refs[0]
    st_ref = out_refs[1] if out_state else None
    of_acc, ob_acc, st_scr, tri_ref, code_ref = scratch

    T = HG_BLOCK
    n = L // T

    @pl.when(jnp.logical_and(pl.program_id(0) == 0, pl.program_id(1) == 0))
    def _():
        ti = lax.broadcasted_iota(jnp.int32, (T, T), 0)
        si = lax.broadcasted_iota(jnp.int32, (T, T), 1)
        x = ti ^ si
        tri_ref[0] = (si <= ti).astype(BF16)
        tri_ref[1] = (si >= ti).astype(BF16)
        lvl = jnp.where(x < HG_LEAF, 0, HG_LEVELS[0][0] - 1 + sum((x >= m).astype(jnp.int32) for _, m in HG_LEVELS))
        code_ref[0] = jnp.where(si <= ti, lvl, -1)
        code_ref[1] = jnp.where(si >= ti, lvl, -1)

    for h in range(P):
        for d in range(2):
            st_scr[2 * h + d] = s0_ref[d, h].T if has_s0 else jnp.zeros((HG_DV, HG_DK), F32)

    def step(i, finalize):
        sls = (pl.ds(pl.multiple_of(i * T, T), T), pl.ds(pl.multiple_of((n - 1 - i) * T, T), T))
        acc_sls = _half_acc_slices(i, n, T, finalize)
        chains = []
        for h in range(P):
            lanes = slice(h * HG_DK, (h + 1) * HG_DK)
            for d, g_ref in enumerate((g_f_ref, g_b_ref)):
                chains.append(dict(q=q_ref[sls[d], lanes].astype(F32), g=g_ref[sls[d], lanes],
                                   v=v_ref[sls[d], lanes],
                                   st=st_scr[2 * h + d], rev=bool(d)))
        res = _gla_blocks(chains, tri_ref, code_ref)
        for h in range(P):
            lanes = slice(h * HG_DK, (h + 1) * HG_DK)
            for d in range(2):
                o, st = res[2 * h + d]
                st_scr[2 * h + d] = st
                if finalize:
                    o = o + (ob_acc, of_acc)[d][acc_sls[d], lanes]
                    ms = jnp.mean(o * o, axis=-1, keepdims=True)
                    o = o * lax.rsqrt(ms + EPS) * nw_ref[:, lanes]
                    o_ref[sls[d], lanes] = (o * gate_ref[sls[d], lanes]).astype(BF16)
                else:
                    (of_acc, ob_acc)[d][acc_sls[d], lanes] = o

    def finish():
        if out_state:
            for h in range(P):
                for d in range(2):
                    st_ref[d, h] = st_scr[2 * h + d].T

    return n, step, finish


def _hgrn_part(acts, g, nw, s0, *, layer, row0, B, L, out_state):
    has_s0 = s0 is not None
    H = HG_HEADS
    P = HG_PACK_SHORT if L <= HG_SHORT_SEQ else HG_PACK_LONG
    W = P * HG_DK
    r0 = row0 // L
    blk = lambda off: pl.BlockSpec((L, W), lambda b, h: (r0 + b, off // W + h))
    st_spec = pl.BlockSpec((None, 2, P, HG_DK, HG_DV), lambda b, h: (b, 0, h, 0, 0))
    in_specs = [blk(P_HQ), blk(0), blk(H * HG_DK), blk(P_HI), blk(P_HOG),
                pl.BlockSpec((1, W), lambda b, h: (0, h))]
    args = [acts, g, g, acts, acts, nw]
    if has_s0:
        in_specs.append(pl.BlockSpec((None, None, 2, P, HG_DK, HG_DV), lambda b, h: (b, layer, 0, h, 0, 0)))
        args.append(s0)
    out_specs = [pl.BlockSpec((L, W), lambda b, h: (b, h))]
    out_shape = [jax.ShapeDtypeStruct((B * L, H * HG_DV), BF16)]
    if out_state:
        out_specs.append(st_spec)
        out_shape.append(jax.ShapeDtypeStruct((B, 2, H, HG_DK, HG_DV), F32))
    scratch = [pltpu.VMEM((L // 2, W), F32), pltpu.VMEM((L // 2, W), F32),
               pltpu.VMEM((2 * P, HG_DV, HG_DK), F32), pltpu.VMEM((2, HG_BLOCK, HG_BLOCK), BF16),
               pltpu.VMEM((2, HG_BLOCK, HG_BLOCK), jnp.int32)]
    return dict(setup=functools.partial(_hgrn_setup, L=L, P=P, has_s0=has_s0, out_state=out_state),
                grid=(B, H // P), in_specs=in_specs, args=args, out_specs=out_specs, out_shape=out_shape,
                scratch=scratch)


def _merge_kernel(x_ref, mod_ref, oret_a, oret_b, ohg_a, ohg_b, gr_ref, gh_ref, wr_hbm, wh_hbm, wo_hbm, o_ref,
                  wr_ref, wh_ref, wo_ref, *, n_first):
    i = pl.program_id(0)
    for w_hbm, w_ref in ((wr_hbm, wr_ref), (wh_hbm, wh_ref), (wo_hbm, wo_ref)):
        _stage_weight(i, w_hbm, w_ref, STAGE_ROWS_MERGE)
    oret = jnp.where(i < n_first, oret_a[...], oret_b[...])
    ohg = jnp.where(i < n_first, ohg_a[...], ohg_b[...])
    y = gr_ref[...] * _dot(oret, wr_ref[...]) + gh_ref[...] * _dot(ohg, wh_ref[...])
    out = _dot(y.astype(BF16), wo_ref[...])
    o_ref[...] = x_ref[...] + mod_ref[5:6, :] * out


def _merge_call(x, mod3, orets, ohgs, acts, wr, wh, wo, *, mod_row_fn, tm, name):
    n = x.shape[0]
    n_first = orets[0].shape[0] // tm
    row = pl.BlockSpec((tm, D_MODEL), lambda i: (i, 0))
    gate = lambda off: pl.BlockSpec((tm, D_MODEL), lambda i: (i, off // D_MODEL))
    first = pl.BlockSpec((tm, D_MODEL), lambda i: (jnp.minimum(i, n_first - 1), 0))
    second = pl.BlockSpec((tm, D_MODEL), lambda i: (jnp.maximum(i - n_first, 0), 0))
    return pl.pallas_call(
        functools.partial(_merge_kernel, n_first=n_first),
        grid=(n // tm,),
        in_specs=[row, pl.BlockSpec((None, N_MOD, D_MODEL), lambda i: (mod_row_fn(i, tm), 0, 0)),
                  first, second, first, second, gate(P_GR), gate(P_GH), _HBM, _HBM, _HBM],
        out_specs=row,
        out_shape=jax.ShapeDtypeStruct((n, D_MODEL), F32),
        scratch_shapes=[_bf16_copy(wr), _bf16_copy(wh), _bf16_copy(wo)],
        compiler_params=_params(("arbitrary",)),
        name=name,
    )(x, mod3, *orets, *ohgs, acts, acts, wr, wh, wo)


def _rope_tables(n_tokens, n_identity):
    pos = np.arange(n_tokens)
    r = (pos // GRID_W).astype(np.float32)
    cl = (pos % GRID_W).astype(np.float32)
    quarter = RET_DK // 4
    inv_freq = (np.float32(ROPE_BASE) ** (-np.arange(quarter, dtype=np.float32) / np.float32(quarter)))
    inv_freq = inv_freq.astype(np.float32)
    ang_r = (r[:, None] * inv_freq[None, :]).astype(np.float64)
    ang_c = (cl[:, None] * inv_freq[None, :]).astype(np.float64)
    cos = np.concatenate([np.cos(ang_r)] * 2 + [np.cos(ang_c)] * 2, axis=-1)
    sins = np.concatenate([-np.sin(ang_r), np.sin(ang_r), -np.sin(ang_c), np.sin(ang_c)], axis=-1)
    cos = np.concatenate([cos, np.ones((n_identity, RET_DK))], axis=0).astype(np.float32)
    sins = np.concatenate([sins, np.zeros((n_identity, RET_DK))], axis=0).astype(np.float32)
    return jnp.asarray(cos), jnp.asarray(sins)


def kernel(x_prompt, x_sample, state_ret, state_hgrn, c, c_ctx, ada_w, ada_b, norm_w, ffn1_w13, ffn1_w2,
           ffn2_w13, ffn2_w2, w_in, ret_decay, hg_lb_logits, hg_norm_w, w_ret_proj, w_hg_proj, w_o,
           final_norm_w):
    depth = ada_w.shape[0]
    assert depth == 1
    l = 0
    bp, lp, _ = x_prompt.shape
    bs, ls, _ = x_sample.shape
    n_ctx = bp * lp
    assert n_ctx % ls == 0 and ls % TM_FFN == 0 and ls % TM_PROJ == 0 and ls % TM_MERGE == 0

    cvec = jnp.concatenate([c_ctx[None, :], c, jnp.zeros((8 - 1 - bs, D_MODEL), F32)], axis=0)
    mod = _mod_call(cvec, ada_w[l], ada_b[l][None, :])
    mod3 = mod.reshape(8, N_MOD, D_MODEL)

    def mod_row(i, tm):
        return jnp.where(i < n_ctx // tm, 0, 1 + (i * tm - n_ctx) // ls)

    def rope_blk(i, tm):
        return jnp.where(i < n_ctx // tm, ls // tm, (i - n_ctx // tm) % (ls // tm))

    nw, fw = norm_w[l], final_norm_w[None, :]
    dec, hg_nw = ret_decay[l], hg_norm_w[l][None, :]

    x = _ffn_call((x_prompt.reshape(n_ctx, D_MODEL), x_sample.reshape(bs * ls, D_MODEL)), mod3, nw, fw,
                  ffn1_w13[l], ffn1_w2[l], n_out_split=None, mod_base=0, norm_row=0, final=False,
                  mod_row_fn=mod_row, tm=TM_FFN, name="ffn1")
    acts, g = _proj_call(x, mod3, nw, hg_lb_logits, w_in[l], _rope_tables(ls, TM_PROJ), rope_blk, layer=l,
                         mod_row_fn=mod_row, tm=TM_PROJ, name="proj")
    (ret_ctx, s_ret), (hg_ctx, s_hg) = _scan_call(
        [_ret_part(dec, acts, None, layer=l, row0=0, B=bp, L=lp, out_state=True),
         _hgrn_part(acts, g, hg_nw, None, layer=l, row0=0, B=bp, L=lp, out_state=True)], "scan_ctx")
    (ret_lat,), = _scan_call([_ret_part(dec, acts, state_ret, layer=l, row0=n_ctx, B=bs, L=ls, out_state=False)],
                             "ret_lat")
    (hg_lat,), = _scan_call([_hgrn_part(acts, g, hg_nw, state_hgrn, layer=l, row0=n_ctx, B=bs, L=ls,
                                        out_state=False)], "hgrn_lat")
    x = _merge_call(x, mod3, (ret_ctx, ret_lat), (hg_ctx, hg_lat), acts, w_ret_proj[l], w_hg_proj[l], w_o[l],
                    mod_row_fn=mod_row, tm=TM_MERGE, name="merge")
    yp, ys = _ffn_call((x,), mod3, nw, fw, ffn2_w13[l], ffn2_w2[l], n_out_split=n_ctx, mod_base=6,
                       norm_row=2, final=True, mod_row_fn=mod_row, tm=TM_FFN, name="ffn2")
    return (yp.reshape(bp, lp, D_MODEL), ys.reshape(bs, ls, D_MODEL), s_ret[:, None], s_hg[:, None])
```

```python
import functools

import numpy as np
import jax
import jax.numpy as jnp
from jax import lax
from jax.experimental import pallas as pl
from jax.experimental.pallas import tpu as pltpu

F32 = jnp.float32
BF16 = jnp.bfloat16

D_MODEL = 1024
GRID_W = 64
RET_HEADS = 4
RET_DK = 128
RET_DV = 256
HG_HEADS = 8
HG_DK = 128
HG_DV = 128
D_FF = 2816
N_MOD = 9
ROPE_BASE = 10000.0
EPS = 1e-6

C_RQ, C_RK, C_RV, C_RG, C_HQ, C_HFF, C_HFB, C_HI, C_HOG, C_GR, C_GH = (
    0, 512, 1024, 2048, 3072, 4096, 5120, 6144, 7168, 8192, 9216)
IN_WIDTH = 10240
P_RQ, P_RK, P_RV, P_RG, P_HQ, P_HI, P_HOG, P_GR, P_GH = 0, 512, 1024, 2048, 3072, 4096, 5120, 6144, 7168
P_WIDTH = 8192

VMEM_LIMIT_BYTES = 56 * 1024 * 1024

TM_FFN = 512
TM_PROJ = 512
TM_MERGE = 512
STAGE_SLOTS = 4
STAGE_ROWS_W13, STAGE_ROWS_W2, STAGE_ROWS_W_IN, STAGE_ROWS_MERGE = 128, 256, 16, 512
MOD_TILE = 2304
SCAN_BLOCK = 128
RET_PACK = 4
SCAN_UNROLL = 8
HG_PACK_LONG = 4
HG_PACK_SHORT = 8
HG_SHORT_SEQ = 512
FF_CHUNK = 256
PROJ_CHUNK = 256
LOG2E = 1.4426950408889634
EXP2_CLAMP = 115.0
HG_BLOCK = 128
HG_LEAF = 16
HG_LEVELS = ((4, 16), (5, 32), (6, 64))


def _dot(a, b):
    return jnp.dot(a, b, preferred_element_type=F32)


def _dot_nt(a, b):
    return lax.dot_general(a, b, (((1,), (1,)), ((), ())), preferred_element_type=F32)


def _dot_tn(a, b):
    return lax.dot_general(a, b, (((0,), (0,)), ((), ())), preferred_element_type=F32)


def _sigmoid(x):
    return 0.5 * jnp.tanh(0.5 * x) + 0.5


def _rms(x, w):
    ms = jnp.mean(x * x, axis=-1, keepdims=True)
    return x * lax.rsqrt(ms + EPS) * w


def _rms_modulate(x, w, shift, scale):
    ms = jnp.mean(x * x, axis=-1, keepdims=True)
    return x * lax.rsqrt(ms + EPS) * (w * (1.0 + scale)) + shift


def _params(sem):
    return pltpu.CompilerParams(dimension_semantics=sem, vmem_limit_bytes=VMEM_LIMIT_BYTES)


def _stage_weight(step, w_hbm, w_vmem, rows):
    r, c = w_hbm.shape
    assert r % rows == 0 and rows % 16 == 0
    n_chunks = r // rows
    ahead = STAGE_SLOTS - 1

    def run(stage, sem):
        def copy(k):
            slot = k % STAGE_SLOTS
            return pltpu.make_async_copy(w_hbm.at[pl.ds(k * rows, rows), :], stage.at[slot], sem.at[slot])

        for k in range(min(ahead, n_chunks)):
            copy(k).start()

        def body(k, carry):
            @pl.when(k + ahead < n_chunks)
            def _():
                copy(k + ahead).start()

            copy(k).wait()
            w_vmem[pl.ds(pl.multiple_of(k * rows, rows), rows), :] = stage[k % STAGE_SLOTS].astype(BF16)
            return carry

        lax.fori_loop(0, n_chunks, body, 0)

    @pl.when(step == 0)
    def _():
        pl.run_scoped(run, pltpu.VMEM((STAGE_SLOTS, rows, c), F32), pltpu.SemaphoreType.DMA((STAGE_SLOTS,)))


_HBM = pl.BlockSpec(memory_space=pl.ANY)


def _bf16_copy(w):
    return pltpu.VMEM(w.shape, BF16)


def _mod_kernel(c_ref, w_ref, b_ref, o_ref):
    c = c_ref[...]
    s = (c * _sigmoid(c)).astype(BF16)
    o_ref[...] = _dot(s, w_ref[...].astype(BF16)) + b_ref[...]


def _mod_call(cvec, ada_w, ada_b):
    n = ada_w.shape[1]
    tn = MOD_TILE
    return pl.pallas_call(
        _mod_kernel,
        grid=(n // tn,),
        in_specs=[
            pl.BlockSpec((8, D_MODEL), lambda j: (0, 0)),
            pl.BlockSpec((D_MODEL, tn), lambda j: (0, j)),
            pl.BlockSpec((1, tn), lambda j: (0, j)),
        ],
        out_specs=pl.BlockSpec((8, tn), lambda j: (0, j)),
        out_shape=jax.ShapeDtypeStruct((8, n), F32),
        compiler_params=_params(("arbitrary",)),
        name="mod",
    )(cvec, ada_w, ada_b)


def _ffn_kernel(*refs, mod_base, norm_row, final, n_first, two_in, two_out):
    it = iter(refs)
    xa_ref = next(it)
    xb_ref = next(it) if two_in else None
    mod_ref, nw_ref, fw_ref, w13_hbm, w2_hbm = (next(it) for _ in range(5))
    oa_ref = next(it)
    ob_ref = next(it) if two_out else None
    w13_ref, w2_ref = next(it), next(it)
    i = pl.program_id(0)
    x = xa_ref[...]
    if two_in:
        x = jnp.where(i < n_first, x, xb_ref[...])
    h = _rms_modulate(x, nw_ref[norm_row:norm_row + 1, :], mod_ref[mod_base:mod_base + 1, :],
                      mod_ref[mod_base + 1:mod_base + 2, :]).astype(BF16)
    n_chunks = D_FF // FF_CHUNK

    def chunk(k, acc):
        c0 = k * FF_CHUNK
        a = _dot(h, w13_ref[:, c0:c0 + FF_CHUNK])
        b = _dot(h, w13_ref[:, D_FF + c0:D_FF + c0 + FF_CHUNK])
        act = (a * _sigmoid(a) * b).astype(BF16)
        part = _dot(act, w2_ref[c0:c0 + FF_CHUNK, :])
        return part if acc is None else acc + part

    def finish(acc):
        xo = x + (0.5 * mod_ref[mod_base + 2:mod_base + 3, :]) * acc
        if final:
            xo = _rms(xo, fw_ref[...])
        if two_out:
            @pl.when(i < n_first)
            def _():
                oa_ref[...] = xo

            @pl.when(i >= n_first)
            def _():
                ob_ref[...] = xo
        else:
            oa_ref[...] = xo

    @pl.when(i == 0)
    def _():
        ahead = STAGE_SLOTS - 1

        def run(sa, sb, s2, sem):
            def copies(k):
                slot, c0 = k % STAGE_SLOTS, k * FF_CHUNK
                return (pltpu.make_async_copy(w13_hbm.at[:, pl.ds(c0, FF_CHUNK)], sa.at[slot], sem.at[0, slot]),
                        pltpu.make_async_copy(w13_hbm.at[:, pl.ds(D_FF + c0, FF_CHUNK)], sb.at[slot],
                                              sem.at[1, slot]),
                        pltpu.make_async_copy(w2_hbm.at[pl.ds(c0, FF_CHUNK), :], s2.at[slot], sem.at[2, slot]))

            for k in range(min(ahead, n_chunks)):
                for cp in copies(k):
                    cp.start()
            acc = None
            for k in range(n_chunks):
                if k + ahead < n_chunks:
                    for cp in copies(k + ahead):
                        cp.start()
                for cp in copies(k):
                    cp.wait()
                slot, c0 = k % STAGE_SLOTS, k * FF_CHUNK
                w13_ref[:, c0:c0 + FF_CHUNK] = sa[slot].astype(BF16)
                w13_ref[:, D_FF + c0:D_FF + c0 + FF_CHUNK] = sb[slot].astype(BF16)
                w2_ref[c0:c0 + FF_CHUNK, :] = s2[slot].astype(BF16)
                acc = chunk(k, acc)
            finish(acc)

        pl.run_scoped(run, pltpu.VMEM((STAGE_SLOTS, D_MODEL, FF_CHUNK), F32),
                      pltpu.VMEM((STAGE_SLOTS, D_MODEL, FF_CHUNK), F32),
                      pltpu.VMEM((STAGE_SLOTS, FF_CHUNK, D_MODEL), F32),
                      pltpu.SemaphoreType.DMA((3, STAGE_SLOTS)))

    @pl.when(i > 0)
    def _():
        acc = None
        for k in range(n_chunks):
            acc = chunk(k, acc)
        finish(acc)


def _ffn_call(xs, mod3, norm_w, final_w, w13, w2, *, n_out_split, mod_base, norm_row, final, mod_row_fn, tm,
              name):
    two_in = len(xs) == 2
    two_out = n_out_split is not None
    n = sum(x.shape[0] for x in xs)
    n_first = (xs[0].shape[0] if two_in else n_out_split if two_out else n) // tm
    first = pl.BlockSpec((tm, D_MODEL), lambda i: (jnp.minimum(i, n_first - 1), 0))
    second = pl.BlockSpec((tm, D_MODEL), lambda i: (jnp.maximum(i - n_first, 0), 0))
    whole = pl.BlockSpec((tm, D_MODEL), lambda i: (i, 0))
    kern = functools.partial(_ffn_kernel, mod_base=mod_base, norm_row=norm_row, final=final, n_first=n_first,
                             two_in=two_in, two_out=two_out)
    if two_out:
        out_specs = [first, second]
        out_shape = [jax.ShapeDtypeStruct((n_out_split, D_MODEL), F32),
                     jax.ShapeDtypeStruct((n - n_out_split, D_MODEL), F32)]
    else:
        out_specs = whole
        out_shape = jax.ShapeDtypeStruct((n, D_MODEL), F32)
    return pl.pallas_call(
        kern,
        grid=(n // tm,),
        in_specs=([first, second] if two_in else [whole]) + [
            pl.BlockSpec((None, N_MOD, D_MODEL), lambda i: (mod_row_fn(i, tm), 0, 0)),
            pl.BlockSpec((3, D_MODEL), lambda i: (0, 0)),
            pl.BlockSpec((1, D_MODEL), lambda i: (0, 0)),
            _HBM,
            _HBM,
        ],
        out_specs=out_specs,
        out_shape=out_shape,
        scratch_shapes=[_bf16_copy(w13), _bf16_copy(w2)],
        compiler_params=_params(("arbitrary",)),
        name=name,
    )(*xs, mod3, norm_w, final_w, w13, w2)


def _rope(x, cos, sins, first_quarter):
    partner = jnp.where(first_quarter, pltpu.roll(x, 96, 1), pltpu.roll(x, 32, 1))
    return x * cos + partner * sins


def _proj_kernel(*refs, layer, rope):
    it = iter(refs)
    x_ref, mod_ref, nw_ref, lbl_ref, w_hbm = (next(it) for _ in range(5))
    cos_ref, sin_ref = (next(it), next(it)) if rope else (None, None)
    p_ref, g_ref = next(it), next(it)
    w_ref = next(it)
    step = pl.program_id(0)
    h = _rms_modulate(x_ref[...], nw_ref[1:2, :], mod_ref[3:4, :], mod_ref[4:5, :]).astype(BF16)

    if rope:
        lane = lax.broadcasted_iota(jnp.int32, (x_ref.shape[0], RET_DK), 1)
        first_quarter = (lane & 32) == 0

    def rot(t):
        if not rope:
            return t
        heads = [_rope(t[:, j:j + RET_DK], cos_ref[...], sin_ref[...], first_quarter)
                 for j in range(0, t.shape[1], RET_DK)]
        return jnp.concatenate(heads, axis=1)

    def silu(t):
        return t * _sigmoid(t)

    def log_forget(d):
        def fn(t, c):
            lg = lbl_ref[d, :, c:c + PROJ_CHUNK]
            e = jnp.exp(lg - jnp.max(lg, axis=0, keepdims=True))
            lb = jnp.sum(e[0:layer + 1, :], axis=0, keepdims=True) / jnp.sum(e, axis=0, keepdims=True)
            return jnp.log(lb + (1.0 - lb) * _sigmoid(t)) * LOG2E
        return fn

    parts = (
        (C_RQ, 512, p_ref, P_RQ, lambda t, c: rot(t)),
        (C_RK, 512, p_ref, P_RK, lambda t, c: rot(t * (RET_DK ** -0.5))),
        (C_RV, 1024, p_ref, P_RV, lambda t, c: t),
        (C_RG, 1024, p_ref, P_RG, lambda t, c: silu(t)),
        (C_HQ, 1024, p_ref, P_HQ, lambda t, c: silu(t) * (HG_DK ** -0.5)),
        (C_HFF, 1024, g_ref, 0, log_forget(0)),
        (C_HFB, 1024, g_ref, 1024, log_forget(1)),
        (C_HI, 1024, p_ref, P_HI, lambda t, c: t),
        (C_HOG, 1024, p_ref, P_HOG, lambda t, c: silu(t)),
        (C_GR, 1024, p_ref, P_GR, lambda t, c: _sigmoid(t)),
        (C_GH, 1024, p_ref, P_GH, lambda t, c: _sigmoid(t)),
    )
    jobs = [(w0 + c, o_ref, o0 + c, fn, c) for c in range(0, 1024, PROJ_CHUNK)
            for w0, width, o_ref, o0, fn in parts if c < width]

    def do(job):
        wc, o_ref, oc, fn, c = job
        t = _dot(h, w_ref[:, wc:wc + PROJ_CHUNK])
        o_ref[:, oc:oc + PROJ_CHUNK] = fn(t, c).astype(o_ref.dtype)

    @pl.when(step == 0)
    def _():
        ahead = STAGE_SLOTS - 1

        def run(stage, sem):
            def copy(k):
                slot = k % STAGE_SLOTS
                return pltpu.make_async_copy(w_hbm.at[:, pl.ds(jobs[k][0], PROJ_CHUNK)], stage.at[slot],
                                             sem.at[slot])

            for k in range(min(ahead, len(jobs))):
                copy(k).start()
            for k, job in enumerate(jobs):
                if k + ahead < len(jobs):
                    copy(k + ahead).start()
                copy(k).wait()
                w_ref[:, job[0]:job[0] + PROJ_CHUNK] = stage[k % STAGE_SLOTS].astype(BF16)
                do(job)

        pl.run_scoped(run, pltpu.VMEM((STAGE_SLOTS, D_MODEL, PROJ_CHUNK), F32),
                      pltpu.SemaphoreType.DMA((STAGE_SLOTS,)))

    @pl.when(step > 0)
    def _():
        for job in jobs:
            do(job)


def _proj_call(x, mod3, norm_w, lb_logits, w_in, rope_tabs, rope_blk_fn, *, layer, mod_row_fn, tm, name):
    n = x.shape[0]
    slots = lb_logits.shape[1]
    row = lambda w: pl.BlockSpec((tm, w), lambda i: (i, 0))
    outs = [(P_WIDTH, BF16), (2 * HG_HEADS * HG_DK, F32)]
    in_specs = [
        row(D_MODEL),
        pl.BlockSpec((None, N_MOD, D_MODEL), lambda i: (mod_row_fn(i, tm), 0, 0)),
        pl.BlockSpec((3, D_MODEL), lambda i: (0, 0)),
        pl.BlockSpec((2, slots, 1024), lambda i: (0, 0, 0)),
        _HBM,
    ]
    args = [x, mod3, norm_w, lb_logits, w_in]
    if rope_tabs is not None:
        in_specs += [pl.BlockSpec((tm, RET_DK), lambda i: (rope_blk_fn(i, tm), 0))] * 2
        args += list(rope_tabs)
    return pl.pallas_call(
        functools.partial(_proj_kernel, layer=layer, rope=rope_tabs is not None),
        grid=(n // tm,),
        in_specs=in_specs,
        out_specs=[row(w) for w, _ in outs],
        out_shape=[jax.ShapeDtypeStruct((n, w), dt) for w, dt in outs],
        scratch_shapes=[_bf16_copy(w_in)],
        compiler_params=_params(("arbitrary",)),
        name=name,
    )(*args)


def _log_sigmoid(x):
    return -(jnp.maximum(-x, 0.0) + jnp.log1p(jnp.exp(-jnp.abs(x))))


def _half_acc_slices(i, n, blk, finalize):
    if finalize:
        starts = ((i - n // 2) * blk, (n - 1 - i) * blk)
    else:
        starts = (i * blk, (n // 2 - 1 - i) * blk)
    return tuple(pl.ds(pl.multiple_of(s, blk), blk) for s in starts)


def _run_scans(scans):
    n = scans[0][0]
    assert all(s[0] == n for s in scans)

    def half(finalize):
        def body(i, carry):
            for _, step, _ in scans:
                step(i, finalize)
            return carry
        return body

    lax.fori_loop(0, n // 2, half(False), 0, unroll=min(SCAN_UNROLL, n // 2))
    lax.fori_loop(n // 2, n, half(True), 0, unroll=min(SCAN_UNROLL, n // 2))
    for _, _, finish in scans:
        finish()


def _scan_kernel(*refs, parts):
    ins = [n for _, n, _, _ in parts]
    outs = [n for _, _, n, _ in parts]
    offs = [0, sum(ins), sum(ins) + sum(outs)]
    scans = []
    for setup, n_in, n_out, n_scr in parts:
        pick = [refs[o:o + c] for o, c in zip(offs, (n_in, n_out, n_scr))]
        offs = [o + c for o, c in zip(offs, (n_in, n_out, n_scr))]
        scans.append(setup(*pick))
    _run_scans(scans)


def _ret_setup(in_refs, out_refs, scratch, *, L, has_s0, out_state):
    it = iter(in_refs)
    dec_ref = next(it)
    q_ref, k_ref, v_ref, gate_ref = next(it), next(it), next(it), next(it)
    s0_ref = next(it) if has_s0 else None
    o_ref = out_refs[0]
    st_ref = out_refs[1] if out_state else None
    of_acc, ob_acc, st_scr, cst = scratch

    C = SCAN_BLOCK
    P = RET_PACK
    n = L // C
    pair = pl.program_id(1)
    ii = lax.broadcasted_iota(jnp.int32, (C, C), 0).astype(F32)
    jj = lax.broadcasted_iota(jnp.int32, (C, C), 1).astype(F32)
    diff = ii - jj

    @pl.when(jnp.logical_and(pl.program_id(0) == 0, pair == 0))
    def _():
        for hd in range(RET_HEADS):
            lgf = _log_sigmoid(jnp.full((C, C), dec_ref[0, hd], F32))
            lgb = _log_sigmoid(jnp.full((C, C), dec_ref[1, hd], F32))
            cst[hd, 0] = (jnp.where(diff >= 0, jnp.exp(jnp.maximum(diff, 0.0) * lgf), 0.0)
                          + jnp.where(diff <= 0, jnp.exp(jnp.maximum(-diff, 0.0) * lgb), 0.0))
            cst[hd, 1] = jnp.exp((ii + 1.0) * lgf)
            cst[hd, 2] = jnp.exp((C - 1.0 - ii) * lgf)
            cst[hd, 3] = jnp.exp((C - ii) * lgb)
            cst[hd, 4] = jnp.exp(ii * lgb)

    cds = []
    for h in range(P):
        hd = pair * P + h
        cds.append([jnp.exp(C * _log_sigmoid(jnp.full((1, RET_DV), dec_ref[d, hd], F32))) for d in range(2)])
        for d in range(2):
            st_scr[2 * h + d] = s0_ref[d, h] if has_s0 else jnp.zeros((RET_DK, RET_DV), F32)

    def step(i, finalize):
        sls = (pl.ds(pl.multiple_of(i * C, C), C), pl.ds(pl.multiple_of((n - 1 - i) * C, C), C))
        acc_sls = _half_acc_slices(i, n, C, finalize)
        chains = []
        for h in range(P):
            ql = slice(h * RET_DK, (h + 1) * RET_DK)
            vl = slice(h * RET_DV, (h + 1) * RET_DV)
            for d in range(2):
                chains.append(dict(h=h, d=d, vl=vl, q=q_ref[sls[d], ql], k=k_ref[sls[d], ql],
                                   v=v_ref[sls[d], vl]))
        for c in chains:
            if c["d"] == 0:
                c["s"] = _dot_nt(c["q"], c["k"])
        for c in chains:
            h, d, v, vl = c["h"], c["d"], c["v"], c["vl"]
            hd = pair * P + h
            st = st_scr[2 * h + d]
            qd = (c["q"].astype(F32) * cst[hd, 1 + 2 * d]).astype(BF16)
            if d == 0:
                p = (c["s"] * cst[hd, 0]).astype(BF16)
                o = _dot(jnp.concatenate([p, qd], axis=1), jnp.concatenate([v, st.astype(BF16)], axis=0))
            else:
                o = _dot(qd, st.astype(BF16))
            kd = (c["k"].astype(F32) * cst[hd, 2 + 2 * d]).astype(BF16)
            st_scr[2 * h + d] = cds[h][d] * st + _dot_tn(kd, v)
            if finalize:
                o = o + (ob_acc, of_acc)[d][acc_sls[d], vl]
                ms = jnp.mean(o * o, axis=-1, keepdims=True)
                o_ref[sls[d], vl] = (o * lax.rsqrt(ms + EPS) * gate_ref[sls[d], vl]).astype(BF16)
            else:
                (of_acc, ob_acc)[d][acc_sls[d], vl] = o

    def finish():
        if out_state:
            for h in range(P):
                for d in range(2):
                    st_ref[d, h] = st_scr[2 * h + d]

    return n, step, finish


def _ret_part(dec, acts, s0, *, layer, row0, B, L, out_state):
    has_s0 = s0 is not None
    H = RET_HEADS
    P = RET_PACK
    r0 = row0 // L
    st_spec = pl.BlockSpec((None, 2, P, RET_DK, RET_DV), lambda b, h: (b, 0, h, 0, 0))
    cols = lambda width, off: pl.BlockSpec((L, width), lambda b, h: (r0 + b, off // width + h))
    in_specs = [
        pl.BlockSpec(memory_space=pltpu.SMEM),
        cols(P * RET_DK, P_RQ), cols(P * RET_DK, P_RK), cols(P * RET_DV, P_RV), cols(P * RET_DV, P_RG),
    ]
    args = [dec, acts, acts, acts, acts]
    if has_s0:
        in_specs.append(pl.BlockSpec((None, None, 2, P, RET_DK, RET_DV), lambda b, h: (b, layer, 0, h, 0, 0)))
        args.append(s0)
    out_specs = [pl.BlockSpec((L, P * RET_DV), lambda b, h: (b, h))]
    out_shape = [jax.ShapeDtypeStruct((B * L, H * RET_DV), BF16)]
    if out_state:
        out_specs.append(st_spec)
        out_shape.append(jax.ShapeDtypeStruct((B, 2, H, RET_DK, RET_DV), F32))
    scratch = [pltpu.VMEM((L // 2, P * RET_DV), F32), pltpu.VMEM((L // 2, P * RET_DV), F32),
               pltpu.VMEM((2 * P, RET_DK, RET_DV), F32), pltpu.VMEM((H, 5, SCAN_BLOCK, SCAN_BLOCK), F32)]
    return dict(setup=functools.partial(_ret_setup, L=L, has_s0=has_s0, out_state=out_state),
                grid=(B, H // P), in_specs=in_specs, args=args, out_specs=out_specs, out_shape=out_shape,
                scratch=scratch)


def _scan_call(parts, name):
    grid = parts[0]["grid"]
    assert all(p["grid"] == grid for p in parts)
    kern = functools.partial(_scan_kernel, parts=[(p["setup"], len(p["in_specs"]), len(p["out_specs"]),
                                                   len(p["scratch"])) for p in parts])
    cat = lambda key: [x for p in parts for x in p[key]]
    res = pl.pallas_call(
        kern,
        grid=grid,
        in_specs=cat("in_specs"),
        out_specs=cat("out_specs"),
        out_shape=cat("out_shape"),
        scratch_shapes=cat("scratch"),
        compiler_params=_params(("arbitrary", "arbitrary")),
        name=name,
    )(*cat("args"))
    out, k = [], 0
    for p in parts:
        out.append(res[k:k + len(p["out_specs"])])
        k += len(p["out_specs"])
    return out


def _gla_blocks(chains, tri_ref, code_ref):
    T = HG_BLOCK
    K = HG_DK

    def cumulate(c):
        g2 = c["g"]
        c["k"] = 1.0 - jnp.exp2(g2)
        hi = g2.astype(BF16)
        lo = (g2 - hi.astype(F32)).astype(BF16)
        tri = tri_ref[int(c["rev"])]
        c["b"] = _dot(tri, hi) + _dot(tri, lo)

    def leaf(c):
        bl = c["b"].reshape(T // HG_LEAF, HG_LEAF, K)
        r = HG_LEAF // 2 if c["rev"] else HG_LEAF // 2 - 1
        d = jnp.clip(bl - bl[:, r:r + 1, :], -EXP2_CLAMP, EXP2_CLAMP).reshape(T, K)
        e = jnp.exp2(d)
        p = _dot_nt((c["q"] * e).astype(BF16), (c["k"] / e).astype(BF16))
        c["a"] = jnp.where(code_ref[int(c["rev"])] == 0, p, 0.0)

    def level(c, lv, m):
        seg = 2 * m
        ng = T // seg
        b3, q3, k3 = (c[n].reshape(ng, seg, K) for n in ("b", "q", "k"))
        if c["rev"]:
            qs, ks, r = slice(0, m), slice(m, seg), m
        else:
            qs, ks, r = slice(m, seg), slice(0, m), m - 1
        ref = b3[:, r:r + 1, :]
        qm = (q3[:, qs, :] * jnp.exp2(b3[:, qs, :] - ref)).reshape(ng * m, K).astype(BF16)
        kz = k3[:, ks, :] * jnp.exp2(ref - b3[:, ks, :])
        zeros = jnp.zeros((ng, m, K), F32)
        kfull = jnp.concatenate([zeros, kz] if c["rev"] else [kz, zeros], axis=1)
        p = _dot_nt(qm, kfull.reshape(T, K).astype(BF16)).reshape(ng, m, T)
        a3 = c["a"].reshape(ng, seg, T)
        c3 = code_ref[int(c["rev"])].reshape(ng, seg, T)
        upd = jnp.where(c3[:, qs, :] == lv, p, a3[:, qs, :])
        parts = [upd, a3[:, ks, :]] if c["rev"] else [a3[:, ks, :], upd]
        c["a"] = jnp.concatenate(parts, axis=1).reshape(T, T)

    def finish(c):
        b, q, k, v, st = c["b"], c["q"], c["k"], c["v"], c["st"]
        o = _dot(c["a"].astype(BF16), v) + _dot_nt((q * jnp.exp2(b)).astype(BF16), st.astype(BF16))
        end_row = 0 if c["rev"] else T - 1
        b1 = b.reshape(1, T, K)
        b_end = b1[:, end_row:end_row + 1, :]
        kd = (k * jnp.exp2(b_end - b1).reshape(T, K)).astype(BF16)
        c["out"] = (o, st * jnp.exp2(b_end.reshape(1, K)) + _dot_tn(v, kd))

    for c in chains:
        cumulate(c)
    for c in chains:
        leaf(c)
    for lv, m in HG_LEVELS:
        for c in chains:
            level(c, lv, m)
    for c in chains:
        finish(c)
    return [c["out"] for c in chains]


def _hgrn_setup(in_refs, out_refs, scratch, *, L, P, has_s0, out_state):
    it = iter(in_refs)
    q_ref, g_f_ref, g_b_ref, v_ref, gate_ref, nw_ref = (next(it) for _ in range(6))
    s0_ref = next(it) if has_s0 else None
    o_ref = out_refs[0]
    st_ref = out_refs[1] if out_state else None
    of_acc, ob_acc, st_scr, tri_ref, code_ref = scratch

    T = HG_BLOCK
    n = L // T

    @pl.when(jnp.logical_and(pl.program_id(0) == 0, pl.program_id(1) == 0))
    def _():
        ti = lax.broadcasted_iota(jnp.int32, (T, T), 0)
        si = lax.broadcasted_iota(jnp.int32, (T, T), 1)
        x = ti ^ si
        tri_ref[0] = (si <= ti).astype(BF16)
        tri_ref[1] = (si >= ti).astype(BF16)
        lvl = jnp.where(x < HG_LEAF, 0, HG_LEVELS[0][0] - 1 + sum((x >= m).astype(jnp.int32) for _, m in HG_LEVELS))
        code_ref[0] = jnp.where(si <= ti, lvl, -1)
        code_ref[1] = jnp.where(si >= ti, lvl, -1)

    for h in range(P):
        for d in range(2):
            st_scr[2 * h + d] = s0_ref[d, h].T if has_s0 else jnp.zeros((HG_DV, HG_DK), F32)

    def step(i, finalize):
        sls = (pl.ds(pl.multiple_of(i * T, T), T), pl.ds(pl.multiple_of((n - 1 - i) * T, T), T))
        acc_sls = _half_acc_slices(i, n, T, finalize)
        chains = []
        for h in range(P):
            lanes = slice(h * HG_DK, (h + 1) * HG_DK)
            for d, g_ref in enumerate((g_f_ref, g_b_ref)):
                chains.append(dict(q=q_ref[sls[d], lanes].astype(F32), g=g_ref[sls[d], lanes],
                                   v=v_ref[sls[d], lanes],
                                   st=st_scr[2 * h + d], rev=bool(d)))
        res = _gla_blocks(chains, tri_ref, code_ref)
        for h in range(P):
            lanes = slice(h * HG_DK, (h + 1) * HG_DK)
            for d in range(2):
                o, st = res[2 * h + d]
                st_scr[2 * h + d] = st
                if finalize:
                    o = o + (ob_acc, of_acc)[d][acc_sls[d], lanes]
                    ms = jnp.mean(o * o, axis=-1, keepdims=True)
                    o = o * lax.rsqrt(ms + EPS) * nw_ref[:, lanes]
                    o_ref[sls[d], lanes] = (o * gate_ref[sls[d], lanes]).astype(BF16)
                else:
                    (of_acc, ob_acc)[d][acc_sls[d], lanes] = o

    def finish():
        if out_state:
            for h in range(P):
                for d in range(2):
                    st_ref[d, h] = st_scr[2 * h + d].T

    return n, step, finish


def _hgrn_part(acts, g, nw, s0, *, layer, row0, B, L, out_state):
    has_s0 = s0 is not None
    H = HG_HEADS
    P = HG_PACK_SHORT if L <= HG_SHORT_SEQ else HG_PACK_LONG
    W = P * HG_DK
    r0 = row0 // L
    blk = lambda off: pl.BlockSpec((L, W), lambda b, h: (r0 + b, off // W + h))
    st_spec = pl.BlockSpec((None, 2, P, HG_DK, HG_DV), lambda b, h: (b, 0, h, 0, 0))
    in_specs = [blk(P_HQ), blk(0), blk(H * HG_DK), blk(P_HI), blk(P_HOG),
                pl.BlockSpec((1, W), lambda b, h: (0, h))]
    args = [acts, g, g, acts, acts, nw]
    if has_s0:
        in_specs.append(pl.BlockSpec((None, None, 2, P, HG_DK, HG_DV), lambda b, h: (b, layer, 0, h, 0, 0)))
        args.append(s0)
    out_specs = [pl.BlockSpec((L, W), lambda b, h: (b, h))]
    out_shape = [jax.ShapeDtypeStruct((B * L, H * HG_DV), BF16)]
    if out_state:
        out_specs.append(st_spec)
        out_shape.append(jax.ShapeDtypeStruct((B, 2, H, HG_DK, HG_DV), F32))
    scratch = [pltpu.VMEM((L // 2, W), F32), pltpu.VMEM((L // 2, W), F32),
               pltpu.VMEM((2 * P, HG_DV, HG_DK), F32), pltpu.VMEM((2, HG_BLOCK, HG_BLOCK), BF16),
               pltpu.VMEM((2, HG_BLOCK, HG_BLOCK), jnp.int32)]
    return dict(setup=functools.partial(_hgrn_setup, L=L, P=P, has_s0=has_s0, out_state=out_state),
                grid=(B, H // P), in_specs=in_specs, args=args, out_specs=out_specs, out_shape=out_shape,
                scratch=scratch)


def _merge_kernel(x_ref, mod_ref, oret_a, oret_b, ohg_a, ohg_b, gr_ref, gh_ref, wr_hbm, wh_hbm, wo_hbm, o_ref,
                  wr_ref, wh_ref, wo_ref, *, n_first):
    i = pl.program_id(0)
    for w_hbm, w_ref in ((wr_hbm, wr_ref), (wh_hbm, wh_ref), (wo_hbm, wo_ref)):
        _stage_weight(i, w_hbm, w_ref, STAGE_ROWS_MERGE)
    oret = jnp.where(i < n_first, oret_a[...], oret_b[...])
    ohg = jnp.where(i < n_first, ohg_a[...], ohg_b[...])
    y = gr_ref[...] * _dot(oret, wr_ref[...]) + gh_ref[...] * _dot(ohg, wh_ref[...])
    out = _dot(y.astype(BF16), wo_ref[...])
    o_ref[...] = x_ref[...] + mod_ref[5:6, :] * out


def _merge_call(x, mod3, orets, ohgs, acts, wr, wh, wo, *, mod_row_fn, tm, name):
    n = x.shape[0]
    n_first = orets[0].shape[0] // tm
    row = pl.BlockSpec((tm, D_MODEL), lambda i: (i, 0))
    gate = lambda off: pl.BlockSpec((tm, D_MODEL), lambda i: (i, off // D_MODEL))
    first = pl.BlockSpec((tm, D_MODEL), lambda i: (jnp.minimum(i, n_first - 1), 0))
    second = pl.BlockSpec((tm, D_MODEL), lambda i: (jnp.maximum(i - n_first, 0), 0))
    return pl.pallas_call(
        functools.partial(_merge_kernel, n_first=n_first),
        grid=(n // tm,),
        in_specs=[row, pl.BlockSpec((None, N_MOD, D_MODEL), lambda i: (mod_row_fn(i, tm), 0, 0)),
                  first, second, first, second, gate(P_GR), gate(P_GH), _HBM, _HBM, _HBM],
        out_specs=row,
        out_shape=jax.ShapeDtypeStruct((n, D_MODEL), F32),
        scratch_shapes=[_bf16_copy(wr), _bf16_copy(wh), _bf16_copy(wo)],
        compiler_params=_params(("arbitrary",)),
        name=name,
    )(x, mod3, *orets, *ohgs, acts, acts, wr, wh, wo)


def _rope_tables(n_tokens, n_identity):
    pos = np.arange(n_tokens)
    r = (pos // GRID_W).astype(np.float32)
    cl = (pos % GRID_W).astype(np.float32)
    quarter = RET_DK // 4
    inv_freq = (np.float32(ROPE_BASE) ** (-np.arange(quarter, dtype=np.float32) / np.float32(quarter)))
    inv_freq = inv_freq.astype(np.float32)
    ang_r = (r[:, None] * inv_freq[None, :]).astype(np.float64)
    ang_c = (cl[:, None] * inv_freq[None, :]).astype(np.float64)
    cos = np.concatenate([np.cos(ang_r)] * 2 + [np.cos(ang_c)] * 2, axis=-1)
    sins = np.concatenate([-np.sin(ang_r), np.sin(ang_r), -np.sin(ang_c), np.sin(ang_c)], axis=-1)
    cos = np.concatenate([cos, np.ones((n_identity, RET_DK))], axis=0).astype(np.float32)
    sins = np.concatenate([sins, np.zeros((n_identity, RET_DK))], axis=0).astype(np.float32)
    return jnp.asarray(cos), jnp.asarray(sins)


def kernel(x_prompt, x_sample, state_ret, state_hgrn, c, c_ctx, ada_w, ada_b, norm_w, ffn1_w13, ffn1_w2,
           ffn2_w13, ffn2_w2, w_in, ret_decay, hg_lb_logits, hg_norm_w, w_ret_proj, w_hg_proj, w_o,
           final_norm_w):
    depth = ada_w.shape[0]
    assert depth == 1
    l = 0
    bp, lp, _ = x_prompt.shape
    bs, ls, _ = x_sample.shape
    n_ctx = bp * lp
    assert n_ctx % ls == 0 and ls % TM_FFN == 0 and ls % TM_PROJ == 0 and ls % TM_MERGE == 0

    cvec = jnp.concatenate([c_ctx[None, :], c, jnp.zeros((8 - 1 - bs, D_MODEL), F32)], axis=0)
    mod = _mod_call(cvec, ada_w[l], ada_b[l][None, :])
    mod3 = mod.reshape(8, N_MOD, D_MODEL)

    def mod_row(i, tm):
        return jnp.where(i < n_ctx // tm, 0, 1 + (i * tm - n_ctx) // ls)

    def rope_blk(i, tm):
        return jnp.where(i < n_ctx // tm, ls // tm, (i - n_ctx // tm) % (ls // tm))

    nw, fw = norm_w[l], final_norm_w[None, :]
    dec, hg_nw = ret_decay[l], hg_norm_w[l][None, :]

    x = _ffn_call((x_prompt.reshape(n_ctx, D_MODEL), x_sample.reshape(bs * ls, D_MODEL)), mod3, nw, fw,
                  ffn1_w13[l], ffn1_w2[l], n_out_split=None, mod_base=0, norm_row=0, final=False,
                  mod_row_fn=mod_row, tm=TM_FFN, name="ffn1")
    acts, g = _proj_call(x, mod3, nw, hg_lb_logits, w_in[l], _rope_tables(ls, TM_PROJ), rope_blk, layer=l,
                         mod_row_fn=mod_row, tm=TM_PROJ, name="proj")
    (ret_ctx, s_ret), (hg_ctx, s_hg) = _scan_call(
        [_ret_part(dec, acts, None, layer=l, row0=0, B=bp, L=lp, out_state=True),
         _hgrn_part(acts, g, hg_nw, None, layer=l, row0=0, B=bp, L=lp, out_state=True)], "scan_ctx")
    (ret_lat,), = _scan_call([_ret_part(dec, acts, state_ret, layer=l, row0=n_ctx, B=bs, L=ls, out_state=False)],
                             "ret_lat")
    (hg_lat,), = _scan_call([_hgrn_part(acts, g, hg_nw, state_hgrn, layer=l, row0=n_ctx, B=bs, L=ls,
                                        out_state=False)], "hgrn_lat")
    x = _merge_call(x, mod3, (ret_ctx, ret_lat), (hg_ctx, hg_lat), acts, w_ret_proj[l], w_hg_proj[l], w_o[l],
                    mod_row_fn=mod_row, tm=TM_MERGE, name="merge")
    yp, ys = _ffn_call((x,), mod3, nw, fw, ffn2_w13[l], ffn2_w2[l], n_out_split=n_ctx, mod_base=6,
                       norm_row=2, final=True, mod_row_fn=mod_row, tm=TM_FFN, name="ffn2")
    return (yp.reshape(bp, lp, D_MODEL), ys.reshape(bs, ls, D_MODEL), s_ret[:, None], s_hg[:, None])
```
